```python
import jax, jax.numpy as jnp
from jax import lax
import numpy as np

D_MODEL = 1024
BATCH = 8
SEQ = 2048
DEPTH = 1

A_HEADS = 8
A_HEAD_DIM = 64
IDX_HEADS = 8
IDX_DIM = 64
TOPK_MAX = 256
B_HEADS = 8
Q_LORA = 384
KV_LORA = 256
QK_NOPE = 64
QK_ROPE = 32
V_DIM = 64
ROPE_THETA = 10000.0
Q_BLOCK = 128
N_EXPERTS = 32
TOP_K = 4
D_FF = 1024
SWIGLU_LIMIT = 7.0
SWIGLU_ALPHA = 1.702
DN_ALPHA = (2 * DEPTH) ** 0.25
DN_BETA = (8 * DEPTH) ** -0.25
LN_EPS = 1e-5
RMS_EPS = 1e-6

A_WIDTH = A_HEADS * A_HEAD_DIM
B_WIDTH = B_HEADS * V_DIM
SPLITS = (A_WIDTH, A_WIDTH, A_WIDTH, IDX_HEADS * IDX_DIM, IDX_DIM, IDX_HEADS,
          Q_LORA, KV_LORA, QK_ROPE, 2 * D_MODEL)
D_IN = sum(SPLITS)

kernel_name = "hybrid_dsa_mla_gated_moe_deepnorm"


def layer_norm(x, g, b):
    xf = x.astype(jnp.float32)
    mu = jnp.mean(xf, axis=-1, keepdims=True)
    var = jnp.mean(jnp.square(xf - mu), axis=-1, keepdims=True)
    return ((xf - mu) * lax.rsqrt(var + LN_EPS) * g + b).astype(x.dtype)


def rms_norm(x, g):
    xf = x.astype(jnp.float32)
    return (xf * lax.rsqrt(jnp.mean(jnp.square(xf), -1, keepdims=True) + RMS_EPS) * g).astype(x.dtype)


def rope(x, positions):
    half = x.shape[-1] // 2
    inv_freq = ROPE_THETA ** (-jnp.arange(half, dtype=jnp.float32) / half)
    ang = positions.astype(jnp.float32)[..., None] * inv_freq
    ang = ang.reshape(ang.shape[:2] + (1,) * (x.ndim - 3) + (half,))
    cos, sin = jnp.cos(ang).astype(x.dtype), jnp.sin(ang).astype(x.dtype)
    x1, x2 = x[..., :half], x[..., half:]
    return jnp.concatenate([x1 * cos - x2 * sin, x1 * sin + x2 * cos], axis=-1)


def sweep_query_blocks(block_fn, seq):
    out = lax.map(block_fn, jnp.arange(seq // Q_BLOCK) * Q_BLOCK)
    out = jnp.moveaxis(out, 0, 1)
    return out.reshape((out.shape[0], seq) + out.shape[3:])


def dsa_attention(q, k, v, q_idx, k_idx, w_idx):
    seq = q.shape[1]
    k_sel = min(TOPK_MAX, seq // 4)
    key_pos = jnp.arange(seq)
    scale = A_HEAD_DIM ** -0.5
    idx_scale = IDX_DIM ** -0.5
    w_scale = IDX_HEADS ** -0.5

    def block(start):
        qb = lax.dynamic_slice_in_dim(q, start, Q_BLOCK, axis=1)
        qib = lax.dynamic_slice_in_dim(q_idx, start, Q_BLOCK, axis=1)
        wb = lax.dynamic_slice_in_dim(w_idx, start, Q_BLOCK, axis=1)
        q_pos = start + jnp.arange(Q_BLOCK)
        causal = key_pos[None, :] <= q_pos[:, None]
        rel = jax.nn.relu(jnp.einsum('bthd,bsd->bths', qib, k_idx).astype(jnp.float32) * idx_scale)
        score = jnp.einsum('bths,bth->bts', rel, wb.astype(jnp.float32) * w_scale)
        score = jnp.where(causal[None], score, -jnp.inf)
        _, sel = lax.top_k(score, k_sel)
        valid = sel <= q_pos[None, :, None]
        k_g = jax.vmap(lambda kb, ib: kb[ib])(k, sel)
        v_g = jax.vmap(lambda vb, ib: vb[ib])(v, sel)
        logits = jnp.einsum('bthd,btkhd->bhtk', qb, k_g).astype(jnp.float32) * scale
        logits = jnp.where(valid[:, None], logits, -jnp.inf)
        p = jax.nn.softmax(logits, axis=-1).astype(v.dtype)
        return jnp.einsum('bhtk,btkhd->bthd', p, v_g)

    return sweep_query_blocks(block, seq)


def mla_attention(q_nope, q_rope, k_nope, k_rope, v):
    seq = q_nope.shape[1]
    key_pos = jnp.arange(seq)
    scale = (QK_NOPE + QK_ROPE) ** -0.5

    def block(start):
        qn = lax.dynamic_slice_in_dim(q_nope, start, Q_BLOCK, axis=1)
        qr = lax.dynamic_slice_in_dim(q_rope, start, Q_BLOCK, axis=1)
        q_pos = start + jnp.arange(Q_BLOCK)
        causal = key_pos[None, :] <= q_pos[:, None]
        logits = (jnp.einsum('bthd,bshd->bhts', qn, k_nope)
                  + jnp.einsum('bthd,bsd->bhts', qr, k_rope)).astype(jnp.float32) * scale
        logits = jnp.where(causal[None, None], logits, -jnp.inf)
        p = jax.nn.softmax(logits, axis=-1).astype(v.dtype)
        return jnp.einsum('bhts,bshd->bthd', p, v)

    return sweep_query_blocks(block, seq)


def routed_experts(h, w_router, b_router, w_up, b_up, w_down, b_down):
    bsz, seq, d = h.shape
    t = h.reshape(-1, d)
    logits = (t @ w_router + b_router).astype(jnp.float32)
    top_vals, top_idx = lax.top_k(logits, TOP_K)
    gates = jax.nn.softmax(top_vals, axis=-1)
    combine = jnp.sum(jax.nn.one_hot(top_idx, N_EXPERTS, dtype=jnp.float32) * gates[..., None], axis=1)

    def expert(acc, p):
        wu, bu, wd, bd, c = p
        a = t @ wu + bu
        g = jnp.minimum(a[:, :D_FF], SWIGLU_LIMIT)
        lin = jnp.clip(a[:, D_FF:], -SWIGLU_LIMIT, SWIGLU_LIMIT)
        y = ((lin + 1.0) * (g * jax.nn.sigmoid(SWIGLU_ALPHA * g))) @ wd + bd
        return acc + c[:, None].astype(t.dtype) * y, None

    acc, _ = lax.scan(expert, jnp.zeros_like(t), (w_up, b_up, w_down, b_down, combine.T))
    return acc.reshape(bsz, seq, d)


def setup_inputs(seed: int = 0) -> dict:
    key = jax.random.key(seed)
    ks = jax.random.split(key, 24)
    f32 = jnp.float32
    L = DEPTH

    def nrm(k, shape, scale):
        return jax.random.normal(k, shape, f32) * scale

    x = nrm(ks[0], (BATCH, SEQ, D_MODEL), 1.0)
    start = jax.random.randint(ks[1], (BATCH, 1), 0, 4096, dtype=jnp.int32)
    positions = start + jnp.arange(SEQ, dtype=jnp.int32)[None, :]
    return {
        "x": x,
        "positions": positions,
        "w_in": nrm(ks[2], (L, D_MODEL, D_IN), D_MODEL ** -0.5),
        "b_gate": nrm(ks[3], (L, 2 * D_MODEL), 0.1),
        "rms_cq": 1.0 + nrm(ks[4], (L, Q_LORA), 0.01),
        "rms_ckv": 1.0 + nrm(ks[5], (L, KV_LORA), 0.01),
        "w_uq": nrm(ks[6], (L, Q_LORA, B_HEADS * (QK_NOPE + QK_ROPE)), Q_LORA ** -0.5),
        "w_ukv": nrm(ks[7], (L, KV_LORA, B_HEADS * (QK_NOPE + V_DIM)), KV_LORA ** -0.5),
        "w_o_a": nrm(ks[8], (L, A_WIDTH, D_MODEL), A_WIDTH ** -0.5),
        "w_o_b": nrm(ks[9], (L, B_WIDTH, D_MODEL), B_WIDTH ** -0.5),
        "w_out": nrm(ks[10], (L, D_MODEL, D_MODEL), D_MODEL ** -0.5 * DN_BETA),
        "ln1_g": 1.0 + nrm(ks[11], (L, D_MODEL), 0.01),
        "ln1_b": nrm(ks[12], (L, D_MODEL), 0.01),
        "w_router": nrm(ks[13], (L, D_MODEL, N_EXPERTS), D_MODEL ** -0.5),
        "b_router": nrm(ks[14], (L, N_EXPERTS), 0.01),
        "w_up": nrm(ks[15], (L, N_EXPERTS, D_MODEL, 2 * D_FF), D_MODEL ** -0.5),
        "b_up": nrm(ks[16], (L, N_EXPERTS, 2 * D_FF), 0.01),
        "w_down": nrm(ks[17], (L, N_EXPERTS, D_FF, D_MODEL), D_FF ** -0.5 * DN_BETA),
        "b_down": nrm(ks[18], (L, N_EXPERTS, D_MODEL), 0.01),
        "ln2_g": 1.0 + nrm(ks[19], (L, D_MODEL), 0.01),
        "ln2_b": nrm(ks[20], (L, D_MODEL), 0.01),
    }


def reference(x, positions, w_in, b_gate, rms_cq, rms_ckv, w_uq, w_ukv, w_o_a, w_o_b,
              w_out, ln1_g, ln1_b, w_router, b_router, w_up, b_up, w_down, b_down,
              ln2_g, ln2_b):
    bsz, seq, _ = x.shape
    offsets = [int(o) for o in np.cumsum(SPLITS)[:-1]]
    for l in range(DEPTH):
        z = x @ w_in[l]
        qa, ka, va, qi, ki, wi, cq, ckv, kr, gates = jnp.split(z, offsets, axis=-1)
        qa = rope(qa.reshape(bsz, seq, A_HEADS, A_HEAD_DIM), positions)
        ka = rope(ka.reshape(bsz, seq, A_HEADS, A_HEAD_DIM), positions)
        va = va.reshape(bsz, seq, A_HEADS, A_HEAD_DIM)
        qi = rope(qi.reshape(bsz, seq, IDX_HEADS, IDX_DIM), positions)
        ki = rope(ki, positions)
        o_a = dsa_attention(qa, ka, va, qi, ki, wi).reshape(bsz, seq, A_WIDTH)
        qb = (rms_norm(cq, rms_cq[l]) @ w_uq[l]).reshape(bsz, seq, B_HEADS, QK_NOPE + QK_ROPE)
        q_nope, q_rope = qb[..., :QK_NOPE], rope(qb[..., QK_NOPE:], positions)
        kv = (rms_norm(ckv, rms_ckv[l]) @ w_ukv[l]).reshape(bsz, seq, B_HEADS, QK_NOPE + V_DIM)
        k_nope, v_b = kv[..., :QK_NOPE], kv[..., QK_NOPE:]
        k_rope = rope(kr, positions)
        o_b = mla_attention(q_nope, q_rope, k_nope, k_rope, v_b).reshape(bsz, seq, B_WIDTH)
        g = jax.nn.sigmoid(gates + b_gate[l]).reshape(bsz, seq, 2, D_MODEL)
        mix = g[:, :, 0] * (o_a @ w_o_a[l]) + g[:, :, 1] * (o_b @ w_o_b[l])
        h = layer_norm(DN_ALPHA * x + mix @ w_out[l], ln1_g[l], ln1_b[l])
        ffn = routed_experts(h, w_router[l], b_router[l], w_up[l], b_up[l], w_down[l], b_down[l])
        x = layer_norm(DN_ALPHA * h + ffn, ln2_g[l], ln2_b[l])
    return x
```

```python
import functools

import jax
import jax.numpy as jnp
import numpy as np
from jax import lax
from jax.experimental import pallas as pl
from jax.experimental.pallas import tpu as pltpu

D_MODEL = 1024
A_HEADS = 8
A_HEAD_DIM = 64
IDX_HEADS = 8
IDX_DIM = 64
TOPK_MAX = 256
B_HEADS = 8
Q_LORA = 384
KV_LORA = 256
QK_NOPE = 64
QK_ROPE = 32
V_DIM = 64
ROPE_THETA = 10000.0
N_EXPERTS = 32
TOP_K = 4
D_FF = 1024
SWIGLU_LIMIT = 7.0
SWIGLU_ALPHA = 1.702
DEPTH = 1
DN_ALPHA = (2 * DEPTH) ** 0.25
LN_EPS = 1e-5
RMS_EPS = 1e-6
A_WIDTH = A_HEADS * A_HEAD_DIM
B_WIDTH = B_HEADS * V_DIM
SPLITS = (A_WIDTH, A_WIDTH, A_WIDTH, IDX_HEADS * IDX_DIM, IDX_DIM, IDX_HEADS,
          Q_LORA, KV_LORA, QK_ROPE, 2 * D_MODEL)

LANES = 128
VMEM_LIMIT_BYTES = 52 * 1024 * 1024

PROJ_TILE = 512
Q_TILE = 256
EXPERT_TILE = 256
MOVE_TILE = 256

INT_MIN = -(2 ** 31)
NEG_BIG = -1e30
HALF = D_MODEL // 2

_NT = (((1,), (1,)), ((), ()))


def _cparams(sem):
    return pltpu.CompilerParams(dimension_semantics=sem, vmem_limit_bytes=VMEM_LIMIT_BYTES)


def _full(shape):
    nd = len(shape)
    return pl.BlockSpec(shape, lambda *_: (0,) * nd)


def _rope_chunk(x, cos, sin, half):
    lane = lax.broadcasted_iota(jnp.int32, x.shape, 1)
    first = (lane % (2 * half)) < half
    up = pltpu.roll(x, LANES - half, 1)
    dn = pltpu.roll(x, half, 1)
    return x * cos + jnp.where(first, -up, dn) * sin


def _layer_norm(u, g, b):
    mu = jnp.mean(u, axis=-1, keepdims=True)
    d = u - mu
    var = jnp.mean(d * d, axis=-1, keepdims=True)
    return d * lax.rsqrt(var + LN_EPS) * g + b


def _pack_bf16_pairs(v):
    w = v.shape[1] // 2
    vb = v.astype(jnp.bfloat16).astype(jnp.float32)
    lo = lax.bitcast_convert_type(vb[:, :w], jnp.uint32) >> 16
    hi = lax.bitcast_convert_type(vb[:, w:], jnp.uint32) & jnp.uint32(0xFFFF0000)
    return lo | hi


def _unpack_bf16_pairs(p):
    lo = lax.bitcast_convert_type(p << 16, jnp.float32)
    hi = lax.bitcast_convert_type(p & jnp.uint32(0xFFFF0000), jnp.float32)
    return lo, hi


def _head_mask(shape, width, which):
    lane = lax.broadcasted_iota(jnp.int32, shape, 1)
    return (lane // width) == which


def _proj_kernel(x_ref, c64_ref, s64_ref, c32_ref, s32_ref, w1_ref, ws_ref, wl_ref,
                 gcq_ref, gckv_ref, wuq_ref, wukv_ref,
                 qa_ref, ka_ref, va_ref, qi_ref, ki_ref, kr_ref, wi_ref,
                 qn_ref, qr_ref, kn_ref, vb_ref):
    xb = x_ref[...].astype(jnp.bfloat16)
    c64, s64 = c64_ref[...], s64_ref[...]
    c32, s32 = c32_ref[...], s32_ref[...]
    f32 = jnp.float32
    bf = jnp.bfloat16

    z1 = jnp.dot(xb, w1_ref[...], preferred_element_type=f32)
    n_chunk = A_WIDTH // LANES
    a_scale = A_HEAD_DIM ** -0.5
    for c in range(n_chunk):
        sl = slice(c * LANES, (c + 1) * LANES)
        qa_ref[:, sl] = (_rope_chunk(z1[:, c * LANES:(c + 1) * LANES], c64, s64, 32) * a_scale).astype(bf)
        o = A_WIDTH + c * LANES
        ka_ref[:, sl] = _rope_chunk(z1[:, o:o + LANES], c64, s64, 32).astype(bf)
        o = 2 * A_WIDTH + c * LANES
        va_ref[:, sl] = z1[:, o:o + LANES].astype(bf)
        o = 3 * A_WIDTH + c * LANES
        qi_ref[:, sl] = _rope_chunk(z1[:, o:o + LANES], c64, s64, 32).astype(bf)

    zs = jnp.dot(xb, ws_ref[...], preferred_element_type=f32)
    ki_ref[...] = _rope_chunk(zs[:, 0:LANES], c64, s64, 32).astype(bf)
    kr_ref[...] = _rope_chunk(zs[:, LANES:2 * LANES], c32, s32, 16).astype(bf)
    wi_ref[...] = zs[:, 2 * LANES:3 * LANES] * (IDX_HEADS ** -0.5 * IDX_DIM ** -0.5)

    zl = jnp.dot(xb, wl_ref[...], preferred_element_type=f32)
    cq = zl[:, :Q_LORA]
    cq = cq * lax.rsqrt(jnp.mean(cq * cq, axis=-1, keepdims=True) + RMS_EPS) * gcq_ref[...]
    ckv = zl[:, Q_LORA:]
    ckv = ckv * lax.rsqrt(jnp.mean(ckv * ckv, axis=-1, keepdims=True) + RMS_EPS) * gckv_ref[...]
    qb = jnp.dot(cq.astype(bf), wuq_ref[...], preferred_element_type=f32)
    kv = jnp.dot(ckv.astype(bf), wukv_ref[...], preferred_element_type=f32)
    b_scale = (QK_NOPE + QK_ROPE) ** -0.5
    qn_ref[...] = (qb[:, :B_HEADS * QK_NOPE] * b_scale).astype(bf)
    for c in range(B_HEADS * QK_ROPE // LANES):
        o = B_HEADS * QK_NOPE + c * LANES
        qr_ref[:, c * LANES:(c + 1) * LANES] = (
            _rope_chunk(qb[:, o:o + LANES], c32, s32, 16) * b_scale).astype(bf)
    kn_ref[...] = kv[:, :B_HEADS * QK_NOPE].astype(bf)
    vb_ref[...] = kv[:, B_HEADS * QK_NOPE:].astype(bf)


def _proj(x2, tabs, w1, ws, wl, gcq, gckv, wuq, wukv):
    n = x2.shape[0]
    t = PROJ_TILE
    tok = lambda w: pl.BlockSpec((t, w), lambda i: (i, 0))
    bf = jnp.bfloat16
    outs = [(A_WIDTH, bf)] * 4 + [(LANES, bf), (LANES, bf), (LANES, jnp.float32),
                                  (B_HEADS * QK_NOPE, bf), (B_HEADS * QK_ROPE, bf),
                                  (B_HEADS * QK_NOPE, bf), (B_WIDTH, bf)]
    return pl.pallas_call(
        _proj_kernel,
        grid=(n // t,),
        in_specs=[tok(D_MODEL)] + [tok(LANES)] * 4 + [_full(a.shape) for a in (w1, ws, wl, gcq, gckv, wuq, wukv)],
        out_specs=[tok(w) for w, _ in outs],
        out_shape=[jax.ShapeDtypeStruct((n, w), d) for w, d in outs],
        compiler_params=_cparams(("arbitrary",)),
        name="proj",
    )(x2, *tabs, w1, ws, wl, gcq, gckv, wuq, wukv)


def _tie_select(keys, thr, gt, need, seq):
    eq = keys == thr
    chunk = 256
    r = lax.broadcasted_iota(jnp.int32, (chunk, chunk), 0)
    c = lax.broadcasted_iota(jnp.int32, (chunk, chunk), 1)
    upper = jnp.where(r <= c, 1.0, 0.0).astype(jnp.bfloat16)
    carry = jnp.zeros((keys.shape[0], 1), jnp.float32)
    needf = need
    parts = []
    for j in range(seq // chunk):
        e = eq[:, j * chunk:(j + 1) * chunk]
        ef = jnp.where(e, 1.0, 0.0).astype(jnp.bfloat16)
        pref = jnp.dot(ef, upper, preferred_element_type=jnp.float32) + carry
        parts.append(gt[:, j * chunk:(j + 1) * chunk] | (e & (pref <= needf)))
        carry = carry + jnp.sum(ef.astype(jnp.float32), axis=1, keepdims=True)
    return jnp.concatenate(parts, axis=1)


def _dsa_kernel(qa_ref, qi_ref, wi_ref, ka_ref, va_ref, ki_ref, o_ref, sel_ref, *, seq, k_sel):
    tq = qa_ref.shape[0]
    f32 = jnp.float32
    bf = jnp.bfloat16
    q0 = pl.program_id(1) * tq
    q_pos = q0 + lax.broadcasted_iota(jnp.int32, (tq, 1), 0)
    key_pos = lax.broadcasted_iota(jnp.int32, (tq, seq), 1)
    causal = key_pos <= q_pos

    ki = ki_ref[...]
    wi = wi_ref[...]
    score = jnp.zeros((tq, seq), f32)
    for h in range(IDX_HEADS):
        c = h // 2
        qc = qi_ref[:, c * LANES:(c + 1) * LANES]
        qm = jnp.where(_head_mask(qc.shape, IDX_DIM, h % 2), qc, jnp.zeros_like(qc))
        s = lax.dot_general(qm, ki, _NT, preferred_element_type=f32)
        score = score + jnp.maximum(s, 0.0) * wi[:, h:h + 1]
    score = score + 0.0
    bits = lax.bitcast_convert_type(score, jnp.int32)
    keys = bits ^ ((bits >> 31) & jnp.int32(0x7FFFFFFF))
    keys = jnp.where(causal, keys, jnp.int32(INT_MIN))

    k_eff = jnp.minimum(q_pos + 1, k_sel).astype(f32)

    def search(it, prefix):
        cand = prefix + lax.shift_left(jnp.int32(1), 31 - it)
        cnt = jnp.sum(jnp.where(keys >= cand, 1.0, 0.0), axis=1, keepdims=True)
        return jnp.where(cnt >= k_eff, cand, prefix)

    thr = lax.fori_loop(0, 32, search, jnp.full((tq, 1), INT_MIN, jnp.int32))
    gt = keys > thr
    n_gt = jnp.sum(jnp.where(gt, 1.0, 0.0), axis=1, keepdims=True)
    n_eq = jnp.sum(jnp.where(keys == thr, 1.0, 0.0), axis=1, keepdims=True)
    need = k_eff - n_gt
    sel_ref[...] = jnp.where(keys >= thr, 1.0, 0.0)
    has_tie = jnp.max(jnp.where(n_eq != need, 1.0, 0.0)) > 0.5

    @pl.when(has_tie)
    def _():
        sel_ref[...] = jnp.where(_tie_select(keys, thr, gt, need, seq), 1.0, 0.0)

    sel = sel_ref[...] > 0.5

    for c in range(A_WIDTH // LANES):
        sl = slice(c * LANES, (c + 1) * LANES)
        qc = qa_ref[:, sl]
        kc = ka_ref[:, sl]
        vc = va_ref[:, sl]
        outs = []
        for hh in range(2):
            qm = jnp.where(_head_mask(qc.shape, A_HEAD_DIM, hh), qc, jnp.zeros_like(qc))
            lg = lax.dot_general(qm, kc, _NT, preferred_element_type=f32)
            lg = jnp.where(sel, lg, NEG_BIG)
            m = jnp.max(lg, axis=1, keepdims=True)
            p = jnp.exp(lg - m)
            den = jnp.sum(p, axis=1, keepdims=True)
            pv = jnp.dot(p.astype(bf), vc, preferred_element_type=f32)
            outs.append(pv / den)
        o_ref[:, sl] = jnp.where(_head_mask(outs[0].shape, A_HEAD_DIM, 0), outs[0], outs[1]).astype(bf)


def _dsa(qa, qi, wi, ka, va, ki, bsz, seq):
    tq = min(Q_TILE, seq)
    nq = seq // tq
    qspec = lambda w: pl.BlockSpec((tq, w), lambda b, i: (b * nq + i, 0))
    kspec = lambda w: pl.BlockSpec((seq, w), lambda b, i: (b, 0))
    k_sel = min(TOPK_MAX, seq // 4)
    return pl.pallas_call(
        functools.partial(_dsa_kernel, seq=seq, k_sel=k_sel),
        grid=(bsz, nq),
        in_specs=[qspec(A_WIDTH), qspec(A_WIDTH), qspec(LANES), kspec(A_WIDTH), kspec(A_WIDTH), kspec(LANES)],
        out_specs=qspec(A_WIDTH),
        out_shape=jax.ShapeDtypeStruct((bsz * seq, A_WIDTH), jnp.bfloat16),
        scratch_shapes=[pltpu.VMEM((tq, seq), jnp.float32)],
        compiler_params=_cparams(("arbitrary", "arbitrary")),
        name="dsa",
    )(qa, qi, wi, ka, va, ki)


def _mla_kernel(qn_ref, qr_ref, kn_ref, kr_ref, vb_ref, o_ref, *, seq):
    tq = qn_ref.shape[0]
    f32 = jnp.float32
    bf = jnp.bfloat16
    q0 = pl.program_id(1) * tq
    q_pos = q0 + lax.broadcasted_iota(jnp.int32, (tq, 1), 0)
    key_pos = lax.broadcasted_iota(jnp.int32, (tq, seq), 1)
    causal = key_pos <= q_pos
    kr = kr_ref[...]
    for c in range(B_WIDTH // LANES):
        sl = slice(c * LANES, (c + 1) * LANES)
        qc = qn_ref[:, sl]
        kcat = jnp.concatenate([kn_ref[:, sl], kr], axis=1)
        vc = vb_ref[:, sl]
        rsl = slice((c // 2) * LANES, (c // 2 + 1) * LANES)
        qrc = qr_ref[:, rsl]
        outs = []
        for hh in range(2):
            h = 2 * c + hh
            qm = jnp.where(_head_mask(qc.shape, QK_NOPE, hh), qc, jnp.zeros_like(qc))
            qrm = jnp.where(_head_mask(qrc.shape, QK_ROPE, h % 4), qrc, jnp.zeros_like(qrc))
            qcat = jnp.concatenate([qm, qrm], axis=1)
            lg = lax.dot_general(qcat, kcat, _NT, preferred_element_type=f32)
            lg = jnp.where(causal, lg, NEG_BIG)
            m = jnp.max(lg, axis=1, keepdims=True)
            p = jnp.exp(lg - m)
            den = jnp.sum(p, axis=1, keepdims=True)
            pv = jnp.dot(p.astype(bf), vc, preferred_element_type=f32)
            outs.append(pv / den)
        o_ref[:, sl] = jnp.where(_head_mask(outs[0].shape, V_DIM, 0), outs[0], outs[1]).astype(bf)


def _mla(qn, qr, kn, kr, vb, bsz, seq):
    tq = min(Q_TILE, seq)
    nq = seq // tq
    qspec = lambda w: pl.BlockSpec((tq, w), lambda b, i: (b * nq + i, 0))
    kspec = lambda w: pl.BlockSpec((seq, w), lambda b, i: (b, 0))
    return pl.pallas_call(
        functools.partial(_mla_kernel, seq=seq),
        grid=(bsz, nq),
        in_specs=[qspec(B_HEADS * QK_NOPE), qspec(B_HEADS * QK_ROPE), kspec(B_HEADS * QK_NOPE),
                  kspec(LANES), kspec(B_WIDTH)],
        out_specs=qspec(B_WIDTH),
        out_shape=jax.ShapeDtypeStruct((bsz * seq, B_WIDTH), jnp.bfloat16),
        compiler_params=_cparams(("arbitrary", "arbitrary")),
        name="mla",
    )(qn, qr, kn, kr, vb)


def _post_kernel(x_ref, oa_ref, ob_ref, wg_ref, bg_ref, woa_ref, wob_ref, wout_ref, g1_ref, b1_ref,
                 wr_ref, br_ref, h_ref, hp_ref, ri_ref, rf_ref, cnt_ref, carry_ref):
    f32 = jnp.float32
    bf = jnp.bfloat16
    t = x_ref.shape[0]

    @pl.when(pl.program_id(0) == 0)
    def _():
        carry_ref[...] = jnp.zeros_like(carry_ref)

    x = x_ref[...]
    gate = jax.nn.sigmoid(jnp.dot(x.astype(bf), wg_ref[...], preferred_element_type=f32) + bg_ref[...])
    pa = jnp.dot(oa_ref[...], woa_ref[...], preferred_element_type=f32)
    pb = jnp.dot(ob_ref[...], wob_ref[...], preferred_element_type=f32)
    mix = gate[:, :D_MODEL] * pa + gate[:, D_MODEL:] * pb
    u = DN_ALPHA * x + jnp.dot(mix.astype(bf), wout_ref[...], preferred_element_type=f32)
    h = _layer_norm(u, g1_ref[...], b1_ref[...])
    h_ref[...] = h
    hp_ref[...] = _pack_bf16_pairs(h)

    logits = jnp.dot(h, wr_ref[...], preferred_element_type=f32,
                     precision=lax.Precision.HIGHEST) + br_ref[...]
    lane = lax.broadcasted_iota(jnp.int32, (t, LANES), 1).astype(f32)
    lg = jnp.where(lane < N_EXPERTS, logits, -jnp.inf)
    vals, ids = [], []
    assign = jnp.zeros((t, LANES), f32)
    for _k in range(TOP_K):
        m = jnp.max(lg, axis=1, keepdims=True)
        idx = jnp.min(jnp.where(lg == m, lane, float(LANES)), axis=1, keepdims=True)
        hit = lane == idx
        vals.append(m)
        ids.append(idx)
        assign = jnp.where(hit, 1.0, assign)
        lg = jnp.where(hit, -jnp.inf, lg)
    exps = [jnp.exp(v - vals[0]) for v in vals]
    den = exps[0] + exps[1] + exps[2] + exps[3]

    r = lax.broadcasted_iota(jnp.int32, (t, t), 0)
    c = lax.broadcasted_iota(jnp.int32, (t, t), 1)
    lower = jnp.where(c < r, 1.0, 0.0).astype(bf)
    before = jnp.dot(lower, assign.astype(bf), preferred_element_type=f32) + carry_ref[0:1, :]
    ri = jnp.zeros((t, LANES), f32)
    rf = jnp.zeros((t, LANES), f32)
    for k in range(TOP_K):
        rank = jnp.sum(jnp.where(lane == ids[k], before, 0.0), axis=1, keepdims=True)
        ri = jnp.where(lane == float(k), ids[k], ri)
        ri = jnp.where(lane == float(TOP_K + k), rank, ri)
        rf = jnp.where(lane == float(k), exps[k] / den, rf)
    ri_ref[...] = ri.astype(jnp.int32)
    rf_ref[...] = rf
    total = carry_ref[0:1, :] + jnp.sum(assign, axis=0, keepdims=True)
    carry_ref[...] = jnp.broadcast_to(total, carry_ref.shape)
    cnt_ref[...] = jnp.broadcast_to(total, cnt_ref.shape)


def _post(x2, oa, ob, wg, bg, woa, wob, wout, g1, b1, wr, br):
    n = x2.shape[0]
    t = PROJ_TILE
    tok = lambda w: pl.BlockSpec((t, w), lambda i: (i, 0))
    consts = (wg, bg, woa, wob, wout, g1, b1, wr, br)
    return pl.pallas_call(
        _post_kernel,
        grid=(n // t,),
        in_specs=[tok(D_MODEL), tok(A_WIDTH), tok(B_WIDTH)] + [_full(a.shape) for a in consts],
        out_specs=[tok(D_MODEL), tok(HALF), tok(LANES), tok(LANES), _full((8, LANES))],
        out_shape=[jax.ShapeDtypeStruct((n, D_MODEL), jnp.float32),
                   jax.ShapeDtypeStruct((n, HALF), jnp.uint32),
                   jax.ShapeDtypeStruct((n, LANES), jnp.int32),
                   jax.ShapeDtypeStruct((n, LANES), jnp.float32),
                   jax.ShapeDtypeStruct((8, LANES), jnp.float32)],
        scratch_shapes=[pltpu.VMEM((8, LANES), jnp.float32)],
        compiler_params=_cparams(("arbitrary",)),
        name="post",
    )(x2, oa, ob, *consts)


def _dispatch_kernel(dest_ref, hp_ref, init_ref, xs_ref, sem):
    del init_ref
    t = MOVE_TILE
    base = pl.program_id(0) * t

    def issue(j, carry):
        for k in range(TOP_K):
            d = dest_ref[0, 0, j * TOP_K + k]
            pltpu.make_async_copy(hp_ref.at[pl.ds(base + j, 1)], xs_ref.at[pl.ds(d, 1)], sem).start()
        return carry

    lax.fori_loop(0, t, issue, 0)

    def drain(j, carry):
        pltpu.make_async_copy(hp_ref.at[pl.ds(0, 1)], xs_ref.at[pl.ds(0, 1)], sem).wait()
        return carry

    lax.fori_loop(0, t * TOP_K, drain, 0)


def _dispatch(dest3, hp, rows):
    n = hp.shape[0]
    t = MOVE_TILE
    return pl.pallas_call(
        _dispatch_kernel,
        grid=(n // t,),
        in_specs=[pl.BlockSpec((1, 1, t * TOP_K), lambda i: (i, 0, 0), memory_space=pltpu.SMEM),
                  pl.BlockSpec(memory_space=pl.ANY), pl.BlockSpec(memory_space=pl.ANY)],
        out_specs=pl.BlockSpec(memory_space=pl.ANY),
        out_shape=jax.ShapeDtypeStruct((rows, HALF), jnp.uint32),
        input_output_aliases={2: 0},
        scratch_shapes=[pltpu.SemaphoreType.DMA(())],
        compiler_params=_cparams(("arbitrary",)),
        name="dispatch",
    )(dest3, hp, jnp.zeros((rows, HALF), jnp.uint32))


def _expert_kernel(te_ref, nt_ref, xs_ref, wu_ref, bu_ref, wd_ref, bd_ref, ys_ref, wub_ref, wdb_ref):
    i = pl.program_id(0)
    f32 = jnp.float32
    bf = jnp.bfloat16
    valid = i < nt_ref[0]
    prev = te_ref[jnp.maximum(i - 1, 0)]
    fresh = valid & ((i == 0) | (te_ref[i] != prev))

    @pl.when(fresh)
    def _():
        rows = 128

        def cast_up(j, c):
            r = pl.multiple_of(j * rows, rows)
            wub_ref[pl.ds(r, rows), :] = wu_ref[pl.ds(r, rows), :].astype(bf)
            return c

        lax.fori_loop(0, D_MODEL // rows, cast_up, 0)

        def cast_dn(j, c):
            r = pl.multiple_of(j * rows, rows)
            wdb_ref[pl.ds(r, rows), :] = wd_ref[pl.ds(r, rows), :].astype(bf)
            return c

        lax.fori_loop(0, D_FF // rows, cast_dn, 0)

    @pl.when(valid)
    def _():
        lo, hi = _unpack_bf16_pairs(xs_ref[...])
        a = (jnp.dot(lo.astype(bf), wub_ref[:HALF, :], preferred_element_type=f32)
             + jnp.dot(hi.astype(bf), wub_ref[HALF:, :], preferred_element_type=f32) + bu_ref[...])
        g = jnp.minimum(a[:, :D_FF], SWIGLU_LIMIT)
        lin = jnp.clip(a[:, D_FF:], -SWIGLU_LIMIT, SWIGLU_LIMIT)
        act = (lin + 1.0) * (g * jax.nn.sigmoid(SWIGLU_ALPHA * g))
        y = jnp.dot(act.astype(bf), wdb_ref[...], preferred_element_type=f32) + bd_ref[...]
        ys_ref[...] = _pack_bf16_pairs(y)

    @pl.when(jnp.logical_not(valid))
    def _():
        ys_ref[...] = jnp.zeros_like(ys_ref)


def _experts(tile_expert, n_tiles, xs, w_up, b_up, w_down, b_down):
    rows = xs.shape[0]
    t = EXPERT_TILE
    max_tiles = rows // t
    row_map = lambda i, te, nt: (jnp.minimum(i, nt[0] - 1), 0)
    grid_spec = pltpu.PrefetchScalarGridSpec(
        num_scalar_prefetch=2,
        grid=(max_tiles,),
        in_specs=[pl.BlockSpec((t, HALF), row_map),
                  pl.BlockSpec((None, D_MODEL, 2 * D_FF), lambda i, te, nt: (te[i], 0, 0)),
                  pl.BlockSpec((None, 1, 2 * D_FF), lambda i, te, nt: (te[i], 0, 0)),
                  pl.BlockSpec((None, D_FF, D_MODEL), lambda i, te, nt: (te[i], 0, 0)),
                  pl.BlockSpec((None, 1, D_MODEL), lambda i, te, nt: (te[i], 0, 0))],
        out_specs=pl.BlockSpec((t, HALF), lambda i, te, nt: (i, 0)),
        scratch_shapes=[pltpu.VMEM((D_MODEL, 2 * D_FF), jnp.bfloat16),
                        pltpu.VMEM((D_FF, D_MODEL), jnp.bfloat16)],
    )
    return pl.pallas_call(
        _expert_kernel,
        grid_spec=grid_spec,
        out_shape=jax.ShapeDtypeStruct((rows, HALF), jnp.uint32),
        compiler_params=_cparams(("arbitrary",)),
        name="experts",
    )(tile_expert, n_tiles, xs, w_up, b_up.reshape(N_EXPERTS, 1, 2 * D_FF), w_down,
      b_down.reshape(N_EXPERTS, 1, D_MODEL))


def _combine_kernel(dest_ref, ys_ref, h_ref, rf_ref, g2_ref, b2_ref, o_ref, buf_ref, sem):
    t = MOVE_TILE

    def issue(j, carry):
        for k in range(TOP_K):
            d = dest_ref[0, 0, j * TOP_K + k]
            pltpu.make_async_copy(ys_ref.at[pl.ds(d, 1)], buf_ref.at[k, pl.ds(j, 1)], sem).start()
        return carry

    lax.fori_loop(0, t, issue, 0)

    def drain(j, carry):
        pltpu.make_async_copy(ys_ref.at[pl.ds(0, 1)], buf_ref.at[0, pl.ds(0, 1)], sem).wait()
        return carry

    lax.fori_loop(0, t * TOP_K, drain, 0)

    h = h_ref[...]
    rf = rf_ref[...]
    acc_lo = DN_ALPHA * h[:, :HALF]
    acc_hi = DN_ALPHA * h[:, HALF:]
    for k in range(TOP_K):
        lo, hi = _unpack_bf16_pairs(buf_ref[k])
        gk = rf[:, k:k + 1]
        acc_lo = acc_lo + gk * lo
        acc_hi = acc_hi + gk * hi
    u = jnp.concatenate([acc_lo, acc_hi], axis=1)
    o_ref[...] = _layer_norm(u, g2_ref[...], b2_ref[...])


def _combine(dest3, ys, h, rf, g2, b2):
    n = h.shape[0]
    t = MOVE_TILE
    tok = lambda w: pl.BlockSpec((t, w), lambda i: (i, 0))
    return pl.pallas_call(
        _combine_kernel,
        grid=(n // t,),
        in_specs=[pl.BlockSpec((1, 1, t * TOP_K), lambda i: (i, 0, 0), memory_space=pltpu.SMEM),
                  pl.BlockSpec(memory_space=pl.ANY), tok(D_MODEL), tok(LANES),
                  _full(g2.shape), _full(b2.shape)],
        out_specs=tok(D_MODEL),
        out_shape=jax.ShapeDtypeStruct((n, D_MODEL), jnp.float32),
        scratch_shapes=[pltpu.VMEM((TOP_K, t, HALF), jnp.uint32), pltpu.SemaphoreType.DMA(())],
        compiler_params=_cparams(("arbitrary",)),
        name="combine",
    )(dest3, ys, h, rf, g2, b2)


def _rope_tables(positions, half):
    inv_freq = ROPE_THETA ** (-jnp.arange(half, dtype=jnp.float32) / half)
    ang = positions.astype(jnp.float32).reshape(-1, 1) * inv_freq
    reps = LANES // half
    return jnp.tile(jnp.cos(ang), (1, reps)), jnp.tile(jnp.sin(ang), (1, reps))


def _layer(x2, tabs, bsz, seq, w_in, b_gate, rms_cq, rms_ckv, w_uq, w_ukv, w_o_a, w_o_b, w_out,
           ln1_g, ln1_b, w_router, b_router, w_up, b_up, w_down, b_down, ln2_g, ln2_b):
    bf = jnp.bfloat16
    n = x2.shape[0]
    off = np.concatenate([[0], np.cumsum(SPLITS)]).tolist()
    col = lambda j: w_in[:, off[j]:off[j + 1]]
    w1 = jnp.concatenate([col(0), col(1), col(2), col(3)], axis=1).astype(bf)
    wi_pad = jnp.zeros((D_MODEL, LANES - IDX_HEADS), w_in.dtype)
    ws = jnp.concatenate([col(4), col(4), jnp.tile(col(8), (1, LANES // QK_ROPE)), col(5), wi_pad], axis=1).astype(bf)
    wl = jnp.concatenate([col(6), col(7)], axis=1).astype(bf)
    wg = col(9).astype(bf)
    uq = w_uq.reshape(Q_LORA, B_HEADS, QK_NOPE + QK_ROPE)
    wuq = jnp.concatenate([uq[:, :, :QK_NOPE].reshape(Q_LORA, -1), uq[:, :, QK_NOPE:].reshape(Q_LORA, -1)],
                          axis=1).astype(bf)
    ukv = w_ukv.reshape(KV_LORA, B_HEADS, QK_NOPE + V_DIM)
    wukv = jnp.concatenate([ukv[:, :, :QK_NOPE].reshape(KV_LORA, -1), ukv[:, :, QK_NOPE:].reshape(KV_LORA, -1)],
                           axis=1).astype(bf)

    qa, ka, va, qi, ki, kr, wi, qn, qr, kn, vb = _proj(
        x2, tabs, w1, ws, wl, rms_cq.reshape(1, -1), rms_ckv.reshape(1, -1), wuq, wukv)
    o_a = _dsa(qa, qi, wi, ka, va, ki, bsz, seq)
    o_b = _mla(qn, qr, kn, kr, vb, bsz, seq)

    wr = jnp.concatenate([w_router, jnp.zeros((D_MODEL, LANES - N_EXPERTS), w_router.dtype)], axis=1)
    br = jnp.concatenate([b_router, jnp.zeros((LANES - N_EXPERTS,), b_router.dtype)]).reshape(1, -1)
    h, hp, ri, rf, cnt = _post(x2, o_a, o_b, wg, b_gate.reshape(1, -1), w_o_a.astype(bf), w_o_b.astype(bf),
                               w_out.astype(bf), ln1_g.reshape(1, -1), ln1_b.reshape(1, -1), wr, br)

    te_rows = EXPERT_TILE
    counts = cnt[0, :N_EXPERTS].astype(jnp.int32)
    padded = ((counts + te_rows - 1) // te_rows) * te_rows
    ends = jnp.cumsum(padded)
    starts = ends - padded
    dest = starts[ri[:, :TOP_K]] + ri[:, TOP_K:2 * TOP_K]
    dest3 = dest.reshape(n // MOVE_TILE, 1, MOVE_TILE * TOP_K)
    max_tiles = n * TOP_K // te_rows + N_EXPERTS
    n_tiles = (ends[-1] // te_rows).astype(jnp.int32)
    tile_start = jnp.arange(max_tiles, dtype=jnp.int32) * te_rows
    tile_start = jnp.minimum(tile_start, ends[-1] - te_rows)
    tile_expert = jnp.searchsorted(ends, tile_start, side="right").astype(jnp.int32)

    xs = _dispatch(dest3, hp, max_tiles * te_rows)
    ys = _experts(tile_expert, n_tiles.reshape(1), xs, w_up, b_up, w_down, b_down)
    return _combine(dest3, ys, h, rf, ln2_g.reshape(1, -1), ln2_b.reshape(1, -1))


def kernel(x, positions, w_in, b_gate, rms_cq, rms_ckv, w_uq, w_ukv, w_o_a, w_o_b, w_out, ln1_g, ln1_b,
           w_router, b_router, w_up, b_up, w_down, b_down, ln2_g, ln2_b):
    bsz, seq, _ = x.shape
    x2 = x.reshape(bsz * seq, D_MODEL)
    c64, s64 = _rope_tables(positions, A_HEAD_DIM // 2)
    c32, s32 = _rope_tables(positions, QK_ROPE // 2)
    tabs = (c64, s64, c32, s32)
    for l in range(DEPTH):
        x2 = _layer(x2, tabs, bsz, seq, w_in[l], b_gate[l], rms_cq[l], rms_ckv[l], w_uq[l], w_ukv[l],
                    w_o_a[l], w_o_b[l], w_out[l], ln1_g[l], ln1_b[l], w_router[l], b_router[l],
                    w_up[l], b_up[l], w_down[l], b_down[l], ln2_g[l], ln2_b[l])
    return x2.reshape(bsz, seq, D_MODEL)
```

```python
import functools

import jax
import jax.numpy as jnp
import numpy as np
from jax import lax
from jax.experimental import pallas as pl
from jax.experimental.pallas import tpu as pltpu

D_MODEL = 1024
A_HEADS = 8
A_HEAD_DIM = 64
IDX_HEADS = 8
IDX_DIM = 64
TOPK_MAX = 256
B_HEADS = 8
Q_LORA = 384
KV_LORA = 256
QK_NOPE = 64
QK_ROPE = 32
V_DIM = 64
ROPE_THETA = 10000.0
N_EXPERTS = 32
TOP_K = 4
D_FF = 1024
SWIGLU_LIMIT = 7.0
SWIGLU_ALPHA = 1.702
DEPTH = 1
DN_ALPHA = (2 * DEPTH) ** 0.25
LN_EPS = 1e-5
RMS_EPS = 1e-6
A_WIDTH = A_HEADS * A_HEAD_DIM
B_WIDTH = B_HEADS * V_DIM
SPLITS = (A_WIDTH, A_WIDTH, A_WIDTH, IDX_HEADS * IDX_DIM, IDX_DIM, IDX_HEADS,
          Q_LORA, KV_LORA, QK_ROPE, 2 * D_MODEL)

LANES = 128
VMEM_LIMIT_BYTES = 52 * 1024 * 1024

PROJ_TILE = 512
Q_TILE = 256
EXPERT_TILE = 256
FF_CHUNK = 256
MOVE_TILE = 256

INT_MIN = -(2 ** 31)
NEG_BIG = -1e30
HALF = D_MODEL // 2

_NT = (((1,), (1,)), ((), ()))


def _cparams(sem):
    return pltpu.CompilerParams(dimension_semantics=sem, vmem_limit_bytes=VMEM_LIMIT_BYTES)


def _full(shape):
    nd = len(shape)
    return pl.BlockSpec(shape, lambda *_: (0,) * nd)


def _rope_chunk(x, cos, sin, half):
    lane = lax.broadcasted_iota(jnp.int32, x.shape, 1)
    first = (lane % (2 * half)) < half
    up = pltpu.roll(x, LANES - half, 1)
    dn = pltpu.roll(x, half, 1)
    return x * cos + jnp.where(first, -up, dn) * sin


def _layer_norm(u, g, b):
    mu = jnp.mean(u, axis=-1, keepdims=True)
    d = u - mu
    var = jnp.mean(d * d, axis=-1, keepdims=True)
    return d * lax.rsqrt(var + LN_EPS) * g + b


def _pack_bf16_pairs(v):
    w = v.shape[1] // 2
    vb = v.astype(jnp.bfloat16).astype(jnp.float32)
    lo = lax.bitcast_convert_type(vb[:, :w], jnp.uint32) >> 16
    hi = lax.bitcast_convert_type(vb[:, w:], jnp.uint32) & jnp.uint32(0xFFFF0000)
    return lo | hi


def _unpack_bf16_pairs(p):
    lo = lax.bitcast_convert_type(p << 16, jnp.float32)
    hi = lax.bitcast_convert_type(p & jnp.uint32(0xFFFF0000), jnp.float32)
    return lo, hi


def _head_mask(shape, width, which):
    lane = lax.broadcasted_iota(jnp.int32, shape, 1)
    return (lane // width) == which


def _proj_kernel(x_ref, c64_ref, s64_ref, c32_ref, s32_ref, w1_ref, ws_ref, wl_ref,
                 gcq_ref, gckv_ref, wuq_ref, wukv_ref,
                 qa_ref, ka_ref, va_ref, qi_ref, ki_ref, kr_ref, wi_ref,
                 qn_ref, qr_ref, kn_ref, vb_ref):
    xb = x_ref[...].astype(jnp.bfloat16)
    c64, s64 = c64_ref[...], s64_ref[...]
    c32, s32 = c32_ref[...], s32_ref[...]
    f32 = jnp.float32
    bf = jnp.bfloat16

    z1 = jnp.dot(xb, w1_ref[...], preferred_element_type=f32)
    n_chunk = A_WIDTH // LANES
    a_scale = A_HEAD_DIM ** -0.5
    for c in range(n_chunk):
        sl = slice(c * LANES, (c + 1) * LANES)
        qa_ref[:, sl] = (_rope_chunk(z1[:, c * LANES:(c + 1) * LANES], c64, s64, 32) * a_scale).astype(bf)
        o = A_WIDTH + c * LANES
        ka_ref[:, sl] = _rope_chunk(z1[:, o:o + LANES], c64, s64, 32).astype(bf)
        o = 2 * A_WIDTH + c * LANES
        va_ref[:, sl] = z1[:, o:o + LANES].astype(bf)
        o = 3 * A_WIDTH + c * LANES
        qi_ref[:, sl] = _rope_chunk(z1[:, o:o + LANES], c64, s64, 32).astype(bf)

    zs = jnp.dot(xb, ws_ref[...], preferred_element_type=f32)
    ki_ref[...] = _rope_chunk(zs[:, 0:LANES], c64, s64, 32).astype(bf)
    kr_ref[...] = _rope_chunk(zs[:, LANES:2 * LANES], c32, s32, 16).astype(bf)
    wi_ref[...] = zs[:, 2 * LANES:3 * LANES] * (IDX_HEADS ** -0.5 * IDX_DIM ** -0.5)

    zl = jnp.dot(xb, wl_ref[...], preferred_element_type=f32)
    cq = zl[:, :Q_LORA]
    cq = cq * lax.rsqrt(jnp.mean(cq * cq, axis=-1, keepdims=True) + RMS_EPS) * gcq_ref[...]
    ckv = zl[:, Q_LORA:]
    ckv = ckv * lax.rsqrt(jnp.mean(ckv * ckv, axis=-1, keepdims=True) + RMS_EPS) * gckv_ref[...]
    qb = jnp.dot(cq.astype(bf), wuq_ref[...], preferred_element_type=f32)
    kv = jnp.dot(ckv.astype(bf), wukv_ref[...], preferred_element_type=f32)
    b_scale = (QK_NOPE + QK_ROPE) ** -0.5
    qn_ref[...] = (qb[:, :B_HEADS * QK_NOPE] * b_scale).astype(bf)
    for c in range(B_HEADS * QK_ROPE // LANES):
        o = B_HEADS * QK_NOPE + c * LANES
        qr_ref[:, c * LANES:(c + 1) * LANES] = (
            _rope_chunk(qb[:, o:o + LANES], c32, s32, 16) * b_scale).astype(bf)
    kn_ref[...] = kv[:, :B_HEADS * QK_NOPE].astype(bf)
    vb_ref[...] = kv[:, B_HEADS * QK_NOPE:].astype(bf)


def _proj(x2, tabs, w1, ws, wl, gcq, gckv, wuq, wukv):
    n = x2.shape[0]
    t = PROJ_TILE
    tok = lambda w: pl.BlockSpec((t, w), lambda i: (i, 0))
    bf = jnp.bfloat16
    outs = [(A_WIDTH, bf)] * 4 + [(LANES, bf), (LANES, bf), (LANES, jnp.float32),
                                  (B_HEADS * QK_NOPE, bf), (B_HEADS * QK_ROPE, bf),
                                  (B_HEADS * QK_NOPE, bf), (B_WIDTH, bf)]
    return pl.pallas_call(
        _proj_kernel,
        grid=(n // t,),
        in_specs=[tok(D_MODEL)] + [tok(LANES)] * 4 + [_full(a.shape) for a in (w1, ws, wl, gcq, gckv, wuq, wukv)],
        out_specs=[tok(w) for w, _ in outs],
        out_shape=[jax.ShapeDtypeStruct((n, w), d) for w, d in outs],
        compiler_params=_cparams(("arbitrary",)),
        name="proj",
    )(x2, *tabs, w1, ws, wl, gcq, gckv, wuq, wukv)


def _tie_select(keys, thr, gt, need, seq):
    eq = keys == thr
    chunk = 256
    r = lax.broadcasted_iota(jnp.int32, (chunk, chunk), 0)
    c = lax.broadcasted_iota(jnp.int32, (chunk, chunk), 1)
    upper = jnp.where(r <= c, 1.0, 0.0).astype(jnp.bfloat16)
    carry = jnp.zeros((keys.shape[0], 1), jnp.float32)
    needf = need
    parts = []
    for j in range(seq // chunk):
        e = eq[:, j * chunk:(j + 1) * chunk]
        ef = jnp.where(e, 1.0, 0.0).astype(jnp.bfloat16)
        pref = jnp.dot(ef, upper, preferred_element_type=jnp.float32) + carry
        parts.append(gt[:, j * chunk:(j + 1) * chunk] | (e & (pref <= needf)))
        carry = carry + jnp.sum(ef.astype(jnp.float32), axis=1, keepdims=True)
    return jnp.concatenate(parts, axis=1)


def _dsa_kernel(qa_ref, qi_ref, wi_ref, ka_ref, va_ref, ki_ref, o_ref, sel_ref, *, seq, k_sel):
    tq = qa_ref.shape[0]
    f32 = jnp.float32
    bf = jnp.bfloat16
    q0 = pl.program_id(1) * tq
    q_pos = q0 + lax.broadcasted_iota(jnp.int32, (tq, 1), 0)
    key_pos = lax.broadcasted_iota(jnp.int32, (tq, seq), 1)
    causal = key_pos <= q_pos

    ki = ki_ref[...]
    wi = wi_ref[...]
    score = jnp.zeros((tq, seq), f32)
    for h in range(IDX_HEADS):
        c = h // 2
        qc = qi_ref[:, c * LANES:(c + 1) * LANES]
        qm = jnp.where(_head_mask(qc.shape, IDX_DIM, h % 2), qc, jnp.zeros_like(qc))
        s = lax.dot_general(qm, ki, _NT, preferred_element_type=f32)
        score = score + jnp.maximum(s, 0.0) * wi[:, h:h + 1]
    score = score + 0.0
    bits = lax.bitcast_convert_type(score, jnp.int32)
    keys = bits ^ ((bits >> 31) & jnp.int32(0x7FFFFFFF))
    keys = jnp.where(causal, keys, jnp.int32(INT_MIN))

    k_eff = jnp.minimum(q_pos + 1, k_sel).astype(f32)

    def search(it, prefix):
        cand = prefix + lax.shift_left(jnp.int32(1), 31 - it)
        cnt = jnp.sum(jnp.where(keys >= cand, 1.0, 0.0), axis=1, keepdims=True)
        return jnp.where(cnt >= k_eff, cand, prefix)

    thr = lax.fori_loop(0, 32, search, jnp.full((tq, 1), INT_MIN, jnp.int32))
    gt = keys > thr
    n_gt = jnp.sum(jnp.where(gt, 1.0, 0.0), axis=1, keepdims=True)
    n_eq = jnp.sum(jnp.where(keys == thr, 1.0, 0.0), axis=1, keepdims=True)
    need = k_eff - n_gt
    sel_ref[...] = jnp.where(keys >= thr, 1.0, 0.0)
    has_tie = jnp.max(jnp.where(n_eq != need, 1.0, 0.0)) > 0.5

    @pl.when(has_tie)
    def _():
        sel_ref[...] = jnp.where(_tie_select(keys, thr, gt, need, seq), 1.0, 0.0)

    sel = sel_ref[...] > 0.5

    for c in range(A_WIDTH // LANES):
        sl = slice(c * LANES, (c + 1) * LANES)
        qc = qa_ref[:, sl]
        kc = ka_ref[:, sl]
        vc = va_ref[:, sl]
        outs = []
        for hh in range(2):
            qm = jnp.where(_head_mask(qc.shape, A_HEAD_DIM, hh), qc, jnp.zeros_like(qc))
            lg = lax.dot_general(qm, kc, _NT, preferred_element_type=f32)
            lg = jnp.where(sel, lg, NEG_BIG)
            m = jnp.max(lg, axis=1, keepdims=True)
            p = jnp.exp(lg - m)
            den = jnp.sum(p, axis=1, keepdims=True)
            pv = jnp.dot(p.astype(bf), vc, preferred_element_type=f32)
            outs.append(pv / den)
        o_ref[:, sl] = jnp.where(_head_mask(outs[0].shape, A_HEAD_DIM, 0), outs[0], outs[1]).astype(bf)


def _dsa(qa, qi, wi, ka, va, ki, bsz, seq):
    tq = min(Q_TILE, seq)
    nq = seq // tq
    qspec = lambda w: pl.BlockSpec((tq, w), lambda b, i: (b * nq + i, 0))
    kspec = lambda w: pl.BlockSpec((seq, w), lambda b, i: (b, 0))
    k_sel = min(TOPK_MAX, seq // 4)
    return pl.pallas_call(
        functools.partial(_dsa_kernel, seq=seq, k_sel=k_sel),
        grid=(bsz, nq),
        in_specs=[qspec(A_WIDTH), qspec(A_WIDTH), qspec(LANES), kspec(A_WIDTH), kspec(A_WIDTH), kspec(LANES)],
        out_specs=qspec(A_WIDTH),
        out_shape=jax.ShapeDtypeStruct((bsz * seq, A_WIDTH), jnp.bfloat16),
        scratch_shapes=[pltpu.VMEM((tq, seq), jnp.float32)],
        compiler_params=_cparams(("arbitrary", "arbitrary")),
        name="dsa",
    )(qa, qi, wi, ka, va, ki)


def _mla_kernel(qn_ref, qr_ref, kn_ref, kr_ref, vb_ref, o_ref, *, seq):
    tq = qn_ref.shape[0]
    f32 = jnp.float32
    bf = jnp.bfloat16
    q0 = pl.program_id(1) * tq
    q_pos = q0 + lax.broadcasted_iota(jnp.int32, (tq, 1), 0)
    key_pos = lax.broadcasted_iota(jnp.int32, (tq, seq), 1)
    causal = key_pos <= q_pos
    kr = kr_ref[...]
    for c in range(B_WIDTH // LANES):
        sl = slice(c * LANES, (c + 1) * LANES)
        qc = qn_ref[:, sl]
        kcat = jnp.concatenate([kn_ref[:, sl], kr], axis=1)
        vc = vb_ref[:, sl]
        rsl = slice((c // 2) * LANES, (c // 2 + 1) * LANES)
        qrc = qr_ref[:, rsl]
        outs = []
        for hh in range(2):
            h = 2 * c + hh
            qm = jnp.where(_head_mask(qc.shape, QK_NOPE, hh), qc, jnp.zeros_like(qc))
            qrm = jnp.where(_head_mask(qrc.shape, QK_ROPE, h % 4), qrc, jnp.zeros_like(qrc))
            qcat = jnp.concatenate([qm, qrm], axis=1)
            lg = lax.dot_general(qcat, kcat, _NT, preferred_element_type=f32)
            lg = jnp.where(causal, lg, NEG_BIG)
            m = jnp.max(lg, axis=1, keepdims=True)
            p = jnp.exp(lg - m)
            den = jnp.sum(p, axis=1, keepdims=True)
            pv = jnp.dot(p.astype(bf), vc, preferred_element_type=f32)
            outs.append(pv / den)
        o_ref[:, sl] = jnp.where(_head_mask(outs[0].shape, V_DIM, 0), outs[0], outs[1]).astype(bf)


def _mla(qn, qr, kn, kr, vb, bsz, seq):
    tq = min(Q_TILE, seq)
    nq = seq // tq
    qspec = lambda w: pl.BlockSpec((tq, w), lambda b, i: (b * nq + i, 0))
    kspec = lambda w: pl.BlockSpec((seq, w), lambda b, i: (b, 0))
    return pl.pallas_call(
        functools.partial(_mla_kernel, seq=seq),
        grid=(bsz, nq),
        in_specs=[qspec(B_HEADS * QK_NOPE), qspec(B_HEADS * QK_ROPE), kspec(B_HEADS * QK_NOPE),
                  kspec(LANES), kspec(B_WIDTH)],
        out_specs=qspec(B_WIDTH),
        out_shape=jax.ShapeDtypeStruct((bsz * seq, B_WIDTH), jnp.bfloat16),
        compiler_params=_cparams(("arbitrary", "arbitrary")),
        name="mla",
    )(qn, qr, kn, kr, vb)


def _post_kernel(x_ref, oa_ref, ob_ref, wg_ref, bg_ref, woa_ref, wob_ref, wout_ref, g1_ref, b1_ref,
                 wr_ref, br_ref, h_ref, hp_ref, ri_ref, rf_ref, cnt_ref, carry_ref):
    f32 = jnp.float32
    bf = jnp.bfloat16
    t = x_ref.shape[0]

    @pl.when(pl.program_id(0) == 0)
    def _():
        carry_ref[...] = jnp.zeros_like(carry_ref)

    x = x_ref[...]
    gate = jax.nn.sigmoid(jnp.dot(x.astype(bf), wg_ref[...], preferred_element_type=f32) + bg_ref[...])
    pa = jnp.dot(oa_ref[...], woa_ref[...], preferred_element_type=f32)
    pb = jnp.dot(ob_ref[...], wob_ref[...], preferred_element_type=f32)
    mix = gate[:, :D_MODEL] * pa + gate[:, D_MODEL:] * pb
    u = DN_ALPHA * x + jnp.dot(mix.astype(bf), wout_ref[...], preferred_element_type=f32)
    h = _layer_norm(u, g1_ref[...], b1_ref[...])
    h_ref[...] = h
    hp_ref[...] = _pack_bf16_pairs(h)

    logits = jnp.dot(h, wr_ref[...], preferred_element_type=f32,
                     precision=lax.Precision.HIGHEST) + br_ref[...]
    lane = lax.broadcasted_iota(jnp.int32, (t, LANES), 1).astype(f32)
    lg = jnp.where(lane < N_EXPERTS, logits, -jnp.inf)
    vals, ids = [], []
    assign = jnp.zeros((t, LANES), f32)
    for _k in range(TOP_K):
        m = jnp.max(lg, axis=1, keepdims=True)
        idx = jnp.min(jnp.where(lg == m, lane, float(LANES)), axis=1, keepdims=True)
        hit = lane == idx
        vals.append(m)
        ids.append(idx)
        assign = jnp.where(hit, 1.0, assign)
        lg = jnp.where(hit, -jnp.inf, lg)
    exps = [jnp.exp(v - vals[0]) for v in vals]
    den = exps[0] + exps[1] + exps[2] + exps[3]

    r = lax.broadcasted_iota(jnp.int32, (t, t), 0)
    c = lax.broadcasted_iota(jnp.int32, (t, t), 1)
    lower = jnp.where(c < r, 1.0, 0.0).astype(bf)
    before = jnp.dot(lower, assign.astype(bf), preferred_element_type=f32) + carry_ref[0:1, :]
    ri = jnp.zeros((t, LANES), f32)
    rf = jnp.zeros((t, LANES), f32)
    for k in range(TOP_K):
        rank = jnp.sum(jnp.where(lane == ids[k], before, 0.0), axis=1, keepdims=True)
        ri = jnp.where(lane == float(k), ids[k], ri)
        ri = jnp.where(lane == float(TOP_K + k), rank, ri)
        rf = jnp.where(lane == float(k), exps[k] / den, rf)
    ri_ref[...] = ri.astype(jnp.int32)
    rf_ref[...] = rf
    total = carry_ref[0:1, :] + jnp.sum(assign, axis=0, keepdims=True)
    carry_ref[...] = jnp.broadcast_to(total, carry_ref.shape)
    cnt_ref[...] = jnp.broadcast_to(total, cnt_ref.shape)


def _post(x2, oa, ob, wg, bg, woa, wob, wout, g1, b1, wr, br):
    n = x2.shape[0]
    t = PROJ_TILE
    tok = lambda w: pl.BlockSpec((t, w), lambda i: (i, 0))
    consts = (wg, bg, woa, wob, wout, g1, b1, wr, br)
    return pl.pallas_call(
        _post_kernel,
        grid=(n // t,),
        in_specs=[tok(D_MODEL), tok(A_WIDTH), tok(B_WIDTH)] + [_full(a.shape) for a in consts],
        out_specs=[tok(D_MODEL), tok(HALF), tok(LANES), tok(LANES), _full((8, LANES))],
        out_shape=[jax.ShapeDtypeStruct((n, D_MODEL), jnp.float32),
                   jax.ShapeDtypeStruct((n, HALF), jnp.uint32),
                   jax.ShapeDtypeStruct((n, LANES), jnp.int32),
                   jax.ShapeDtypeStruct((n, LANES), jnp.float32),
                   jax.ShapeDtypeStruct((8, LANES), jnp.float32)],
        scratch_shapes=[pltpu.VMEM((8, LANES), jnp.float32)],
        compiler_params=_cparams(("arbitrary",)),
        name="post",
    )(x2, oa, ob, *consts)


def _row_gather(idx_ref, src_ref, dst_ref, sem, rows):
    for j in range(rows):
        pltpu.make_async_copy(src_ref.at[pl.ds(idx_ref[0, 0, j], 1)], dst_ref.at[pl.ds(j, 1)], sem).start()


def _row_gather_wait(src_ref, dst_ref, sem, rows):
    pltpu.make_async_copy(src_ref.at[pl.ds(0, rows)], dst_ref, sem).wait()


def _expert_kernel(te_ref, cur_ref, nxt_ref, hp_ref, wu_ref, bu_ref, wd_ref, bd_ref, ys_ref,
                   wub_ref, wdb_ref, xbuf_ref, sems):
    i = pl.program_id(0)
    n_steps = pl.num_programs(0)
    f32 = jnp.float32
    bf = jnp.bfloat16
    t = EXPERT_TILE
    slot = i % 2
    prev = te_ref[jnp.maximum(i - 1, 0)]
    fresh = (i == 0) | (te_ref[i] != prev)

    @pl.when(i == 0)
    def _():
        _row_gather(cur_ref, hp_ref, xbuf_ref.at[0], sems.at[0], t)

    @pl.when(fresh)
    def _():
        rows = 128

        def cast_up(j, c):
            r = pl.multiple_of(j * rows, rows)
            wub_ref[pl.ds(r, rows), :] = wu_ref[pl.ds(r, rows), :].astype(bf)
            return c

        lax.fori_loop(0, D_MODEL // rows, cast_up, 0)

        def cast_dn(j, c):
            r = pl.multiple_of(j * rows, rows)
            wdb_ref[pl.ds(r, rows), :] = wd_ref[pl.ds(r, rows), :].astype(bf)
            return c

        lax.fori_loop(0, D_FF // rows, cast_dn, 0)

    _row_gather_wait(hp_ref, xbuf_ref.at[slot], sems.at[slot], t)
    _row_gather(nxt_ref, hp_ref, xbuf_ref.at[1 - slot], sems.at[1 - slot], t)

    lo, hi = _unpack_bf16_pairs(xbuf_ref[slot])
    xlo, xhi = lo.astype(bf), hi.astype(bf)
    y = jnp.broadcast_to(bd_ref[...], (t, D_MODEL))
    for c in range(D_FF // FF_CHUNK):
        gs = slice(c * FF_CHUNK, (c + 1) * FF_CHUNK)
        ls = slice(D_FF + c * FF_CHUNK, D_FF + (c + 1) * FF_CHUNK)
        ag = (jnp.dot(xlo, wub_ref[:HALF, gs], preferred_element_type=f32)
              + jnp.dot(xhi, wub_ref[HALF:, gs], preferred_element_type=f32) + bu_ref[:, gs])
        al = (jnp.dot(xlo, wub_ref[:HALF, ls], preferred_element_type=f32)
              + jnp.dot(xhi, wub_ref[HALF:, ls], preferred_element_type=f32) + bu_ref[:, ls])
        g = jnp.minimum(ag, SWIGLU_LIMIT)
        lin = jnp.clip(al, -SWIGLU_LIMIT, SWIGLU_LIMIT)
        act = (lin + 1.0) * (g * jax.nn.sigmoid(SWIGLU_ALPHA * g))
        y = y + jnp.dot(act.astype(bf), wdb_ref[gs, :], preferred_element_type=f32)
    ys_ref[...] = _pack_bf16_pairs(y)

    @pl.when(i == n_steps - 1)
    def _():
        _row_gather_wait(hp_ref, xbuf_ref.at[1 - slot], sems.at[1 - slot], t)


def _experts(tile_expert, row_token3, hp, w_up, b_up, w_down, b_down):
    t = EXPERT_TILE
    max_tiles = row_token3.shape[0]
    last = max_tiles - 1
    idx_spec = lambda f: pl.BlockSpec((1, 1, t), f, memory_space=pltpu.SMEM)
    grid_spec = pltpu.PrefetchScalarGridSpec(
        num_scalar_prefetch=1,
        grid=(max_tiles,),
        in_specs=[idx_spec(lambda i, te: (i, 0, 0)),
                  idx_spec(lambda i, te: (jnp.minimum(i + 1, last), 0, 0)),
                  pl.BlockSpec(memory_space=pl.ANY),
                  pl.BlockSpec((None, D_MODEL, 2 * D_FF), lambda i, te: (te[i], 0, 0)),
                  pl.BlockSpec((None, 1, 2 * D_FF), lambda i, te: (te[i], 0, 0)),
                  pl.BlockSpec((None, D_FF, D_MODEL), lambda i, te: (te[i], 0, 0)),
                  pl.BlockSpec((None, 1, D_MODEL), lambda i, te: (te[i], 0, 0))],
        out_specs=pl.BlockSpec((t, HALF), lambda i, te: (i, 0)),
        scratch_shapes=[pltpu.VMEM((D_MODEL, 2 * D_FF), jnp.bfloat16),
                        pltpu.VMEM((D_FF, D_MODEL), jnp.bfloat16),
                        pltpu.VMEM((2, t, HALF), jnp.uint32),
                        pltpu.SemaphoreType.DMA((2,))],
    )
    return pl.pallas_call(
        _expert_kernel,
        grid_spec=grid_spec,
        out_shape=jax.ShapeDtypeStruct((max_tiles * t, HALF), jnp.uint32),
        compiler_params=_cparams(("arbitrary",)),
        name="experts",
    )(tile_expert, row_token3, row_token3, hp, w_up, b_up.reshape(N_EXPERTS, 1, 2 * D_FF), w_down,
      b_down.reshape(N_EXPERTS, 1, D_MODEL))


def _combine_kernel(cur_ref, nxt_ref, ys_ref, h_ref, rf_ref, g2_ref, b2_ref, o_ref, buf_ref, sems):
    i = pl.program_id(0)
    n_steps = pl.num_programs(0)
    t = MOVE_TILE
    rows = t * TOP_K
    slot = i % 2

    @pl.when(i == 0)
    def _():
        _row_gather(cur_ref, ys_ref, buf_ref.at[0], sems.at[0], rows)

    _row_gather_wait(ys_ref, buf_ref.at[slot], sems.at[slot], rows)
    _row_gather(nxt_ref, ys_ref, buf_ref.at[1 - slot], sems.at[1 - slot], rows)

    h = h_ref[...]
    rf = rf_ref[...]
    acc_lo = DN_ALPHA * h[:, :HALF]
    acc_hi = DN_ALPHA * h[:, HALF:]
    for k in range(TOP_K):
        lo, hi = _unpack_bf16_pairs(buf_ref[slot, k * t:(k + 1) * t, :])
        gk = rf[:, k:k + 1]
        acc_lo = acc_lo + gk * lo
        acc_hi = acc_hi + gk * hi
    u = jnp.concatenate([acc_lo, acc_hi], axis=1)
    o_ref[...] = _layer_norm(u, g2_ref[...], b2_ref[...])

    @pl.when(i == n_steps - 1)
    def _():
        _row_gather_wait(ys_ref, buf_ref.at[1 - slot], sems.at[1 - slot], rows)


def _combine(dest3, ys, h, rf, g2, b2):
    n = h.shape[0]
    t = MOVE_TILE
    last = n // t - 1
    tok = lambda w: pl.BlockSpec((t, w), lambda i: (i, 0))
    idx_spec = lambda f: pl.BlockSpec((1, 1, t * TOP_K), f, memory_space=pltpu.SMEM)
    return pl.pallas_call(
        _combine_kernel,
        grid=(n // t,),
        in_specs=[idx_spec(lambda i: (i, 0, 0)), idx_spec(lambda i: (jnp.minimum(i + 1, last), 0, 0)),
                  pl.BlockSpec(memory_space=pl.ANY), tok(D_MODEL), tok(LANES),
                  _full(g2.shape), _full(b2.shape)],
        out_specs=tok(D_MODEL),
        out_shape=jax.ShapeDtypeStruct((n, D_MODEL), jnp.float32),
        scratch_shapes=[pltpu.VMEM((2, t * TOP_K, HALF), jnp.uint32), pltpu.SemaphoreType.DMA((2,))],
        compiler_params=_cparams(("arbitrary",)),
        name="combine",
    )(dest3, dest3, ys, h, rf, g2, b2)


def _rope_tables(positions, half):
    inv_freq = ROPE_THETA ** (-jnp.arange(half, dtype=jnp.float32) / half)
    ang = positions.astype(jnp.float32).reshape(-1, 1) * inv_freq
    reps = LANES // half
    return jnp.tile(jnp.cos(ang), (1, reps)), jnp.tile(jnp.sin(ang), (1, reps))


def _layer(x2, tabs, bsz, seq, w_in, b_gate, rms_cq, rms_ckv, w_uq, w_ukv, w_o_a, w_o_b, w_out,
           ln1_g, ln1_b, w_router, b_router, w_up, b_up, w_down, b_down, ln2_g, ln2_b):
    bf = jnp.bfloat16
    n = x2.shape[0]
    off = np.concatenate([[0], np.cumsum(SPLITS)]).tolist()
    col = lambda j: w_in[:, off[j]:off[j + 1]]
    w1 = jnp.concatenate([col(0), col(1), col(2), col(3)], axis=1).astype(bf)
    wi_pad = jnp.zeros((D_MODEL, LANES - IDX_HEADS), w_in.dtype)
    ws = jnp.concatenate([col(4), col(4), jnp.tile(col(8), (1, LANES // QK_ROPE)), col(5), wi_pad], axis=1).astype(bf)
    wl = jnp.concatenate([col(6), col(7)], axis=1).astype(bf)
    wg = col(9).astype(bf)
    uq = w_uq.reshape(Q_LORA, B_HEADS, QK_NOPE + QK_ROPE)
    wuq = jnp.concatenate([uq[:, :, :QK_NOPE].reshape(Q_LORA, -1), uq[:, :, QK_NOPE:].reshape(Q_LORA, -1)],
                          axis=1).astype(bf)
    ukv = w_ukv.reshape(KV_LORA, B_HEADS, QK_NOPE + V_DIM)
    wukv = jnp.concatenate([ukv[:, :, :QK_NOPE].reshape(KV_LORA, -1), ukv[:, :, QK_NOPE:].reshape(KV_LORA, -1)],
                           axis=1).astype(bf)

    qa, ka, va, qi, ki, kr, wi, qn, qr, kn, vb = _proj(
        x2, tabs, w1, ws, wl, rms_cq.reshape(1, -1), rms_ckv.reshape(1, -1), wuq, wukv)
    o_a = _dsa(qa, qi, wi, ka, va, ki, bsz, seq)
    o_b = _mla(qn, qr, kn, kr, vb, bsz, seq)

    wr = jnp.concatenate([w_router, jnp.zeros((D_MODEL, LANES - N_EXPERTS), w_router.dtype)], axis=1)
    br = jnp.concatenate([b_router, jnp.zeros((LANES - N_EXPERTS,), b_router.dtype)]).reshape(1, -1)
    h, hp, ri, rf, cnt = _post(x2, o_a, o_b, wg, b_gate.reshape(1, -1), w_o_a.astype(bf), w_o_b.astype(bf),
                               w_out.astype(bf), ln1_g.reshape(1, -1), ln1_b.reshape(1, -1), wr, br)

    te_rows = EXPERT_TILE
    counts = cnt[0, :N_EXPERTS].astype(jnp.int32)
    padded = ((counts + te_rows - 1) // te_rows) * te_rows
    ends = jnp.cumsum(padded)
    starts = ends - padded
    ids = ri[:, :TOP_K]
    experts = jnp.arange(N_EXPERTS, dtype=jnp.int32)
    dest = jnp.sum(jnp.where(ids[:, :, None] == experts, starts, 0), axis=-1) + ri[:, TOP_K:2 * TOP_K]
    dest3 = dest.reshape(n // MOVE_TILE, MOVE_TILE, TOP_K).transpose(0, 2, 1).reshape(
        n // MOVE_TILE, 1, MOVE_TILE * TOP_K)
    max_tiles = n * TOP_K // te_rows + N_EXPERTS
    tile_start = jnp.arange(max_tiles, dtype=jnp.int32) * te_rows
    tile_start = jnp.minimum(tile_start, ends[-1] - te_rows)
    tile_expert = jnp.sum((ends[None, :] <= tile_start[:, None]).astype(jnp.int32), axis=1)
    token_ids = jnp.repeat(jnp.arange(n, dtype=jnp.int32), TOP_K)
    row_token = jnp.zeros((max_tiles * te_rows,), jnp.int32).at[dest.reshape(-1)].set(
        token_ids, unique_indices=True)
    row_token3 = row_token.reshape(max_tiles, 1, te_rows)

    ys = _experts(tile_expert, row_token3, hp, w_up, b_up, w_down, b_down)
    return _combine(dest3, ys, h, rf, ln2_g.reshape(1, -1), ln2_b.reshape(1, -1))


def kernel(x, positions, w_in, b_gate, rms_cq, rms_ckv, w_uq, w_ukv, w_o_a, w_o_b, w_out, ln1_g, ln1_b,
           w_router, b_router, w_up, b_up, w_down, b_down, ln2_g, ln2_b):
    bsz, seq, _ = x.shape
    x2 = x.reshape(bsz * seq, D_MODEL)
    c64, s64 = _rope_tables(positions, A_HEAD_DIM // 2)
    c32, s32 = _rope_tables(positions, QK_ROPE // 2)
    tabs = (c64, s64, c32, s32)
    for l in range(DEPTH):
        x2 = _layer(x2, tabs, bsz, seq, w_in[l], b_gate[l], rms_cq[l], rms_ckv[l], w_uq[l], w_ukv[l],
                    w_o_a[l], w_o_b[l], w_out[l], ln1_g[l], ln1_b[l], w_router[l], b_router[l],
                    w_up[l], b_up[l], w_down[l], b_down[l], ln2_g[l], ln2_b[l])
    return x2.reshape(bsz, seq, D_MODEL)
```

```python
import functools

import jax
import jax.numpy as jnp
import numpy as np
from jax import lax
from jax.experimental import pallas as pl
from jax.experimental.pallas import tpu as pltpu

D_MODEL = 1024
A_HEADS = 8
A_HEAD_DIM = 64
IDX_HEADS = 8
IDX_DIM = 64
TOPK_MAX = 256
B_HEADS = 8
Q_LORA = 384
KV_LORA = 256
QK_NOPE = 64
QK_ROPE = 32
V_DIM = 64
ROPE_THETA = 10000.0
N_EXPERTS = 32
TOP_K = 4
D_FF = 1024
SWIGLU_LIMIT = 7.0
SWIGLU_ALPHA = 1.702
DEPTH = 1
DN_ALPHA = (2 * DEPTH) ** 0.25
LN_EPS = 1e-5
RMS_EPS = 1e-6
A_WIDTH = A_HEADS * A_HEAD_DIM
B_WIDTH = B_HEADS * V_DIM
SPLITS = (A_WIDTH, A_WIDTH, A_WIDTH, IDX_HEADS * IDX_DIM, IDX_DIM, IDX_HEADS,
          Q_LORA, KV_LORA, QK_ROPE, 2 * D_MODEL)

LANES = 128
VMEM_LIMIT_BYTES = 52 * 1024 * 1024

PROJ_TILE = 512
Q_TILE = 256
EXPERT_TILE = 256
FF_CHUNK = 256
MOVE_TILE = 256

INT_MIN = -(2 ** 31)
NEG_BIG = -1e30
HALF = D_MODEL // 2

_NT = (((1,), (1,)), ((), ()))


def _cparams(sem):
    return pltpu.CompilerParams(dimension_semantics=sem, vmem_limit_bytes=VMEM_LIMIT_BYTES)


def _full(shape):
    nd = len(shape)
    return pl.BlockSpec(shape, lambda *_: (0,) * nd)


def _rope_chunk(x, cos, sin, half):
    lane = lax.broadcasted_iota(jnp.int32, x.shape, 1)
    first = (lane % (2 * half)) < half
    up = pltpu.roll(x, LANES - half, 1)
    dn = pltpu.roll(x, half, 1)
    return x * cos + jnp.where(first, -up, dn) * sin


def _layer_norm(u, g, b):
    mu = jnp.mean(u, axis=-1, keepdims=True)
    d = u - mu
    var = jnp.mean(d * d, axis=-1, keepdims=True)
    return d * lax.rsqrt(var + LN_EPS) * g + b


def _pack_bf16_pairs(v):
    w = v.shape[1] // 2
    vb = v.astype(jnp.bfloat16).astype(jnp.float32)
    lo = lax.bitcast_convert_type(vb[:, :w], jnp.uint32) >> 16
    hi = lax.bitcast_convert_type(vb[:, w:], jnp.uint32) & jnp.uint32(0xFFFF0000)
    return lo | hi


def _unpack_bf16_pairs(p):
    lo = lax.bitcast_convert_type(p << 16, jnp.float32)
    hi = lax.bitcast_convert_type(p & jnp.uint32(0xFFFF0000), jnp.float32)
    return lo, hi


def _head_mask(shape, width, which):
    lane = lax.broadcasted_iota(jnp.int32, shape, 1)
    return (lane // width) == which


def _proj_kernel(x_ref, c64_ref, s64_ref, c32_ref, s32_ref, w1_ref, ws_ref, wl_ref,
                 gcq_ref, gckv_ref, wuq_ref, wukv_ref,
                 qa_ref, ka_ref, va_ref, qi_ref, ki_ref, kr_ref, wi_ref,
                 qn_ref, qr_ref, kn_ref, vb_ref):
    xb = x_ref[...].astype(jnp.bfloat16)
    c64, s64 = c64_ref[...], s64_ref[...]
    c32, s32 = c32_ref[...], s32_ref[...]
    f32 = jnp.float32
    bf = jnp.bfloat16

    z1 = jnp.dot(xb, w1_ref[...], preferred_element_type=f32)
    n_chunk = A_WIDTH // LANES
    a_scale = A_HEAD_DIM ** -0.5
    for c in range(n_chunk):
        sl = slice(c * LANES, (c + 1) * LANES)
        qa_ref[:, sl] = (_rope_chunk(z1[:, c * LANES:(c + 1) * LANES], c64, s64, 32) * a_scale).astype(bf)
        o = A_WIDTH + c * LANES
        ka_ref[:, sl] = _rope_chunk(z1[:, o:o + LANES], c64, s64, 32).astype(bf)
        o = 2 * A_WIDTH + c * LANES
        va_ref[:, sl] = z1[:, o:o + LANES].astype(bf)
        o = 3 * A_WIDTH + c * LANES
        qi_ref[:, sl] = _rope_chunk(z1[:, o:o + LANES], c64, s64, 32).astype(bf)

    zs = jnp.dot(xb, ws_ref[...], preferred_element_type=f32)
    ki_ref[...] = _rope_chunk(zs[:, 0:LANES], c64, s64, 32).astype(bf)
    kr_ref[...] = _rope_chunk(zs[:, LANES:2 * LANES], c32, s32, 16).astype(bf)
    wi_ref[...] = zs[:, 2 * LANES:3 * LANES] * (IDX_HEADS ** -0.5 * IDX_DIM ** -0.5)

    zl = jnp.dot(xb, wl_ref[...], preferred_element_type=f32)
    cq = zl[:, :Q_LORA]
    cq = cq * lax.rsqrt(jnp.mean(cq * cq, axis=-1, keepdims=True) + RMS_EPS) * gcq_ref[...]
    ckv = zl[:, Q_LORA:]
    ckv = ckv * lax.rsqrt(jnp.mean(ckv * ckv, axis=-1, keepdims=True) + RMS_EPS) * gckv_ref[...]
    qb = jnp.dot(cq.astype(bf), wuq_ref[...], preferred_element_type=f32)
    kv = jnp.dot(ckv.astype(bf), wukv_ref[...], preferred_element_type=f32)
    b_scale = (QK_NOPE + QK_ROPE) ** -0.5
    qn_ref[...] = (qb[:, :B_HEADS * QK_NOPE] * b_scale).astype(bf)
    for c in range(B_HEADS * QK_ROPE // LANES):
        o = B_HEADS * QK_NOPE + c * LANES
        qr_ref[:, c * LANES:(c + 1) * LANES] = (
            _rope_chunk(qb[:, o:o + LANES], c32, s32, 16) * b_scale).astype(bf)
    kn_ref[...] = kv[:, :B_HEADS * QK_NOPE].astype(bf)
    vb_ref[...] = kv[:, B_HEADS * QK_NOPE:].astype(bf)


def _proj(x2, tabs, w1, ws, wl, gcq, gckv, wuq, wukv):
    n = x2.shape[0]
    t = PROJ_TILE
    tok = lambda w: pl.BlockSpec((t, w), lambda i: (i, 0))
    bf = jnp.bfloat16
    outs = [(A_WIDTH, bf)] * 4 + [(LANES, bf), (LANES, bf), (LANES, jnp.float32),
                                  (B_HEADS * QK_NOPE, bf), (B_HEADS * QK_ROPE, bf),
                                  (B_HEADS * QK_NOPE, bf), (B_WIDTH, bf)]
    return pl.pallas_call(
        _proj_kernel,
        grid=(n // t,),
        in_specs=[tok(D_MODEL)] + [tok(LANES)] * 4 + [_full(a.shape) for a in (w1, ws, wl, gcq, gckv, wuq, wukv)],
        out_specs=[tok(w) for w, _ in outs],
        out_shape=[jax.ShapeDtypeStruct((n, w), d) for w, d in outs],
        compiler_params=_cparams(("arbitrary",)),
        name="proj",
    )(x2, *tabs, w1, ws, wl, gcq, gckv, wuq, wukv)


def _tie_select(eq, gt, need, kv):
    chunk = 256
    r = lax.broadcasted_iota(jnp.int32, (chunk, chunk), 0)
    c = lax.broadcasted_iota(jnp.int32, (chunk, chunk), 1)
    upper = jnp.where(r <= c, 1.0, 0.0).astype(jnp.bfloat16)
    carry = jnp.zeros((eq.shape[0], 1), jnp.float32)
    parts = []
    for j in range(kv // chunk):
        e = eq[:, j * chunk:(j + 1) * chunk]
        ef = jnp.where(e, 1.0, 0.0).astype(jnp.bfloat16)
        pref = jnp.dot(ef, upper, preferred_element_type=jnp.float32) + carry
        parts.append(gt[:, j * chunk:(j + 1) * chunk] | (e & (pref <= need)))
        carry = carry + jnp.sum(ef.astype(jnp.float32), axis=1, keepdims=True)
    return jnp.concatenate(parts, axis=1)


def _key_to_float(key):
    return lax.bitcast_convert_type(key ^ ((key >> 31) & jnp.int32(0x7FFFFFFF)), jnp.float32)


def _select_topk(score, k_eff, sel_ref, kv):
    tq = score.shape[0]

    def search(it, prefix):
        cand = prefix + lax.shift_left(jnp.int32(1), 31 - it)
        cnt = jnp.sum(jnp.where(score >= _key_to_float(cand), 1.0, 0.0), axis=1, keepdims=True)
        return jnp.where(cnt >= k_eff, cand, prefix)

    thr = _key_to_float(lax.fori_loop(0, 32, search, jnp.full((tq, 1), INT_MIN, jnp.int32)))
    gt = score > thr
    eq = score == thr
    n_gt = jnp.sum(jnp.where(gt, 1.0, 0.0), axis=1, keepdims=True)
    n_eq = jnp.sum(jnp.where(eq, 1.0, 0.0), axis=1, keepdims=True)
    need = k_eff - n_gt
    sel_ref[:, :kv] = jnp.where(score >= thr, 1.0, 0.0)
    has_tie = jnp.max(jnp.where(n_eq != need, 1.0, 0.0)) > 0.5

    @pl.when(has_tie)
    def _():
        sel_ref[:, :kv] = jnp.where(_tie_select(eq, gt, need, kv), 1.0, 0.0)


def _dsa_block(qa_ref, qi_ref, wi_ref, ka_ref, va_ref, ki_ref, o_ref, sel_ref, *, kv, k_sel):
    tq = qa_ref.shape[0]
    f32 = jnp.float32
    bf = jnp.bfloat16
    q_pos = (kv - tq) + lax.broadcasted_iota(jnp.int32, (tq, 1), 0)
    key_pos = lax.broadcasted_iota(jnp.int32, (tq, kv), 1)
    causal = key_pos <= q_pos

    if kv <= k_sel:
        sel = causal
    else:
        ki = ki_ref[:kv, :]
        wi = wi_ref[...]
        score = jnp.zeros((tq, kv), f32)
        for h in range(IDX_HEADS):
            c = h // 2
            qc = qi_ref[:, c * LANES:(c + 1) * LANES]
            qm = jnp.where(_head_mask(qc.shape, IDX_DIM, h % 2), qc, jnp.zeros_like(qc))
            s = lax.dot_general(qm, ki, _NT, preferred_element_type=f32)
            score = score + jnp.maximum(s, 0.0) * wi[:, h:h + 1]
        score = jnp.where(causal, score, -jnp.inf)
        k_eff = jnp.minimum(q_pos + 1, k_sel).astype(f32)
        _select_topk(score, k_eff, sel_ref, kv)
        sel = sel_ref[:, :kv] > 0.5

    for c in range(A_WIDTH // LANES):
        sl = slice(c * LANES, (c + 1) * LANES)
        qc = qa_ref[:, sl]
        kc = ka_ref[:kv, sl]
        vc = va_ref[:kv, sl]
        outs = []
        for hh in range(2):
            qm = jnp.where(_head_mask(qc.shape, A_HEAD_DIM, hh), qc, jnp.zeros_like(qc))
            lg = lax.dot_general(qm, kc, _NT, preferred_element_type=f32)
            lg = jnp.where(sel, lg, NEG_BIG)
            m = jnp.max(lg, axis=1, keepdims=True)
            p = jnp.exp(lg - m)
            den = jnp.sum(p, axis=1, keepdims=True)
            pv = jnp.dot(p.astype(bf), vc, preferred_element_type=f32)
            outs.append(pv / den)
        o_ref[:, sl] = jnp.where(_head_mask(outs[0].shape, A_HEAD_DIM, 0), outs[0], outs[1]).astype(bf)


def _per_query_block(block_fn, refs, n_blocks, tq):
    j = pl.program_id(1)
    for b in range(n_blocks):
        pl.when(j == b)(functools.partial(block_fn, *refs, kv=(b + 1) * tq))


def _dsa_kernel(*refs, seq, k_sel):
    tq = refs[0].shape[0]
    _per_query_block(functools.partial(_dsa_block, k_sel=k_sel), refs, seq // tq, tq)


def _dsa(qa, qi, wi, ka, va, ki, bsz, seq):
    tq = min(Q_TILE, seq)
    nq = seq // tq
    qspec = lambda w: pl.BlockSpec((tq, w), lambda b, i: (b * nq + i, 0))
    kspec = lambda w: pl.BlockSpec((seq, w), lambda b, i: (b, 0))
    k_sel = min(TOPK_MAX, seq // 4)
    return pl.pallas_call(
        functools.partial(_dsa_kernel, seq=seq, k_sel=k_sel),
        grid=(bsz, nq),
        in_specs=[qspec(A_WIDTH), qspec(A_WIDTH), qspec(LANES), kspec(A_WIDTH), kspec(A_WIDTH), kspec(LANES)],
        out_specs=qspec(A_WIDTH),
        out_shape=jax.ShapeDtypeStruct((bsz * seq, A_WIDTH), jnp.bfloat16),
        scratch_shapes=[pltpu.VMEM((tq, seq), jnp.float32)],
        compiler_params=_cparams(("arbitrary", "arbitrary")),
        name="dsa",
    )(qa, qi, wi, ka, va, ki)


def _mla_block(qn_ref, qr_ref, kn_ref, kr_ref, vb_ref, o_ref, *, kv):
    tq = qn_ref.shape[0]
    f32 = jnp.float32
    bf = jnp.bfloat16
    q_pos = (kv - tq) + lax.broadcasted_iota(jnp.int32, (tq, 1), 0)
    key_pos = lax.broadcasted_iota(jnp.int32, (tq, kv), 1)
    causal = key_pos <= q_pos
    kr = kr_ref[:kv, :]
    for c in range(B_WIDTH // LANES):
        sl = slice(c * LANES, (c + 1) * LANES)
        qc = qn_ref[:, sl]
        kcat = jnp.concatenate([kn_ref[:kv, sl], kr], axis=1)
        vc = vb_ref[:kv, sl]
        rsl = slice((c // 2) * LANES, (c // 2 + 1) * LANES)
        qrc = qr_ref[:, rsl]
        outs = []
        for hh in range(2):
            h = 2 * c + hh
            qm = jnp.where(_head_mask(qc.shape, QK_NOPE, hh), qc, jnp.zeros_like(qc))
            qrm = jnp.where(_head_mask(qrc.shape, QK_ROPE, h % 4), qrc, jnp.zeros_like(qrc))
            qcat = jnp.concatenate([qm, qrm], axis=1)
            lg = lax.dot_general(qcat, kcat, _NT, preferred_element_type=f32)
            lg = jnp.where(causal, lg, NEG_BIG)
            m = jnp.max(lg, axis=1, keepdims=True)
            p = jnp.exp(lg - m)
            den = jnp.sum(p, axis=1, keepdims=True)
            pv = jnp.dot(p.astype(bf), vc, preferred_element_type=f32)
            outs.append(pv / den)
        o_ref[:, sl] = jnp.where(_head_mask(outs[0].shape, V_DIM, 0), outs[0], outs[1]).astype(bf)


def _mla_kernel(*refs, seq):
    tq = refs[0].shape[0]
    _per_query_block(_mla_block, refs, seq // tq, tq)


def _mla(qn, qr, kn, kr, vb, bsz, seq):
    tq = min(Q_TILE, seq)
    nq = seq // tq
    qspec = lambda w: pl.BlockSpec((tq, w), lambda b, i: (b * nq + i, 0))
    kspec = lambda w: pl.BlockSpec((seq, w), lambda b, i: (b, 0))
    return pl.pallas_call(
        functools.partial(_mla_kernel, seq=seq),
        grid=(bsz, nq),
        in_specs=[qspec(B_HEADS * QK_NOPE), qspec(B_HEADS * QK_ROPE), kspec(B_HEADS * QK_NOPE),
                  kspec(LANES), kspec(B_WIDTH)],
        out_specs=qspec(B_WIDTH),
        out_shape=jax.ShapeDtypeStruct((bsz * seq, B_WIDTH), jnp.bfloat16),
        compiler_params=_cparams(("arbitrary", "arbitrary")),
        name="mla",
    )(qn, qr, kn, kr, vb)


def _post_kernel(x_ref, oa_ref, ob_ref, wg_ref, bg_ref, woa_ref, wob_ref, wout_ref, g1_ref, b1_ref,
                 wr_ref, br_ref, h_ref, hp_ref, ri_ref, rf_ref, cnt_ref, carry_ref):
    f32 = jnp.float32
    bf = jnp.bfloat16
    t = x_ref.shape[0]

    @pl.when(pl.program_id(0) == 0)
    def _():
        carry_ref[...] = jnp.zeros_like(carry_ref)

    x = x_ref[...]
    gate = jax.nn.sigmoid(jnp.dot(x.astype(bf), wg_ref[...], preferred_element_type=f32) + bg_ref[...])
    pa = jnp.dot(oa_ref[...], woa_ref[...], preferred_element_type=f32)
    pb = jnp.dot(ob_ref[...], wob_ref[...], preferred_element_type=f32)
    mix = gate[:, :D_MODEL] * pa + gate[:, D_MODEL:] * pb
    u = DN_ALPHA * x + jnp.dot(mix.astype(bf), wout_ref[...], preferred_element_type=f32)
    h = _layer_norm(u, g1_ref[...], b1_ref[...])
    h_ref[...] = h
    hp_ref[...] = _pack_bf16_pairs(h)

    logits = jnp.dot(h, wr_ref[...], preferred_element_type=f32,
                     precision=lax.Precision.HIGHEST) + br_ref[...]
    lane = lax.broadcasted_iota(jnp.int32, (t, LANES), 1).astype(f32)
    lg = jnp.where(lane < N_EXPERTS, logits, -jnp.inf)
    vals, ids = [], []
    assign = jnp.zeros((t, LANES), f32)
    for _k in range(TOP_K):
        m = jnp.max(lg, axis=1, keepdims=True)
        idx = jnp.min(jnp.where(lg == m, lane, float(LANES)), axis=1, keepdims=True)
        hit = lane == idx
        vals.append(m)
        ids.append(idx)
        assign = jnp.where(hit, 1.0, assign)
        lg = jnp.where(hit, -jnp.inf, lg)
    exps = [jnp.exp(v - vals[0]) for v in vals]
    den = exps[0] + exps[1] + exps[2] + exps[3]

    r = lax.broadcasted_iota(jnp.int32, (t, t), 0)
    c = lax.broadcasted_iota(jnp.int32, (t, t), 1)
    lower = jnp.where(c < r, 1.0, 0.0).astype(bf)
    before = jnp.dot(lower, assign.astype(bf), preferred_element_type=f32) + carry_ref[0:1, :]
    ri = jnp.zeros((t, LANES), f32)
    rf = jnp.zeros((t, LANES), f32)
    for k in range(TOP_K):
        rank = jnp.sum(jnp.where(lane == ids[k], before, 0.0), axis=1, keepdims=True)
        ri = jnp.where(lane == float(k), ids[k], ri)
        ri = jnp.where(lane == float(TOP_K + k), rank, ri)
        rf = jnp.where(lane == float(k), exps[k] / den, rf)
    ri_ref[...] = ri.astype(jnp.int32)
    rf_ref[...] = rf
    total = carry_ref[0:1, :] + jnp.sum(assign, axis=0, keepdims=True)
    carry_ref[...] = jnp.broadcast_to(total, carry_ref.shape)
    cnt_ref[...] = jnp.broadcast_to(total, cnt_ref.shape)


def _post(x2, oa, ob, wg, bg, woa, wob, wout, g1, b1, wr, br):
    n = x2.shape[0]
    t = PROJ_TILE
    tok = lambda w: pl.BlockSpec((t, w), lambda i: (i, 0))
    consts = (wg, bg, woa, wob, wout, g1, b1, wr, br)
    return pl.pallas_call(
        _post_kernel,
        grid=(n // t,),
        in_specs=[tok(D_MODEL), tok(A_WIDTH), tok(B_WIDTH)] + [_full(a.shape) for a in consts],
        out_specs=[tok(D_MODEL), tok(HALF), tok(LANES), tok(LANES), _full((8, LANES))],
        out_shape=[jax.ShapeDtypeStruct((n, D_MODEL), jnp.float32),
                   jax.ShapeDtypeStruct((n, HALF), jnp.uint32),
                   jax.ShapeDtypeStruct((n, LANES), jnp.int32),
                   jax.ShapeDtypeStruct((n, LANES), jnp.float32),
                   jax.ShapeDtypeStruct((8, LANES), jnp.float32)],
        scratch_shapes=[pltpu.VMEM((8, LANES), jnp.float32)],
        compiler_params=_cparams(("arbitrary",)),
        name="post",
    )(x2, oa, ob, *consts)


def _row_gather(idx_ref, src_ref, dst_ref, sem, rows):
    for j in range(rows):
        pltpu.make_async_copy(src_ref.at[pl.ds(idx_ref[0, 0, j], 1)], dst_ref.at[pl.ds(j, 1)], sem).start()


def _rows_wait(src_ref, dst_ref, sem):
    pltpu.make_async_copy(src_ref, dst_ref, sem).wait()


def _dispatch_kernel(dest_ref, hp_ref, init_ref, xs_ref, sem):
    del init_ref
    t = MOVE_TILE
    for k in range(TOP_K):
        for j in range(t):
            pltpu.make_async_copy(hp_ref.at[pl.ds(j, 1)], xs_ref.at[pl.ds(dest_ref[0, 0, k * t + j], 1)],
                                  sem).start()
    for k in range(TOP_K):
        _rows_wait(hp_ref, xs_ref.at[pl.ds(0, t)], sem)


def _dispatch(dest3, hp, rows):
    n = hp.shape[0]
    t = MOVE_TILE
    return pl.pallas_call(
        _dispatch_kernel,
        grid=(n // t,),
        in_specs=[pl.BlockSpec((1, 1, t * TOP_K), lambda i: (i, 0, 0), memory_space=pltpu.SMEM),
                  pl.BlockSpec((t, HALF), lambda i: (i, 0)), pl.BlockSpec(memory_space=pl.ANY)],
        out_specs=pl.BlockSpec(memory_space=pl.ANY),
        out_shape=jax.ShapeDtypeStruct((rows, HALF), jnp.uint32),
        input_output_aliases={2: 0},
        scratch_shapes=[pltpu.SemaphoreType.DMA(())],
        compiler_params=_cparams(("arbitrary",)),
        name="dispatch",
    )(dest3, hp, jnp.zeros((rows, HALF), jnp.uint32))


def _expert_kernel(te_ref, nt_ref, xs_ref, wu_ref, bu_ref, wd_ref, bd_ref, ys_ref, wub_ref, wdb_ref):
    i = pl.program_id(0)
    f32 = jnp.float32
    bf = jnp.bfloat16
    t = EXPERT_TILE
    valid = i < nt_ref[0]
    prev = te_ref[jnp.maximum(i - 1, 0)]
    fresh = valid & ((i == 0) | (te_ref[i] != prev))

    @pl.when(fresh)
    def _():
        rows = 128

        def cast_up(j, c):
            r = pl.multiple_of(j * rows, rows)
            wub_ref[pl.ds(r, rows), :] = wu_ref[pl.ds(r, rows), :].astype(bf)
            return c

        lax.fori_loop(0, D_MODEL // rows, cast_up, 0)

        def cast_dn(j, c):
            r = pl.multiple_of(j * rows, rows)
            wdb_ref[pl.ds(r, rows), :] = wd_ref[pl.ds(r, rows), :].astype(bf)
            return c

        lax.fori_loop(0, D_FF // rows, cast_dn, 0)

    @pl.when(valid)
    def _():
        lo, hi = _unpack_bf16_pairs(xs_ref[...])
        xlo, xhi = lo.astype(bf), hi.astype(bf)
        y = jnp.broadcast_to(bd_ref[...], (t, D_MODEL))
        for c in range(D_FF // FF_CHUNK):
            gs = slice(c * FF_CHUNK, (c + 1) * FF_CHUNK)
            ls = slice(D_FF + c * FF_CHUNK, D_FF + (c + 1) * FF_CHUNK)
            ag = (jnp.dot(xlo, wub_ref[:HALF, gs], preferred_element_type=f32)
                  + jnp.dot(xhi, wub_ref[HALF:, gs], preferred_element_type=f32) + bu_ref[:, gs])
            al = (jnp.dot(xlo, wub_ref[:HALF, ls], preferred_element_type=f32)
                  + jnp.dot(xhi, wub_ref[HALF:, ls], preferred_element_type=f32) + bu_ref[:, ls])
            g = jnp.minimum(ag, SWIGLU_LIMIT)
            lin = jnp.clip(al, -SWIGLU_LIMIT, SWIGLU_LIMIT)
            act = (lin + 1.0) * (g * jax.nn.sigmoid(SWIGLU_ALPHA * g))
            y = y + jnp.dot(act.astype(bf), wdb_ref[gs, :], preferred_element_type=f32)
        ys_ref[...] = _pack_bf16_pairs(y)

    @pl.when(jnp.logical_not(valid))
    def _():
        ys_ref[...] = jnp.zeros_like(ys_ref)


def _experts(tile_expert, n_tiles, xs, w_up, b_up, w_down, b_down):
    rows = xs.shape[0]
    t = EXPERT_TILE
    max_tiles = rows // t
    row_map = lambda i, te, nt: (jnp.minimum(i, nt[0] - 1), 0)
    grid_spec = pltpu.PrefetchScalarGridSpec(
        num_scalar_prefetch=2,
        grid=(max_tiles,),
        in_specs=[pl.BlockSpec((t, HALF), row_map),
                  pl.BlockSpec((None, D_MODEL, 2 * D_FF), lambda i, te, nt: (te[i], 0, 0)),
                  pl.BlockSpec((None, 1, 2 * D_FF), lambda i, te, nt: (te[i], 0, 0)),
                  pl.BlockSpec((None, D_FF, D_MODEL), lambda i, te, nt: (te[i], 0, 0)),
                  pl.BlockSpec((None, 1, D_MODEL), lambda i, te, nt: (te[i], 0, 0))],
        out_specs=pl.BlockSpec((t, HALF), lambda i, te, nt: (i, 0)),
        scratch_shapes=[pltpu.VMEM((D_MODEL, 2 * D_FF), jnp.bfloat16),
                        pltpu.VMEM((D_FF, D_MODEL), jnp.bfloat16)],
    )
    return pl.pallas_call(
        _expert_kernel,
        grid_spec=grid_spec,
        out_shape=jax.ShapeDtypeStruct((rows, HALF), jnp.uint32),
        compiler_params=_cparams(("arbitrary",)),
        name="experts",
    )(tile_expert, n_tiles, xs, w_up, b_up.reshape(N_EXPERTS, 1, 2 * D_FF), w_down,
      b_down.reshape(N_EXPERTS, 1, D_MODEL))


def _combine_kernel(cur_ref, nxt_ref, ys_ref, h_ref, rf_ref, g2_ref, b2_ref, o_ref, buf_ref, sems):
    i = pl.program_id(0)
    n_steps = pl.num_programs(0)
    t = MOVE_TILE
    rows = t * TOP_K
    slot = i % 2

    @pl.when(i == 0)
    def _():
        _row_gather(cur_ref, ys_ref, buf_ref.at[0], sems.at[0], rows)

    _rows_wait(ys_ref.at[pl.ds(0, rows)], buf_ref.at[slot], sems.at[slot])
    _row_gather(nxt_ref, ys_ref, buf_ref.at[1 - slot], sems.at[1 - slot], rows)

    h = h_ref[...]
    rf = rf_ref[...]
    acc_lo = DN_ALPHA * h[:, :HALF]
    acc_hi = DN_ALPHA * h[:, HALF:]
    for k in range(TOP_K):
        lo, hi = _unpack_bf16_pairs(buf_ref[slot, k * t:(k + 1) * t, :])
        gk = rf[:, k:k + 1]
        acc_lo = acc_lo + gk * lo
        acc_hi = acc_hi + gk * hi
    u = jnp.concatenate([acc_lo, acc_hi], axis=1)
    o_ref[...] = _layer_norm(u, g2_ref[...], b2_ref[...])

    @pl.when(i == n_steps - 1)
    def _():
        _rows_wait(ys_ref.at[pl.ds(0, rows)], buf_ref.at[1 - slot], sems.at[1 - slot])


def _combine(dest3, ys, h, rf, g2, b2):
    n = h.shape[0]
    t = MOVE_TILE
    last = n // t - 1
    tok = lambda w: pl.BlockSpec((t, w), lambda i: (i, 0))
    idx_spec = lambda f: pl.BlockSpec((1, 1, t * TOP_K), f, memory_space=pltpu.SMEM)
    return pl.pallas_call(
        _combine_kernel,
        grid=(n // t,),
        in_specs=[idx_spec(lambda i: (i, 0, 0)), idx_spec(lambda i: (jnp.minimum(i + 1, last), 0, 0)),
                  pl.BlockSpec(memory_space=pl.ANY), tok(D_MODEL), tok(LANES),
                  _full(g2.shape), _full(b2.shape)],
        out_specs=tok(D_MODEL),
        out_shape=jax.ShapeDtypeStruct((n, D_MODEL), jnp.float32),
        scratch_shapes=[pltpu.VMEM((2, t * TOP_K, HALF), jnp.uint32), pltpu.SemaphoreType.DMA((2,))],
        compiler_params=_cparams(("arbitrary",)),
        name="combine",
    )(dest3, dest3, ys, h, rf, g2, b2)


def _rope_tables(positions, half):
    inv_freq = ROPE_THETA ** (-jnp.arange(half, dtype=jnp.float32) / half)
    ang = positions.astype(jnp.float32).reshape(-1, 1) * inv_freq
    reps = LANES // half
    return jnp.tile(jnp.cos(ang), (1, reps)), jnp.tile(jnp.sin(ang), (1, reps))


def _layer(x2, tabs, bsz, seq, w_in, b_gate, rms_cq, rms_ckv, w_uq, w_ukv, w_o_a, w_o_b, w_out,
           ln1_g, ln1_b, w_router, b_router, w_up, b_up, w_down, b_down, ln2_g, ln2_b):
    bf = jnp.bfloat16
    n = x2.shape[0]
    off = np.concatenate([[0], np.cumsum(SPLITS)]).tolist()
    col = lambda j: w_in[:, off[j]:off[j + 1]]
    w1 = jnp.concatenate([col(0), col(1), col(2), col(3)], axis=1).astype(bf)
    wi_pad = jnp.zeros((D_MODEL, LANES - IDX_HEADS), w_in.dtype)
    ws = jnp.concatenate([col(4), col(4), jnp.tile(col(8), (1, LANES // QK_ROPE)), col(5), wi_pad], axis=1).astype(bf)
    wl = jnp.concatenate([col(6), col(7)], axis=1).astype(bf)
    wg = col(9).astype(bf)
    uq = w_uq.reshape(Q_LORA, B_HEADS, QK_NOPE + QK_ROPE)
    wuq = jnp.concatenate([uq[:, :, :QK_NOPE].reshape(Q_LORA, -1), uq[:, :, QK_NOPE:].reshape(Q_LORA, -1)],
                          axis=1).astype(bf)
    ukv = w_ukv.reshape(KV_LORA, B_HEADS, QK_NOPE + V_DIM)
    wukv = jnp.concatenate([ukv[:, :, :QK_NOPE].reshape(KV_LORA, -1), ukv[:, :, QK_NOPE:].reshape(KV_LORA, -1)],
                           axis=1).astype(bf)

    qa, ka, va, qi, ki, kr, wi, qn, qr, kn, vb = _proj(
        x2, tabs, w1, ws, wl, rms_cq.reshape(1, -1), rms_ckv.reshape(1, -1), wuq, wukv)
    o_a = _dsa(qa, qi, wi, ka, va, ki, bsz, seq)
    o_b = _mla(qn, qr, kn, kr, vb, bsz, seq)

    wr = jnp.concatenate([w_router, jnp.zeros((D_MODEL, LANES - N_EXPERTS), w_router.dtype)], axis=1)
    br = jnp.concatenate([b_router, jnp.zeros((LANES - N_EXPERTS,), b_router.dtype)]).reshape(1, -1)
    h, hp, ri, rf, cnt = _post(x2, o_a, o_b, wg, b_gate.reshape(1, -1), w_o_a.astype(bf), w_o_b.astype(bf),
                               w_out.astype(bf), ln1_g.reshape(1, -1), ln1_b.reshape(1, -1), wr, br)

    te_rows = EXPERT_TILE
    counts = cnt[0, :N_EXPERTS].astype(jnp.int32)
    padded = ((counts + te_rows - 1) // te_rows) * te_rows
    ends = jnp.cumsum(padded)
    starts = ends - padded
    ids = ri[:, :TOP_K]
    experts = jnp.arange(N_EXPERTS, dtype=jnp.int32)
    dest = jnp.sum(jnp.where(ids[:, :, None] == experts, starts, 0), axis=-1) + ri[:, TOP_K:2 * TOP_K]
    dest3 = dest.reshape(n // MOVE_TILE, MOVE_TILE, TOP_K).transpose(0, 2, 1).reshape(
        n // MOVE_TILE, 1, MOVE_TILE * TOP_K)
    max_tiles = n * TOP_K // te_rows + N_EXPERTS
    tile_start = jnp.arange(max_tiles, dtype=jnp.int32) * te_rows
    tile_start = jnp.minimum(tile_start, ends[-1] - te_rows)
    tile_expert = jnp.sum((ends[None, :] <= tile_start[:, None]).astype(jnp.int32), axis=1)
    n_tiles = (ends[-1] // te_rows).astype(jnp.int32).reshape(1)

    xs = _dispatch(dest3, hp, max_tiles * te_rows)
    ys = _experts(tile_expert, n_tiles, xs, w_up, b_up, w_down, b_down)
    return _combine(dest3, ys, h, rf, ln2_g.reshape(1, -1), ln2_b.reshape(1, -1))


def kernel(x, positions, w_in, b_gate, rms_cq, rms_ckv, w_uq, w_ukv, w_o_a, w_o_b, w_out, ln1_g, ln1_b,
           w_router, b_router, w_up, b_up, w_down, b_down, ln2_g, ln2_b):
    bsz, seq, _ = x.shape
    x2 = x.reshape(bsz * seq, D_MODEL)
    c64, s64 = _rope_tables(positions, A_HEAD_DIM // 2)
    c32, s32 = _rope_tables(positions, QK_ROPE // 2)
    tabs = (c64, s64, c32, s32)
    for l in range(DEPTH):
        x2 = _layer(x2, tabs, bsz, seq, w_in[l], b_gate[l], rms_cq[l], rms_ckv[l], w_uq[l], w_ukv[l],
                    w_o_a[l], w_o_b[l], w_out[l], ln1_g[l], ln1_b[l], w_router[l], b_router[l],
                    w_up[l], b_up[l], w_down[l], b_down[l], ln2_g[l], ln2_b[l])
    return x2.reshape(bsz, seq, D_MODEL)
```

```python
import functools

import jax
import jax.numpy as jnp
import numpy as np
from jax import lax
from jax.experimental import pallas as pl
from jax.experimental.pallas import tpu as pltpu

D_MODEL = 1024
A_HEADS = 8
A_HEAD_DIM = 64
IDX_HEADS = 8
IDX_DIM = 64
TOPK_MAX = 256
B_HEADS = 8
Q_LORA = 384
KV_LORA = 256
QK_NOPE = 64
QK_ROPE = 32
V_DIM = 64
ROPE_THETA = 10000.0
N_EXPERTS = 32
TOP_K = 4
D_FF = 1024
SWIGLU_LIMIT = 7.0
SWIGLU_ALPHA = 1.702
DEPTH = 1
DN_ALPHA = (2 * DEPTH) ** 0.25
LN_EPS = 1e-5
RMS_EPS = 1e-6
A_WIDTH = A_HEADS * A_HEAD_DIM
B_WIDTH = B_HEADS * V_DIM
SPLITS = (A_WIDTH, A_WIDTH, A_WIDTH, IDX_HEADS * IDX_DIM, IDX_DIM, IDX_HEADS,
          Q_LORA, KV_LORA, QK_ROPE, 2 * D_MODEL)

LANES = 128
VMEM_LIMIT_BYTES = 52 * 1024 * 1024

PROJ_TILE = 512
Q_TILE = 256
EXPERT_TILE = 256
FF_CHUNK = 256
MOVE_TILE = 256

INT_MIN = -(2 ** 31)
NEG_BIG = -1e30

_NT = (((1,), (1,)), ((), ()))


def _cparams(sem):
    return pltpu.CompilerParams(dimension_semantics=sem, vmem_limit_bytes=VMEM_LIMIT_BYTES)


def _full(shape):
    nd = len(shape)
    return pl.BlockSpec(shape, lambda *_: (0,) * nd)


def _rope_chunk(x, cos, sin, half):
    lane = lax.broadcasted_iota(jnp.int32, x.shape, 1)
    first = (lane % (2 * half)) < half
    up = pltpu.roll(x, LANES - half, 1)
    dn = pltpu.roll(x, half, 1)
    return x * cos + jnp.where(first, -up, dn) * sin


def _layer_norm(u, g, b):
    mu = jnp.mean(u, axis=-1, keepdims=True)
    d = u - mu
    var = jnp.mean(d * d, axis=-1, keepdims=True)
    return d * lax.rsqrt(var + LN_EPS) * g + b


def _head_mask(shape, width, which):
    lane = lax.broadcasted_iota(jnp.int32, shape, 1)
    return (lane // width) == which


def _proj_kernel(x_ref, c64_ref, s64_ref, c32_ref, s32_ref, w1_ref, ws_ref, wl_ref,
                 gcq_ref, gckv_ref, wuq_ref, wukv_ref,
                 qa_ref, ka_ref, va_ref, qi_ref, ki_ref, kr_ref, wi_ref,
                 qn_ref, qr_ref, kn_ref, vb_ref):
    xb = x_ref[...].astype(jnp.bfloat16)
    c64, s64 = c64_ref[...], s64_ref[...]
    c32, s32 = c32_ref[...], s32_ref[...]
    f32 = jnp.float32
    bf = jnp.bfloat16

    z1 = jnp.dot(xb, w1_ref[...], preferred_element_type=f32)
    n_chunk = A_WIDTH // LANES
    a_scale = A_HEAD_DIM ** -0.5
    for c in range(n_chunk):
        sl = slice(c * LANES, (c + 1) * LANES)
        qa_ref[:, sl] = (_rope_chunk(z1[:, c * LANES:(c + 1) * LANES], c64, s64, 32) * a_scale).astype(bf)
        o = A_WIDTH + c * LANES
        ka_ref[:, sl] = _rope_chunk(z1[:, o:o + LANES], c64, s64, 32).astype(bf)
        o = 2 * A_WIDTH + c * LANES
        va_ref[:, sl] = z1[:, o:o + LANES].astype(bf)
        o = 3 * A_WIDTH + c * LANES
        qi_ref[:, sl] = _rope_chunk(z1[:, o:o + LANES], c64, s64, 32).astype(bf)

    zs = jnp.dot(xb, ws_ref[...], preferred_element_type=f32)
    ki_ref[...] = _rope_chunk(zs[:, 0:LANES], c64, s64, 32).astype(bf)
    kr_ref[...] = _rope_chunk(zs[:, LANES:2 * LANES], c32, s32, 16).astype(bf)
    wi_ref[...] = zs[:, 2 * LANES:3 * LANES] * (IDX_HEADS ** -0.5 * IDX_DIM ** -0.5)

    zl = jnp.dot(xb, wl_ref[...], preferred_element_type=f32)
    cq = zl[:, :Q_LORA]
    cq = cq * lax.rsqrt(jnp.mean(cq * cq, axis=-1, keepdims=True) + RMS_EPS) * gcq_ref[...]
    ckv = zl[:, Q_LORA:]
    ckv = ckv * lax.rsqrt(jnp.mean(ckv * ckv, axis=-1, keepdims=True) + RMS_EPS) * gckv_ref[...]
    qb = jnp.dot(cq.astype(bf), wuq_ref[...], preferred_element_type=f32)
    kv = jnp.dot(ckv.astype(bf), wukv_ref[...], preferred_element_type=f32)
    b_scale = (QK_NOPE + QK_ROPE) ** -0.5
    qn_ref[...] = (qb[:, :B_HEADS * QK_NOPE] * b_scale).astype(bf)
    for c in range(B_HEADS * QK_ROPE // LANES):
        o = B_HEADS * QK_NOPE + c * LANES
        qr_ref[:, c * LANES:(c + 1) * LANES] = (
            _rope_chunk(qb[:, o:o + LANES], c32, s32, 16) * b_scale).astype(bf)
    kn_ref[...] = kv[:, :B_HEADS * QK_NOPE].astype(bf)
    vb_ref[...] = kv[:, B_HEADS * QK_NOPE:].astype(bf)


def _proj(x2, tabs, w1, ws, wl, gcq, gckv, wuq, wukv):
    n = x2.shape[0]
    t = PROJ_TILE
    tok = lambda w: pl.BlockSpec((t, w), lambda i: (i, 0))
    bf = jnp.bfloat16
    outs = [(A_WIDTH, bf)] * 4 + [(LANES, bf), (LANES, bf), (LANES, jnp.float32),
                                  (B_HEADS * QK_NOPE, bf), (B_HEADS * QK_ROPE, bf),
                                  (B_HEADS * QK_NOPE, bf), (B_WIDTH, bf)]
    return pl.pallas_call(
        _proj_kernel,
        grid=(n // t,),
        in_specs=[tok(D_MODEL)] + [tok(LANES)] * 4 + [_full(a.shape) for a in (w1, ws, wl, gcq, gckv, wuq, wukv)],
        out_specs=[tok(w) for w, _ in outs],
        out_shape=[jax.ShapeDtypeStruct((n, w), d) for w, d in outs],
        compiler_params=_cparams(("arbitrary",)),
        name="proj",
    )(x2, *tabs, w1, ws, wl, gcq, gckv, wuq, wukv)


def _tie_select(eq, gt, need, kv):
    chunk = 256
    r = lax.broadcasted_iota(jnp.int32, (chunk, chunk), 0)
    c = lax.broadcasted_iota(jnp.int32, (chunk, chunk), 1)
    upper = jnp.where(r <= c, 1.0, 0.0).astype(jnp.bfloat16)
    carry = jnp.zeros((eq.shape[0], 1), jnp.float32)
    parts = []
    for j in range(kv // chunk):
        e = eq[:, j * chunk:(j + 1) * chunk]
        ef = jnp.where(e, 1.0, 0.0).astype(jnp.bfloat16)
        pref = jnp.dot(ef, upper, preferred_element_type=jnp.float32) + carry
        parts.append(gt[:, j * chunk:(j + 1) * chunk] | (e & (pref <= need)))
        carry = carry + jnp.sum(ef.astype(jnp.float32), axis=1, keepdims=True)
    return jnp.concatenate(parts, axis=1)


def _key_to_float(key):
    return lax.bitcast_convert_type(key ^ ((key >> 31) & jnp.int32(0x7FFFFFFF)), jnp.float32)


def _select_topk(score, k_eff, sel_ref, kv):
    tq = score.shape[0]

    def search(it, prefix):
        cand = prefix + lax.shift_left(jnp.int32(1), 31 - it)
        cnt = jnp.sum(jnp.where(score >= _key_to_float(cand), 1.0, 0.0), axis=1, keepdims=True)
        return jnp.where(cnt >= k_eff, cand, prefix)

    thr = _key_to_float(lax.fori_loop(0, 32, search, jnp.full((tq, 1), INT_MIN, jnp.int32)))
    gt = score > thr
    eq = score == thr
    n_gt = jnp.sum(jnp.where(gt, 1.0, 0.0), axis=1, keepdims=True)
    n_eq = jnp.sum(jnp.where(eq, 1.0, 0.0), axis=1, keepdims=True)
    need = k_eff - n_gt
    sel_ref[:, :kv] = jnp.where(score >= thr, 1.0, 0.0)
    has_tie = jnp.max(jnp.where(n_eq != need, 1.0, 0.0)) > 0.5

    @pl.when(has_tie)
    def _():
        sel_ref[:, :kv] = jnp.where(_tie_select(eq, gt, need, kv), 1.0, 0.0)


def _dsa_block(qa_ref, qi_ref, wi_ref, ka_ref, va_ref, ki_ref, o_ref, sel_ref, score_ref, *, kv, k_sel):
    tq = qa_ref.shape[0]
    f32 = jnp.float32
    bf = jnp.bfloat16
    q_pos = (kv - tq) + lax.broadcasted_iota(jnp.int32, (tq, 1), 0)
    key_pos = lax.broadcasted_iota(jnp.int32, (tq, kv), 1)
    causal = key_pos <= q_pos

    if kv <= k_sel:
        sel_ref[:, :kv] = jnp.where(causal, 1.0, 0.0)
    else:
        ki = ki_ref[:kv, :]
        lane = lax.broadcasted_iota(jnp.int32, (tq, LANES), 1)
        score_ref[:, :kv] = jnp.zeros((tq, kv), f32)

        def idx_slab(c, carry):
            qc = qi_ref[:, pl.ds(pl.multiple_of(c * LANES, LANES), LANES)]
            acc = score_ref[:, :kv]
            for hh in range(2):
                qm = jnp.where(_head_mask(qc.shape, IDX_DIM, hh), qc, jnp.zeros_like(qc))
                s = lax.dot_general(qm, ki, _NT, preferred_element_type=f32)
                w = jnp.sum(jnp.where(lane == 2 * c + hh, wi_ref[...], 0.0), axis=1, keepdims=True)
                acc = acc + jnp.maximum(s, 0.0) * w
            score_ref[:, :kv] = acc
            return carry

        lax.fori_loop(0, IDX_HEADS // 2, idx_slab, 0)
        score = jnp.where(causal, score_ref[:, :kv], -jnp.inf)
        k_eff = jnp.minimum(q_pos + 1, k_sel).astype(f32)
        _select_topk(score, k_eff, sel_ref, kv)

    def attn_slab(c, carry):
        sl = pl.ds(pl.multiple_of(c * LANES, LANES), LANES)
        qc = qa_ref[:, sl]
        kc = ka_ref[:kv, sl]
        vc = va_ref[:kv, sl]
        sel = sel_ref[:, :kv] > 0.5
        outs = []
        for hh in range(2):
            qm = jnp.where(_head_mask(qc.shape, A_HEAD_DIM, hh), qc, jnp.zeros_like(qc))
            lg = lax.dot_general(qm, kc, _NT, preferred_element_type=f32)
            lg = jnp.where(sel, lg, NEG_BIG)
            m = jnp.max(lg, axis=1, keepdims=True)
            p = jnp.exp(lg - m)
            den = jnp.sum(p, axis=1, keepdims=True)
            pv = jnp.dot(p.astype(bf), vc, preferred_element_type=f32)
            outs.append(pv / den)
        o_ref[:, sl] = jnp.where(_head_mask(outs[0].shape, A_HEAD_DIM, 0), outs[0], outs[1]).astype(bf)
        return carry

    lax.fori_loop(0, A_WIDTH // LANES, attn_slab, 0)


def _per_query_block(block_fn, refs, n_blocks, tq):
    j = pl.program_id(1)
    for b in range(n_blocks):
        pl.when(j == b)(functools.partial(block_fn, *refs, kv=(b + 1) * tq))


def _dsa_kernel(*refs, seq, k_sel):
    tq = refs[0].shape[0]
    _per_query_block(functools.partial(_dsa_block, k_sel=k_sel), refs, seq // tq, tq)


def _dsa(qa, qi, wi, ka, va, ki, bsz, seq):
    tq = min(Q_TILE, seq)
    nq = seq // tq
    qspec = lambda w: pl.BlockSpec((tq, w), lambda b, i: (b * nq + i, 0))
    kspec = lambda w: pl.BlockSpec((seq, w), lambda b, i: (b, 0))
    k_sel = min(TOPK_MAX, seq // 4)
    return pl.pallas_call(
        functools.partial(_dsa_kernel, seq=seq, k_sel=k_sel),
        grid=(bsz, nq),
        in_specs=[qspec(A_WIDTH), qspec(A_WIDTH), qspec(LANES), kspec(A_WIDTH), kspec(A_WIDTH), kspec(LANES)],
        out_specs=qspec(A_WIDTH),
        out_shape=jax.ShapeDtypeStruct((bsz * seq, A_WIDTH), jnp.bfloat16),
        scratch_shapes=[pltpu.VMEM((tq, seq), jnp.float32), pltpu.VMEM((tq, seq), jnp.float32)],
        compiler_params=_cparams(("arbitrary", "arbitrary")),
        name="dsa",
    )(qa, qi, wi, ka, va, ki)


def _mla_block(qn_ref, qr_ref, kn_ref, kr_ref, vb_ref, o_ref, *, kv):
    tq = qn_ref.shape[0]
    f32 = jnp.float32
    bf = jnp.bfloat16
    q_pos = (kv - tq) + lax.broadcasted_iota(jnp.int32, (tq, 1), 0)
    key_pos = lax.broadcasted_iota(jnp.int32, (tq, kv), 1)
    causal = key_pos <= q_pos
    kr = kr_ref[:kv, :]

    def slab(c, carry):
        sl = pl.ds(pl.multiple_of(c * LANES, LANES), LANES)
        qc = qn_ref[:, sl]
        kcat = jnp.concatenate([kn_ref[:kv, sl], kr], axis=1)
        vc = vb_ref[:kv, sl]
        qrc = qr_ref[:, pl.ds(pl.multiple_of((c // 2) * LANES, LANES), LANES)]
        outs = []
        for hh in range(2):
            h = 2 * c + hh
            qm = jnp.where(_head_mask(qc.shape, QK_NOPE, hh), qc, jnp.zeros_like(qc))
            qrm = jnp.where(_head_mask(qrc.shape, QK_ROPE, h % 4), qrc, jnp.zeros_like(qrc))
            qcat = jnp.concatenate([qm, qrm], axis=1)
            lg = lax.dot_general(qcat, kcat, _NT, preferred_element_type=f32)
            lg = jnp.where(causal, lg, NEG_BIG)
            m = jnp.max(lg, axis=1, keepdims=True)
            p = jnp.exp(lg - m)
            den = jnp.sum(p, axis=1, keepdims=True)
            pv = jnp.dot(p.astype(bf), vc, preferred_element_type=f32)
            outs.append(pv / den)
        o_ref[:, sl] = jnp.where(_head_mask(outs[0].shape, V_DIM, 0), outs[0], outs[1]).astype(bf)
        return carry

    lax.fori_loop(0, B_WIDTH // LANES, slab, 0, unroll=2)


def _mla_kernel(*refs, seq):
    tq = refs[0].shape[0]
    _per_query_block(_mla_block, refs, seq // tq, tq)


def _mla(qn, qr, kn, kr, vb, bsz, seq):
    tq = min(Q_TILE, seq)
    nq = seq // tq
    qspec = lambda w: pl.BlockSpec((tq, w), lambda b, i: (b * nq + i, 0))
    kspec = lambda w: pl.BlockSpec((seq, w), lambda b, i: (b, 0))
    return pl.pallas_call(
        functools.partial(_mla_kernel, seq=seq),
        grid=(bsz, nq),
        in_specs=[qspec(B_HEADS * QK_NOPE), qspec(B_HEADS * QK_ROPE), kspec(B_HEADS * QK_NOPE),
                  kspec(LANES), kspec(B_WIDTH)],
        out_specs=qspec(B_WIDTH),
        out_shape=jax.ShapeDtypeStruct((bsz * seq, B_WIDTH), jnp.bfloat16),
        compiler_params=_cparams(("arbitrary", "arbitrary")),
        name="mla",
    )(qn, qr, kn, kr, vb)


def _post_kernel(x_ref, oa_ref, ob_ref, wg_ref, bg_ref, woa_ref, wob_ref, wout_ref, g1_ref, b1_ref,
                 wr_ref, br_ref, h_ref, ri_ref, rf_ref, cnt_ref, carry_ref):
    f32 = jnp.float32
    bf = jnp.bfloat16
    t = x_ref.shape[0]

    @pl.when(pl.program_id(0) == 0)
    def _():
        carry_ref[...] = jnp.zeros_like(carry_ref)

    x = x_ref[...]
    gate = jax.nn.sigmoid(jnp.dot(x.astype(bf), wg_ref[...], preferred_element_type=f32) + bg_ref[...])
    pa = jnp.dot(oa_ref[...], woa_ref[...], preferred_element_type=f32)
    pb = jnp.dot(ob_ref[...], wob_ref[...], preferred_element_type=f32)
    mix = gate[:, :D_MODEL] * pa + gate[:, D_MODEL:] * pb
    u = DN_ALPHA * x + jnp.dot(mix.astype(bf), wout_ref[...], preferred_element_type=f32)
    h = _layer_norm(u, g1_ref[...], b1_ref[...])
    h_ref[...] = h

    logits = jnp.dot(h, wr_ref[...], preferred_element_type=f32,
                     precision=lax.Precision.HIGHEST) + br_ref[...]
    lane = lax.broadcasted_iota(jnp.int32, (t, LANES), 1).astype(f32)
    lg = jnp.where(lane < N_EXPERTS, logits, -jnp.inf)
    vals, ids = [], []
    assign = jnp.zeros((t, LANES), f32)
    for _k in range(TOP_K):
        m = jnp.max(lg, axis=1, keepdims=True)
        idx = jnp.min(jnp.where(lg == m, lane, float(LANES)), axis=1, keepdims=True)
        hit = lane == idx
        vals.append(m)
        ids.append(idx)
        assign = jnp.where(hit, 1.0, assign)
        lg = jnp.where(hit, -jnp.inf, lg)
    exps = [jnp.exp(v - vals[0]) for v in vals]
    den = exps[0] + exps[1] + exps[2] + exps[3]

    r = lax.broadcasted_iota(jnp.int32, (t, t), 0)
    c = lax.broadcasted_iota(jnp.int32, (t, t), 1)
    lower = jnp.where(c < r, 1.0, 0.0).astype(bf)
    before = jnp.dot(lower, assign.astype(bf), preferred_element_type=f32) + carry_ref[0:1, :]
    ri = jnp.zeros((t, LANES), f32)
    rf = jnp.zeros((t, LANES), f32)
    for k in range(TOP_K):
        rank = jnp.sum(jnp.where(lane == ids[k], before, 0.0), axis=1, keepdims=True)
        ri = jnp.where(lane == float(k), ids[k], ri)
        ri = jnp.where(lane == float(TOP_K + k), rank, ri)
        rf = jnp.where(lane == float(k), exps[k] / den, rf)
    ri_ref[...] = ri.astype(jnp.int32)
    rf_ref[...] = rf
    total = carry_ref[0:1, :] + jnp.sum(assign, axis=0, keepdims=True)
    carry_ref[...] = jnp.broadcast_to(total, carry_ref.shape)
    cnt_ref[...] = jnp.broadcast_to(total, cnt_ref.shape)


def _post(x2, oa, ob, wg, bg, woa, wob, wout, g1, b1, wr, br):
    n = x2.shape[0]
    t = PROJ_TILE
    tok = lambda w: pl.BlockSpec((t, w), lambda i: (i, 0))
    consts = (wg, bg, woa, wob, wout, g1, b1, wr, br)
    return pl.pallas_call(
        _post_kernel,
        grid=(n // t,),
        in_specs=[tok(D_MODEL), tok(A_WIDTH), tok(B_WIDTH)] + [_full(a.shape) for a in consts],
        out_specs=[tok(D_MODEL), tok(LANES), tok(LANES), _full((8, LANES))],
        out_shape=[jax.ShapeDtypeStruct((n, D_MODEL), jnp.float32),
                   jax.ShapeDtypeStruct((n, LANES), jnp.int32),
                   jax.ShapeDtypeStruct((n, LANES), jnp.float32),
                   jax.ShapeDtypeStruct((8, LANES), jnp.float32)],
        scratch_shapes=[pltpu.VMEM((8, LANES), jnp.float32)],
        compiler_params=_cparams(("arbitrary",)),
        name="post",
    )(x2, oa, ob, *consts)


def _row_gather(idx_ref, src_ref, dst_ref, sem, rows):
    for j in range(rows):
        pltpu.make_async_copy(src_ref.at[pl.ds(idx_ref[0, 0, j], 1)], dst_ref.at[pl.ds(j, 1)], sem).start()


def _rows_wait(src_ref, dst_ref, sem):
    pltpu.make_async_copy(src_ref, dst_ref, sem).wait()


def _dispatch_kernel(dest_ref, h_ref, init_ref, xs_ref, sem):
    del init_ref
    t = MOVE_TILE
    for k in range(TOP_K):
        for j in range(t):
            pltpu.make_async_copy(h_ref.at[pl.ds(j, 1)], xs_ref.at[pl.ds(dest_ref[0, 0, k * t + j], 1)],
                                  sem).start()
    for k in range(TOP_K):
        _rows_wait(h_ref, xs_ref.at[pl.ds(0, t)], sem)


def _dispatch(dest3, h, rows):
    n = h.shape[0]
    t = MOVE_TILE
    return pl.pallas_call(
        _dispatch_kernel,
        grid=(n // t,),
        in_specs=[pl.BlockSpec((1, 1, t * TOP_K), lambda i: (i, 0, 0), memory_space=pltpu.SMEM),
                  pl.BlockSpec((t, D_MODEL), lambda i: (i, 0)), pl.BlockSpec(memory_space=pl.ANY)],
        out_specs=pl.BlockSpec(memory_space=pl.ANY),
        out_shape=jax.ShapeDtypeStruct((rows, D_MODEL), jnp.float32),
        input_output_aliases={2: 0},
        scratch_shapes=[pltpu.SemaphoreType.DMA(())],
        compiler_params=_cparams(("arbitrary",)),
        name="dispatch",
    )(dest3, h, jnp.zeros((rows, D_MODEL), jnp.float32))


def _expert_kernel(te_ref, nt_ref, xs_ref, wu_ref, bu_ref, wd_ref, bd_ref, ys_ref, wub_ref, wdb_ref):
    i = pl.program_id(0)
    f32 = jnp.float32
    bf = jnp.bfloat16
    t = EXPERT_TILE
    valid = i < nt_ref[0]
    prev = te_ref[jnp.maximum(i - 1, 0)]
    fresh = valid & ((i == 0) | (te_ref[i] != prev))

    @pl.when(fresh)
    def _():
        rows = 128

        def cast_up(j, c):
            r = pl.multiple_of(j * rows, rows)
            wub_ref[pl.ds(r, rows), :] = wu_ref[pl.ds(r, rows), :].astype(bf)
            return c

        lax.fori_loop(0, D_MODEL // rows, cast_up, 0)

        def cast_dn(j, c):
            r = pl.multiple_of(j * rows, rows)
            wdb_ref[pl.ds(r, rows), :] = wd_ref[pl.ds(r, rows), :].astype(bf)
            return c

        lax.fori_loop(0, D_FF // rows, cast_dn, 0)

    @pl.when(valid)
    def _():
        xb = xs_ref[...].astype(bf)
        y = jnp.broadcast_to(bd_ref[...], (t, D_MODEL))
        for c in range(D_FF // FF_CHUNK):
            gs = slice(c * FF_CHUNK, (c + 1) * FF_CHUNK)
            ls = slice(D_FF + c * FF_CHUNK, D_FF + (c + 1) * FF_CHUNK)
            ag = jnp.dot(xb, wub_ref[:, gs], preferred_element_type=f32) + bu_ref[:, gs]
            al = jnp.dot(xb, wub_ref[:, ls], preferred_element_type=f32) + bu_ref[:, ls]
            g = jnp.minimum(ag, SWIGLU_LIMIT)
            lin = jnp.clip(al, -SWIGLU_LIMIT, SWIGLU_LIMIT)
            act = (lin + 1.0) * (g * jax.nn.sigmoid(SWIGLU_ALPHA * g))
            y = y + jnp.dot(act.astype(bf), wdb_ref[gs, :], preferred_element_type=f32)
        ys_ref[...] = y

    @pl.when(jnp.logical_not(valid))
    def _():
        ys_ref[...] = jnp.zeros_like(ys_ref)


def _experts(tile_expert, n_tiles, xs, w_up, b_up, w_down, b_down):
    rows = xs.shape[0]
    t = EXPERT_TILE
    max_tiles = rows // t
    row_map = lambda i, te, nt: (jnp.minimum(i, nt[0] - 1), 0)
    grid_spec = pltpu.PrefetchScalarGridSpec(
        num_scalar_prefetch=2,
        grid=(max_tiles,),
        in_specs=[pl.BlockSpec((t, D_MODEL), row_map),
                  pl.BlockSpec((None, D_MODEL, 2 * D_FF), lambda i, te, nt: (te[i], 0, 0)),
                  pl.BlockSpec((None, 1, 2 * D_FF), lambda i, te, nt: (te[i], 0, 0)),
                  pl.BlockSpec((None, D_FF, D_MODEL), lambda i, te, nt: (te[i], 0, 0)),
                  pl.BlockSpec((None, 1, D_MODEL), lambda i, te, nt: (te[i], 0, 0))],
        out_specs=pl.BlockSpec((t, D_MODEL), lambda i, te, nt: (i, 0)),
        scratch_shapes=[pltpu.VMEM((D_MODEL, 2 * D_FF), jnp.bfloat16),
                        pltpu.VMEM((D_FF, D_MODEL), jnp.bfloat16)],
    )
    return pl.pallas_call(
        _expert_kernel,
        grid_spec=grid_spec,
        out_shape=jax.ShapeDtypeStruct((rows, D_MODEL), jnp.float32),
        compiler_params=_cparams(("arbitrary",)),
        name="experts",
    )(tile_expert, n_tiles, xs, w_up, b_up.reshape(N_EXPERTS, 1, 2 * D_FF), w_down,
      b_down.reshape(N_EXPERTS, 1, D_MODEL))


def _combine_kernel(cur_ref, nxt_ref, ys_ref, h_ref, rf_ref, g2_ref, b2_ref, o_ref, buf_ref, sems):
    i = pl.program_id(0)
    n_steps = pl.num_programs(0)
    t = MOVE_TILE
    rows = t * TOP_K
    slot = i % 2

    @pl.when(i == 0)
    def _():
        _row_gather(cur_ref, ys_ref, buf_ref.at[0], sems.at[0], rows)

    _rows_wait(ys_ref.at[pl.ds(0, rows)], buf_ref.at[slot], sems.at[slot])
    _row_gather(nxt_ref, ys_ref, buf_ref.at[1 - slot], sems.at[1 - slot], rows)

    rf = rf_ref[...]
    u = DN_ALPHA * h_ref[...]
    for k in range(TOP_K):
        u = u + rf[:, k:k + 1] * buf_ref[slot, k * t:(k + 1) * t, :]
    o_ref[...] = _layer_norm(u, g2_ref[...], b2_ref[...])

    @pl.when(i == n_steps - 1)
    def _():
        _rows_wait(ys_ref.at[pl.ds(0, rows)], buf_ref.at[1 - slot], sems.at[1 - slot])


def _combine(dest3, ys, h, rf, g2, b2):
    n = h.shape[0]
    t = MOVE_TILE
    last = n // t - 1
    tok = lambda w: pl.BlockSpec((t, w), lambda i: (i, 0))
    idx_spec = lambda f: pl.BlockSpec((1, 1, t * TOP_K), f, memory_space=pltpu.SMEM)
    return pl.pallas_call(
        _combine_kernel,
        grid=(n // t,),
        in_specs=[idx_spec(lambda i: (i, 0, 0)), idx_spec(lambda i: (jnp.minimum(i + 1, last), 0, 0)),
                  pl.BlockSpec(memory_space=pl.ANY), tok(D_MODEL), tok(LANES),
                  _full(g2.shape), _full(b2.shape)],
        out_specs=tok(D_MODEL),
        out_shape=jax.ShapeDtypeStruct((n, D_MODEL), jnp.float32),
        scratch_shapes=[pltpu.VMEM((2, t * TOP_K, D_MODEL), jnp.float32), pltpu.SemaphoreType.DMA((2,))],
        compiler_params=_cparams(("arbitrary",)),
        name="combine",
    )(dest3, dest3, ys, h, rf, g2, b2)


def _rope_tables(positions, half):
    inv_freq = ROPE_THETA ** (-jnp.arange(half, dtype=jnp.float32) / half)
    ang = positions.astype(jnp.float32).reshape(-1, 1) * inv_freq
    reps = LANES // half
    return jnp.tile(jnp.cos(ang), (1, reps)), jnp.tile(jnp.sin(ang), (1, reps))


def _layer(x2, tabs, bsz, seq, w_in, b_gate, rms_cq, rms_ckv, w_uq, w_ukv, w_o_a, w_o_b, w_out,
           ln1_g, ln1_b, w_router, b_router, w_up, b_up, w_down, b_down, ln2_g, ln2_b):
    bf = jnp.bfloat16
    n = x2.shape[0]
    off = np.concatenate([[0], np.cumsum(SPLITS)]).tolist()
    col = lambda j: w_in[:, off[j]:off[j + 1]]
    w1 = jnp.concatenate([col(0), col(1), col(2), col(3)], axis=1).astype(bf)
    wi_pad = jnp.zeros((D_MODEL, LANES - IDX_HEADS), w_in.dtype)
    ws = jnp.concatenate([col(4), col(4), jnp.tile(col(8), (1, LANES // QK_ROPE)), col(5), wi_pad], axis=1).astype(bf)
    wl = jnp.concatenate([col(6), col(7)], axis=1).astype(bf)
    wg = col(9).astype(bf)
    uq = w_uq.reshape(Q_LORA, B_HEADS, QK_NOPE + QK_ROPE)
    wuq = jnp.concatenate([uq[:, :, :QK_NOPE].reshape(Q_LORA, -1), uq[:, :, QK_NOPE:].reshape(Q_LORA, -1)],
                          axis=1).astype(bf)
    ukv = w_ukv.reshape(KV_LORA, B_HEADS, QK_NOPE + V_DIM)
    wukv = jnp.concatenate([ukv[:, :, :QK_NOPE].reshape(KV_LORA, -1), ukv[:, :, QK_NOPE:].reshape(KV_LORA, -1)],
                           axis=1).astype(bf)

    qa, ka, va, qi, ki, kr, wi, qn, qr, kn, vb = _proj(
        x2, tabs, w1, ws, wl, rms_cq.reshape(1, -1), rms_ckv.reshape(1, -1), wuq, wukv)
    o_a = _dsa(qa, qi, wi, ka, va, ki, bsz, seq)
    o_b = _mla(qn, qr, kn, kr, vb, bsz, seq)

    wr = jnp.concatenate([w_router, jnp.zeros((D_MODEL, LANES - N_EXPERTS), w_router.dtype)], axis=1)
    br = jnp.concatenate([b_router, jnp.zeros((LANES - N_EXPERTS,), b_router.dtype)]).reshape(1, -1)
    h, ri, rf, cnt = _post(x2, o_a, o_b, wg, b_gate.reshape(1, -1), w_o_a.astype(bf), w_o_b.astype(bf),
                               w_out.astype(bf), ln1_g.reshape(1, -1), ln1_b.reshape(1, -1), wr, br)

    te_rows = EXPERT_TILE
    counts = cnt[0, :N_EXPERTS].astype(jnp.int32)
    padded = ((counts + te_rows - 1) // te_rows) * te_rows
    ends = jnp.cumsum(padded)
    starts = ends - padded
    ids = ri[:, :TOP_K]
    experts = jnp.arange(N_EXPERTS, dtype=jnp.int32)
    dest = jnp.sum(jnp.where(ids[:, :, None] == experts, starts, 0), axis=-1) + ri[:, TOP_K:2 * TOP_K]
    dest3 = dest.reshape(n // MOVE_TILE, MOVE_TILE, TOP_K).transpose(0, 2, 1).reshape(
        n // MOVE_TILE, 1, MOVE_TILE * TOP_K)
    max_tiles = n * TOP_K // te_rows + N_EXPERTS
    tile_start = jnp.arange(max_tiles, dtype=jnp.int32) * te_rows
    tile_start = jnp.minimum(tile_start, ends[-1] - te_rows)
    tile_expert = jnp.sum((ends[None, :] <= tile_start[:, None]).astype(jnp.int32), axis=1)
    n_tiles = (ends[-1] // te_rows).astype(jnp.int32).reshape(1)

    xs = _dispatch(dest3, h, max_tiles * te_rows)
    ys = _experts(tile_expert, n_tiles, xs, w_up, b_up, w_down, b_down)
    return _combine(dest3, ys, h, rf, ln2_g.reshape(1, -1), ln2_b.reshape(1, -1))


def kernel(x, positions, w_in, b_gate, rms_cq, rms_ckv, w_uq, w_ukv, w_o_a, w_o_b, w_out, ln1_g, ln1_b,
           w_router, b_router, w_up, b_up, w_down, b_down, ln2_g, ln2_b):
    bsz, seq, _ = x.shape
    x2 = x.reshape(bsz * seq, D_MODEL)
    c64, s64 = _rope_tables(positions, A_HEAD_DIM // 2)
    c32, s32 = _rope_tables(positions, QK_ROPE // 2)
    tabs = (c64, s64, c32, s32)
    for l in range(DEPTH):
        x2 = _layer(x2, tabs, bsz, seq, w_in[l], b_gate[l], rms_cq[l], rms_ckv[l], w_uq[l], w_ukv[l],
                    w_o_a[l], w_o_b[l], w_out[l], ln1_g[l], ln1_b[l], w_router[l], b_router[l],
                    w_up[l], b_up[l], w_down[l], b_down[l], ln2_g[l], ln2_b[l])
    return x2.reshape(bsz, seq, D_MODEL)
```

```python
import functools

import jax
import jax.numpy as jnp
import numpy as np
from jax import lax
from jax.experimental import pallas as pl
from jax.experimental.pallas import tpu as pltpu

D_MODEL = 1024
A_HEADS = 8
A_HEAD_DIM = 64
IDX_HEADS = 8
IDX_DIM = 64
TOPK_MAX = 256
B_HEADS = 8
Q_LORA = 384
KV_LORA = 256
QK_NOPE = 64
QK_ROPE = 32
V_DIM = 64
ROPE_THETA = 10000.0
N_EXPERTS = 32
TOP_K = 4
D_FF = 1024
SWIGLU_LIMIT = 7.0
SWIGLU_ALPHA = 1.702
DEPTH = 1
DN_ALPHA = (2 * DEPTH) ** 0.25
LN_EPS = 1e-5
RMS_EPS = 1e-6
A_WIDTH = A_HEADS * A_HEAD_DIM
B_WIDTH = B_HEADS * V_DIM
SPLITS = (A_WIDTH, A_WIDTH, A_WIDTH, IDX_HEADS * IDX_DIM, IDX_DIM, IDX_HEADS,
          Q_LORA, KV_LORA, QK_ROPE, 2 * D_MODEL)

LANES = 128
VMEM_LIMIT_BYTES = 52 * 1024 * 1024

PROJ_TILE = 512
Q_TILE = 256
EXPERT_TILE = 256
FF_CHUNK = 256
MOVE_TILE = 256

ROW_TILES = D_MODEL // LANES
assert ROW_TILES == 8

INT_MIN = -(2 ** 31)
NEG_BIG = -1e30

_NT = (((1,), (1,)), ((), ()))


def _cparams(sem):
    return pltpu.CompilerParams(dimension_semantics=sem, vmem_limit_bytes=VMEM_LIMIT_BYTES)


def _full(shape):
    nd = len(shape)
    return pl.BlockSpec(shape, lambda *_: (0,) * nd)


def _rope_chunk(x, cos, sin, half):
    lane = lax.broadcasted_iota(jnp.int32, x.shape, 1)
    first = (lane % (2 * half)) < half
    up = pltpu.roll(x, LANES - half, 1)
    dn = pltpu.roll(x, half, 1)
    return x * cos + jnp.where(first, -up, dn) * sin


def _layer_norm(u, g, b):
    mu = jnp.mean(u, axis=-1, keepdims=True)
    d = u - mu
    var = jnp.mean(d * d, axis=-1, keepdims=True)
    return d * lax.rsqrt(var + LN_EPS) * g + b


def _load_token_rows(ref, first, t):
    return jnp.concatenate([ref[pl.ds(first * ROW_TILES + s, t, stride=ROW_TILES), :] for s in range(ROW_TILES)],
                           axis=1)


def _store_token_rows(ref, v):
    t = v.shape[0]
    for s in range(ROW_TILES):
        ref[pl.ds(s, t, stride=ROW_TILES), :] = v[:, s * LANES:(s + 1) * LANES]


def _head_mask(shape, width, which):
    lane = lax.broadcasted_iota(jnp.int32, shape, 1)
    return (lane // width) == which


def _proj_kernel(x_ref, c64_ref, s64_ref, c32_ref, s32_ref, w1_ref, ws_ref, wl_ref,
                 gcq_ref, gckv_ref, wuq_ref, wukv_ref,
                 qa_ref, ka_ref, va_ref, qi_ref, ki_ref, kr_ref, wi_ref,
                 qn_ref, qr_ref, kn_ref, vb_ref):
    xb = x_ref[...].astype(jnp.bfloat16)
    c64, s64 = c64_ref[...], s64_ref[...]
    c32, s32 = c32_ref[...], s32_ref[...]
    f32 = jnp.float32
    bf = jnp.bfloat16

    z1 = jnp.dot(xb, w1_ref[...], preferred_element_type=f32)
    n_chunk = A_WIDTH // LANES
    a_scale = A_HEAD_DIM ** -0.5
    for c in range(n_chunk):
        sl = slice(c * LANES, (c + 1) * LANES)
        qa_ref[:, sl] = (_rope_chunk(z1[:, c * LANES:(c + 1) * LANES], c64, s64, 32) * a_scale).astype(bf)
        o = A_WIDTH + c * LANES
        ka_ref[:, sl] = _rope_chunk(z1[:, o:o + LANES], c64, s64, 32).astype(bf)
        o = 2 * A_WIDTH + c * LANES
        va_ref[:, sl] = z1[:, o:o + LANES].astype(bf)
        o = 3 * A_WIDTH + c * LANES
        qi_ref[:, sl] = _rope_chunk(z1[:, o:o + LANES], c64, s64, 32).astype(bf)

    zs = jnp.dot(xb, ws_ref[...], preferred_element_type=f32)
    ki_ref[...] = _rope_chunk(zs[:, 0:LANES], c64, s64, 32).astype(bf)
    kr_ref[...] = _rope_chunk(zs[:, LANES:2 * LANES], c32, s32, 16).astype(bf)
    wi_ref[...] = zs[:, 2 * LANES:3 * LANES] * (IDX_HEADS ** -0.5 * IDX_DIM ** -0.5)

    zl = jnp.dot(xb, wl_ref[...], preferred_element_type=f32)
    cq = zl[:, :Q_LORA]
    cq = cq * lax.rsqrt(jnp.mean(cq * cq, axis=-1, keepdims=True) + RMS_EPS) * gcq_ref[...]
    ckv = zl[:, Q_LORA:]
    ckv = ckv * lax.rsqrt(jnp.mean(ckv * ckv, axis=-1, keepdims=True) + RMS_EPS) * gckv_ref[...]
    qb = jnp.dot(cq.astype(bf), wuq_ref[...], preferred_element_type=f32)
    kv = jnp.dot(ckv.astype(bf), wukv_ref[...], preferred_element_type=f32)
    b_scale = (QK_NOPE + QK_ROPE) ** -0.5
    qn_ref[...] = (qb[:, :B_HEADS * QK_NOPE] * b_scale).astype(bf)
    for c in range(B_HEADS * QK_ROPE // LANES):
        o = B_HEADS * QK_NOPE + c * LANES
        qr_ref[:, c * LANES:(c + 1) * LANES] = (
            _rope_chunk(qb[:, o:o + LANES], c32, s32, 16) * b_scale).astype(bf)
    kn_ref[...] = kv[:, :B_HEADS * QK_NOPE].astype(bf)
    vb_ref[...] = kv[:, B_HEADS * QK_NOPE:].astype(bf)


def _proj(x2, tabs, w1, ws, wl, gcq, gckv, wuq, wukv):
    n = x2.shape[0]
    t = PROJ_TILE
    tok = lambda w: pl.BlockSpec((t, w), lambda i: (i, 0))
    bf = jnp.bfloat16
    outs = [(A_WIDTH, bf)] * 4 + [(LANES, bf), (LANES, bf), (LANES, jnp.float32),
                                  (B_HEADS * QK_NOPE, bf), (B_HEADS * QK_ROPE, bf),
                                  (B_HEADS * QK_NOPE, bf), (B_WIDTH, bf)]
    return pl.pallas_call(
        _proj_kernel,
        grid=(n // t,),
        in_specs=[tok(D_MODEL)] + [tok(LANES)] * 4 + [_full(a.shape) for a in (w1, ws, wl, gcq, gckv, wuq, wukv)],
        out_specs=[tok(w) for w, _ in outs],
        out_shape=[jax.ShapeDtypeStruct((n, w), d) for w, d in outs],
        compiler_params=_cparams(("arbitrary",)),
        name="proj",
    )(x2, *tabs, w1, ws, wl, gcq, gckv, wuq, wukv)


def _tie_select(eq, gt, need, kv):
    chunk = 256
    r = lax.broadcasted_iota(jnp.int32, (chunk, chunk), 0)
    c = lax.broadcasted_iota(jnp.int32, (chunk, chunk), 1)
    upper = jnp.where(r <= c, 1.0, 0.0).astype(jnp.bfloat16)
    carry = jnp.zeros((eq.shape[0], 1), jnp.float32)
    parts = []
    for j in range(kv // chunk):
        e = eq[:, j * chunk:(j + 1) * chunk]
        ef = jnp.where(e, 1.0, 0.0).astype(jnp.bfloat16)
        pref = jnp.dot(ef, upper, preferred_element_type=jnp.float32) + carry
        parts.append(gt[:, j * chunk:(j + 1) * chunk] | (e & (pref <= need)))
        carry = carry + jnp.sum(ef.astype(jnp.float32), axis=1, keepdims=True)
    return jnp.concatenate(parts, axis=1)


def _key_to_float(key):
    return lax.bitcast_convert_type(key ^ ((key >> 31) & jnp.int32(0x7FFFFFFF)), jnp.float32)


def _select_topk(score, k_eff, sel_ref, kv):
    tq = score.shape[0]

    def search(it, prefix):
        cand = prefix + lax.shift_left(jnp.int32(1), 31 - it)
        cnt = jnp.sum(jnp.where(score >= _key_to_float(cand), 1.0, 0.0), axis=1, keepdims=True)
        return jnp.where(cnt >= k_eff, cand, prefix)

    thr = _key_to_float(lax.fori_loop(0, 32, search, jnp.full((tq, 1), INT_MIN, jnp.int32)))
    gt = score > thr
    eq = score == thr
    n_gt = jnp.sum(jnp.where(gt, 1.0, 0.0), axis=1, keepdims=True)
    n_eq = jnp.sum(jnp.where(eq, 1.0, 0.0), axis=1, keepdims=True)
    need = k_eff - n_gt
    sel_ref[:, :kv] = jnp.where(score >= thr, 1.0, 0.0)
    has_tie = jnp.max(jnp.where(n_eq != need, 1.0, 0.0)) > 0.5

    @pl.when(has_tie)
    def _():
        sel_ref[:, :kv] = jnp.where(_tie_select(eq, gt, need, kv), 1.0, 0.0)


def _dsa_block(qa_ref, qi_ref, wi_ref, ka_ref, va_ref, ki_ref, o_ref, sel_ref, score_ref, *, kv, k_sel):
    tq = qa_ref.shape[0]
    f32 = jnp.float32
    bf = jnp.bfloat16
    q_pos = (kv - tq) + lax.broadcasted_iota(jnp.int32, (tq, 1), 0)
    key_pos = lax.broadcasted_iota(jnp.int32, (tq, kv), 1)
    causal = key_pos <= q_pos

    if kv <= k_sel:
        sel_ref[:, :kv] = jnp.where(causal, 1.0, 0.0)
    else:
        ki = ki_ref[:kv, :]
        lane = lax.broadcasted_iota(jnp.int32, (tq, LANES), 1)
        score_ref[:, :kv] = jnp.zeros((tq, kv), f32)

        def idx_slab(c, carry):
            qc = qi_ref[:, pl.ds(pl.multiple_of(c * LANES, LANES), LANES)]
            acc = score_ref[:, :kv]
            for hh in range(2):
                qm = jnp.where(_head_mask(qc.shape, IDX_DIM, hh), qc, jnp.zeros_like(qc))
                s = lax.dot_general(qm, ki, _NT, preferred_element_type=f32)
                w = jnp.sum(jnp.where(lane == 2 * c + hh, wi_ref[...], 0.0), axis=1, keepdims=True)
                acc = acc + jnp.maximum(s, 0.0) * w
            score_ref[:, :kv] = acc
            return carry

        lax.fori_loop(0, IDX_HEADS // 2, idx_slab, 0)
        score = jnp.where(causal, score_ref[:, :kv], -jnp.inf)
        k_eff = jnp.minimum(q_pos + 1, k_sel).astype(f32)
        _select_topk(score, k_eff, sel_ref, kv)

    def attn_slab(c, carry):
        sl = pl.ds(pl.multiple_of(c * LANES, LANES), LANES)
        qc = qa_ref[:, sl]
        kc = ka_ref[:kv, sl]
        vc = va_ref[:kv, sl]
        sel = sel_ref[:, :kv] > 0.5
        outs = []
        for hh in range(2):
            qm = jnp.where(_head_mask(qc.shape, A_HEAD_DIM, hh), qc, jnp.zeros_like(qc))
            lg = lax.dot_general(qm, kc, _NT, preferred_element_type=f32)
            lg = jnp.where(sel, lg, NEG_BIG)
            m = jnp.max(lg, axis=1, keepdims=True)
            p = jnp.exp(lg - m)
            den = jnp.sum(p, axis=1, keepdims=True)
            pv = jnp.dot(p.astype(bf), vc, preferred_element_type=f32)
            outs.append(pv / den)
        o_ref[:, sl] = jnp.where(_head_mask(outs[0].shape, A_HEAD_DIM, 0), outs[0], outs[1]).astype(bf)
        return carry

    lax.fori_loop(0, A_WIDTH // LANES, attn_slab, 0)


def _per_query_block(block_fn, refs, n_blocks, tq):
    j = pl.program_id(1)
    for b in range(n_blocks):
        pl.when(j == b)(functools.partial(block_fn, *refs, kv=(b + 1) * tq))


def _dsa_kernel(*refs, seq, k_sel):
    tq = refs[0].shape[0]
    _per_query_block(functools.partial(_dsa_block, k_sel=k_sel), refs, seq // tq, tq)


def _dsa(qa, qi, wi, ka, va, ki, bsz, seq):
    tq = min(Q_TILE, seq)
    nq = seq // tq
    qspec = lambda w: pl.BlockSpec((tq, w), lambda b, i: (b * nq + i, 0))
    kspec = lambda w: pl.BlockSpec((seq, w), lambda b, i: (b, 0))
    k_sel = min(TOPK_MAX, seq // 4)
    return pl.pallas_call(
        functools.partial(_dsa_kernel, seq=seq, k_sel=k_sel),
        grid=(bsz, nq),
        in_specs=[qspec(A_WIDTH), qspec(A_WIDTH), qspec(LANES), kspec(A_WIDTH), kspec(A_WIDTH), kspec(LANES)],
        out_specs=qspec(A_WIDTH),
        out_shape=jax.ShapeDtypeStruct((bsz * seq, A_WIDTH), jnp.bfloat16),
        scratch_shapes=[pltpu.VMEM((tq, seq), jnp.float32), pltpu.VMEM((tq, seq), jnp.float32)],
        compiler_params=_cparams(("arbitrary", "arbitrary")),
        name="dsa",
    )(qa, qi, wi, ka, va, ki)


def _mla_block(qn_ref, qr_ref, kn_ref, kr_ref, vb_ref, o_ref, *, kv):
    tq = qn_ref.shape[0]
    f32 = jnp.float32
    bf = jnp.bfloat16
    q_pos = (kv - tq) + lax.broadcasted_iota(jnp.int32, (tq, 1), 0)
    key_pos = lax.broadcasted_iota(jnp.int32, (tq, kv), 1)
    causal = key_pos <= q_pos
    kr = kr_ref[:kv, :]

    def slab(c, carry):
        sl = pl.ds(pl.multiple_of(c * LANES, LANES), LANES)
        qc = qn_ref[:, sl]
        kcat = jnp.concatenate([kn_ref[:kv, sl], kr], axis=1)
        vc = vb_ref[:kv, sl]
        qrc = qr_ref[:, pl.ds(pl.multiple_of((c // 2) * LANES, LANES), LANES)]
        outs = []
        for hh in range(2):
            h = 2 * c + hh
            qm = jnp.where(_head_mask(qc.shape, QK_NOPE, hh), qc, jnp.zeros_like(qc))
            qrm = jnp.where(_head_mask(qrc.shape, QK_ROPE, h % 4), qrc, jnp.zeros_like(qrc))
            qcat = jnp.concatenate([qm, qrm], axis=1)
            lg = lax.dot_general(qcat, kcat, _NT, preferred_element_type=f32)
            lg = jnp.where(causal, lg, NEG_BIG)
            m = jnp.max(lg, axis=1, keepdims=True)
            p = jnp.exp(lg - m)
            den = jnp.sum(p, axis=1, keepdims=True)
            pv = jnp.dot(p.astype(bf), vc, preferred_element_type=f32)
            outs.append(pv / den)
        o_ref[:, sl] = jnp.where(_head_mask(outs[0].shape, V_DIM, 0), outs[0], outs[1]).astype(bf)
        return carry

    lax.fori_loop(0, B_WIDTH // LANES, slab, 0, unroll=2)


def _mla_kernel(*refs, seq):
    tq = refs[0].shape[0]
    _per_query_block(_mla_block, refs, seq // tq, tq)


def _mla(qn, qr, kn, kr, vb, bsz, seq):
    tq = min(Q_TILE, seq)
    nq = seq // tq
    qspec = lambda w: pl.BlockSpec((tq, w), lambda b, i: (b * nq + i, 0))
    kspec = lambda w: pl.BlockSpec((seq, w), lambda b, i: (b, 0))
    return pl.pallas_call(
        functools.partial(_mla_kernel, seq=seq),
        grid=(bsz, nq),
        in_specs=[qspec(B_HEADS * QK_NOPE), qspec(B_HEADS * QK_ROPE), kspec(B_HEADS * QK_NOPE),
                  kspec(LANES), kspec(B_WIDTH)],
        out_specs=qspec(B_WIDTH),
        out_shape=jax.ShapeDtypeStruct((bsz * seq, B_WIDTH), jnp.bfloat16),
        compiler_params=_cparams(("arbitrary", "arbitrary")),
        name="mla",
    )(qn, qr, kn, kr, vb)


def _post_kernel(x_ref, oa_ref, ob_ref, wg_ref, bg_ref, woa_ref, wob_ref, wout_ref, g1_ref, b1_ref,
                 wr_ref, br_ref, h_ref, ri_ref, rf_ref, cnt_ref, carry_ref):
    f32 = jnp.float32
    bf = jnp.bfloat16
    t = x_ref.shape[0]

    @pl.when(pl.program_id(0) == 0)
    def _():
        carry_ref[...] = jnp.zeros_like(carry_ref)

    x = x_ref[...]
    gate = jax.nn.sigmoid(jnp.dot(x.astype(bf), wg_ref[...], preferred_element_type=f32) + bg_ref[...])
    pa = jnp.dot(oa_ref[...], woa_ref[...], preferred_element_type=f32)
    pb = jnp.dot(ob_ref[...], wob_ref[...], preferred_element_type=f32)
    mix = gate[:, :D_MODEL] * pa + gate[:, D_MODEL:] * pb
    u = DN_ALPHA * x + jnp.dot(mix.astype(bf), wout_ref[...], preferred_element_type=f32)
    h = _layer_norm(u, g1_ref[...], b1_ref[...])
    _store_token_rows(h_ref, h)

    logits = jnp.dot(h, wr_ref[...], preferred_element_type=f32,
                     precision=lax.Precision.HIGHEST) + br_ref[...]
    lane = lax.broadcasted_iota(jnp.int32, (t, LANES), 1).astype(f32)
    lg = jnp.where(lane < N_EXPERTS, logits, -jnp.inf)
    vals, ids = [], []
    assign = jnp.zeros((t, LANES), f32)
    for _k in range(TOP_K):
        m = jnp.max(lg, axis=1, keepdims=True)
        idx = jnp.min(jnp.where(lg == m, lane, float(LANES)), axis=1, keepdims=True)
        hit = lane == idx
        vals.append(m)
        ids.append(idx)
        assign = jnp.where(hit, 1.0, assign)
        lg = jnp.where(hit, -jnp.inf, lg)
    exps = [jnp.exp(v - vals[0]) for v in vals]
    den = exps[0] + exps[1] + exps[2] + exps[3]

    r = lax.broadcasted_iota(jnp.int32, (t, t), 0)
    c = lax.broadcasted_iota(jnp.int32, (t, t), 1)
    lower = jnp.where(c < r, 1.0, 0.0).astype(bf)
    before = jnp.dot(lower, assign.astype(bf), preferred_element_type=f32) + carry_ref[0:1, :]
    ri = jnp.zeros((t, LANES), f32)
    rf = jnp.zeros((t, LANES), f32)
    for k in range(TOP_K):
        rank = jnp.sum(jnp.where(lane == ids[k], before, 0.0), axis=1, keepdims=True)
        ri = jnp.where(lane == float(k), ids[k], ri)
        ri = jnp.where(lane == float(TOP_K + k), rank, ri)
        rf = jnp.where(lane == float(k), exps[k] / den, rf)
    ri_ref[...] = ri.astype(jnp.int32)
    rf_ref[...] = rf
    total = carry_ref[0:1, :] + jnp.sum(assign, axis=0, keepdims=True)
    carry_ref[...] = jnp.broadcast_to(total, carry_ref.shape)
    cnt_ref[...] = jnp.broadcast_to(total, cnt_ref.shape)


def _post(x2, oa, ob, wg, bg, woa, wob, wout, g1, b1, wr, br):
    n = x2.shape[0]
    t = PROJ_TILE
    tok = lambda w: pl.BlockSpec((t, w), lambda i: (i, 0))
    consts = (wg, bg, woa, wob, wout, g1, b1, wr, br)
    return pl.pallas_call(
        _post_kernel,
        grid=(n // t,),
        in_specs=[tok(D_MODEL), tok(A_WIDTH), tok(B_WIDTH)] + [_full(a.shape) for a in consts],
        out_specs=[pl.BlockSpec((t * ROW_TILES, LANES), lambda i: (i, 0)), tok(LANES), tok(LANES),
                   _full((8, LANES))],
        out_shape=[jax.ShapeDtypeStruct((n * ROW_TILES, LANES), jnp.float32),
                   jax.ShapeDtypeStruct((n, LANES), jnp.int32),
                   jax.ShapeDtypeStruct((n, LANES), jnp.float32),
                   jax.ShapeDtypeStruct((8, LANES), jnp.float32)],
        scratch_shapes=[pltpu.VMEM((8, LANES), jnp.float32)],
        compiler_params=_cparams(("arbitrary",)),
        name="post",
    )(x2, oa, ob, *consts)


def _token_row(ref, r):
    start = r * ROW_TILES if isinstance(r, int) else pl.multiple_of(r * ROW_TILES, ROW_TILES)
    return ref.at[pl.ds(start, ROW_TILES)]


def _row_gather(idx_ref, src_ref, dst_ref, sem, rows):
    for j in range(rows):
        pltpu.make_async_copy(_token_row(src_ref, idx_ref[0, 0, j]), _token_row(dst_ref, j), sem).start()


def _rows_wait(src_ref, dst_ref, sem):
    pltpu.make_async_copy(src_ref, dst_ref, sem).wait()


def _dispatch_kernel(dest_ref, h_ref, xs_ref, sem):
    t = MOVE_TILE
    for k in range(TOP_K):
        for j in range(t):
            pltpu.make_async_copy(_token_row(h_ref, j), _token_row(xs_ref, dest_ref[0, 0, k * t + j]), sem).start()
    for k in range(TOP_K):
        _rows_wait(h_ref, xs_ref.at[pl.ds(0, t * ROW_TILES)], sem)


def _dispatch(dest3, h, rows):
    t = MOVE_TILE
    n = h.shape[0] // ROW_TILES
    return pl.pallas_call(
        _dispatch_kernel,
        grid=(n // t,),
        in_specs=[pl.BlockSpec((1, 1, t * TOP_K), lambda i: (i, 0, 0), memory_space=pltpu.SMEM),
                  pl.BlockSpec((t * ROW_TILES, LANES), lambda i: (i, 0))],
        out_specs=pl.BlockSpec(memory_space=pl.ANY),
        out_shape=jax.ShapeDtypeStruct((rows * ROW_TILES, LANES), jnp.float32),
        scratch_shapes=[pltpu.SemaphoreType.DMA(())],
        compiler_params=_cparams(("arbitrary",)),
        name="dispatch",
    )(dest3, h)


def _expert_kernel(te_ref, tt_ref, lo_ref, hi_ref, ns_ref, xs_ref, wu_ref, bu_ref, wd_ref, bd_ref, ys_ref,
                   wub_ref, wdb_ref):
    i = pl.program_id(0)
    f32 = jnp.float32
    bf = jnp.bfloat16
    t = EXPERT_TILE
    valid = i < ns_ref[0]
    before = jnp.maximum(i - 1, 0)
    fresh = valid & ((i == 0) | (te_ref[i] != te_ref[before]))
    first_visit = (i == 0) | (tt_ref[i] != tt_ref[before])

    @pl.when(fresh)
    def _():
        rows = 128

        def cast_up(j, c):
            r = pl.multiple_of(j * rows, rows)
            wub_ref[pl.ds(r, rows), :] = wu_ref[pl.ds(r, rows), :].astype(bf)
            return c

        lax.fori_loop(0, D_MODEL // rows, cast_up, 0)

        def cast_dn(j, c):
            r = pl.multiple_of(j * rows, rows)
            wdb_ref[pl.ds(r, rows), :] = wd_ref[pl.ds(r, rows), :].astype(bf)
            return c

        lax.fori_loop(0, D_FF // rows, cast_dn, 0)

    def compute():
        xb = _load_token_rows(xs_ref, 0, t).astype(bf)
        y = jnp.broadcast_to(bd_ref[...], (t, D_MODEL))
        for c in range(D_FF // FF_CHUNK):
            gs = slice(c * FF_CHUNK, (c + 1) * FF_CHUNK)
            ls = slice(D_FF + c * FF_CHUNK, D_FF + (c + 1) * FF_CHUNK)
            ag = jnp.dot(xb, wub_ref[:, gs], preferred_element_type=f32) + bu_ref[:, gs]
            al = jnp.dot(xb, wub_ref[:, ls], preferred_element_type=f32) + bu_ref[:, ls]
            g = jnp.minimum(ag, SWIGLU_LIMIT)
            lin = jnp.clip(al, -SWIGLU_LIMIT, SWIGLU_LIMIT)
            act = (lin + 1.0) * (g * jax.nn.sigmoid(SWIGLU_ALPHA * g))
            y = y + jnp.dot(act.astype(bf), wdb_ref[gs, :], preferred_element_type=f32)
        row = lax.broadcasted_iota(jnp.int32, (t, 1), 0)
        return y, (row >= lo_ref[i]) & (row < hi_ref[i])

    @pl.when(valid & first_visit)
    def _():
        y, mine = compute()
        _store_token_rows(ys_ref, jnp.where(mine, y, 0.0))

    @pl.when(valid & jnp.logical_not(first_visit))
    def _():
        y, mine = compute()
        _store_token_rows(ys_ref, jnp.where(mine, y, _load_token_rows(ys_ref, 0, t)))


def _experts(steps, xs, w_up, b_up, w_down, b_down):
    t = EXPERT_TILE
    max_steps = steps[0].shape[0]
    row_block = pl.BlockSpec((t * ROW_TILES, LANES), lambda i, te, tt, lo, hi, ns: (tt[i], 0))
    per_expert = lambda shape: pl.BlockSpec((None,) + shape, lambda i, te, tt, lo, hi, ns: (te[i], 0, 0))
    grid_spec = pltpu.PrefetchScalarGridSpec(
        num_scalar_prefetch=5,
        grid=(max_steps,),
        in_specs=[row_block, per_expert((D_MODEL, 2 * D_FF)), per_expert((1, 2 * D_FF)),
                  per_expert((D_FF, D_MODEL)), per_expert((1, D_MODEL))],
        out_specs=row_block,
        scratch_shapes=[pltpu.VMEM((D_MODEL, 2 * D_FF), jnp.bfloat16),
                        pltpu.VMEM((D_FF, D_MODEL), jnp.bfloat16)],
    )
    return pl.pallas_call(
        _expert_kernel,
        grid_spec=grid_spec,
        out_shape=jax.ShapeDtypeStruct(xs.shape, jnp.float32),
        compiler_params=_cparams(("arbitrary",)),
        name="experts",
    )(*steps, xs, w_up, b_up.reshape(N_EXPERTS, 1, 2 * D_FF), w_down, b_down.reshape(N_EXPERTS, 1, D_MODEL))


def _combine_kernel(cur_ref, nxt_ref, ys_ref, h_ref, rf_ref, g2_ref, b2_ref, o_ref, buf_ref, sems):
    i = pl.program_id(0)
    n_steps = pl.num_programs(0)
    t = MOVE_TILE
    rows = t * TOP_K
    slot = i % 2

    @pl.when(i == 0)
    def _():
        _row_gather(cur_ref, ys_ref, buf_ref.at[0], sems.at[0], rows)

    _rows_wait(ys_ref.at[pl.ds(0, rows * ROW_TILES)], buf_ref.at[slot], sems.at[slot])
    _row_gather(nxt_ref, ys_ref, buf_ref.at[1 - slot], sems.at[1 - slot], rows)

    rf = rf_ref[...]
    u = DN_ALPHA * _load_token_rows(h_ref, 0, t)
    for k in range(TOP_K):
        u = u + rf[:, k:k + 1] * _load_token_rows(buf_ref.at[slot], k * t, t)
    o_ref[...] = _layer_norm(u, g2_ref[...], b2_ref[...])

    @pl.when(i == n_steps - 1)
    def _():
        _rows_wait(ys_ref.at[pl.ds(0, rows * ROW_TILES)], buf_ref.at[1 - slot], sems.at[1 - slot])


def _combine(dest3, ys, h, rf, g2, b2):
    t = MOVE_TILE
    n = h.shape[0] // ROW_TILES
    last = n // t - 1
    tok = lambda w: pl.BlockSpec((t, w), lambda i: (i, 0))
    idx_spec = lambda f: pl.BlockSpec((1, 1, t * TOP_K), f, memory_space=pltpu.SMEM)
    return pl.pallas_call(
        _combine_kernel,
        grid=(n // t,),
        in_specs=[idx_spec(lambda i: (i, 0, 0)), idx_spec(lambda i: (jnp.minimum(i + 1, last), 0, 0)),
                  pl.BlockSpec(memory_space=pl.ANY), pl.BlockSpec((t * ROW_TILES, LANES), lambda i: (i, 0)),
                  tok(LANES), _full(g2.shape), _full(b2.shape)],
        out_specs=tok(D_MODEL),
        out_shape=jax.ShapeDtypeStruct((n, D_MODEL), jnp.float32),
        scratch_shapes=[pltpu.VMEM((2, t * TOP_K * ROW_TILES, LANES), jnp.float32),
                        pltpu.SemaphoreType.DMA((2,))],
        compiler_params=_cparams(("arbitrary",)),
        name="combine",
    )(dest3, dest3, ys, h, rf, g2, b2)


def _rope_tables(positions, half):
    inv_freq = ROPE_THETA ** (-jnp.arange(half, dtype=jnp.float32) / half)
    ang = positions.astype(jnp.float32).reshape(-1, 1) * inv_freq
    reps = LANES // half
    return jnp.tile(jnp.cos(ang), (1, reps)), jnp.tile(jnp.sin(ang), (1, reps))


def _lookup(table, idx):
    experts = jnp.arange(N_EXPERTS, dtype=jnp.int32)
    return jnp.sum(jnp.where(idx[..., None] == experts, table, 0), axis=-1)


def _expert_schedule(starts, ends, rows):
    t = EXPERT_TILE
    first_tile = starts // t
    n_tile = jnp.where(ends > starts, (ends - 1) // t - first_tile + 1, 0)
    cum = jnp.cumsum(n_tile)
    n_steps = cum[-1]
    max_steps = rows // t + N_EXPERTS
    s = jnp.minimum(jnp.arange(max_steps, dtype=jnp.int32), n_steps - 1)
    expert = jnp.sum((cum[None, :] <= s[:, None]).astype(jnp.int32), axis=1)
    tile = _lookup(first_tile, expert) + s - _lookup(cum - n_tile, expert)
    lo = jnp.maximum(_lookup(starts, expert) - tile * t, 0)
    hi = jnp.minimum(_lookup(ends, expert) - tile * t, t)
    return expert, tile, lo, hi, n_steps.astype(jnp.int32).reshape(1)


def _layer(x2, tabs, bsz, seq, w_in, b_gate, rms_cq, rms_ckv, w_uq, w_ukv, w_o_a, w_o_b, w_out,
           ln1_g, ln1_b, w_router, b_router, w_up, b_up, w_down, b_down, ln2_g, ln2_b):
    bf = jnp.bfloat16
    n = x2.shape[0]
    off = np.concatenate([[0], np.cumsum(SPLITS)]).tolist()
    col = lambda j: w_in[:, off[j]:off[j + 1]]
    w1 = jnp.concatenate([col(0), col(1), col(2), col(3)], axis=1).astype(bf)
    wi_pad = jnp.zeros((D_MODEL, LANES - IDX_HEADS), w_in.dtype)
    ws = jnp.concatenate([col(4), col(4), jnp.tile(col(8), (1, LANES // QK_ROPE)), col(5), wi_pad], axis=1).astype(bf)
    wl = jnp.concatenate([col(6), col(7)], axis=1).astype(bf)
    wg = col(9).astype(bf)
    uq = w_uq.reshape(Q_LORA, B_HEADS, QK_NOPE + QK_ROPE)
    wuq = jnp.concatenate([uq[:, :, :QK_NOPE].reshape(Q_LORA, -1), uq[:, :, QK_NOPE:].reshape(Q_LORA, -1)],
                          axis=1).astype(bf)
    ukv = w_ukv.reshape(KV_LORA, B_HEADS, QK_NOPE + V_DIM)
    wukv = jnp.concatenate([ukv[:, :, :QK_NOPE].reshape(KV_LORA, -1), ukv[:, :, QK_NOPE:].reshape(KV_LORA, -1)],
                           axis=1).astype(bf)

    qa, ka, va, qi, ki, kr, wi, qn, qr, kn, vb = _proj(
        x2, tabs, w1, ws, wl, rms_cq.reshape(1, -1), rms_ckv.reshape(1, -1), wuq, wukv)
    o_a = _dsa(qa, qi, wi, ka, va, ki, bsz, seq)
    o_b = _mla(qn, qr, kn, kr, vb, bsz, seq)

    wr = jnp.concatenate([w_router, jnp.zeros((D_MODEL, LANES - N_EXPERTS), w_router.dtype)], axis=1)
    br = jnp.concatenate([b_router, jnp.zeros((LANES - N_EXPERTS,), b_router.dtype)]).reshape(1, -1)
    h, ri, rf, cnt = _post(x2, o_a, o_b, wg, b_gate.reshape(1, -1), w_o_a.astype(bf), w_o_b.astype(bf),
                               w_out.astype(bf), ln1_g.reshape(1, -1), ln1_b.reshape(1, -1), wr, br)

    counts = cnt[0, :N_EXPERTS].astype(jnp.int32)
    ends = jnp.cumsum(counts)
    starts = ends - counts
    dest = _lookup(starts, ri[:, :TOP_K]) + ri[:, TOP_K:2 * TOP_K]
    dest3 = dest.reshape(n // MOVE_TILE, MOVE_TILE, TOP_K).transpose(0, 2, 1).reshape(
        n // MOVE_TILE, 1, MOVE_TILE * TOP_K)

    xs = _dispatch(dest3, h, n * TOP_K)
    ys = _experts(_expert_schedule(starts, ends, n * TOP_K), xs, w_up, b_up, w_down, b_down)
    return _combine(dest3, ys, h, rf, ln2_g.reshape(1, -1), ln2_b.reshape(1, -1))


def kernel(x, positions, w_in, b_gate, rms_cq, rms_ckv, w_uq, w_ukv, w_o_a, w_o_b, w_out, ln1_g, ln1_b,
           w_router, b_router, w_up, b_up, w_down, b_down, ln2_g, ln2_b):
    bsz, seq, _ = x.shape
    x2 = x.reshape(bsz * seq, D_MODEL)
    c64, s64 = _rope_tables(positions, A_HEAD_DIM // 2)
    c32, s32 = _rope_tables(positions, QK_ROPE // 2)
    tabs = (c64, s64, c32, s32)
    for l in range(DEPTH):
        x2 = _layer(x2, tabs, bsz, seq, w_in[l], b_gate[l], rms_cq[l], rms_ckv[l], w_uq[l], w_ukv[l],
                    w_o_a[l], w_o_b[l], w_out[l], ln1_g[l], ln1_b[l], w_router[l], b_router[l],
                    w_up[l], b_up[l], w_down[l], b_down[l], ln2_g[l], ln2_b[l])
    return x2.reshape(bsz, seq, D_MODEL)
```

```python
import functools

import jax
import jax.numpy as jnp
import numpy as np
from jax import lax
from jax.experimental import pallas as pl
from jax.experimental.pallas import tpu as pltpu

D_MODEL = 1024
A_HEADS = 8
A_HEAD_DIM = 64
IDX_HEADS = 8
IDX_DIM = 64
TOPK_MAX = 256
B_HEADS = 8
Q_LORA = 384
KV_LORA = 256
QK_NOPE = 64
QK_ROPE = 32
V_DIM = 64
ROPE_THETA = 10000.0
N_EXPERTS = 32
TOP_K = 4
D_FF = 1024
SWIGLU_LIMIT = 7.0
SWIGLU_ALPHA = 1.702
DEPTH = 1
DN_ALPHA = (2 * DEPTH) ** 0.25
LN_EPS = 1e-5
RMS_EPS = 1e-6
A_WIDTH = A_HEADS * A_HEAD_DIM
B_WIDTH = B_HEADS * V_DIM
SPLITS = (A_WIDTH, A_WIDTH, A_WIDTH, IDX_HEADS * IDX_DIM, IDX_DIM, IDX_HEADS,
          Q_LORA, KV_LORA, QK_ROPE, 2 * D_MODEL)

LANES = 128
VMEM_LIMIT_BYTES = 52 * 1024 * 1024

PROJ_TILE = 512
Q_TILE = 256
EXPERT_TILE = 512
FF_CHUNK = 256
MOVE_TILE = 256

ROW_TILES = D_MODEL // LANES
assert ROW_TILES == 8

INT_MIN = -(2 ** 31)
NEG_BIG = -1e30

_NT = (((1,), (1,)), ((), ()))


def _cparams(sem):
    return pltpu.CompilerParams(dimension_semantics=sem, vmem_limit_bytes=VMEM_LIMIT_BYTES)


def _full(shape):
    nd = len(shape)
    return pl.BlockSpec(shape, lambda *_: (0,) * nd)


def _rope_chunk(x, cos, sin, half):
    lane = lax.broadcasted_iota(jnp.int32, x.shape, 1)
    first = (lane % (2 * half)) < half
    up = pltpu.roll(x, LANES - half, 1)
    dn = pltpu.roll(x, half, 1)
    return x * cos + jnp.where(first, -up, dn) * sin


def _layer_norm(u, g, b):
    mu = jnp.mean(u, axis=-1, keepdims=True)
    d = u - mu
    var = jnp.mean(d * d, axis=-1, keepdims=True)
    return d * lax.rsqrt(var + LN_EPS) * g + b


def _load_token_rows(ref, first, t):
    return jnp.concatenate([ref[pl.ds(first * ROW_TILES + s, t, stride=ROW_TILES), :] for s in range(ROW_TILES)],
                           axis=1)


def _store_token_rows(ref, v):
    t = v.shape[0]
    for s in range(ROW_TILES):
        ref[pl.ds(s, t, stride=ROW_TILES), :] = v[:, s * LANES:(s + 1) * LANES]


def _head_mask(shape, width, which):
    lane = lax.broadcasted_iota(jnp.int32, shape, 1)
    return (lane // width) == which


def _proj_kernel(x_ref, c64_ref, s64_ref, c32_ref, s32_ref, w1_ref, ws_ref, wl_ref,
                 gcq_ref, gckv_ref, wuq_ref, wukv_ref,
                 qa_ref, ka_ref, va_ref, qi_ref, ki_ref, kr_ref, wi_ref,
                 qn_ref, qr_ref, kn_ref, vb_ref):
    xb = x_ref[...].astype(jnp.bfloat16)
    c64, s64 = c64_ref[...], s64_ref[...]
    c32, s32 = c32_ref[...], s32_ref[...]
    f32 = jnp.float32
    bf = jnp.bfloat16

    z1 = jnp.dot(xb, w1_ref[...], preferred_element_type=f32)
    n_chunk = A_WIDTH // LANES
    a_scale = A_HEAD_DIM ** -0.5
    for c in range(n_chunk):
        sl = slice(c * LANES, (c + 1) * LANES)
        qa_ref[:, sl] = (_rope_chunk(z1[:, c * LANES:(c + 1) * LANES], c64, s64, 32) * a_scale).astype(bf)
        o = A_WIDTH + c * LANES
        ka_ref[:, sl] = _rope_chunk(z1[:, o:o + LANES], c64, s64, 32).astype(bf)
        o = 2 * A_WIDTH + c * LANES
        va_ref[:, sl] = z1[:, o:o + LANES].astype(bf)
        o = 3 * A_WIDTH + c * LANES
        qi_ref[:, sl] = _rope_chunk(z1[:, o:o + LANES], c64, s64, 32).astype(bf)

    zs = jnp.dot(xb, ws_ref[...], preferred_element_type=f32)
    ki_ref[...] = _rope_chunk(zs[:, 0:LANES], c64, s64, 32).astype(bf)
    kr_ref[...] = _rope_chunk(zs[:, LANES:2 * LANES], c32, s32, 16).astype(bf)
    wi_ref[...] = zs[:, 2 * LANES:3 * LANES] * (IDX_HEADS ** -0.5 * IDX_DIM ** -0.5)

    zl = jnp.dot(xb, wl_ref[...], preferred_element_type=f32)
    cq = zl[:, :Q_LORA]
    cq = cq * lax.rsqrt(jnp.mean(cq * cq, axis=-1, keepdims=True) + RMS_EPS) * gcq_ref[...]
    ckv = zl[:, Q_LORA:]
    ckv = ckv * lax.rsqrt(jnp.mean(ckv * ckv, axis=-1, keepdims=True) + RMS_EPS) * gckv_ref[...]
    qb = jnp.dot(cq.astype(bf), wuq_ref[...], preferred_element_type=f32)
    kv = jnp.dot(ckv.astype(bf), wukv_ref[...], preferred_element_type=f32)
    b_scale = (QK_NOPE + QK_ROPE) ** -0.5
    qn_ref[...] = (qb[:, :B_HEADS * QK_NOPE] * b_scale).astype(bf)
    for c in range(B_HEADS * QK_ROPE // LANES):
        o = B_HEADS * QK_NOPE + c * LANES
        qr_ref[:, c * LANES:(c + 1) * LANES] = (
            _rope_chunk(qb[:, o:o + LANES], c32, s32, 16) * b_scale).astype(bf)
    kn_ref[...] = kv[:, :B_HEADS * QK_NOPE].astype(bf)
    vb_ref[...] = kv[:, B_HEADS * QK_NOPE:].astype(bf)


def _proj(x2, tabs, w1, ws, wl, gcq, gckv, wuq, wukv):
    n = x2.shape[0]
    t = PROJ_TILE
    tok = lambda w: pl.BlockSpec((t, w), lambda i: (i, 0))
    bf = jnp.bfloat16
    outs = [(A_WIDTH, bf)] * 4 + [(LANES, bf), (LANES, bf), (LANES, jnp.float32),
                                  (B_HEADS * QK_NOPE, bf), (B_HEADS * QK_ROPE, bf),
                                  (B_HEADS * QK_NOPE, bf), (B_WIDTH, bf)]
    return pl.pallas_call(
        _proj_kernel,
        grid=(n // t,),
        in_specs=[tok(D_MODEL)] + [tok(LANES)] * 4 + [_full(a.shape) for a in (w1, ws, wl, gcq, gckv, wuq, wukv)],
        out_specs=[tok(w) for w, _ in outs],
        out_shape=[jax.ShapeDtypeStruct((n, w), d) for w, d in outs],
        compiler_params=_cparams(("arbitrary",)),
        name="proj",
    )(x2, *tabs, w1, ws, wl, gcq, gckv, wuq, wukv)


def _tie_select(eq, gt, need, kv):
    chunk = 256
    r = lax.broadcasted_iota(jnp.int32, (chunk, chunk), 0)
    c = lax.broadcasted_iota(jnp.int32, (chunk, chunk), 1)
    upper = jnp.where(r <= c, 1.0, 0.0).astype(jnp.bfloat16)
    carry = jnp.zeros((eq.shape[0], 1), jnp.float32)
    parts = []
    for j in range(kv // chunk):
        e = eq[:, j * chunk:(j + 1) * chunk]
        ef = jnp.where(e, 1.0, 0.0).astype(jnp.bfloat16)
        pref = jnp.dot(ef, upper, preferred_element_type=jnp.float32) + carry
        parts.append(gt[:, j * chunk:(j + 1) * chunk] | (e & (pref <= need)))
        carry = carry + jnp.sum(ef.astype(jnp.float32), axis=1, keepdims=True)
    return jnp.concatenate(parts, axis=1)


def _key_to_float(key):
    return lax.bitcast_convert_type(key ^ ((key >> 31) & jnp.int32(0x7FFFFFFF)), jnp.float32)


def _select_topk(score, k_eff, sel_ref, kv):
    tq = score.shape[0]

    def search(it, prefix):
        cand = prefix + lax.shift_left(jnp.int32(1), 31 - it)
        cnt = jnp.sum(jnp.where(score >= _key_to_float(cand), 1.0, 0.0), axis=1, keepdims=True)
        return jnp.where(cnt >= k_eff, cand, prefix)

    thr = _key_to_float(lax.fori_loop(0, 32, search, jnp.full((tq, 1), INT_MIN, jnp.int32)))
    gt = score > thr
    eq = score == thr
    n_gt = jnp.sum(jnp.where(gt, 1.0, 0.0), axis=1, keepdims=True)
    n_eq = jnp.sum(jnp.where(eq, 1.0, 0.0), axis=1, keepdims=True)
    need = k_eff - n_gt
    sel_ref[:, :kv] = jnp.where(score >= thr, 1.0, 0.0)
    has_tie = jnp.max(jnp.where(n_eq != need, 1.0, 0.0)) > 0.5

    @pl.when(has_tie)
    def _():
        sel_ref[:, :kv] = jnp.where(_tie_select(eq, gt, need, kv), 1.0, 0.0)


def _dsa_block(qa_ref, qi_ref, wi_ref, ka_ref, va_ref, ki_ref, o_ref, sel_ref, score_ref, *, kv, k_sel):
    tq = qa_ref.shape[0]
    f32 = jnp.float32
    bf = jnp.bfloat16
    q_pos = (kv - tq) + lax.broadcasted_iota(jnp.int32, (tq, 1), 0)
    key_pos = lax.broadcasted_iota(jnp.int32, (tq, kv), 1)
    causal = key_pos <= q_pos

    if kv <= k_sel:
        sel_ref[:, :kv] = jnp.where(causal, 1.0, 0.0)
    else:
        ki = ki_ref[:kv, :]
        lane = lax.broadcasted_iota(jnp.int32, (tq, LANES), 1)
        score_ref[:, :kv] = jnp.zeros((tq, kv), f32)

        def idx_slab(c, carry):
            qc = qi_ref[:, pl.ds(pl.multiple_of(c * LANES, LANES), LANES)]
            acc = score_ref[:, :kv]
            for hh in range(2):
                qm = jnp.where(_head_mask(qc.shape, IDX_DIM, hh), qc, jnp.zeros_like(qc))
                s = lax.dot_general(qm, ki, _NT, preferred_element_type=f32)
                w = jnp.sum(jnp.where(lane == 2 * c + hh, wi_ref[...], 0.0), axis=1, keepdims=True)
                acc = acc + jnp.maximum(s, 0.0) * w
            score_ref[:, :kv] = acc
            return carry

        lax.fori_loop(0, IDX_HEADS // 2, idx_slab, 0)
        score = jnp.where(causal, score_ref[:, :kv], -jnp.inf)
        k_eff = jnp.minimum(q_pos + 1, k_sel).astype(f32)
        _select_topk(score, k_eff, sel_ref, kv)

    def attn_slab(c, carry):
        sl = pl.ds(pl.multiple_of(c * LANES, LANES), LANES)
        qc = qa_ref[:, sl]
        kc = ka_ref[:kv, sl]
        vc = va_ref[:kv, sl]
        sel = sel_ref[:, :kv] > 0.5
        outs = []
        for hh in range(2):
            qm = jnp.where(_head_mask(qc.shape, A_HEAD_DIM, hh), qc, jnp.zeros_like(qc))
            lg = lax.dot_general(qm, kc, _NT, preferred_element_type=f32)
            lg = jnp.where(sel, lg, NEG_BIG)
            m = jnp.max(lg, axis=1, keepdims=True)
            p = jnp.exp(lg - m)
            den = jnp.sum(p, axis=1, keepdims=True)
            pv = jnp.dot(p.astype(bf), vc, preferred_element_type=f32)
            outs.append(pv / den)
        o_ref[:, sl] = jnp.where(_head_mask(outs[0].shape, A_HEAD_DIM, 0), outs[0], outs[1]).astype(bf)
        return carry

    lax.fori_loop(0, A_WIDTH // LANES, attn_slab, 0)


def _per_query_block(block_fn, refs, n_blocks, tq):
    j = pl.program_id(1)
    for b in range(n_blocks):
        pl.when(j == b)(functools.partial(block_fn, *refs, kv=(b + 1) * tq))


def _dsa_kernel(*refs, seq, k_sel):
    tq = refs[0].shape[0]
    _per_query_block(functools.partial(_dsa_block, k_sel=k_sel), refs, seq // tq, tq)


def _dsa(qa, qi, wi, ka, va, ki, bsz, seq):
    tq = min(Q_TILE, seq)
    nq = seq // tq
    qspec = lambda w: pl.BlockSpec((tq, w), lambda b, i: (b * nq + i, 0))
    kspec = lambda w: pl.BlockSpec((seq, w), lambda b, i: (b, 0))
    k_sel = min(TOPK_MAX, seq // 4)
    return pl.pallas_call(
        functools.partial(_dsa_kernel, seq=seq, k_sel=k_sel),
        grid=(bsz, nq),
        in_specs=[qspec(A_WIDTH), qspec(A_WIDTH), qspec(LANES), kspec(A_WIDTH), kspec(A_WIDTH), kspec(LANES)],
        out_specs=qspec(A_WIDTH),
        out_shape=jax.ShapeDtypeStruct((bsz * seq, A_WIDTH), jnp.bfloat16),
        scratch_shapes=[pltpu.VMEM((tq, seq), jnp.float32), pltpu.VMEM((tq, seq), jnp.float32)],
        compiler_params=_cparams(("arbitrary", "arbitrary")),
        name="dsa",
    )(qa, qi, wi, ka, va, ki)


def _mla_block(qn_ref, qr_ref, kn_ref, kr_ref, vb_ref, o_ref, *, kv):
    tq = qn_ref.shape[0]
    f32 = jnp.float32
    bf = jnp.bfloat16
    q_pos = (kv - tq) + lax.broadcasted_iota(jnp.int32, (tq, 1), 0)
    key_pos = lax.broadcasted_iota(jnp.int32, (tq, kv), 1)
    causal = key_pos <= q_pos
    kr = kr_ref[:kv, :]

    def slab(c, carry):
        sl = pl.ds(pl.multiple_of(c * LANES, LANES), LANES)
        qc = qn_ref[:, sl]
        kcat = jnp.concatenate([kn_ref[:kv, sl], kr], axis=1)
        vc = vb_ref[:kv, sl]
        qrc = qr_ref[:, pl.ds(pl.multiple_of((c // 2) * LANES, LANES), LANES)]
        outs = []
        for hh in range(2):
            h = 2 * c + hh
            qm = jnp.where(_head_mask(qc.shape, QK_NOPE, hh), qc, jnp.zeros_like(qc))
            qrm = jnp.where(_head_mask(qrc.shape, QK_ROPE, h % 4), qrc, jnp.zeros_like(qrc))
            qcat = jnp.concatenate([qm, qrm], axis=1)
            lg = lax.dot_general(qcat, kcat, _NT, preferred_element_type=f32)
            lg = jnp.where(causal, lg, NEG_BIG)
            m = jnp.max(lg, axis=1, keepdims=True)
            p = jnp.exp(lg - m)
            den = jnp.sum(p, axis=1, keepdims=True)
            pv = jnp.dot(p.astype(bf), vc, preferred_element_type=f32)
            outs.append(pv / den)
        o_ref[:, sl] = jnp.where(_head_mask(outs[0].shape, V_DIM, 0), outs[0], outs[1]).astype(bf)
        return carry

    lax.fori_loop(0, B_WIDTH // LANES, slab, 0, unroll=2)


def _mla_kernel(*refs, seq):
    tq = refs[0].shape[0]
    _per_query_block(_mla_block, refs, seq // tq, tq)


def _mla(qn, qr, kn, kr, vb, bsz, seq):
    tq = min(Q_TILE, seq)
    nq = seq // tq
    qspec = lambda w: pl.BlockSpec((tq, w), lambda b, i: (b * nq + i, 0))
    kspec = lambda w: pl.BlockSpec((seq, w), lambda b, i: (b, 0))
    return pl.pallas_call(
        functools.partial(_mla_kernel, seq=seq),
        grid=(bsz, nq),
        in_specs=[qspec(B_HEADS * QK_NOPE), qspec(B_HEADS * QK_ROPE), kspec(B_HEADS * QK_NOPE),
                  kspec(LANES), kspec(B_WIDTH)],
        out_specs=qspec(B_WIDTH),
        out_shape=jax.ShapeDtypeStruct((bsz * seq, B_WIDTH), jnp.bfloat16),
        compiler_params=_cparams(("arbitrary", "arbitrary")),
        name="mla",
    )(qn, qr, kn, kr, vb)


def _post_kernel(x_ref, oa_ref, ob_ref, wg_ref, bg_ref, woa_ref, wob_ref, wout_ref, g1_ref, b1_ref,
                 wrh_ref, wrl_ref, br_ref, h_ref, ri_ref, rf_ref, cnt_ref, carry_ref):
    f32 = jnp.float32
    bf = jnp.bfloat16
    t = x_ref.shape[0]

    @pl.when(pl.program_id(0) == 0)
    def _():
        carry_ref[...] = jnp.zeros_like(carry_ref)

    x = x_ref[...]
    gate = jax.nn.sigmoid(jnp.dot(x.astype(bf), wg_ref[...], preferred_element_type=f32) + bg_ref[...])
    pa = jnp.dot(oa_ref[...], woa_ref[...], preferred_element_type=f32)
    pb = jnp.dot(ob_ref[...], wob_ref[...], preferred_element_type=f32)
    mix = gate[:, :D_MODEL] * pa + gate[:, D_MODEL:] * pb
    u = DN_ALPHA * x + jnp.dot(mix.astype(bf), wout_ref[...], preferred_element_type=f32)
    h = _layer_norm(u, g1_ref[...], b1_ref[...])
    _store_token_rows(h_ref, h)

    h_hi = h.astype(bf)
    h_lo = (h - h_hi.astype(f32)).astype(bf)
    logits = (jnp.dot(h_hi, wrh_ref[...], preferred_element_type=f32)
              + jnp.dot(h_lo, wrh_ref[...], preferred_element_type=f32)
              + jnp.dot(h_hi, wrl_ref[...], preferred_element_type=f32) + br_ref[...])
    lane = lax.broadcasted_iota(jnp.int32, (t, LANES), 1).astype(f32)
    lg = jnp.where(lane < N_EXPERTS, logits, -jnp.inf)
    vals, ids = [], []
    assign = jnp.zeros((t, LANES), f32)
    for _k in range(TOP_K):
        m = jnp.max(lg, axis=1, keepdims=True)
        idx = jnp.min(jnp.where(lg == m, lane, float(LANES)), axis=1, keepdims=True)
        hit = lane == idx
        vals.append(m)
        ids.append(idx)
        assign = jnp.where(hit, 1.0, assign)
        lg = jnp.where(hit, -jnp.inf, lg)
    exps = [jnp.exp(v - vals[0]) for v in vals]
    den = exps[0] + exps[1] + exps[2] + exps[3]

    r = lax.broadcasted_iota(jnp.int32, (t, t), 0)
    c = lax.broadcasted_iota(jnp.int32, (t, t), 1)
    lower = jnp.where(c < r, 1.0, 0.0).astype(bf)
    before = jnp.dot(lower, assign.astype(bf), preferred_element_type=f32) + carry_ref[0:1, :]
    ri = jnp.zeros((t, LANES), f32)
    rf = jnp.zeros((t, LANES), f32)
    for k in range(TOP_K):
        rank = jnp.sum(jnp.where(lane == ids[k], before, 0.0), axis=1, keepdims=True)
        ri = jnp.where(lane == float(k), ids[k], ri)
        ri = jnp.where(lane == float(TOP_K + k), rank, ri)
        rf = jnp.where(lane == float(k), exps[k] / den, rf)
    ri_ref[...] = ri.astype(jnp.int32)
    rf_ref[...] = rf
    total = carry_ref[0:1, :] + jnp.sum(assign, axis=0, keepdims=True)
    carry_ref[...] = jnp.broadcast_to(total, carry_ref.shape)
    cnt_ref[...] = jnp.broadcast_to(total, cnt_ref.shape)


def _post(x2, oa, ob, wg, bg, woa, wob, wout, g1, b1, wrh, wrl, br):
    n = x2.shape[0]
    t = PROJ_TILE
    tok = lambda w: pl.BlockSpec((t, w), lambda i: (i, 0))
    consts = (wg, bg, woa, wob, wout, g1, b1, wrh, wrl, br)
    return pl.pallas_call(
        _post_kernel,
        grid=(n // t,),
        in_specs=[tok(D_MODEL), tok(A_WIDTH), tok(B_WIDTH)] + [_full(a.shape) for a in consts],
        out_specs=[pl.BlockSpec((t * ROW_TILES, LANES), lambda i: (i, 0)), tok(LANES), tok(LANES),
                   _full((8, LANES))],
        out_shape=[jax.ShapeDtypeStruct((n * ROW_TILES, LANES), jnp.float32),
                   jax.ShapeDtypeStruct((n, LANES), jnp.int32),
                   jax.ShapeDtypeStruct((n, LANES), jnp.float32),
                   jax.ShapeDtypeStruct((8, LANES), jnp.float32)],
        scratch_shapes=[pltpu.VMEM((8, LANES), jnp.float32)],
        compiler_params=_cparams(("arbitrary",)),
        name="post",
    )(x2, oa, ob, *consts)


def _token_row(ref, r):
    start = r * ROW_TILES if isinstance(r, int) else pl.multiple_of(r * ROW_TILES, ROW_TILES)
    return ref.at[pl.ds(start, ROW_TILES)]


def _row_gather(idx_ref, src_ref, dst_ref, sem, rows):
    for j in range(rows):
        pltpu.make_async_copy(_token_row(src_ref, idx_ref[0, 0, j]), _token_row(dst_ref, j), sem).start(
            priority=j % 2)


def _rows_wait(src_ref, dst_ref, sem):
    pltpu.make_async_copy(src_ref, dst_ref, sem).wait()


def _dispatch_kernel(dest_ref, h_ref, xs_ref, sem):
    t = MOVE_TILE
    for k in range(TOP_K):
        for j in range(t):
            pltpu.make_async_copy(_token_row(h_ref, j), _token_row(xs_ref, dest_ref[0, 0, k * t + j]), sem).start(
                priority=j % 2)
    for k in range(TOP_K):
        _rows_wait(h_ref, xs_ref.at[pl.ds(0, t * ROW_TILES)], sem)


def _dispatch(dest3, h, rows):
    t = MOVE_TILE
    n = h.shape[0] // ROW_TILES
    return pl.pallas_call(
        _dispatch_kernel,
        grid=(n // t,),
        in_specs=[pl.BlockSpec((1, 1, t * TOP_K), lambda i: (i, 0, 0), memory_space=pltpu.SMEM),
                  pl.BlockSpec((t * ROW_TILES, LANES), lambda i: (i, 0))],
        out_specs=pl.BlockSpec(memory_space=pl.ANY),
        out_shape=jax.ShapeDtypeStruct((rows * ROW_TILES, LANES), jnp.float32),
        scratch_shapes=[pltpu.SemaphoreType.DMA(())],
        compiler_params=_cparams(("arbitrary",)),
        name="dispatch",
    )(dest3, h)


def _expert_kernel(te_ref, tt_ref, lo_ref, hi_ref, ns_ref, xs_ref, wu_ref, bu_ref, wd_ref, bd_ref, ys_ref,
                   wub_ref, wdb_ref):
    i = pl.program_id(0)
    f32 = jnp.float32
    bf = jnp.bfloat16
    t = EXPERT_TILE
    valid = i < ns_ref[0]
    before = jnp.maximum(i - 1, 0)
    fresh = valid & ((i == 0) | (te_ref[i] != te_ref[before]))
    first_visit = (i == 0) | (tt_ref[i] != tt_ref[before])

    @pl.when(fresh)
    def _():
        rows = 128

        def cast_up(j, c):
            r = pl.multiple_of(j * rows, rows)
            wub_ref[pl.ds(r, rows), :] = wu_ref[pl.ds(r, rows), :].astype(bf)
            return c

        lax.fori_loop(0, D_MODEL // rows, cast_up, 0)

        def cast_dn(j, c):
            r = pl.multiple_of(j * rows, rows)
            wdb_ref[pl.ds(r, rows), :] = wd_ref[pl.ds(r, rows), :].astype(bf)
            return c

        lax.fori_loop(0, D_FF // rows, cast_dn, 0)

    def compute():
        xb = _load_token_rows(xs_ref, 0, t).astype(bf)
        y = jnp.broadcast_to(bd_ref[...], (t, D_MODEL))
        for c in range(D_FF // FF_CHUNK):
            gs = slice(c * FF_CHUNK, (c + 1) * FF_CHUNK)
            ls = slice(D_FF + c * FF_CHUNK, D_FF + (c + 1) * FF_CHUNK)
            ag = jnp.dot(xb, wub_ref[:, gs], preferred_element_type=f32) + bu_ref[:, gs]
            al = jnp.dot(xb, wub_ref[:, ls], preferred_element_type=f32) + bu_ref[:, ls]
            g = jnp.minimum(ag, SWIGLU_LIMIT)
            lin = jnp.clip(al, -SWIGLU_LIMIT, SWIGLU_LIMIT)
            act = (lin + 1.0) * (g * jax.nn.sigmoid(SWIGLU_ALPHA * g))
            y = y + jnp.dot(act.astype(bf), wdb_ref[gs, :], preferred_element_type=f32)
        row = lax.broadcasted_iota(jnp.int32, (t, 1), 0)
        return y, (row >= lo_ref[i]) & (row < hi_ref[i])

    @pl.when(valid & first_visit)
    def _():
        y, mine = compute()
        _store_token_rows(ys_ref, jnp.where(mine, y, 0.0))

    @pl.when(valid & jnp.logical_not(first_visit))
    def _():
        y, mine = compute()
        _store_token_rows(ys_ref, jnp.where(mine, y, _load_token_rows(ys_ref, 0, t)))


def _experts(steps, xs, w_up, b_up, w_down, b_down):
    t = EXPERT_TILE
    max_steps = steps[0].shape[0]
    row_block = pl.BlockSpec((t * ROW_TILES, LANES), lambda i, te, tt, lo, hi, ns: (tt[i], 0))
    per_expert = lambda shape: pl.BlockSpec((None,) + shape, lambda i, te, tt, lo, hi, ns: (te[i], 0, 0))
    grid_spec = pltpu.PrefetchScalarGridSpec(
        num_scalar_prefetch=5,
        grid=(max_steps,),
        in_specs=[row_block, per_expert((D_MODEL, 2 * D_FF)), per_expert((1, 2 * D_FF)),
                  per_expert((D_FF, D_MODEL)), per_expert((1, D_MODEL))],
        out_specs=row_block,
        scratch_shapes=[pltpu.VMEM((D_MODEL, 2 * D_FF), jnp.bfloat16),
                        pltpu.VMEM((D_FF, D_MODEL), jnp.bfloat16)],
    )
    return pl.pallas_call(
        _expert_kernel,
        grid_spec=grid_spec,
        out_shape=jax.ShapeDtypeStruct(xs.shape, jnp.float32),
        compiler_params=_cparams(("arbitrary",)),
        name="experts",
    )(*steps, xs, w_up, b_up.reshape(N_EXPERTS, 1, 2 * D_FF), w_down, b_down.reshape(N_EXPERTS, 1, D_MODEL))


def _combine_kernel(cur_ref, nxt_ref, ys_ref, h_ref, rf_ref, g2_ref, b2_ref, o_ref, buf_ref, sems):
    i = pl.program_id(0)
    n_steps = pl.num_programs(0)
    t = MOVE_TILE
    rows = t * TOP_K
    slot = i % 2

    @pl.when(i == 0)
    def _():
        _row_gather(cur_ref, ys_ref, buf_ref.at[0], sems.at[0], rows)

    _rows_wait(ys_ref.at[pl.ds(0, rows * ROW_TILES)], buf_ref.at[slot], sems.at[slot])
    _row_gather(nxt_ref, ys_ref, buf_ref.at[1 - slot], sems.at[1 - slot], rows)

    rf = rf_ref[...]
    u = DN_ALPHA * _load_token_rows(h_ref, 0, t)
    for k in range(TOP_K):
        u = u + rf[:, k:k + 1] * _load_token_rows(buf_ref.at[slot], k * t, t)
    o_ref[...] = _layer_norm(u, g2_ref[...], b2_ref[...])

    @pl.when(i == n_steps - 1)
    def _():
        _rows_wait(ys_ref.at[pl.ds(0, rows * ROW_TILES)], buf_ref.at[1 - slot], sems.at[1 - slot])


def _combine(dest3, ys, h, rf, g2, b2):
    t = MOVE_TILE
    n = h.shape[0] // ROW_TILES
    last = n // t - 1
    tok = lambda w: pl.BlockSpec((t, w), lambda i: (i, 0))
    idx_spec = lambda f: pl.BlockSpec((1, 1, t * TOP_K), f, memory_space=pltpu.SMEM)
    return pl.pallas_call(
        _combine_kernel,
        grid=(n // t,),
        in_specs=[idx_spec(lambda i: (i, 0, 0)), idx_spec(lambda i: (jnp.minimum(i + 1, last), 0, 0)),
                  pl.BlockSpec(memory_space=pl.ANY), pl.BlockSpec((t * ROW_TILES, LANES), lambda i: (i, 0)),
                  tok(LANES), _full(g2.shape), _full(b2.shape)],
        out_specs=tok(D_MODEL),
        out_shape=jax.ShapeDtypeStruct((n, D_MODEL), jnp.float32),
        scratch_shapes=[pltpu.VMEM((2, t * TOP_K * ROW_TILES, LANES), jnp.float32),
                        pltpu.SemaphoreType.DMA((2,))],
        compiler_params=_cparams(("arbitrary",)),
        name="combine",
    )(dest3, dest3, ys, h, rf, g2, b2)


def _rope_tables(positions, half):
    inv_freq = ROPE_THETA ** (-jnp.arange(half, dtype=jnp.float32) / half)
    ang = positions.astype(jnp.float32).reshape(-1, 1) * inv_freq
    reps = LANES // half
    return jnp.tile(jnp.cos(ang), (1, reps)), jnp.tile(jnp.sin(ang), (1, reps))


def _lookup(table, idx):
    experts = jnp.arange(N_EXPERTS, dtype=jnp.int32)
    return jnp.sum(jnp.where(idx[..., None] == experts, table, 0), axis=-1)


def _expert_schedule(starts, ends, rows):
    t = EXPERT_TILE
    first_tile = starts // t
    n_tile = jnp.where(ends > starts, (ends - 1) // t - first_tile + 1, 0)
    cum = jnp.cumsum(n_tile)
    n_steps = cum[-1]
    max_steps = rows // t + N_EXPERTS
    s = jnp.minimum(jnp.arange(max_steps, dtype=jnp.int32), n_steps - 1)
    expert = jnp.sum((cum[None, :] <= s[:, None]).astype(jnp.int32), axis=1)
    tile = _lookup(first_tile, expert) + s - _lookup(cum - n_tile, expert)
    lo = jnp.maximum(_lookup(starts, expert) - tile * t, 0)
    hi = jnp.minimum(_lookup(ends, expert) - tile * t, t)
    return expert, tile, lo, hi, n_steps.astype(jnp.int32).reshape(1)


def _layer(x2, tabs, bsz, seq, w_in, b_gate, rms_cq, rms_ckv, w_uq, w_ukv, w_o_a, w_o_b, w_out,
           ln1_g, ln1_b, w_router, b_router, w_up, b_up, w_down, b_down, ln2_g, ln2_b):
    bf = jnp.bfloat16
    n = x2.shape[0]
    off = np.concatenate([[0], np.cumsum(SPLITS)]).tolist()
    col = lambda j: w_in[:, off[j]:off[j + 1]]
    w1 = jnp.concatenate([col(0), col(1), col(2), col(3)], axis=1).astype(bf)
    wi_pad = jnp.zeros((D_MODEL, LANES - IDX_HEADS), w_in.dtype)
    ws = jnp.concatenate([col(4), col(4), jnp.tile(col(8), (1, LANES // QK_ROPE)), col(5), wi_pad], axis=1).astype(bf)
    wl = jnp.concatenate([col(6), col(7)], axis=1).astype(bf)
    wg = col(9).astype(bf)
    uq = w_uq.reshape(Q_LORA, B_HEADS, QK_NOPE + QK_ROPE)
    wuq = jnp.concatenate([uq[:, :, :QK_NOPE].reshape(Q_LORA, -1), uq[:, :, QK_NOPE:].reshape(Q_LORA, -1)],
                          axis=1).astype(bf)
    ukv = w_ukv.reshape(KV_LORA, B_HEADS, QK_NOPE + V_DIM)
    wukv = jnp.concatenate([ukv[:, :, :QK_NOPE].reshape(KV_LORA, -1), ukv[:, :, QK_NOPE:].reshape(KV_LORA, -1)],
                           axis=1).astype(bf)

    qa, ka, va, qi, ki, kr, wi, qn, qr, kn, vb = _proj(
        x2, tabs, w1, ws, wl, rms_cq.reshape(1, -1), rms_ckv.reshape(1, -1), wuq, wukv)
    o_a = _dsa(qa, qi, wi, ka, va, ki, bsz, seq)
    o_b = _mla(qn, qr, kn, kr, vb, bsz, seq)

    wr = jnp.concatenate([w_router, jnp.zeros((D_MODEL, LANES - N_EXPERTS), w_router.dtype)], axis=1)
    br = jnp.concatenate([b_router, jnp.zeros((LANES - N_EXPERTS,), b_router.dtype)]).reshape(1, -1)
    wr_hi = wr.astype(bf)
    wr_lo = (wr - wr_hi.astype(jnp.float32)).astype(bf)
    h, ri, rf, cnt = _post(x2, o_a, o_b, wg, b_gate.reshape(1, -1), w_o_a.astype(bf), w_o_b.astype(bf),
                           w_out.astype(bf), ln1_g.reshape(1, -1), ln1_b.reshape(1, -1), wr_hi, wr_lo, br)

    counts = cnt[0, :N_EXPERTS].astype(jnp.int32)
    ends = jnp.cumsum(counts)
    starts = ends - counts
    dest = _lookup(starts, ri[:, :TOP_K]) + ri[:, TOP_K:2 * TOP_K]
    dest3 = dest.reshape(n // MOVE_TILE, MOVE_TILE, TOP_K).transpose(0, 2, 1).reshape(
        n // MOVE_TILE, 1, MOVE_TILE * TOP_K)

    xs = _dispatch(dest3, h, n * TOP_K)
    ys = _experts(_expert_schedule(starts, ends, n * TOP_K), xs, w_up, b_up, w_down, b_down)
    return _combine(dest3, ys, h, rf, ln2_g.reshape(1, -1), ln2_b.reshape(1, -1))


def kernel(x, positions, w_in, b_gate, rms_cq, rms_ckv, w_uq, w_ukv, w_o_a, w_o_b, w_out, ln1_g, ln1_b,
           w_router, b_router, w_up, b_up, w_down, b_down, ln2_g, ln2_b):
    bsz, seq, _ = x.shape
    x2 = x.reshape(bsz * seq, D_MODEL)
    c64, s64 = _rope_tables(positions, A_HEAD_DIM // 2)
    c32, s32 = _rope_tables(positions, QK_ROPE // 2)
    tabs = (c64, s64, c32, s32)
    for l in range(DEPTH):
        x2 = _layer(x2, tabs, bsz, seq, w_in[l], b_gate[l], rms_cq[l], rms_ckv[l], w_uq[l], w_ukv[l],
                    w_o_a[l], w_o_b[l], w_out[l], ln1_g[l], ln1_b[l], w_router[l], b_router[l],
                    w_up[l], b_up[l], w_down[l], b_down[l], ln2_g[l], ln2_b[l])
    return x2.reshape(bsz, seq, D_MODEL)
```

```python
import functools

import jax
import jax.numpy as jnp
import numpy as np
from jax import lax
from jax.experimental import pallas as pl
from jax.experimental.pallas import tpu as pltpu

D_MODEL = 1024
A_HEADS = 8
A_HEAD_DIM = 64
IDX_HEADS = 8
IDX_DIM = 64
TOPK_MAX = 256
B_HEADS = 8
Q_LORA = 384
KV_LORA = 256
QK_NOPE = 64
QK_ROPE = 32
V_DIM = 64
ROPE_THETA = 10000.0
N_EXPERTS = 32
TOP_K = 4
D_FF = 1024
SWIGLU_LIMIT = 7.0
SWIGLU_ALPHA = 1.702
DEPTH = 1
DN_ALPHA = (2 * DEPTH) ** 0.25
LN_EPS = 1e-5
RMS_EPS = 1e-6
A_WIDTH = A_HEADS * A_HEAD_DIM
B_WIDTH = B_HEADS * V_DIM
SPLITS = (A_WIDTH, A_WIDTH, A_WIDTH, IDX_HEADS * IDX_DIM, IDX_DIM, IDX_HEADS,
          Q_LORA, KV_LORA, QK_ROPE, 2 * D_MODEL)

LANES = 128
VMEM_LIMIT_BYTES = 52 * 1024 * 1024

PROJ_TILE = 512
Q_TILE = 256
EXPERT_TILE = 512
FF_CHUNK = 256
MOVE_TILE = 256

ROW_TILES = D_MODEL // LANES
assert ROW_TILES == 8

LOG2_E = 1.4426950408889634
INT_MIN = -(2 ** 31)
NEG_BIG = -1e30

_NT = (((1,), (1,)), ((), ()))


def _cparams(sem):
    return pltpu.CompilerParams(dimension_semantics=sem, vmem_limit_bytes=VMEM_LIMIT_BYTES)


def _full(shape):
    nd = len(shape)
    return pl.BlockSpec(shape, lambda *_: (0,) * nd)


def _rope_chunk(x, cos, sin, half):
    lane = lax.broadcasted_iota(jnp.int32, x.shape, 1)
    first = (lane % (2 * half)) < half
    up = pltpu.roll(x, LANES - half, 1)
    dn = pltpu.roll(x, half, 1)
    return x * cos + jnp.where(first, -up, dn) * sin


def _layer_norm(u, g, b):
    mu = jnp.mean(u, axis=-1, keepdims=True)
    d = u - mu
    var = jnp.mean(d * d, axis=-1, keepdims=True)
    return d * lax.rsqrt(var + LN_EPS) * g + b


def _load_token_rows(ref, first, t):
    return jnp.concatenate([ref[pl.ds(first * ROW_TILES + s, t, stride=ROW_TILES), :] for s in range(ROW_TILES)],
                           axis=1)


def _store_token_rows(ref, v):
    t = v.shape[0]
    for s in range(ROW_TILES):
        ref[pl.ds(s, t, stride=ROW_TILES), :] = v[:, s * LANES:(s + 1) * LANES]


def _two_head_softmax_pv(logits, vc, width):
    lane_v = lax.broadcasted_iota(jnp.int32, vc.shape, 1)
    pvs = []
    for hh in range(2):
        lg = logits[hh]
        p = jnp.exp2(lg - jnp.max(lg, axis=1, keepdims=True)).astype(jnp.bfloat16)
        v_h = jnp.where((lane_v // width) == hh, vc, jnp.ones_like(vc))
        pvs.append(jnp.dot(p, v_h, preferred_element_type=jnp.float32))
    lane = lax.broadcasted_iota(jnp.int32, pvs[0].shape, 1)
    first = lane < width
    num = jnp.where(first, pvs[0], pvs[1])
    den = jnp.where(first, pltpu.roll(pvs[0], width, 1), pltpu.roll(pvs[1], width, 1))
    return num / den


def _head_mask(shape, width, which):
    lane = lax.broadcasted_iota(jnp.int32, shape, 1)
    return (lane // width) == which


def _proj_kernel(x_ref, c64_ref, s64_ref, c32_ref, s32_ref, w1_ref, ws_ref, wl_ref,
                 gcq_ref, gckv_ref, wuq_ref, wukv_ref,
                 qa_ref, ka_ref, va_ref, qi_ref, ki_ref, kr_ref, wi_ref,
                 qn_ref, qr_ref, kn_ref, vb_ref):
    xb = x_ref[...].astype(jnp.bfloat16)
    c64, s64 = c64_ref[...], s64_ref[...]
    c32, s32 = c32_ref[...], s32_ref[...]
    f32 = jnp.float32
    bf = jnp.bfloat16

    z1 = jnp.dot(xb, w1_ref[...], preferred_element_type=f32)
    n_chunk = A_WIDTH // LANES
    a_scale = A_HEAD_DIM ** -0.5 * LOG2_E
    for c in range(n_chunk):
        sl = slice(c * LANES, (c + 1) * LANES)
        qa_ref[:, sl] = (_rope_chunk(z1[:, c * LANES:(c + 1) * LANES], c64, s64, 32) * a_scale).astype(bf)
        o = A_WIDTH + c * LANES
        ka_ref[:, sl] = _rope_chunk(z1[:, o:o + LANES], c64, s64, 32).astype(bf)
        o = 2 * A_WIDTH + c * LANES
        va_ref[:, sl] = z1[:, o:o + LANES].astype(bf)
        o = 3 * A_WIDTH + c * LANES
        qi_ref[:, sl] = _rope_chunk(z1[:, o:o + LANES], c64, s64, 32).astype(bf)

    zs = jnp.dot(xb, ws_ref[...], preferred_element_type=f32)
    ki_ref[...] = _rope_chunk(zs[:, 0:LANES], c64, s64, 32).astype(bf)
    kr_ref[...] = _rope_chunk(zs[:, LANES:2 * LANES], c32, s32, 16).astype(bf)
    wi_ref[...] = zs[:, 2 * LANES:3 * LANES] * (IDX_HEADS ** -0.5 * IDX_DIM ** -0.5)

    zl = jnp.dot(xb, wl_ref[...], preferred_element_type=f32)
    cq = zl[:, :Q_LORA]
    cq = cq * lax.rsqrt(jnp.mean(cq * cq, axis=-1, keepdims=True) + RMS_EPS) * gcq_ref[...]
    ckv = zl[:, Q_LORA:]
    ckv = ckv * lax.rsqrt(jnp.mean(ckv * ckv, axis=-1, keepdims=True) + RMS_EPS) * gckv_ref[...]
    qb = jnp.dot(cq.astype(bf), wuq_ref[...], preferred_element_type=f32)
    kv = jnp.dot(ckv.astype(bf), wukv_ref[...], preferred_element_type=f32)
    b_scale = (QK_NOPE + QK_ROPE) ** -0.5 * LOG2_E
    qn_ref[...] = (qb[:, :B_HEADS * QK_NOPE] * b_scale).astype(bf)
    for c in range(B_HEADS * QK_ROPE // LANES):
        o = B_HEADS * QK_NOPE + c * LANES
        qr_ref[:, c * LANES:(c + 1) * LANES] = (
            _rope_chunk(qb[:, o:o + LANES], c32, s32, 16) * b_scale).astype(bf)
    kn_ref[...] = kv[:, :B_HEADS * QK_NOPE].astype(bf)
    vb_ref[...] = kv[:, B_HEADS * QK_NOPE:].astype(bf)


def _proj(x2, tabs, w1, ws, wl, gcq, gckv, wuq, wukv):
    n = x2.shape[0]
    t = PROJ_TILE
    tok = lambda w: pl.BlockSpec((t, w), lambda i: (i, 0))
    bf = jnp.bfloat16
    outs = [(A_WIDTH, bf)] * 4 + [(LANES, bf), (LANES, bf), (LANES, jnp.float32),
                                  (B_HEADS * QK_NOPE, bf), (B_HEADS * QK_ROPE, bf),
                                  (B_HEADS * QK_NOPE, bf), (B_WIDTH, bf)]
    return pl.pallas_call(
        _proj_kernel,
        grid=(n // t,),
        in_specs=[tok(D_MODEL)] + [tok(LANES)] * 4 + [_full(a.shape) for a in (w1, ws, wl, gcq, gckv, wuq, wukv)],
        out_specs=[tok(w) for w, _ in outs],
        out_shape=[jax.ShapeDtypeStruct((n, w), d) for w, d in outs],
        compiler_params=_cparams(("arbitrary",)),
        name="proj",
    )(x2, *tabs, w1, ws, wl, gcq, gckv, wuq, wukv)


def _tie_select(eq, gt, need, kv):
    chunk = 256
    r = lax.broadcasted_iota(jnp.int32, (chunk, chunk), 0)
    c = lax.broadcasted_iota(jnp.int32, (chunk, chunk), 1)
    upper = jnp.where(r <= c, 1.0, 0.0).astype(jnp.bfloat16)
    carry = jnp.zeros((eq.shape[0], 1), jnp.float32)
    parts = []
    for j in range(kv // chunk):
        e = eq[:, j * chunk:(j + 1) * chunk]
        ef = jnp.where(e, 1.0, 0.0).astype(jnp.bfloat16)
        pref = jnp.dot(ef, upper, preferred_element_type=jnp.float32) + carry
        parts.append(gt[:, j * chunk:(j + 1) * chunk] | (e & (pref <= need)))
        carry = carry + jnp.sum(ef.astype(jnp.float32), axis=1, keepdims=True)
    return jnp.concatenate(parts, axis=1)


def _key_to_float(key):
    return lax.bitcast_convert_type(key ^ ((key >> 31) & jnp.int32(0x7FFFFFFF)), jnp.float32)


def _select_topk(score, k_eff, sel_ref, kv):
    tq = score.shape[0]

    def search(it, prefix):
        cand = prefix + lax.shift_left(jnp.int32(1), 31 - it)
        cnt = jnp.sum(jnp.where(score >= _key_to_float(cand), 1.0, 0.0), axis=1, keepdims=True)
        return jnp.where(cnt >= k_eff, cand, prefix)

    thr = _key_to_float(lax.fori_loop(0, 32, search, jnp.full((tq, 1), INT_MIN, jnp.int32)))
    gt = score > thr
    eq = score == thr
    n_gt = jnp.sum(jnp.where(gt, 1.0, 0.0), axis=1, keepdims=True)
    n_eq = jnp.sum(jnp.where(eq, 1.0, 0.0), axis=1, keepdims=True)
    need = k_eff - n_gt
    sel_ref[:, :kv] = jnp.where(score >= thr, 1.0, 0.0)
    has_tie = jnp.max(jnp.where(n_eq != need, 1.0, 0.0)) > 0.5

    @pl.when(has_tie)
    def _():
        sel_ref[:, :kv] = jnp.where(_tie_select(eq, gt, need, kv), 1.0, 0.0)


def _dsa_block(qa_ref, qi_ref, wi_ref, ka_ref, va_ref, ki_ref, o_ref, sel_ref, score_ref, *, kv, k_sel):
    tq = qa_ref.shape[0]
    f32 = jnp.float32
    bf = jnp.bfloat16
    q_pos = (kv - tq) + lax.broadcasted_iota(jnp.int32, (tq, 1), 0)
    key_pos = lax.broadcasted_iota(jnp.int32, (tq, kv), 1)
    causal = key_pos <= q_pos

    if kv <= k_sel:
        sel_ref[:, :kv] = jnp.where(causal, 1.0, 0.0)
    else:
        ki = ki_ref[:kv, :]
        lane = lax.broadcasted_iota(jnp.int32, (tq, LANES), 1)
        score_ref[:, :kv] = jnp.zeros((tq, kv), f32)

        def idx_slab(c, carry):
            qc = qi_ref[:, pl.ds(pl.multiple_of(c * LANES, LANES), LANES)]
            acc = score_ref[:, :kv]
            for hh in range(2):
                qm = jnp.where(_head_mask(qc.shape, IDX_DIM, hh), qc, jnp.zeros_like(qc))
                s = lax.dot_general(qm, ki, _NT, preferred_element_type=f32)
                w = jnp.sum(jnp.where(lane == 2 * c + hh, wi_ref[...], 0.0), axis=1, keepdims=True)
                acc = acc + jnp.maximum(s, 0.0) * w
            score_ref[:, :kv] = acc
            return carry

        lax.fori_loop(0, IDX_HEADS // 2, idx_slab, 0)
        score = jnp.where(causal, score_ref[:, :kv], -jnp.inf)
        k_eff = jnp.minimum(q_pos + 1, k_sel).astype(f32)
        _select_topk(score, k_eff, sel_ref, kv)

    def attn_slab(c, carry):
        sl = pl.ds(pl.multiple_of(c * LANES, LANES), LANES)
        qc = qa_ref[:, sl]
        kc = ka_ref[:kv, sl]
        vc = va_ref[:kv, sl]
        sel = sel_ref[:, :kv] > 0.5
        logits = []
        for hh in range(2):
            qm = jnp.where(_head_mask(qc.shape, A_HEAD_DIM, hh), qc, jnp.zeros_like(qc))
            lg = lax.dot_general(qm, kc, _NT, preferred_element_type=f32)
            logits.append(jnp.where(sel, lg, NEG_BIG))
        o_ref[:, sl] = _two_head_softmax_pv(logits, vc, A_HEAD_DIM).astype(bf)
        return carry

    lax.fori_loop(0, A_WIDTH // LANES, attn_slab, 0)


def _per_query_block(block_fn, refs, n_blocks, tq):
    j = pl.program_id(1)
    for b in range(n_blocks):
        pl.when(j == b)(functools.partial(block_fn, *refs, kv=(b + 1) * tq))


def _dsa_kernel(*refs, seq, k_sel):
    tq = refs[0].shape[0]
    _per_query_block(functools.partial(_dsa_block, k_sel=k_sel), refs, seq // tq, tq)


def _dsa(qa, qi, wi, ka, va, ki, bsz, seq):
    tq = min(Q_TILE, seq)
    nq = seq // tq
    qspec = lambda w: pl.BlockSpec((tq, w), lambda b, i: (b * nq + i, 0))
    kspec = lambda w: pl.BlockSpec((seq, w), lambda b, i: (b, 0))
    k_sel = min(TOPK_MAX, seq // 4)
    return pl.pallas_call(
        functools.partial(_dsa_kernel, seq=seq, k_sel=k_sel),
        grid=(bsz, nq),
        in_specs=[qspec(A_WIDTH), qspec(A_WIDTH), qspec(LANES), kspec(A_WIDTH), kspec(A_WIDTH), kspec(LANES)],
        out_specs=qspec(A_WIDTH),
        out_shape=jax.ShapeDtypeStruct((bsz * seq, A_WIDTH), jnp.bfloat16),
        scratch_shapes=[pltpu.VMEM((tq, seq), jnp.float32), pltpu.VMEM((tq, seq), jnp.float32)],
        compiler_params=_cparams(("arbitrary", "arbitrary")),
        name="dsa",
    )(qa, qi, wi, ka, va, ki)


def _mla_block(qn_ref, qr_ref, kn_ref, kr_ref, vb_ref, o_ref, *, kv):
    tq = qn_ref.shape[0]
    f32 = jnp.float32
    bf = jnp.bfloat16
    q_pos = (kv - tq) + lax.broadcasted_iota(jnp.int32, (tq, 1), 0)
    key_pos = lax.broadcasted_iota(jnp.int32, (tq, kv), 1)
    causal = key_pos <= q_pos
    kr = kr_ref[:kv, :]

    def slab(c, carry):
        sl = pl.ds(pl.multiple_of(c * LANES, LANES), LANES)
        qc = qn_ref[:, sl]
        kcat = jnp.concatenate([kn_ref[:kv, sl], kr], axis=1)
        vc = vb_ref[:kv, sl]
        qrc = qr_ref[:, pl.ds(pl.multiple_of((c // 2) * LANES, LANES), LANES)]
        logits = []
        for hh in range(2):
            h = 2 * c + hh
            qm = jnp.where(_head_mask(qc.shape, QK_NOPE, hh), qc, jnp.zeros_like(qc))
            qrm = jnp.where(_head_mask(qrc.shape, QK_ROPE, h % 4), qrc, jnp.zeros_like(qrc))
            qcat = jnp.concatenate([qm, qrm], axis=1)
            lg = lax.dot_general(qcat, kcat, _NT, preferred_element_type=f32)
            logits.append(jnp.where(causal, lg, NEG_BIG))
        o_ref[:, sl] = _two_head_softmax_pv(logits, vc, V_DIM).astype(bf)
        return carry

    lax.fori_loop(0, B_WIDTH // LANES, slab, 0, unroll=2)


def _mla_kernel(*refs, seq):
    tq = refs[0].shape[0]
    _per_query_block(_mla_block, refs, seq // tq, tq)


def _mla(qn, qr, kn, kr, vb, bsz, seq):
    tq = min(Q_TILE, seq)
    nq = seq // tq
    qspec = lambda w: pl.BlockSpec((tq, w), lambda b, i: (b * nq + i, 0))
    kspec = lambda w: pl.BlockSpec((seq, w), lambda b, i: (b, 0))
    return pl.pallas_call(
        functools.partial(_mla_kernel, seq=seq),
        grid=(bsz, nq),
        in_specs=[qspec(B_HEADS * QK_NOPE), qspec(B_HEADS * QK_ROPE), kspec(B_HEADS * QK_NOPE),
                  kspec(LANES), kspec(B_WIDTH)],
        out_specs=qspec(B_WIDTH),
        out_shape=jax.ShapeDtypeStruct((bsz * seq, B_WIDTH), jnp.bfloat16),
        compiler_params=_cparams(("arbitrary", "arbitrary")),
        name="mla",
    )(qn, qr, kn, kr, vb)


def _post_kernel(x_ref, oa_ref, ob_ref, wg_ref, bg_ref, woa_ref, wob_ref, wout_ref, g1_ref, b1_ref,
                 wrh_ref, wrl_ref, br_ref, h_ref, ri_ref, rf_ref, cnt_ref, carry_ref):
    f32 = jnp.float32
    bf = jnp.bfloat16
    t = x_ref.shape[0]

    @pl.when(pl.program_id(0) == 0)
    def _():
        carry_ref[...] = jnp.zeros_like(carry_ref)

    x = x_ref[...]
    gate = jax.nn.sigmoid(jnp.dot(x.astype(bf), wg_ref[...], preferred_element_type=f32) + bg_ref[...])
    pa = jnp.dot(oa_ref[...], woa_ref[...], preferred_element_type=f32)
    pb = jnp.dot(ob_ref[...], wob_ref[...], preferred_element_type=f32)
    mix = gate[:, :D_MODEL] * pa + gate[:, D_MODEL:] * pb
    u = DN_ALPHA * x + jnp.dot(mix.astype(bf), wout_ref[...], preferred_element_type=f32)
    h = _layer_norm(u, g1_ref[...], b1_ref[...])
    _store_token_rows(h_ref, h)

    h_hi = h.astype(bf)
    h_lo = (h - h_hi.astype(f32)).astype(bf)
    logits = (jnp.dot(h_hi, wrh_ref[...], preferred_element_type=f32)
              + jnp.dot(h_lo, wrh_ref[...], preferred_element_type=f32)
              + jnp.dot(h_hi, wrl_ref[...], preferred_element_type=f32) + br_ref[...])
    lane = lax.broadcasted_iota(jnp.int32, (t, LANES), 1).astype(f32)
    lg = jnp.where(lane < N_EXPERTS, logits, -jnp.inf)
    vals, ids = [], []
    assign = jnp.zeros((t, LANES), f32)
    for _k in range(TOP_K):
        m = jnp.max(lg, axis=1, keepdims=True)
        idx = jnp.min(jnp.where(lg == m, lane, float(LANES)), axis=1, keepdims=True)
        hit = lane == idx
        vals.append(m)
        ids.append(idx)
        assign = jnp.where(hit, 1.0, assign)
        lg = jnp.where(hit, -jnp.inf, lg)
    exps = [jnp.exp(v - vals[0]) for v in vals]
    den = exps[0] + exps[1] + exps[2] + exps[3]

    r = lax.broadcasted_iota(jnp.int32, (t, t), 0)
    c = lax.broadcasted_iota(jnp.int32, (t, t), 1)
    lower = jnp.where(c < r, 1.0, 0.0).astype(bf)
    before = jnp.dot(lower, assign.astype(bf), preferred_element_type=f32) + carry_ref[0:1, :]
    ri = jnp.zeros((t, LANES), f32)
    rf = jnp.zeros((t, LANES), f32)
    for k in range(TOP_K):
        rank = jnp.sum(jnp.where(lane == ids[k], before, 0.0), axis=1, keepdims=True)
        ri = jnp.where(lane == float(k), ids[k], ri)
        ri = jnp.where(lane == float(TOP_K + k), rank, ri)
        rf = jnp.where(lane == float(k), exps[k] / den, rf)
    ri_ref[...] = ri.astype(jnp.int32)
    rf_ref[...] = rf
    total = carry_ref[0:1, :] + jnp.sum(assign, axis=0, keepdims=True)
    carry_ref[...] = jnp.broadcast_to(total, carry_ref.shape)
    cnt_ref[...] = jnp.broadcast_to(total, cnt_ref.shape)


def _post(x2, oa, ob, wg, bg, woa, wob, wout, g1, b1, wrh, wrl, br):
    n = x2.shape[0]
    t = PROJ_TILE
    tok = lambda w: pl.BlockSpec((t, w), lambda i: (i, 0))
    consts = (wg, bg, woa, wob, wout, g1, b1, wrh, wrl, br)
    return pl.pallas_call(
        _post_kernel,
        grid=(n // t,),
        in_specs=[tok(D_MODEL), tok(A_WIDTH), tok(B_WIDTH)] + [_full(a.shape) for a in consts],
        out_specs=[pl.BlockSpec((t * ROW_TILES, LANES), lambda i: (i, 0)), tok(LANES), tok(LANES),
                   _full((8, LANES))],
        out_shape=[jax.ShapeDtypeStruct((n * ROW_TILES, LANES), jnp.float32),
                   jax.ShapeDtypeStruct((n, LANES), jnp.int32),
                   jax.ShapeDtypeStruct((n, LANES), jnp.float32),
                   jax.ShapeDtypeStruct((8, LANES), jnp.float32)],
        scratch_shapes=[pltpu.VMEM((8, LANES), jnp.float32)],
        compiler_params=_cparams(("arbitrary",)),
        name="post",
    )(x2, oa, ob, *consts)


def _token_row(ref, r):
    start = r * ROW_TILES if isinstance(r, int) else pl.multiple_of(r * ROW_TILES, ROW_TILES)
    return ref.at[pl.ds(start, ROW_TILES)]


def _row_gather(idx_ref, src_ref, dst_ref, sem, rows):
    for j in range(rows):
        pltpu.make_async_copy(_token_row(src_ref, idx_ref[0, 0, j]), _token_row(dst_ref, j), sem).start(
            priority=j % 2)


def _rows_wait(src_ref, dst_ref, sem):
    pltpu.make_async_copy(src_ref, dst_ref, sem).wait()


def _dispatch_kernel(dest_ref, h_ref, xs_ref, sem):
    t = MOVE_TILE
    for k in range(TOP_K):
        for j in range(t):
            pltpu.make_async_copy(_token_row(h_ref, j), _token_row(xs_ref, dest_ref[0, 0, k * t + j]), sem).start(
                priority=j % 2)
    for k in range(TOP_K):
        _rows_wait(h_ref, xs_ref.at[pl.ds(0, t * ROW_TILES)], sem)


def _dispatch(dest3, h, rows):
    t = MOVE_TILE
    n = h.shape[0] // ROW_TILES
    return pl.pallas_call(
        _dispatch_kernel,
        grid=(n // t,),
        in_specs=[pl.BlockSpec((1, 1, t * TOP_K), lambda i: (i, 0, 0), memory_space=pltpu.SMEM),
                  pl.BlockSpec((t * ROW_TILES, LANES), lambda i: (i, 0))],
        out_specs=pl.BlockSpec(memory_space=pl.ANY),
        out_shape=jax.ShapeDtypeStruct((rows * ROW_TILES, LANES), jnp.float32),
        scratch_shapes=[pltpu.SemaphoreType.DMA(())],
        compiler_params=_cparams(("arbitrary",)),
        name="dispatch",
    )(dest3, h)


def _expert_kernel(te_ref, tt_ref, lo_ref, hi_ref, ns_ref, xs_ref, wu_ref, bu_ref, wd_ref, bd_ref, ys_ref,
                   wub_ref, wdb_ref):
    i = pl.program_id(0)
    f32 = jnp.float32
    bf = jnp.bfloat16
    t = EXPERT_TILE
    valid = i < ns_ref[0]
    before = jnp.maximum(i - 1, 0)
    fresh = valid & ((i == 0) | (te_ref[i] != te_ref[before]))
    first_visit = (i == 0) | (tt_ref[i] != tt_ref[before])

    @pl.when(fresh)
    def _():
        rows = 128

        def cast_up(j, c):
            r = pl.multiple_of(j * rows, rows)
            wub_ref[pl.ds(r, rows), :] = wu_ref[pl.ds(r, rows), :].astype(bf)
            return c

        lax.fori_loop(0, D_MODEL // rows, cast_up, 0)

        def cast_dn(j, c):
            r = pl.multiple_of(j * rows, rows)
            wdb_ref[pl.ds(r, rows), :] = wd_ref[pl.ds(r, rows), :].astype(bf)
            return c

        lax.fori_loop(0, D_FF // rows, cast_dn, 0)

    def compute():
        xb = _load_token_rows(xs_ref, 0, t).astype(bf)
        y = jnp.broadcast_to(bd_ref[...], (t, D_MODEL))
        for c in range(D_FF // FF_CHUNK):
            gs = slice(c * FF_CHUNK, (c + 1) * FF_CHUNK)
            ls = slice(D_FF + c * FF_CHUNK, D_FF + (c + 1) * FF_CHUNK)
            ag = jnp.dot(xb, wub_ref[:, gs], preferred_element_type=f32) + bu_ref[:, gs]
            al = jnp.dot(xb, wub_ref[:, ls], preferred_element_type=f32) + bu_ref[:, ls]
            g = jnp.minimum(ag, SWIGLU_LIMIT)
            lin = jnp.clip(al, -SWIGLU_LIMIT, SWIGLU_LIMIT)
            act = (lin + 1.0) * (g * jax.nn.sigmoid(SWIGLU_ALPHA * g))
            y = y + jnp.dot(act.astype(bf), wdb_ref[gs, :], preferred_element_type=f32)
        row = lax.broadcasted_iota(jnp.int32, (t, 1), 0)
        return y, (row >= lo_ref[i]) & (row < hi_ref[i])

    @pl.when(valid & first_visit)
    def _():
        y, mine = compute()
        _store_token_rows(ys_ref, jnp.where(mine, y, 0.0))

    @pl.when(valid & jnp.logical_not(first_visit))
    def _():
        y, mine = compute()
        _store_token_rows(ys_ref, jnp.where(mine, y, _load_token_rows(ys_ref, 0, t)))


def _experts(steps, xs, w_up, b_up, w_down, b_down):
    t = EXPERT_TILE
    max_steps = steps[0].shape[0]
    row_block = pl.BlockSpec((t * ROW_TILES, LANES), lambda i, te, tt, lo, hi, ns: (tt[i], 0))
    per_expert = lambda shape: pl.BlockSpec((None,) + shape, lambda i, te, tt, lo, hi, ns: (te[i], 0, 0))
    grid_spec = pltpu.PrefetchScalarGridSpec(
        num_scalar_prefetch=5,
        grid=(max_steps,),
        in_specs=[row_block, per_expert((D_MODEL, 2 * D_FF)), per_expert((1, 2 * D_FF)),
                  per_expert((D_FF, D_MODEL)), per_expert((1, D_MODEL))],
        out_specs=row_block,
        scratch_shapes=[pltpu.VMEM((D_MODEL, 2 * D_FF), jnp.bfloat16),
                        pltpu.VMEM((D_FF, D_MODEL), jnp.bfloat16)],
    )
    return pl.pallas_call(
        _expert_kernel,
        grid_spec=grid_spec,
        out_shape=jax.ShapeDtypeStruct(xs.shape, jnp.float32),
        compiler_params=_cparams(("arbitrary",)),
        name="experts",
    )(*steps, xs, w_up, b_up.reshape(N_EXPERTS, 1, 2 * D_FF), w_down, b_down.reshape(N_EXPERTS, 1, D_MODEL))


def _combine_kernel(cur_ref, nxt_ref, ys_ref, h_ref, rf_ref, g2_ref, b2_ref, o_ref, buf_ref, sems):
    i = pl.program_id(0)
    n_steps = pl.num_programs(0)
    t = MOVE_TILE
    rows = t * TOP_K
    slot = i % 2

    @pl.when(i == 0)
    def _():
        _row_gather(cur_ref, ys_ref, buf_ref.at[0], sems.at[0], rows)

    _rows_wait(ys_ref.at[pl.ds(0, rows * ROW_TILES)], buf_ref.at[slot], sems.at[slot])
    _row_gather(nxt_ref, ys_ref, buf_ref.at[1 - slot], sems.at[1 - slot], rows)

    rf = rf_ref[...]
    u = DN_ALPHA * _load_token_rows(h_ref, 0, t)
    for k in range(TOP_K):
        u = u + rf[:, k:k + 1] * _load_token_rows(buf_ref.at[slot], k * t, t)
    o_ref[...] = _layer_norm(u, g2_ref[...], b2_ref[...])

    @pl.when(i == n_steps - 1)
    def _():
        _rows_wait(ys_ref.at[pl.ds(0, rows * ROW_TILES)], buf_ref.at[1 - slot], sems.at[1 - slot])


def _combine(dest3, ys, h, rf, g2, b2):
    t = MOVE_TILE
    n = h.shape[0] // ROW_TILES
    last = n // t - 1
    tok = lambda w: pl.BlockSpec((t, w), lambda i: (i, 0))
    idx_spec = lambda f: pl.BlockSpec((1, 1, t * TOP_K), f, memory_space=pltpu.SMEM)
    return pl.pallas_call(
        _combine_kernel,
        grid=(n // t,),
        in_specs=[idx_spec(lambda i: (i, 0, 0)), idx_spec(lambda i: (jnp.minimum(i + 1, last), 0, 0)),
                  pl.BlockSpec(memory_space=pl.ANY), pl.BlockSpec((t * ROW_TILES, LANES), lambda i: (i, 0)),
                  tok(LANES), _full(g2.shape), _full(b2.shape)],
        out_specs=tok(D_MODEL),
        out_shape=jax.ShapeDtypeStruct((n, D_MODEL), jnp.float32),
        scratch_shapes=[pltpu.VMEM((2, t * TOP_K * ROW_TILES, LANES), jnp.float32),
                        pltpu.SemaphoreType.DMA((2,))],
        compiler_params=_cparams(("arbitrary",)),
        name="combine",
    )(dest3, dest3, ys, h, rf, g2, b2)


def _rope_tables(positions, half):
    inv_freq = ROPE_THETA ** (-jnp.arange(half, dtype=jnp.float32) / half)
    ang = positions.astype(jnp.float32).reshape(-1, 1) * inv_freq
    reps = LANES // half
    return jnp.tile(jnp.cos(ang), (1, reps)), jnp.tile(jnp.sin(ang), (1, reps))


def _lookup(table, idx):
    experts = jnp.arange(N_EXPERTS, dtype=jnp.int32)
    return jnp.sum(jnp.where(idx[..., None] == experts, table, 0), axis=-1)


def _expert_schedule(starts, ends, rows):
    t = EXPERT_TILE
    first_tile = starts // t
    n_tile = jnp.where(ends > starts, (ends - 1) // t - first_tile + 1, 0)
    cum = jnp.cumsum(n_tile)
    n_steps = cum[-1]
    max_steps = rows // t + N_EXPERTS
    s = jnp.minimum(jnp.arange(max_steps, dtype=jnp.int32), n_steps - 1)
    expert = jnp.sum((cum[None, :] <= s[:, None]).astype(jnp.int32), axis=1)
    tile = _lookup(first_tile, expert) + s - _lookup(cum - n_tile, expert)
    lo = jnp.maximum(_lookup(starts, expert) - tile * t, 0)
    hi = jnp.minimum(_lookup(ends, expert) - tile * t, t)
    return expert, tile, lo, hi, n_steps.astype(jnp.int32).reshape(1)


def _layer(x2, tabs, bsz, seq, w_in, b_gate, rms_cq, rms_ckv, w_uq, w_ukv, w_o_a, w_o_b, w_out,
           ln1_g, ln1_b, w_router, b_router, w_up, b_up, w_down, b_down, ln2_g, ln2_b):
    bf = jnp.bfloat16
    n = x2.shape[0]
    off = np.concatenate([[0], np.cumsum(SPLITS)]).tolist()
    col = lambda j: w_in[:, off[j]:off[j + 1]]
    w1 = jnp.concatenate([col(0), col(1), col(2), col(3)], axis=1).astype(bf)
    wi_pad = jnp.zeros((D_MODEL, LANES - IDX_HEADS), w_in.dtype)
    ws = jnp.concatenate([col(4), col(4), jnp.tile(col(8), (1, LANES // QK_ROPE)), col(5), wi_pad], axis=1).astype(bf)
    wl = jnp.concatenate([col(6), col(7)], axis=1).astype(bf)
    wg = col(9).astype(bf)
    uq = w_uq.reshape(Q_LORA, B_HEADS, QK_NOPE + QK_ROPE)
    wuq = jnp.concatenate([uq[:, :, :QK_NOPE].reshape(Q_LORA, -1), uq[:, :, QK_NOPE:].reshape(Q_LORA, -1)],
                          axis=1).astype(bf)
    ukv = w_ukv.reshape(KV_LORA, B_HEADS, QK_NOPE + V_DIM)
    wukv = jnp.concatenate([ukv[:, :, :QK_NOPE].reshape(KV_LORA, -1), ukv[:, :, QK_NOPE:].reshape(KV_LORA, -1)],
                           axis=1).astype(bf)

    qa, ka, va, qi, ki, kr, wi, qn, qr, kn, vb = _proj(
        x2, tabs, w1, ws, wl, rms_cq.reshape(1, -1), rms_ckv.reshape(1, -1), wuq, wukv)
    o_a = _dsa(qa, qi, wi, ka, va, ki, bsz, seq)
    o_b = _mla(qn, qr, kn, kr, vb, bsz, seq)

    wr = jnp.concatenate([w_router, jnp.zeros((D_MODEL, LANES - N_EXPERTS), w_router.dtype)], axis=1)
    br = jnp.concatenate([b_router, jnp.zeros((LANES - N_EXPERTS,), b_router.dtype)]).reshape(1, -1)
    wr_hi = wr.astype(bf)
    wr_lo = (wr - wr_hi.astype(jnp.float32)).astype(bf)
    h, ri, rf, cnt = _post(x2, o_a, o_b, wg, b_gate.reshape(1, -1), w_o_a.astype(bf), w_o_b.astype(bf),
                           w_out.astype(bf), ln1_g.reshape(1, -1), ln1_b.reshape(1, -1), wr_hi, wr_lo, br)

    counts = cnt[0, :N_EXPERTS].astype(jnp.int32)
    ends = jnp.cumsum(counts)
    starts = ends - counts
    dest = _lookup(starts, ri[:, :TOP_K]) + ri[:, TOP_K:2 * TOP_K]
    dest3 = dest.reshape(n // MOVE_TILE, MOVE_TILE, TOP_K).transpose(0, 2, 1).reshape(
        n // MOVE_TILE, 1, MOVE_TILE * TOP_K)

    xs = _dispatch(dest3, h, n * TOP_K)
    ys = _experts(_expert_schedule(starts, ends, n * TOP_K), xs, w_up, b_up, w_down, b_down)
    return _combine(dest3, ys, h, rf, ln2_g.reshape(1, -1), ln2_b.reshape(1, -1))


def kernel(x, positions, w_in, b_gate, rms_cq, rms_ckv, w_uq, w_ukv, w_o_a, w_o_b, w_out, ln1_g, ln1_b,
           w_router, b_router, w_up, b_up, w_down, b_down, ln2_g, ln2_b):
    bsz, seq, _ = x.shape
    x2 = x.reshape(bsz * seq, D_MODEL)
    c64, s64 = _rope_tables(positions, A_HEAD_DIM // 2)
    c32, s32 = _rope_tables(positions, QK_ROPE // 2)
    tabs = (c64, s64, c32, s32)
    for l in range(DEPTH):
        x2 = _layer(x2, tabs, bsz, seq, w_in[l], b_gate[l], rms_cq[l], rms_ckv[l], w_uq[l], w_ukv[l],
                    w_o_a[l], w_o_b[l], w_out[l], ln1_g[l], ln1_b[l], w_router[l], b_router[l],
                    w_up[l], b_up[l], w_down[l], b_down[l], ln2_g[l], ln2_b[l])
    return x2.reshape(bsz, seq, D_MODEL)
```

```python
import functools

import jax
import jax.numpy as jnp
import numpy as np
from jax import lax
from jax.experimental import pallas as pl
from jax.experimental.pallas import tpu as pltpu

D_MODEL = 1024
A_HEADS = 8
A_HEAD_DIM = 64
IDX_HEADS = 8
IDX_DIM = 64
TOPK_MAX = 256
B_HEADS = 8
Q_LORA = 384
KV_LORA = 256
QK_NOPE = 64
QK_ROPE = 32
V_DIM = 64
ROPE_THETA = 10000.0
N_EXPERTS = 32
TOP_K = 4
D_FF = 1024
SWIGLU_LIMIT = 7.0
SWIGLU_ALPHA = 1.702
DEPTH = 1
DN_ALPHA = (2 * DEPTH) ** 0.25
LN_EPS = 1e-5
RMS_EPS = 1e-6
A_WIDTH = A_HEADS * A_HEAD_DIM
B_WIDTH = B_HEADS * V_DIM
SPLITS = (A_WIDTH, A_WIDTH, A_WIDTH, IDX_HEADS * IDX_DIM, IDX_DIM, IDX_HEADS,
          Q_LORA, KV_LORA, QK_ROPE, 2 * D_MODEL)

LANES = 128
VMEM_LIMIT_BYTES = 52 * 1024 * 1024

PROJ_TILE = 512
Q_TILE = 256
EXPERT_TILE = 512
FF_CHUNK = 256
MOVE_TILE = 512

ROW_TILES = D_MODEL // LANES
assert ROW_TILES == 8

LOG2_E = 1.4426950408889634
INT_MIN = -(2 ** 31)
NEG_BIG = -1e30

_NT = (((1,), (1,)), ((), ()))


def _cparams(sem):
    return pltpu.CompilerParams(dimension_semantics=sem, vmem_limit_bytes=VMEM_LIMIT_BYTES)


def _full(shape):
    nd = len(shape)
    return pl.BlockSpec(shape, lambda *_: (0,) * nd)


def _rope_chunk(x, cos, sin, half):
    lane = lax.broadcasted_iota(jnp.int32, x.shape, 1)
    first = (lane % (2 * half)) < half
    up = pltpu.roll(x, LANES - half, 1)
    dn = pltpu.roll(x, half, 1)
    return x * cos + jnp.where(first, -up, dn) * sin


def _layer_norm(u, g, b):
    mu = jnp.mean(u, axis=-1, keepdims=True)
    d = u - mu
    var = jnp.mean(d * d, axis=-1, keepdims=True)
    return d * lax.rsqrt(var + LN_EPS) * g + b


def _load_token_rows(ref, first, t):
    return jnp.concatenate([ref[pl.ds(first * ROW_TILES + s, t, stride=ROW_TILES), :] for s in range(ROW_TILES)],
                           axis=1)


def _store_token_rows(ref, v, first=0):
    t = v.shape[0]
    for s in range(ROW_TILES):
        ref[pl.ds(first * ROW_TILES + s, t, stride=ROW_TILES), :] = v[:, s * LANES:(s + 1) * LANES]


def _two_head_softmax_pv(logits, vc, width):
    lane_v = lax.broadcasted_iota(jnp.int32, vc.shape, 1)
    pvs = []
    for hh in range(2):
        lg = logits[hh]
        p = jnp.exp2(lg - jnp.max(lg, axis=1, keepdims=True)).astype(jnp.bfloat16)
        v_h = jnp.where((lane_v // width) == hh, vc, jnp.ones_like(vc))
        pvs.append(jnp.dot(p, v_h, preferred_element_type=jnp.float32))
    lane = lax.broadcasted_iota(jnp.int32, pvs[0].shape, 1)
    first = lane < width
    num = jnp.where(first, pvs[0], pvs[1])
    den = jnp.where(first, pltpu.roll(pvs[0], width, 1), pltpu.roll(pvs[1], width, 1))
    return num / den


def _head_mask(shape, width, which):
    lane = lax.broadcasted_iota(jnp.int32, shape, 1)
    return (lane // width) == which


def _proj_kernel(x_ref, c64_ref, s64_ref, c32_ref, s32_ref, w1_ref, ws_ref, wl_ref,
                 gcq_ref, gckv_ref, wuq_ref, wukv_ref,
                 qa_ref, ka_ref, va_ref, qi_ref, ki_ref, kr_ref, wi_ref,
                 qn_ref, qr_ref, kn_ref, vb_ref):
    xb = x_ref[...].astype(jnp.bfloat16)
    c64, s64 = c64_ref[...], s64_ref[...]
    c32, s32 = c32_ref[...], s32_ref[...]
    f32 = jnp.float32
    bf = jnp.bfloat16

    z1 = jnp.dot(xb, w1_ref[...], preferred_element_type=f32)
    n_chunk = A_WIDTH // LANES
    a_scale = A_HEAD_DIM ** -0.5 * LOG2_E
    for c in range(n_chunk):
        sl = slice(c * LANES, (c + 1) * LANES)
        qa_ref[:, sl] = (_rope_chunk(z1[:, c * LANES:(c + 1) * LANES], c64, s64, 32) * a_scale).astype(bf)
        o = A_WIDTH + c * LANES
        ka_ref[:, sl] = _rope_chunk(z1[:, o:o + LANES], c64, s64, 32).astype(bf)
        o = 2 * A_WIDTH + c * LANES
        va_ref[:, sl] = z1[:, o:o + LANES].astype(bf)
        o = 3 * A_WIDTH + c * LANES
        qi_ref[:, sl] = _rope_chunk(z1[:, o:o + LANES], c64, s64, 32).astype(bf)

    zs = jnp.dot(xb, ws_ref[...], preferred_element_type=f32)
    ki_ref[...] = _rope_chunk(zs[:, 0:LANES], c64, s64, 32).astype(bf)
    kr_ref[...] = _rope_chunk(zs[:, LANES:2 * LANES], c32, s32, 16).astype(bf)
    wi_ref[...] = zs[:, 2 * LANES:3 * LANES] * (IDX_HEADS ** -0.5 * IDX_DIM ** -0.5)

    zl = jnp.dot(xb, wl_ref[...], preferred_element_type=f32)
    cq = zl[:, :Q_LORA]
    cq = cq * lax.rsqrt(jnp.mean(cq * cq, axis=-1, keepdims=True) + RMS_EPS) * gcq_ref[...]
    ckv = zl[:, Q_LORA:]
    ckv = ckv * lax.rsqrt(jnp.mean(ckv * ckv, axis=-1, keepdims=True) + RMS_EPS) * gckv_ref[...]
    qb = jnp.dot(cq.astype(bf), wuq_ref[...], preferred_element_type=f32)
    kv = jnp.dot(ckv.astype(bf), wukv_ref[...], preferred_element_type=f32)
    b_scale = (QK_NOPE + QK_ROPE) ** -0.5 * LOG2_E
    qn_ref[...] = (qb[:, :B_HEADS * QK_NOPE] * b_scale).astype(bf)
    for c in range(B_HEADS * QK_ROPE // LANES):
        o = B_HEADS * QK_NOPE + c * LANES
        qr_ref[:, c * LANES:(c + 1) * LANES] = (
            _rope_chunk(qb[:, o:o + LANES], c32, s32, 16) * b_scale).astype(bf)
    kn_ref[...] = kv[:, :B_HEADS * QK_NOPE].astype(bf)
    vb_ref[...] = kv[:, B_HEADS * QK_NOPE:].astype(bf)


def _proj(x2, tabs, w1, ws, wl, gcq, gckv, wuq, wukv):
    n = x2.shape[0]
    t = PROJ_TILE
    tok = lambda w: pl.BlockSpec((t, w), lambda i: (i, 0))
    bf = jnp.bfloat16
    outs = [(A_WIDTH, bf)] * 4 + [(LANES, bf), (LANES, bf), (LANES, jnp.float32),
                                  (B_HEADS * QK_NOPE, bf), (B_HEADS * QK_ROPE, bf),
                                  (B_HEADS * QK_NOPE, bf), (B_WIDTH, bf)]
    return pl.pallas_call(
        _proj_kernel,
        grid=(n // t,),
        in_specs=[tok(D_MODEL)] + [tok(LANES)] * 4 + [_full(a.shape) for a in (w1, ws, wl, gcq, gckv, wuq, wukv)],
        out_specs=[tok(w) for w, _ in outs],
        out_shape=[jax.ShapeDtypeStruct((n, w), d) for w, d in outs],
        compiler_params=_cparams(("arbitrary",)),
        name="proj",
    )(x2, *tabs, w1, ws, wl, gcq, gckv, wuq, wukv)


def _tie_select(eq, gt, need, kv):
    chunk = 256
    r = lax.broadcasted_iota(jnp.int32, (chunk, chunk), 0)
    c = lax.broadcasted_iota(jnp.int32, (chunk, chunk), 1)
    upper = jnp.where(r <= c, 1.0, 0.0).astype(jnp.bfloat16)
    carry = jnp.zeros((eq.shape[0], 1), jnp.float32)
    parts = []
    for j in range(kv // chunk):
        e = eq[:, j * chunk:(j + 1) * chunk]
        ef = jnp.where(e, 1.0, 0.0).astype(jnp.bfloat16)
        pref = jnp.dot(ef, upper, preferred_element_type=jnp.float32) + carry
        parts.append(gt[:, j * chunk:(j + 1) * chunk] | (e & (pref <= need)))
        carry = carry + jnp.sum(ef.astype(jnp.float32), axis=1, keepdims=True)
    return jnp.concatenate(parts, axis=1)


def _key_to_float(key):
    return lax.bitcast_convert_type(key ^ ((key >> 31) & jnp.int32(0x7FFFFFFF)), jnp.float32)


def _select_topk(score, k_eff, sel_ref, kv):
    tq = score.shape[0]

    def search(it, prefix):
        cand = prefix + lax.shift_left(jnp.int32(1), 31 - it)
        cnt = jnp.sum(jnp.where(score >= _key_to_float(cand), 1.0, 0.0), axis=1, keepdims=True)
        return jnp.where(cnt >= k_eff, cand, prefix)

    thr = _key_to_float(lax.fori_loop(0, 32, search, jnp.full((tq, 1), INT_MIN, jnp.int32)))
    gt = score > thr
    eq = score == thr
    n_gt = jnp.sum(jnp.where(gt, 1.0, 0.0), axis=1, keepdims=True)
    n_eq = jnp.sum(jnp.where(eq, 1.0, 0.0), axis=1, keepdims=True)
    need = k_eff - n_gt
    sel_ref[:, :kv] = jnp.where(score >= thr, 0.0, NEG_BIG)
    has_tie = jnp.max(jnp.where(n_eq != need, 1.0, 0.0)) > 0.5

    @pl.when(has_tie)
    def _():
        sel_ref[:, :kv] = jnp.where(_tie_select(eq, gt, need, kv), 0.0, NEG_BIG)


def _dsa_block(qa_ref, qi_ref, wi_ref, ka_ref, va_ref, ki_ref, o_ref, sel_ref, score_ref, *, kv, k_sel):
    tq = qa_ref.shape[0]
    f32 = jnp.float32
    bf = jnp.bfloat16
    q_pos = (kv - tq) + lax.broadcasted_iota(jnp.int32, (tq, 1), 0)
    key_pos = lax.broadcasted_iota(jnp.int32, (tq, kv), 1)
    causal = key_pos <= q_pos

    if kv <= k_sel:
        sel_ref[:, :kv] = jnp.where(causal, 0.0, NEG_BIG)
    else:
        ki = ki_ref[:kv, :]
        lane = lax.broadcasted_iota(jnp.int32, (tq, LANES), 1)
        score_ref[:, :kv] = jnp.zeros((tq, kv), f32)

        def idx_slab(c, carry):
            qc = qi_ref[:, pl.ds(pl.multiple_of(c * LANES, LANES), LANES)]
            acc = score_ref[:, :kv]
            for hh in range(2):
                qm = jnp.where(_head_mask(qc.shape, IDX_DIM, hh), qc, jnp.zeros_like(qc))
                s = lax.dot_general(qm, ki, _NT, preferred_element_type=f32)
                w = jnp.sum(jnp.where(lane == 2 * c + hh, wi_ref[...], 0.0), axis=1, keepdims=True)
                acc = acc + jnp.maximum(s, 0.0) * w
            score_ref[:, :kv] = acc
            return carry

        lax.fori_loop(0, IDX_HEADS // 2, idx_slab, 0)
        score = jnp.where(causal, score_ref[:, :kv], -jnp.inf)
        k_eff = jnp.minimum(q_pos + 1, k_sel).astype(f32)
        _select_topk(score, k_eff, sel_ref, kv)

    def attn_slab(c, carry):
        sl = pl.ds(pl.multiple_of(c * LANES, LANES), LANES)
        qc = qa_ref[:, sl]
        kc = ka_ref[:kv, sl]
        vc = va_ref[:kv, sl]
        mask = sel_ref[:, :kv]
        logits = []
        for hh in range(2):
            qm = jnp.where(_head_mask(qc.shape, A_HEAD_DIM, hh), qc, jnp.zeros_like(qc))
            logits.append(lax.dot_general(qm, kc, _NT, preferred_element_type=f32) + mask)
        o_ref[:, sl] = _two_head_softmax_pv(logits, vc, A_HEAD_DIM).astype(bf)
        return carry

    lax.fori_loop(0, A_WIDTH // LANES, attn_slab, 0)


def _per_query_block(block_fn, refs, n_blocks, tq):
    j = pl.program_id(1)
    for b in range(n_blocks):
        pl.when(j == b)(functools.partial(block_fn, *refs, kv=(b + 1) * tq))


def _dsa_kernel(*refs, seq, k_sel):
    tq = refs[0].shape[0]
    _per_query_block(functools.partial(_dsa_block, k_sel=k_sel), refs, seq // tq, tq)


def _dsa(qa, qi, wi, ka, va, ki, bsz, seq):
    tq = min(Q_TILE, seq)
    nq = seq // tq
    qspec = lambda w: pl.BlockSpec((tq, w), lambda b, i: (b * nq + i, 0))
    kspec = lambda w: pl.BlockSpec((seq, w), lambda b, i: (b, 0))
    k_sel = min(TOPK_MAX, seq // 4)
    return pl.pallas_call(
        functools.partial(_dsa_kernel, seq=seq, k_sel=k_sel),
        grid=(bsz, nq),
        in_specs=[qspec(A_WIDTH), qspec(A_WIDTH), qspec(LANES), kspec(A_WIDTH), kspec(A_WIDTH), kspec(LANES)],
        out_specs=qspec(A_WIDTH),
        out_shape=jax.ShapeDtypeStruct((bsz * seq, A_WIDTH), jnp.bfloat16),
        scratch_shapes=[pltpu.VMEM((tq, seq), jnp.float32), pltpu.VMEM((tq, seq), jnp.float32)],
        compiler_params=_cparams(("arbitrary", "arbitrary")),
        name="dsa",
    )(qa, qi, wi, ka, va, ki)


def _mla_block(qn_ref, qr_ref, kn_ref, kr_ref, vb_ref, o_ref, *, kv):
    tq = qn_ref.shape[0]
    f32 = jnp.float32
    bf = jnp.bfloat16
    diag = lax.broadcasted_iota(jnp.int32, (tq, tq), 1) <= lax.broadcasted_iota(jnp.int32, (tq, tq), 0)
    kr = kr_ref[:kv, :]

    def causal_masked(lg):
        own = jnp.where(diag, lg[:, kv - tq:], NEG_BIG)
        return own if kv == tq else jnp.concatenate([lg[:, :kv - tq], own], axis=1)

    def slab(c, carry):
        sl = pl.ds(pl.multiple_of(c * LANES, LANES), LANES)
        qc = qn_ref[:, sl]
        kcat = jnp.concatenate([kn_ref[:kv, sl], kr], axis=1)
        vc = vb_ref[:kv, sl]
        qrc = qr_ref[:, pl.ds(pl.multiple_of((c // 2) * LANES, LANES), LANES)]
        logits = []
        for hh in range(2):
            h = 2 * c + hh
            qm = jnp.where(_head_mask(qc.shape, QK_NOPE, hh), qc, jnp.zeros_like(qc))
            qrm = jnp.where(_head_mask(qrc.shape, QK_ROPE, h % 4), qrc, jnp.zeros_like(qrc))
            qcat = jnp.concatenate([qm, qrm], axis=1)
            lg = lax.dot_general(qcat, kcat, _NT, preferred_element_type=f32)
            logits.append(causal_masked(lg))
        o_ref[:, sl] = _two_head_softmax_pv(logits, vc, V_DIM).astype(bf)
        return carry

    lax.fori_loop(0, B_WIDTH // LANES, slab, 0, unroll=2)


def _mla_kernel(*refs, seq):
    tq = refs[0].shape[0]
    _per_query_block(_mla_block, refs, seq // tq, tq)


def _mla(qn, qr, kn, kr, vb, bsz, seq):
    tq = min(Q_TILE, seq)
    nq = seq // tq
    qspec = lambda w: pl.BlockSpec((tq, w), lambda b, i: (b * nq + i, 0))
    kspec = lambda w: pl.BlockSpec((seq, w), lambda b, i: (b, 0))
    return pl.pallas_call(
        functools.partial(_mla_kernel, seq=seq),
        grid=(bsz, nq),
        in_specs=[qspec(B_HEADS * QK_NOPE), qspec(B_HEADS * QK_ROPE), kspec(B_HEADS * QK_NOPE),
                  kspec(LANES), kspec(B_WIDTH)],
        out_specs=qspec(B_WIDTH),
        out_shape=jax.ShapeDtypeStruct((bsz * seq, B_WIDTH), jnp.bfloat16),
        compiler_params=_cparams(("arbitrary", "arbitrary")),
        name="mla",
    )(qn, qr, kn, kr, vb)


def _post_kernel(x_ref, oa_ref, ob_ref, wg_ref, bg_ref, woa_ref, wob_ref, wout_ref, g1_ref, b1_ref,
                 wrh_ref, wrl_ref, br_ref, h_ref, ri_ref, rf_ref, cnt_ref, carry_ref):
    f32 = jnp.float32
    bf = jnp.bfloat16
    t = x_ref.shape[0]

    @pl.when(pl.program_id(0) == 0)
    def _():
        carry_ref[...] = jnp.zeros_like(carry_ref)

    x = x_ref[...]
    gate = jax.nn.sigmoid(jnp.dot(x.astype(bf), wg_ref[...], preferred_element_type=f32) + bg_ref[...])
    pa = jnp.dot(oa_ref[...], woa_ref[...], preferred_element_type=f32)
    pb = jnp.dot(ob_ref[...], wob_ref[...], preferred_element_type=f32)
    mix = gate[:, :D_MODEL] * pa + gate[:, D_MODEL:] * pb
    u = DN_ALPHA * x + jnp.dot(mix.astype(bf), wout_ref[...], preferred_element_type=f32)
    h = _layer_norm(u, g1_ref[...], b1_ref[...])
    _store_token_rows(h_ref, h)

    h_hi = h.astype(bf)
    h_lo = (h - h_hi.astype(f32)).astype(bf)
    logits = (jnp.dot(h_hi, wrh_ref[...], preferred_element_type=f32)
              + jnp.dot(h_lo, wrh_ref[...], preferred_element_type=f32)
              + jnp.dot(h_hi, wrl_ref[...], preferred_element_type=f32) + br_ref[...])
    lane = lax.broadcasted_iota(jnp.int32, (t, LANES), 1).astype(f32)
    lg = jnp.where(lane < N_EXPERTS, logits, -jnp.inf)
    vals, ids = [], []
    assign = jnp.zeros((t, LANES), f32)
    for _k in range(TOP_K):
        m = jnp.max(lg, axis=1, keepdims=True)
        idx = jnp.min(jnp.where(lg == m, lane, float(LANES)), axis=1, keepdims=True)
        hit = lane == idx
        vals.append(m)
        ids.append(idx)
        assign = jnp.where(hit, 1.0, assign)
        lg = jnp.where(hit, -jnp.inf, lg)
    exps = [jnp.exp(v - vals[0]) for v in vals]
    den = exps[0] + exps[1] + exps[2] + exps[3]

    r = lax.broadcasted_iota(jnp.int32, (t, t), 0)
    c = lax.broadcasted_iota(jnp.int32, (t, t), 1)
    lower = jnp.where(c < r, 1.0, 0.0).astype(bf)
    before = jnp.dot(lower, assign.astype(bf), preferred_element_type=f32) + carry_ref[0:1, :]
    ri = jnp.zeros((t, LANES), f32)
    rf = jnp.zeros((t, LANES), f32)
    for k in range(TOP_K):
        rank = jnp.sum(jnp.where(lane == ids[k], before, 0.0), axis=1, keepdims=True)
        ri = jnp.where(lane == float(k), ids[k], ri)
        ri = jnp.where(lane == float(TOP_K + k), rank, ri)
        rf = jnp.where(lane == float(k), exps[k] / den, rf)
    ri_ref[...] = ri.astype(jnp.int32)
    rf_ref[...] = rf
    total = carry_ref[0:1, :] + jnp.sum(assign, axis=0, keepdims=True)
    carry_ref[...] = jnp.broadcast_to(total, carry_ref.shape)
    cnt_ref[...] = jnp.broadcast_to(total, cnt_ref.shape)


def _post(x2, oa, ob, wg, bg, woa, wob, wout, g1, b1, wrh, wrl, br):
    n = x2.shape[0]
    t = PROJ_TILE
    tok = lambda w: pl.BlockSpec((t, w), lambda i: (i, 0))
    consts = (wg, bg, woa, wob, wout, g1, b1, wrh, wrl, br)
    return pl.pallas_call(
        _post_kernel,
        grid=(n // t,),
        in_specs=[tok(D_MODEL), tok(A_WIDTH), tok(B_WIDTH)] + [_full(a.shape) for a in consts],
        out_specs=[pl.BlockSpec((t * ROW_TILES, LANES), lambda i: (i, 0)), tok(LANES), tok(LANES),
                   _full((8, LANES))],
        out_shape=[jax.ShapeDtypeStruct((n * ROW_TILES, LANES), jnp.float32),
                   jax.ShapeDtypeStruct((n, LANES), jnp.int32),
                   jax.ShapeDtypeStruct((n, LANES), jnp.float32),
                   jax.ShapeDtypeStruct((8, LANES), jnp.float32)],
        scratch_shapes=[pltpu.VMEM((8, LANES), jnp.float32)],
        compiler_params=_cparams(("arbitrary",)),
        name="post",
    )(x2, oa, ob, *consts)


def _token_row(ref, r):
    start = r * ROW_TILES if isinstance(r, int) else pl.multiple_of(r * ROW_TILES, ROW_TILES)
    return ref.at[pl.ds(start, ROW_TILES)]


def _row_gather(idx_ref, src_ref, dst_ref, sem, rows):
    for j in range(rows):
        pltpu.make_async_copy(_token_row(src_ref, idx_ref[0, 0, j]), _token_row(dst_ref, j), sem).start(
            priority=j % 2)


def _rows_wait(src_ref, dst_ref, sem):
    pltpu.make_async_copy(src_ref, dst_ref, sem).wait()


def _dispatch_kernel(dest_ref, h_ref, xs_ref, sem):
    t = MOVE_TILE
    for k in range(TOP_K):
        for j in range(t):
            pltpu.make_async_copy(_token_row(h_ref, j), _token_row(xs_ref, dest_ref[0, 0, k * t + j]), sem).start(
                priority=j % 2)
    for k in range(TOP_K):
        _rows_wait(h_ref, xs_ref.at[pl.ds(0, t * ROW_TILES)], sem)


def _dispatch(dest3, h, rows):
    t = MOVE_TILE
    n = h.shape[0] // ROW_TILES
    return pl.pallas_call(
        _dispatch_kernel,
        grid=(n // t,),
        in_specs=[pl.BlockSpec((1, 1, t * TOP_K), lambda i: (i, 0, 0), memory_space=pltpu.SMEM),
                  pl.BlockSpec((t * ROW_TILES, LANES), lambda i: (i, 0))],
        out_specs=pl.BlockSpec(memory_space=pl.ANY),
        out_shape=jax.ShapeDtypeStruct((rows * ROW_TILES, LANES), jnp.float32),
        scratch_shapes=[pltpu.SemaphoreType.DMA(())],
        compiler_params=_cparams(("arbitrary",)),
        name="dispatch",
    )(dest3, h)


def _expert_kernel(te_ref, tt_ref, lo_ref, hi_ref, ns_ref, xs_ref, wu_ref, bu_ref, wd_ref, bd_ref, ys_ref,
                   wub_ref, wdb_ref):
    i = pl.program_id(0)
    f32 = jnp.float32
    bf = jnp.bfloat16
    t = EXPERT_TILE
    valid = i < ns_ref[0]
    before = jnp.maximum(i - 1, 0)
    fresh = valid & ((i == 0) | (te_ref[i] != te_ref[before]))
    first_visit = (i == 0) | (tt_ref[i] != tt_ref[before])

    @pl.when(fresh)
    def _():
        rows = 128

        def cast_up(j, c):
            r = pl.multiple_of(j * rows, rows)
            wub_ref[pl.ds(r, rows), :] = wu_ref[pl.ds(r, rows), :].astype(bf)
            return c

        lax.fori_loop(0, D_MODEL // rows, cast_up, 0)

        def cast_dn(j, c):
            r = pl.multiple_of(j * rows, rows)
            wdb_ref[pl.ds(r, rows), :] = wd_ref[pl.ds(r, rows), :].astype(bf)
            return c

        lax.fori_loop(0, D_FF // rows, cast_dn, 0)

    def mlp(first, rows):
        xb = _load_token_rows(xs_ref, first, rows).astype(bf)
        y = jnp.broadcast_to(bd_ref[...], (rows, D_MODEL))
        for c in range(D_FF // FF_CHUNK):
            gs = slice(c * FF_CHUNK, (c + 1) * FF_CHUNK)
            ls = slice(D_FF + c * FF_CHUNK, D_FF + (c + 1) * FF_CHUNK)
            ag = jnp.dot(xb, wub_ref[:, gs], preferred_element_type=f32) + bu_ref[:, gs]
            al = jnp.dot(xb, wub_ref[:, ls], preferred_element_type=f32) + bu_ref[:, ls]
            g = jnp.minimum(ag, SWIGLU_LIMIT)
            lin = jnp.clip(al, -SWIGLU_LIMIT, SWIGLU_LIMIT)
            act = (lin + 1.0) * (g * jax.nn.sigmoid(SWIGLU_ALPHA * g))
            y = y + jnp.dot(act.astype(bf), wdb_ref[gs, :], preferred_element_type=f32)
        return y

    lo, hi = lo_ref[i], hi_ref[i]
    half = t // 2
    whole = (lo == 0) & (hi == t)
    partial = valid & jnp.logical_not(whole)

    @pl.when(valid & whole)
    def _():
        _store_token_rows(ys_ref, mlp(0, t))

    @pl.when(partial & first_visit)
    def _():
        ys_ref[...] = jnp.zeros_like(ys_ref)

    def fill(first, rows):
        row = first + lax.broadcasted_iota(jnp.int32, (rows, 1), 0)
        mine = (row >= lo) & (row < hi)
        _store_token_rows(ys_ref, jnp.where(mine, mlp(first, rows), _load_token_rows(ys_ref, first, rows)), first)

    pl.when(partial & (lo < half) & (hi > half))(functools.partial(fill, 0, t))
    pl.when(partial & (hi <= half))(functools.partial(fill, 0, half))
    pl.when(partial & (lo >= half))(functools.partial(fill, half, half))


def _experts(steps, xs, w_up, b_up, w_down, b_down):
    t = EXPERT_TILE
    max_steps = steps[0].shape[0]
    row_block = pl.BlockSpec((t * ROW_TILES, LANES), lambda i, te, tt, lo, hi, ns: (tt[i], 0))
    per_expert = lambda shape: pl.BlockSpec((None,) + shape, lambda i, te, tt, lo, hi, ns: (te[i], 0, 0))
    grid_spec = pltpu.PrefetchScalarGridSpec(
        num_scalar_prefetch=5,
        grid=(max_steps,),
        in_specs=[row_block, per_expert((D_MODEL, 2 * D_FF)), per_expert((1, 2 * D_FF)),
                  per_expert((D_FF, D_MODEL)), per_expert((1, D_MODEL))],
        out_specs=row_block,
        scratch_shapes=[pltpu.VMEM((D_MODEL, 2 * D_FF), jnp.bfloat16),
                        pltpu.VMEM((D_FF, D_MODEL), jnp.bfloat16)],
    )
    return pl.pallas_call(
        _expert_kernel,
        grid_spec=grid_spec,
        out_shape=jax.ShapeDtypeStruct(xs.shape, jnp.float32),
        compiler_params=_cparams(("arbitrary",)),
        name="experts",
    )(*steps, xs, w_up, b_up.reshape(N_EXPERTS, 1, 2 * D_FF), w_down, b_down.reshape(N_EXPERTS, 1, D_MODEL))


def _combine_kernel(cur_ref, nxt_ref, ys_ref, h_ref, rf_ref, g2_ref, b2_ref, o_ref, buf_ref, sems):
    i = pl.program_id(0)
    n_steps = pl.num_programs(0)
    t = MOVE_TILE
    rows = t * TOP_K
    slot = i % 2

    @pl.when(i == 0)
    def _():
        _row_gather(cur_ref, ys_ref, buf_ref.at[0], sems.at[0], rows)

    _rows_wait(ys_ref.at[pl.ds(0, rows * ROW_TILES)], buf_ref.at[slot], sems.at[slot])
    _row_gather(nxt_ref, ys_ref, buf_ref.at[1 - slot], sems.at[1 - slot], rows)

    rf = rf_ref[...]
    u = DN_ALPHA * _load_token_rows(h_ref, 0, t)
    for k in range(TOP_K):
        u = u + rf[:, k:k + 1] * _load_token_rows(buf_ref.at[slot], k * t, t)
    o_ref[...] = _layer_norm(u, g2_ref[...], b2_ref[...])

    @pl.when(i == n_steps - 1)
    def _():
        _rows_wait(ys_ref.at[pl.ds(0, rows * ROW_TILES)], buf_ref.at[1 - slot], sems.at[1 - slot])


def _combine(dest3, ys, h, rf, g2, b2):
    t = MOVE_TILE
    n = h.shape[0] // ROW_TILES
    last = n // t - 1
    tok = lambda w: pl.BlockSpec((t, w), lambda i: (i, 0))
    idx_spec = lambda f: pl.BlockSpec((1, 1, t * TOP_K), f, memory_space=pltpu.SMEM)
    return pl.pallas_call(
        _combine_kernel,
        grid=(n // t,),
        in_specs=[idx_spec(lambda i: (i, 0, 0)), idx_spec(lambda i: (jnp.minimum(i + 1, last), 0, 0)),
                  pl.BlockSpec(memory_space=pl.ANY), pl.BlockSpec((t * ROW_TILES, LANES), lambda i: (i, 0)),
                  tok(LANES), _full(g2.shape), _full(b2.shape)],
        out_specs=tok(D_MODEL),
        out_shape=jax.ShapeDtypeStruct((n, D_MODEL), jnp.float32),
        scratch_shapes=[pltpu.VMEM((2, t * TOP_K * ROW_TILES, LANES), jnp.float32),
                        pltpu.SemaphoreType.DMA((2,))],
        compiler_params=_cparams(("arbitrary",)),
        name="combine",
    )(dest3, dest3, ys, h, rf, g2, b2)


def _rope_tables(positions, half):
    inv_freq = ROPE_THETA ** (-jnp.arange(half, dtype=jnp.float32) / half)
    ang = positions.astype(jnp.float32).reshape(-1, 1) * inv_freq
    reps = LANES // half
    return jnp.tile(jnp.cos(ang), (1, reps)), jnp.tile(jnp.sin(ang), (1, reps))


def _lookup(table, idx):
    experts = jnp.arange(N_EXPERTS, dtype=jnp.int32)
    return jnp.sum(jnp.where(idx[..., None] == experts, table, 0), axis=-1)


def _expert_schedule(starts, ends, rows):
    t = EXPERT_TILE
    first_tile = starts // t
    n_tile = jnp.where(ends > starts, (ends - 1) // t - first_tile + 1, 0)
    cum = jnp.cumsum(n_tile)
    n_steps = cum[-1]
    max_steps = rows // t + N_EXPERTS
    s = jnp.minimum(jnp.arange(max_steps, dtype=jnp.int32), n_steps - 1)
    expert = jnp.sum((cum[None, :] <= s[:, None]).astype(jnp.int32), axis=1)
    tile = _lookup(first_tile, expert) + s - _lookup(cum - n_tile, expert)
    lo = jnp.maximum(_lookup(starts, expert) - tile * t, 0)
    hi = jnp.minimum(_lookup(ends, expert) - tile * t, t)
    return expert, tile, lo, hi, n_steps.astype(jnp.int32).reshape(1)


def _layer(x2, tabs, bsz, seq, w_in, b_gate, rms_cq, rms_ckv, w_uq, w_ukv, w_o_a, w_o_b, w_out,
           ln1_g, ln1_b, w_router, b_router, w_up, b_up, w_down, b_down, ln2_g, ln2_b):
    bf = jnp.bfloat16
    n = x2.shape[0]
    off = np.concatenate([[0], np.cumsum(SPLITS)]).tolist()
    col = lambda j: w_in[:, off[j]:off[j + 1]]
    w1 = jnp.concatenate([col(0), col(1), col(2), col(3)], axis=1).astype(bf)
    wi_pad = jnp.zeros((D_MODEL, LANES - IDX_HEADS), w_in.dtype)
    ws = jnp.concatenate([col(4), col(4), jnp.tile(col(8), (1, LANES // QK_ROPE)), col(5), wi_pad], axis=1).astype(bf)
    wl = jnp.concatenate([col(6), col(7)], axis=1).astype(bf)
    wg = col(9).astype(bf)
    uq = w_uq.reshape(Q_LORA, B_HEADS, QK_NOPE + QK_ROPE)
    wuq = jnp.concatenate([uq[:, :, :QK_NOPE].reshape(Q_LORA, -1), uq[:, :, QK_NOPE:].reshape(Q_LORA, -1)],
                          axis=1).astype(bf)
    ukv = w_ukv.reshape(KV_LORA, B_HEADS, QK_NOPE + V_DIM)
    wukv = jnp.concatenate([ukv[:, :, :QK_NOPE].reshape(KV_LORA, -1), ukv[:, :, QK_NOPE:].reshape(KV_LORA, -1)],
                           axis=1).astype(bf)

    qa, ka, va, qi, ki, kr, wi, qn, qr, kn, vb = _proj(
        x2, tabs, w1, ws, wl, rms_cq.reshape(1, -1), rms_ckv.reshape(1, -1), wuq, wukv)
    o_a = _dsa(qa, qi, wi, ka, va, ki, bsz, seq)
    o_b = _mla(qn, qr, kn, kr, vb, bsz, seq)

    wr = jnp.concatenate([w_router, jnp.zeros((D_MODEL, LANES - N_EXPERTS), w_router.dtype)], axis=1)
    br = jnp.concatenate([b_router, jnp.zeros((LANES - N_EXPERTS,), b_router.dtype)]).reshape(1, -1)
    wr_hi = wr.astype(bf)
    wr_lo = (wr - wr_hi.astype(jnp.float32)).astype(bf)
    h, ri, rf, cnt = _post(x2, o_a, o_b, wg, b_gate.reshape(1, -1), w_o_a.astype(bf), w_o_b.astype(bf),
                           w_out.astype(bf), ln1_g.reshape(1, -1), ln1_b.reshape(1, -1), wr_hi, wr_lo, br)

    counts = cnt[0, :N_EXPERTS].astype(jnp.int32)
    ends = jnp.cumsum(counts)
    starts = ends - counts
    dest = _lookup(starts, ri[:, :TOP_K]) + ri[:, TOP_K:2 * TOP_K]
    dest3 = dest.reshape(n // MOVE_TILE, MOVE_TILE, TOP_K).transpose(0, 2, 1).reshape(
        n // MOVE_TILE, 1, MOVE_TILE * TOP_K)

    xs = _dispatch(dest3, h, n * TOP_K)
    ys = _experts(_expert_schedule(starts, ends, n * TOP_K), xs, w_up, b_up, w_down, b_down)
    return _combine(dest3, ys, h, rf, ln2_g.reshape(1, -1), ln2_b.reshape(1, -1))


def kernel(x, positions, w_in, b_gate, rms_cq, rms_ckv, w_uq, w_ukv, w_o_a, w_o_b, w_out, ln1_g, ln1_b,
           w_router, b_router, w_up, b_up, w_down, b_down, ln2_g, ln2_b):
    bsz, seq, _ = x.shape
    x2 = x.reshape(bsz * seq, D_MODEL)
    c64, s64 = _rope_tables(positions, A_HEAD_DIM // 2)
    c32, s32 = _rope_tables(positions, QK_ROPE // 2)
    tabs = (c64, s64, c32, s32)
    for l in range(DEPTH):
        x2 = _layer(x2, tabs, bsz, seq, w_in[l], b_gate[l], rms_cq[l], rms_ckv[l], w_uq[l], w_ukv[l],
                    w_o_a[l], w_o_b[l], w_out[l], ln1_g[l], ln1_b[l], w_router[l], b_router[l],
                    w_up[l], b_up[l], w_down[l], b_down[l], ln2_g[l], ln2_b[l])
    return x2.reshape(bsz, seq, D_MODEL)
```

```python
import functools

import jax
import jax.numpy as jnp
import numpy as np
from jax import lax
from jax.experimental import pallas as pl
from jax.experimental.pallas import tpu as pltpu

D_MODEL = 1024
A_HEADS = 8
A_HEAD_DIM = 64
IDX_HEADS = 8
IDX_DIM = 64
TOPK_MAX = 256
B_HEADS = 8
Q_LORA = 384
KV_LORA = 256
QK_NOPE = 64
QK_ROPE = 32
V_DIM = 64
ROPE_THETA = 10000.0
N_EXPERTS = 32
TOP_K = 4
D_FF = 1024
SWIGLU_LIMIT = 7.0
SWIGLU_ALPHA = 1.702
DEPTH = 1
DN_ALPHA = (2 * DEPTH) ** 0.25
LN_EPS = 1e-5
RMS_EPS = 1e-6
A_WIDTH = A_HEADS * A_HEAD_DIM
B_WIDTH = B_HEADS * V_DIM
SPLITS = (A_WIDTH, A_WIDTH, A_WIDTH, IDX_HEADS * IDX_DIM, IDX_DIM, IDX_HEADS,
          Q_LORA, KV_LORA, QK_ROPE, 2 * D_MODEL)

LANES = 128
VMEM_LIMIT_BYTES = 52 * 1024 * 1024

PROJ_TILE = 512
Q_TILE = 256
EXPERT_TILE = 512
FF_CHUNK = 1024
DISPATCH_TILE = 512
COMBINE_TILE = 256

ROW_TILES = D_MODEL // LANES
assert ROW_TILES == 8

LOG2_E = 1.4426950408889634
INT_MIN = -(2 ** 31)
NEG_BIG = -1e30

_NT = (((1,), (1,)), ((), ()))


def _cparams(sem):
    return pltpu.CompilerParams(dimension_semantics=sem, vmem_limit_bytes=VMEM_LIMIT_BYTES)


def _full(shape):
    nd = len(shape)
    return pl.BlockSpec(shape, lambda *_: (0,) * nd)


def _rope_chunk(x, cos, sin, half):
    lane = lax.broadcasted_iota(jnp.int32, x.shape, 1)
    first = (lane % (2 * half)) < half
    up = pltpu.roll(x, LANES - half, 1)
    dn = pltpu.roll(x, half, 1)
    return x * cos + jnp.where(first, -up, dn) * sin


def _layer_norm(u, g, b):
    mu = jnp.mean(u, axis=-1, keepdims=True)
    d = u - mu
    var = jnp.mean(d * d, axis=-1, keepdims=True)
    return d * lax.rsqrt(var + LN_EPS) * g + b


def _load_token_rows(ref, first, t):
    return jnp.concatenate([ref[pl.ds(first * ROW_TILES + s, t, stride=ROW_TILES), :] for s in range(ROW_TILES)],
                           axis=1)


def _store_token_rows(ref, v, first=0):
    t = v.shape[0]
    for s in range(ROW_TILES):
        ref[pl.ds(first * ROW_TILES + s, t, stride=ROW_TILES), :] = v[:, s * LANES:(s + 1) * LANES]


def _two_head_softmax_pv(logits, vc, width):
    lane_v = lax.broadcasted_iota(jnp.int32, vc.shape, 1)
    pvs = []
    for hh in range(2):
        lg = logits[hh]
        p = jnp.exp2(lg - jnp.max(lg, axis=1, keepdims=True)).astype(jnp.bfloat16)
        v_h = jnp.where((lane_v // width) == hh, vc, jnp.ones_like(vc))
        pvs.append(jnp.dot(p, v_h, preferred_element_type=jnp.float32))
    lane = lax.broadcasted_iota(jnp.int32, pvs[0].shape, 1)
    first = lane < width
    num = jnp.where(first, pvs[0], pvs[1])
    den = jnp.where(first, pltpu.roll(pvs[0], width, 1), pltpu.roll(pvs[1], width, 1))
    return num / den


def _head_mask(shape, width, which):
    lane = lax.broadcasted_iota(jnp.int32, shape, 1)
    return (lane // width) == which


def _proj_kernel(x_ref, c64_ref, s64_ref, c32_ref, s32_ref, w1_ref, ws_ref, wl_ref,
                 gcq_ref, gckv_ref, wuq_ref, wukv_ref,
                 qa_ref, ka_ref, va_ref, qi_ref, ki_ref, kr_ref, wi_ref,
                 qn_ref, qr_ref, kn_ref, vb_ref):
    xb = x_ref[...].astype(jnp.bfloat16)
    c64, s64 = c64_ref[...], s64_ref[...]
    c32, s32 = c32_ref[...], s32_ref[...]
    f32 = jnp.float32
    bf = jnp.bfloat16

    z1 = jnp.dot(xb, w1_ref[...], preferred_element_type=f32)
    n_chunk = A_WIDTH // LANES
    a_scale = A_HEAD_DIM ** -0.5 * LOG2_E
    for c in range(n_chunk):
        sl = slice(c * LANES, (c + 1) * LANES)
        qa_ref[:, sl] = (_rope_chunk(z1[:, c * LANES:(c + 1) * LANES], c64, s64, 32) * a_scale).astype(bf)
        o = A_WIDTH + c * LANES
        ka_ref[:, sl] = _rope_chunk(z1[:, o:o + LANES], c64, s64, 32).astype(bf)
        o = 2 * A_WIDTH + c * LANES
        va_ref[:, sl] = z1[:, o:o + LANES].astype(bf)
        o = 3 * A_WIDTH + c * LANES
        qi_ref[:, sl] = _rope_chunk(z1[:, o:o + LANES], c64, s64, 32).astype(bf)

    zs = jnp.dot(xb, ws_ref[...], preferred_element_type=f32)
    ki_ref[...] = _rope_chunk(zs[:, 0:LANES], c64, s64, 32).astype(bf)
    kr_ref[...] = _rope_chunk(zs[:, LANES:2 * LANES], c32, s32, 16).astype(bf)
    wi_ref[...] = zs[:, 2 * LANES:3 * LANES] * (IDX_HEADS ** -0.5 * IDX_DIM ** -0.5)

    zl = jnp.dot(xb, wl_ref[...], preferred_element_type=f32)
    cq = zl[:, :Q_LORA]
    cq = cq * lax.rsqrt(jnp.mean(cq * cq, axis=-1, keepdims=True) + RMS_EPS) * gcq_ref[...]
    ckv = zl[:, Q_LORA:]
    ckv = ckv * lax.rsqrt(jnp.mean(ckv * ckv, axis=-1, keepdims=True) + RMS_EPS) * gckv_ref[...]
    qb = jnp.dot(cq.astype(bf), wuq_ref[...], preferred_element_type=f32)
    kv = jnp.dot(ckv.astype(bf), wukv_ref[...], preferred_element_type=f32)
    b_scale = (QK_NOPE + QK_ROPE) ** -0.5 * LOG2_E
    qn_ref[...] = (qb[:, :B_HEADS * QK_NOPE] * b_scale).astype(bf)
    for c in range(B_HEADS * QK_ROPE // LANES):
        o = B_HEADS * QK_NOPE + c * LANES
        qr_ref[:, c * LANES:(c + 1) * LANES] = (
            _rope_chunk(qb[:, o:o + LANES], c32, s32, 16) * b_scale).astype(bf)
    kn_ref[...] = kv[:, :B_HEADS * QK_NOPE].astype(bf)
    vb_ref[...] = kv[:, B_HEADS * QK_NOPE:].astype(bf)


def _proj(x2, tabs, w1, ws, wl, gcq, gckv, wuq, wukv):
    n = x2.shape[0]
    t = PROJ_TILE
    tok = lambda w: pl.BlockSpec((t, w), lambda i: (i, 0))
    bf = jnp.bfloat16
    outs = [(A_WIDTH, bf)] * 4 + [(LANES, bf), (LANES, bf), (LANES, jnp.float32),
                                  (B_HEADS * QK_NOPE, bf), (B_HEADS * QK_ROPE, bf),
                                  (B_HEADS * QK_NOPE, bf), (B_WIDTH, bf)]
    return pl.pallas_call(
        _proj_kernel,
        grid=(n // t,),
        in_specs=[tok(D_MODEL)] + [tok(LANES)] * 4 + [_full(a.shape) for a in (w1, ws, wl, gcq, gckv, wuq, wukv)],
        out_specs=[tok(w) for w, _ in outs],
        out_shape=[jax.ShapeDtypeStruct((n, w), d) for w, d in outs],
        compiler_params=_cparams(("arbitrary",)),
        name="proj",
    )(x2, *tabs, w1, ws, wl, gcq, gckv, wuq, wukv)


def _tie_select(eq, gt, need, kv):
    chunk = 256
    r = lax.broadcasted_iota(jnp.int32, (chunk, chunk), 0)
    c = lax.broadcasted_iota(jnp.int32, (chunk, chunk), 1)
    upper = jnp.where(r <= c, 1.0, 0.0).astype(jnp.bfloat16)
    carry = jnp.zeros((eq.shape[0], 1), jnp.float32)
    parts = []
    for j in range(kv // chunk):
        e = eq[:, j * chunk:(j + 1) * chunk]
        ef = jnp.where(e, 1.0, 0.0).astype(jnp.bfloat16)
        pref = jnp.dot(ef, upper, preferred_element_type=jnp.float32) + carry
        parts.append(gt[:, j * chunk:(j + 1) * chunk] | (e & (pref <= need)))
        carry = carry + jnp.sum(ef.astype(jnp.float32), axis=1, keepdims=True)
    return jnp.concatenate(parts, axis=1)


def _key_to_float(key):
    return lax.bitcast_convert_type(key ^ ((key >> 31) & jnp.int32(0x7FFFFFFF)), jnp.float32)


def _select_topk(score, k_eff, sel_ref, kv):
    tq = score.shape[0]

    def search(it, prefix):
        cand = prefix + lax.shift_left(jnp.int32(1), 31 - it)
        cnt = jnp.sum(jnp.where(score >= _key_to_float(cand), 1.0, 0.0), axis=1, keepdims=True)
        return jnp.where(cnt >= k_eff, cand, prefix)

    thr = _key_to_float(lax.fori_loop(0, 32, search, jnp.full((tq, 1), INT_MIN, jnp.int32)))
    ge = score >= thr
    n_ge = jnp.sum(jnp.where(ge, 1.0, 0.0), axis=1, keepdims=True)
    sel_ref[:, :kv] = jnp.where(ge, 0.0, NEG_BIG)
    has_tie = jnp.max(jnp.where(n_ge != k_eff, 1.0, 0.0)) > 0.5

    @pl.when(has_tie)
    def _():
        gt = score > thr
        need = k_eff - jnp.sum(jnp.where(gt, 1.0, 0.0), axis=1, keepdims=True)
        sel_ref[:, :kv] = jnp.where(_tie_select(score == thr, gt, need, kv), 0.0, NEG_BIG)


def _dsa_block(qa_ref, qi_ref, wi_ref, ka_ref, va_ref, ki_ref, o_ref, sel_ref, score_ref, *, kv, k_sel):
    tq = qa_ref.shape[0]
    f32 = jnp.float32
    bf = jnp.bfloat16
    q_pos = (kv - tq) + lax.broadcasted_iota(jnp.int32, (tq, 1), 0)
    key_pos = lax.broadcasted_iota(jnp.int32, (tq, kv), 1)
    causal = key_pos <= q_pos

    if kv <= k_sel:
        sel_ref[:, :kv] = jnp.where(causal, 0.0, NEG_BIG)
    else:
        ki = ki_ref[:kv, :]
        lane = lax.broadcasted_iota(jnp.int32, (tq, LANES), 1)
        score_ref[:, :kv] = jnp.zeros((tq, kv), f32)

        def idx_slab(c, carry):
            qc = qi_ref[:, pl.ds(pl.multiple_of(c * LANES, LANES), LANES)]
            acc = score_ref[:, :kv]
            for hh in range(2):
                qm = jnp.where(_head_mask(qc.shape, IDX_DIM, hh), qc, jnp.zeros_like(qc))
                s = lax.dot_general(qm, ki, _NT, preferred_element_type=f32)
                w = jnp.sum(jnp.where(lane == 2 * c + hh, wi_ref[...], 0.0), axis=1, keepdims=True)
                acc = acc + jnp.maximum(s, 0.0) * w
            score_ref[:, :kv] = acc
            return carry

        lax.fori_loop(0, IDX_HEADS // 2, idx_slab, 0)
        score = jnp.where(causal, score_ref[:, :kv], -jnp.inf)
        k_eff = jnp.minimum(q_pos + 1, k_sel).astype(f32)
        _select_topk(score, k_eff, sel_ref, kv)

    def attn_slab(c, carry):
        sl = pl.ds(pl.multiple_of(c * LANES, LANES), LANES)
        qc = qa_ref[:, sl]
        kc = ka_ref[:kv, sl]
        vc = va_ref[:kv, sl]
        mask = sel_ref[:, :kv]
        logits = []
        for hh in range(2):
            qm = jnp.where(_head_mask(qc.shape, A_HEAD_DIM, hh), qc, jnp.zeros_like(qc))
            logits.append(lax.dot_general(qm, kc, _NT, preferred_element_type=f32) + mask)
        o_ref[:, sl] = _two_head_softmax_pv(logits, vc, A_HEAD_DIM).astype(bf)
        return carry

    lax.fori_loop(0, A_WIDTH // LANES, attn_slab, 0)


def _per_query_block(block_fn, refs, n_blocks, tq):
    j = pl.program_id(1)
    for b in range(n_blocks):
        pl.when(j == b)(functools.partial(block_fn, *refs, kv=(b + 1) * tq))


def _dsa_kernel(*refs, seq, k_sel):
    tq = refs[0].shape[0]
    _per_query_block(functools.partial(_dsa_block, k_sel=k_sel), refs, seq // tq, tq)


def _dsa(qa, qi, wi, ka, va, ki, bsz, seq):
    tq = min(Q_TILE, seq)
    nq = seq // tq
    qspec = lambda w: pl.BlockSpec((tq, w), lambda b, i: (b * nq + i, 0))
    kspec = lambda w: pl.BlockSpec((seq, w), lambda b, i: (b, 0))
    k_sel = min(TOPK_MAX, seq // 4)
    return pl.pallas_call(
        functools.partial(_dsa_kernel, seq=seq, k_sel=k_sel),
        grid=(bsz, nq),
        in_specs=[qspec(A_WIDTH), qspec(A_WIDTH), qspec(LANES), kspec(A_WIDTH), kspec(A_WIDTH), kspec(LANES)],
        out_specs=qspec(A_WIDTH),
        out_shape=jax.ShapeDtypeStruct((bsz * seq, A_WIDTH), jnp.bfloat16),
        scratch_shapes=[pltpu.VMEM((tq, seq), jnp.float32), pltpu.VMEM((tq, seq), jnp.float32)],
        compiler_params=_cparams(("arbitrary", "arbitrary")),
        name="dsa",
    )(qa, qi, wi, ka, va, ki)


def _mla_block(qn_ref, qr_ref, kn_ref, kr_ref, vb_ref, o_ref, *, kv):
    tq = qn_ref.shape[0]
    f32 = jnp.float32
    bf = jnp.bfloat16
    diag = lax.broadcasted_iota(jnp.int32, (tq, tq), 1) <= lax.broadcasted_iota(jnp.int32, (tq, tq), 0)
    kr = kr_ref[:kv, :]

    def causal_masked(lg):
        own = jnp.where(diag, lg[:, kv - tq:], NEG_BIG)
        return own if kv == tq else jnp.concatenate([lg[:, :kv - tq], own], axis=1)

    def slab(c, carry):
        sl = pl.ds(pl.multiple_of(c * LANES, LANES), LANES)
        qc = qn_ref[:, sl]
        kcat = jnp.concatenate([kn_ref[:kv, sl], kr], axis=1)
        vc = vb_ref[:kv, sl]
        qrc = qr_ref[:, pl.ds(pl.multiple_of((c // 2) * LANES, LANES), LANES)]
        logits = []
        for hh in range(2):
            h = 2 * c + hh
            qm = jnp.where(_head_mask(qc.shape, QK_NOPE, hh), qc, jnp.zeros_like(qc))
            qrm = jnp.where(_head_mask(qrc.shape, QK_ROPE, h % 4), qrc, jnp.zeros_like(qrc))
            qcat = jnp.concatenate([qm, qrm], axis=1)
            lg = lax.dot_general(qcat, kcat, _NT, preferred_element_type=f32)
            logits.append(causal_masked(lg))
        o_ref[:, sl] = _two_head_softmax_pv(logits, vc, V_DIM).astype(bf)
        return carry

    lax.fori_loop(0, B_WIDTH // LANES, slab, 0, unroll=2)


def _mla_kernel(*refs, seq):
    tq = refs[0].shape[0]
    _per_query_block(_mla_block, refs, seq // tq, tq)


def _mla(qn, qr, kn, kr, vb, bsz, seq):
    tq = min(Q_TILE, seq)
    nq = seq // tq
    qspec = lambda w: pl.BlockSpec((tq, w), lambda b, i: (b * nq + i, 0))
    kspec = lambda w: pl.BlockSpec((seq, w), lambda b, i: (b, 0))
    return pl.pallas_call(
        functools.partial(_mla_kernel, seq=seq),
        grid=(bsz, nq),
        in_specs=[qspec(B_HEADS * QK_NOPE), qspec(B_HEADS * QK_ROPE), kspec(B_HEADS * QK_NOPE),
                  kspec(LANES), kspec(B_WIDTH)],
        out_specs=qspec(B_WIDTH),
        out_shape=jax.ShapeDtypeStruct((bsz * seq, B_WIDTH), jnp.bfloat16),
        compiler_params=_cparams(("arbitrary", "arbitrary")),
        name="mla",
    )(qn, qr, kn, kr, vb)


def _post_kernel(x_ref, oa_ref, ob_ref, wg_ref, bg_ref, woa_ref, wob_ref, wout_ref, g1_ref, b1_ref,
                 wrh_ref, wrl_ref, br_ref, h_ref, ri_ref, rf_ref, cnt_ref, carry_ref):
    f32 = jnp.float32
    bf = jnp.bfloat16
    t = x_ref.shape[0]

    @pl.when(pl.program_id(0) == 0)
    def _():
        carry_ref[...] = jnp.zeros_like(carry_ref)

    x = x_ref[...]
    gate = jax.nn.sigmoid(jnp.dot(x.astype(bf), wg_ref[...], preferred_element_type=f32) + bg_ref[...])
    pa = jnp.dot(oa_ref[...], woa_ref[...], preferred_element_type=f32)
    pb = jnp.dot(ob_ref[...], wob_ref[...], preferred_element_type=f32)
    mix = gate[:, :D_MODEL] * pa + gate[:, D_MODEL:] * pb
    u = DN_ALPHA * x + jnp.dot(mix.astype(bf), wout_ref[...], preferred_element_type=f32)
    h = _layer_norm(u, g1_ref[...], b1_ref[...])
    _store_token_rows(h_ref, h)

    h_hi = h.astype(bf)
    h_lo = (h - h_hi.astype(f32)).astype(bf)
    logits = (jnp.dot(h_hi, wrh_ref[...], preferred_element_type=f32)
              + jnp.dot(h_lo, wrh_ref[...], preferred_element_type=f32)
              + jnp.dot(h_hi, wrl_ref[...], preferred_element_type=f32) + br_ref[...])
    lane = lax.broadcasted_iota(jnp.int32, (t, LANES), 1).astype(f32)
    lg = jnp.where(lane < N_EXPERTS, logits, -jnp.inf)
    vals, ids = [], []
    assign = jnp.zeros((t, LANES), f32)
    for _k in range(TOP_K):
        m = jnp.max(lg, axis=1, keepdims=True)
        idx = jnp.min(jnp.where(lg == m, lane, float(LANES)), axis=1, keepdims=True)
        hit = lane == idx
        vals.append(m)
        ids.append(idx)
        assign = jnp.where(hit, 1.0, assign)
        lg = jnp.where(hit, -jnp.inf, lg)
    exps = [jnp.exp(v - vals[0]) for v in vals]
    den = exps[0] + exps[1] + exps[2] + exps[3]

    r = lax.broadcasted_iota(jnp.int32, (t, t), 0)
    c = lax.broadcasted_iota(jnp.int32, (t, t), 1)
    lower = jnp.where(c < r, 1.0, 0.0).astype(bf)
    before = jnp.dot(lower, assign.astype(bf), preferred_element_type=f32) + carry_ref[0:1, :]
    ri = jnp.zeros((t, LANES), f32)
    rf = jnp.zeros((t, LANES), f32)
    for k in range(TOP_K):
        rank = jnp.sum(jnp.where(lane == ids[k], before, 0.0), axis=1, keepdims=True)
        ri = jnp.where(lane == float(k), ids[k], ri)
        ri = jnp.where(lane == float(TOP_K + k), rank, ri)
        rf = jnp.where(lane == float(k), exps[k] / den, rf)
    ri_ref[...] = ri.astype(jnp.int32)
    rf_ref[...] = rf
    total = carry_ref[0:1, :] + jnp.sum(assign, axis=0, keepdims=True)
    carry_ref[...] = jnp.broadcast_to(total, carry_ref.shape)
    cnt_ref[...] = jnp.broadcast_to(total, cnt_ref.shape)


def _post(x2, oa, ob, wg, bg, woa, wob, wout, g1, b1, wrh, wrl, br):
    n = x2.shape[0]
    t = PROJ_TILE
    tok = lambda w: pl.BlockSpec((t, w), lambda i: (i, 0))
    consts = (wg, bg, woa, wob, wout, g1, b1, wrh, wrl, br)
    return pl.pallas_call(
        _post_kernel,
        grid=(n // t,),
        in_specs=[tok(D_MODEL), tok(A_WIDTH), tok(B_WIDTH)] + [_full(a.shape) for a in consts],
        out_specs=[pl.BlockSpec((t * ROW_TILES, LANES), lambda i: (i, 0)), tok(LANES), tok(LANES),
                   _full((8, LANES))],
        out_shape=[jax.ShapeDtypeStruct((n * ROW_TILES, LANES), jnp.float32),
                   jax.ShapeDtypeStruct((n, LANES), jnp.int32),
                   jax.ShapeDtypeStruct((n, LANES), jnp.float32),
                   jax.ShapeDtypeStruct((8, LANES), jnp.float32)],
        scratch_shapes=[pltpu.VMEM((8, LANES), jnp.float32)],
        compiler_params=_cparams(("arbitrary",)),
        name="post",
    )(x2, oa, ob, *consts)


def _token_row(ref, r):
    start = r * ROW_TILES if isinstance(r, int) else pl.multiple_of(r * ROW_TILES, ROW_TILES)
    return ref.at[pl.ds(start, ROW_TILES)]


def _row_gather(idx_ref, src_ref, dst_ref, sem, rows):
    for j in range(rows):
        pltpu.make_async_copy(_token_row(src_ref, idx_ref[0, 0, j]), _token_row(dst_ref, j), sem).start(
            priority=j % 2)


def _rows_wait(src_ref, dst_ref, sem):
    pltpu.make_async_copy(src_ref, dst_ref, sem).wait()


def _dispatch_kernel(dest_ref, h_ref, xs_ref, sem):
    t = DISPATCH_TILE
    for k in range(TOP_K):
        for j in range(t):
            pltpu.make_async_copy(_token_row(h_ref, j), _token_row(xs_ref, dest_ref[0, 0, k * t + j]), sem).start(
                priority=j % 2)
    for k in range(TOP_K):
        _rows_wait(h_ref, xs_ref.at[pl.ds(0, t * ROW_TILES)], sem)


def _dispatch(dest3, h, rows):
    t = DISPATCH_TILE
    n = h.shape[0] // ROW_TILES
    return pl.pallas_call(
        _dispatch_kernel,
        grid=(n // t,),
        in_specs=[pl.BlockSpec((1, 1, t * TOP_K), lambda i: (i, 0, 0), memory_space=pltpu.SMEM),
                  pl.BlockSpec((t * ROW_TILES, LANES), lambda i: (i, 0))],
        out_specs=pl.BlockSpec(memory_space=pl.ANY),
        out_shape=jax.ShapeDtypeStruct((rows * ROW_TILES, LANES), jnp.float32),
        scratch_shapes=[pltpu.SemaphoreType.DMA(())],
        compiler_params=_cparams(("arbitrary",)),
        name="dispatch",
    )(dest3, h)


def _expert_kernel(te_ref, tt_ref, lo_ref, hi_ref, ns_ref, xs_ref, wu_ref, bu_ref, wd_ref, bd_ref, ys_ref,
                   wub_ref, wdb_ref):
    i = pl.program_id(0)
    f32 = jnp.float32
    bf = jnp.bfloat16
    t = EXPERT_TILE
    valid = i < ns_ref[0]
    before = jnp.maximum(i - 1, 0)
    fresh = valid & ((i == 0) | (te_ref[i] != te_ref[before]))
    first_visit = (i == 0) | (tt_ref[i] != tt_ref[before])

    @pl.when(fresh)
    def _():
        rows = 128

        def cast_up(j, c):
            r = pl.multiple_of(j * rows, rows)
            wub_ref[pl.ds(r, rows), :] = wu_ref[pl.ds(r, rows), :].astype(bf)
            return c

        lax.fori_loop(0, D_MODEL // rows, cast_up, 0)

        def cast_dn(j, c):
            r = pl.multiple_of(j * rows, rows)
            wdb_ref[pl.ds(r, rows), :] = wd_ref[pl.ds(r, rows), :].astype(bf)
            return c

        lax.fori_loop(0, D_FF // rows, cast_dn, 0)

    def mlp(first, rows):
        xb = _load_token_rows(xs_ref, first, rows).astype(bf)
        y = jnp.broadcast_to(bd_ref[...], (rows, D_MODEL))
        for c in range(D_FF // FF_CHUNK):
            gs = slice(c * FF_CHUNK, (c + 1) * FF_CHUNK)
            ls = slice(D_FF + c * FF_CHUNK, D_FF + (c + 1) * FF_CHUNK)
            ag = jnp.dot(xb, wub_ref[:, gs], preferred_element_type=f32) + bu_ref[:, gs]
            al = jnp.dot(xb, wub_ref[:, ls], preferred_element_type=f32) + bu_ref[:, ls]
            g = jnp.minimum(ag, SWIGLU_LIMIT)
            lin = jnp.clip(al, -SWIGLU_LIMIT, SWIGLU_LIMIT)
            act = (lin + 1.0) * (g * jax.nn.sigmoid(SWIGLU_ALPHA * g))
            y = y + jnp.dot(act.astype(bf), wdb_ref[gs, :], preferred_element_type=f32)
        return y

    lo, hi = lo_ref[i], hi_ref[i]
    half = t // 2
    whole = (lo == 0) & (hi == t)
    partial = valid & jnp.logical_not(whole)

    @pl.when(valid & whole)
    def _():
        _store_token_rows(ys_ref, mlp(0, t))

    @pl.when(partial & first_visit)
    def _():
        ys_ref[...] = jnp.zeros_like(ys_ref)

    def fill(first, rows):
        row = first + lax.broadcasted_iota(jnp.int32, (rows, 1), 0)
        mine = (row >= lo) & (row < hi)
        _store_token_rows(ys_ref, jnp.where(mine, mlp(first, rows), _load_token_rows(ys_ref, first, rows)), first)

    pl.when(partial & (lo < half) & (hi > half))(functools.partial(fill, 0, t))
    pl.when(partial & (hi <= half))(functools.partial(fill, 0, half))
    pl.when(partial & (lo >= half))(functools.partial(fill, half, half))


def _experts(steps, xs, w_up, b_up, w_down, b_down):
    t = EXPERT_TILE
    max_steps = steps[0].shape[0]
    row_block = pl.BlockSpec((t * ROW_TILES, LANES), lambda i, te, tt, lo, hi, ns: (tt[i], 0))
    per_expert = lambda shape: pl.BlockSpec((None,) + shape, lambda i, te, tt, lo, hi, ns: (te[i], 0, 0))
    grid_spec = pltpu.PrefetchScalarGridSpec(
        num_scalar_prefetch=5,
        grid=(max_steps,),
        in_specs=[row_block, per_expert((D_MODEL, 2 * D_FF)), per_expert((1, 2 * D_FF)),
                  per_expert((D_FF, D_MODEL)), per_expert((1, D_MODEL))],
        out_specs=row_block,
        scratch_shapes=[pltpu.VMEM((D_MODEL, 2 * D_FF), jnp.bfloat16),
                        pltpu.VMEM((D_FF, D_MODEL), jnp.bfloat16)],
    )
    return pl.pallas_call(
        _expert_kernel,
        grid_spec=grid_spec,
        out_shape=jax.ShapeDtypeStruct(xs.shape, jnp.float32),
        compiler_params=_cparams(("arbitrary",)),
        name="experts",
    )(*steps, xs, w_up, b_up.reshape(N_EXPERTS, 1, 2 * D_FF), w_down, b_down.reshape(N_EXPERTS, 1, D_MODEL))


def _combine_kernel(cur_ref, nxt_ref, ys_ref, h_ref, rf_ref, g2_ref, b2_ref, o_ref, buf_ref, sems):
    i = pl.program_id(0)
    n_steps = pl.num_programs(0)
    t = COMBINE_TILE
    rows = t * TOP_K
    slot = i % 2

    @pl.when(i == 0)
    def _():
        _row_gather(cur_ref, ys_ref, buf_ref.at[0], sems.at[0], rows)

    _rows_wait(ys_ref.at[pl.ds(0, rows * ROW_TILES)], buf_ref.at[slot], sems.at[slot])
    _row_gather(nxt_ref, ys_ref, buf_ref.at[1 - slot], sems.at[1 - slot], rows)

    rf = rf_ref[...]
    u = DN_ALPHA * _load_token_rows(h_ref, 0, t)
    for k in range(TOP_K):
        u = u + rf[:, k:k + 1] * _load_token_rows(buf_ref.at[slot], k * t, t)
    o_ref[...] = _layer_norm(u, g2_ref[...], b2_ref[...])

    @pl.when(i == n_steps - 1)
    def _():
        _rows_wait(ys_ref.at[pl.ds(0, rows * ROW_TILES)], buf_ref.at[1 - slot], sems.at[1 - slot])


def _combine(dest3, ys, h, rf, g2, b2):
    t = COMBINE_TILE
    n = h.shape[0] // ROW_TILES
    last = n // t - 1
    tok = lambda w: pl.BlockSpec((t, w), lambda i: (i, 0))
    idx_spec = lambda f: pl.BlockSpec((1, 1, t * TOP_K), f, memory_space=pltpu.SMEM)
    return pl.pallas_call(
        _combine_kernel,
        grid=(n // t,),
        in_specs=[idx_spec(lambda i: (i, 0, 0)), idx_spec(lambda i: (jnp.minimum(i + 1, last), 0, 0)),
                  pl.BlockSpec(memory_space=pl.ANY), pl.BlockSpec((t * ROW_TILES, LANES), lambda i: (i, 0)),
                  tok(LANES), _full(g2.shape), _full(b2.shape)],
        out_specs=tok(D_MODEL),
        out_shape=jax.ShapeDtypeStruct((n, D_MODEL), jnp.float32),
        scratch_shapes=[pltpu.VMEM((2, t * TOP_K * ROW_TILES, LANES), jnp.float32),
                        pltpu.SemaphoreType.DMA((2,))],
        compiler_params=_cparams(("arbitrary",)),
        name="combine",
    )(dest3, dest3, ys, h, rf, g2, b2)


def _rope_tables(positions, half):
    inv_freq = ROPE_THETA ** (-jnp.arange(half, dtype=jnp.float32) / half)
    ang = positions.astype(jnp.float32).reshape(-1, 1) * inv_freq
    reps = LANES // half
    return jnp.tile(jnp.cos(ang), (1, reps)), jnp.tile(jnp.sin(ang), (1, reps))


def _lookup(table, idx):
    experts = jnp.arange(N_EXPERTS, dtype=jnp.int32)
    return jnp.sum(jnp.where(idx[..., None] == experts, table, 0), axis=-1)


def _expert_schedule(starts, ends, rows):
    t = EXPERT_TILE
    first_tile = starts // t
    n_tile = jnp.where(ends > starts, (ends - 1) // t - first_tile + 1, 0)
    cum = jnp.cumsum(n_tile)
    n_steps = cum[-1]
    max_steps = rows // t + N_EXPERTS
    s = jnp.minimum(jnp.arange(max_steps, dtype=jnp.int32), n_steps - 1)
    expert = jnp.sum((cum[None, :] <= s[:, None]).astype(jnp.int32), axis=1)
    tile = _lookup(first_tile, expert) + s - _lookup(cum - n_tile, expert)
    lo = jnp.maximum(_lookup(starts, expert) - tile * t, 0)
    hi = jnp.minimum(_lookup(ends, expert) - tile * t, t)
    return expert, tile, lo, hi, n_steps.astype(jnp.int32).reshape(1)


def _layer(x2, tabs, bsz, seq, w_in, b_gate, rms_cq, rms_ckv, w_uq, w_ukv, w_o_a, w_o_b, w_out,
           ln1_g, ln1_b, w_router, b_router, w_up, b_up, w_down, b_down, ln2_g, ln2_b):
    bf = jnp.bfloat16
    n = x2.shape[0]
    off = np.concatenate([[0], np.cumsum(SPLITS)]).tolist()
    col = lambda j: w_in[:, off[j]:off[j + 1]]
    w1 = jnp.concatenate([col(0), col(1), col(2), col(3)], axis=1).astype(bf)
    wi_pad = jnp.zeros((D_MODEL, LANES - IDX_HEADS), w_in.dtype)
    ws = jnp.concatenate([col(4), col(4), jnp.tile(col(8), (1, LANES // QK_ROPE)), col(5), wi_pad], axis=1).astype(bf)
    wl = jnp.concatenate([col(6), col(7)], axis=1).astype(bf)
    wg = col(9).astype(bf)
    uq = w_uq.reshape(Q_LORA, B_HEADS, QK_NOPE + QK_ROPE)
    wuq = jnp.concatenate([uq[:, :, :QK_NOPE].reshape(Q_LORA, -1), uq[:, :, QK_NOPE:].reshape(Q_LORA, -1)],
                          axis=1).astype(bf)
    ukv = w_ukv.reshape(KV_LORA, B_HEADS, QK_NOPE + V_DIM)
    wukv = jnp.concatenate([ukv[:, :, :QK_NOPE].reshape(KV_LORA, -1), ukv[:, :, QK_NOPE:].reshape(KV_LORA, -1)],
                           axis=1).astype(bf)

    qa, ka, va, qi, ki, kr, wi, qn, qr, kn, vb = _proj(
        x2, tabs, w1, ws, wl, rms_cq.reshape(1, -1), rms_ckv.reshape(1, -1), wuq, wukv)
    o_a = _dsa(qa, qi, wi, ka, va, ki, bsz, seq)
    o_b = _mla(qn, qr, kn, kr, vb, bsz, seq)

    wr = jnp.concatenate([w_router, jnp.zeros((D_MODEL, LANES - N_EXPERTS), w_router.dtype)], axis=1)
    br = jnp.concatenate([b_router, jnp.zeros((LANES - N_EXPERTS,), b_router.dtype)]).reshape(1, -1)
    wr_hi = wr.astype(bf)
    wr_lo = (wr - wr_hi.astype(jnp.float32)).astype(bf)
    h, ri, rf, cnt = _post(x2, o_a, o_b, wg, b_gate.reshape(1, -1), w_o_a.astype(bf), w_o_b.astype(bf),
                           w_out.astype(bf), ln1_g.reshape(1, -1), ln1_b.reshape(1, -1), wr_hi, wr_lo, br)

    counts = cnt[0, :N_EXPERTS].astype(jnp.int32)
    ends = jnp.cumsum(counts)
    starts = ends - counts
    dest = _lookup(starts, ri[:, :TOP_K]) + ri[:, TOP_K:2 * TOP_K]
    by_step = lambda t: dest.reshape(n // t, t, TOP_K).transpose(0, 2, 1).reshape(n // t, 1, t * TOP_K)

    xs = _dispatch(by_step(DISPATCH_TILE), h, n * TOP_K)
    ys = _experts(_expert_schedule(starts, ends, n * TOP_K), xs, w_up, b_up, w_down, b_down)
    return _combine(by_step(COMBINE_TILE), ys, h, rf, ln2_g.reshape(1, -1), ln2_b.reshape(1, -1))


def kernel(x, positions, w_in, b_gate, rms_cq, rms_ckv, w_uq, w_ukv, w_o_a, w_o_b, w_out, ln1_g, ln1_b,
           w_router, b_router, w_up, b_up, w_down, b_down, ln2_g, ln2_b):
    bsz, seq, _ = x.shape
    x2 = x.reshape(bsz * seq, D_MODEL)
    c64, s64 = _rope_tables(positions, A_HEAD_DIM // 2)
    c32, s32 = _rope_tables(positions, QK_ROPE // 2)
    tabs = (c64, s64, c32, s32)
    for l in range(DEPTH):
        x2 = _layer(x2, tabs, bsz, seq, w_in[l], b_gate[l], rms_cq[l], rms_ckv[l], w_uq[l], w_ukv[l],
                    w_o_a[l], w_o_b[l], w_out[l], ln1_g[l], ln1_b[l], w_router[l], b_router[l],
                    w_up[l], b_up[l], w_down[l], b_down[l], ln2_g[l], ln2_b[l])
    return x2.reshape(bsz, seq, D_MODEL)
```

```python
import functools

import jax
import jax.numpy as jnp
import numpy as np
from jax import lax
from jax.experimental import pallas as pl
from jax.experimental.pallas import tpu as pltpu

D_MODEL = 1024
A_HEADS = 8
A_HEAD_DIM = 64
IDX_HEADS = 8
IDX_DIM = 64
TOPK_MAX = 256
B_HEADS = 8
Q_LORA = 384
KV_LORA = 256
QK_NOPE = 64
QK_ROPE = 32
V_DIM = 64
ROPE_THETA = 10000.0
N_EXPERTS = 32
TOP_K = 4
D_FF = 1024
SWIGLU_LIMIT = 7.0
SWIGLU_ALPHA = 1.702
DEPTH = 1
DN_ALPHA = (2 * DEPTH) ** 0.25
LN_EPS = 1e-5
RMS_EPS = 1e-6
A_WIDTH = A_HEADS * A_HEAD_DIM
B_WIDTH = B_HEADS * V_DIM
SPLITS = (A_WIDTH, A_WIDTH, A_WIDTH, IDX_HEADS * IDX_DIM, IDX_DIM, IDX_HEADS,
          Q_LORA, KV_LORA, QK_ROPE, 2 * D_MODEL)

LANES = 128
VMEM_LIMIT_BYTES = 52 * 1024 * 1024

PROJ_TILE = 512
Q_TILE = 256
SEARCH_GROUPS = 2
EXPERT_TILE = 512
FF_CHUNK = 1024
DISPATCH_TILE = 512
COMBINE_TILE = 256

ROW_TILES = D_MODEL // LANES
assert ROW_TILES == 8

LOG2_E = 1.4426950408889634
INT_MIN = -(2 ** 31)
NEG_BIG = -1e30

_NT = (((1,), (1,)), ((), ()))


def _cparams(sem):
    return pltpu.CompilerParams(dimension_semantics=sem, vmem_limit_bytes=VMEM_LIMIT_BYTES)


def _full(shape):
    nd = len(shape)
    return pl.BlockSpec(shape, lambda *_: (0,) * nd)


def _rope_chunk(x, cos, sin, half):
    lane = lax.broadcasted_iota(jnp.int32, x.shape, 1)
    first = (lane % (2 * half)) < half
    up = pltpu.roll(x, LANES - half, 1)
    dn = pltpu.roll(x, half, 1)
    return x * cos + jnp.where(first, -up, dn) * sin


def _layer_norm(u, g, b):
    mu = jnp.mean(u, axis=-1, keepdims=True)
    d = u - mu
    var = jnp.mean(d * d, axis=-1, keepdims=True)
    return d * lax.rsqrt(var + LN_EPS) * g + b


def _load_token_rows(ref, first, t):
    return jnp.concatenate([ref[pl.ds(first * ROW_TILES + s, t, stride=ROW_TILES), :] for s in range(ROW_TILES)],
                           axis=1)


def _store_token_rows(ref, v, first=0):
    t = v.shape[0]
    for s in range(ROW_TILES):
        ref[pl.ds(first * ROW_TILES + s, t, stride=ROW_TILES), :] = v[:, s * LANES:(s + 1) * LANES]


def _two_head_softmax_pv(logits, vc, width):
    lane_v = lax.broadcasted_iota(jnp.int32, vc.shape, 1)
    pvs = []
    for hh in range(2):
        lg = logits[hh]
        p = jnp.exp2(lg - jnp.max(lg, axis=1, keepdims=True)).astype(jnp.bfloat16)
        v_h = jnp.where((lane_v // width) == hh, vc, jnp.ones_like(vc))
        pvs.append(jnp.dot(p, v_h, preferred_element_type=jnp.float32))
    lane = lax.broadcasted_iota(jnp.int32, pvs[0].shape, 1)
    first = lane < width
    num = jnp.where(first, pvs[0], pvs[1])
    den = jnp.where(first, pltpu.roll(pvs[0], width, 1), pltpu.roll(pvs[1], width, 1))
    return num / den


def _head_mask(shape, width, which):
    lane = lax.broadcasted_iota(jnp.int32, shape, 1)
    return (lane // width) == which


def _proj_kernel(x_ref, c64_ref, s64_ref, c32_ref, s32_ref, w1_ref, ws_ref, wl_ref,
                 gcq_ref, gckv_ref, wuq_ref, wukv_ref,
                 qa_ref, ka_ref, va_ref, qi_ref, ki_ref, kr_ref, wi_ref,
                 qn_ref, qr_ref, kn_ref, vb_ref):
    xb = x_ref[...].astype(jnp.bfloat16)
    c64, s64 = c64_ref[...], s64_ref[...]
    c32, s32 = c32_ref[...], s32_ref[...]
    f32 = jnp.float32
    bf = jnp.bfloat16

    z1 = jnp.dot(xb, w1_ref[...], preferred_element_type=f32)
    n_chunk = A_WIDTH // LANES
    a_scale = A_HEAD_DIM ** -0.5 * LOG2_E
    for c in range(n_chunk):
        sl = slice(c * LANES, (c + 1) * LANES)
        qa_ref[:, sl] = (_rope_chunk(z1[:, c * LANES:(c + 1) * LANES], c64, s64, 32) * a_scale).astype(bf)
        o = A_WIDTH + c * LANES
        ka_ref[:, sl] = _rope_chunk(z1[:, o:o + LANES], c64, s64, 32).astype(bf)
        o = 2 * A_WIDTH + c * LANES
        va_ref[:, sl] = z1[:, o:o + LANES].astype(bf)
        o = 3 * A_WIDTH + c * LANES
        qi_ref[:, sl] = _rope_chunk(z1[:, o:o + LANES], c64, s64, 32).astype(bf)

    zs = jnp.dot(xb, ws_ref[...], preferred_element_type=f32)
    ki_ref[...] = _rope_chunk(zs[:, 0:LANES], c64, s64, 32).astype(bf)
    kr_ref[...] = _rope_chunk(zs[:, LANES:2 * LANES], c32, s32, 16).astype(bf)
    wi_ref[...] = zs[:, 2 * LANES:3 * LANES] * (IDX_HEADS ** -0.5 * IDX_DIM ** -0.5)

    zl = jnp.dot(xb, wl_ref[...], preferred_element_type=f32)
    cq = zl[:, :Q_LORA]
    cq = cq * lax.rsqrt(jnp.mean(cq * cq, axis=-1, keepdims=True) + RMS_EPS) * gcq_ref[...]
    ckv = zl[:, Q_LORA:]
    ckv = ckv * lax.rsqrt(jnp.mean(ckv * ckv, axis=-1, keepdims=True) + RMS_EPS) * gckv_ref[...]
    qb = jnp.dot(cq.astype(bf), wuq_ref[...], preferred_element_type=f32)
    kv = jnp.dot(ckv.astype(bf), wukv_ref[...], preferred_element_type=f32)
    b_scale = (QK_NOPE + QK_ROPE) ** -0.5 * LOG2_E
    qn_ref[...] = (qb[:, :B_HEADS * QK_NOPE] * b_scale).astype(bf)
    for c in range(B_HEADS * QK_ROPE // LANES):
        o = B_HEADS * QK_NOPE + c * LANES
        qr_ref[:, c * LANES:(c + 1) * LANES] = (
            _rope_chunk(qb[:, o:o + LANES], c32, s32, 16) * b_scale).astype(bf)
    kn_ref[...] = kv[:, :B_HEADS * QK_NOPE].astype(bf)
    vb_ref[...] = kv[:, B_HEADS * QK_NOPE:].astype(bf)


def _proj(x2, tabs, w1, ws, wl, gcq, gckv, wuq, wukv):
    n = x2.shape[0]
    t = PROJ_TILE
    tok = lambda w: pl.BlockSpec((t, w), lambda i: (i, 0))
    bf = jnp.bfloat16
    outs = [(A_WIDTH, bf)] * 4 + [(LANES, bf), (LANES, bf), (LANES, jnp.float32),
                                  (B_HEADS * QK_NOPE, bf), (B_HEADS * QK_ROPE, bf),
                                  (B_HEADS * QK_NOPE, bf), (B_WIDTH, bf)]
    return pl.pallas_call(
        _proj_kernel,
        grid=(n // t,),
        in_specs=[tok(D_MODEL)] + [tok(LANES)] * 4 + [_full(a.shape) for a in (w1, ws, wl, gcq, gckv, wuq, wukv)],
        out_specs=[tok(w) for w, _ in outs],
        out_shape=[jax.ShapeDtypeStruct((n, w), d) for w, d in outs],
        compiler_params=_cparams(("arbitrary",)),
        name="proj",
    )(x2, *tabs, w1, ws, wl, gcq, gckv, wuq, wukv)


def _tie_select(eq, gt, need, kv):
    chunk = 256
    r = lax.broadcasted_iota(jnp.int32, (chunk, chunk), 0)
    c = lax.broadcasted_iota(jnp.int32, (chunk, chunk), 1)
    upper = jnp.where(r <= c, 1.0, 0.0).astype(jnp.bfloat16)
    carry = jnp.zeros((eq.shape[0], 1), jnp.float32)
    parts = []
    for j in range(kv // chunk):
        e = eq[:, j * chunk:(j + 1) * chunk]
        ef = jnp.where(e, 1.0, 0.0).astype(jnp.bfloat16)
        pref = jnp.dot(ef, upper, preferred_element_type=jnp.float32) + carry
        parts.append(gt[:, j * chunk:(j + 1) * chunk] | (e & (pref <= need)))
        carry = carry + jnp.sum(ef.astype(jnp.float32), axis=1, keepdims=True)
    return jnp.concatenate(parts, axis=1)


def _key_to_float(key):
    return lax.bitcast_convert_type(key ^ ((key >> 31) & jnp.int32(0x7FFFFFFF)), jnp.float32)


def _select_topk(score, k_eff, sel_ref, kv):
    tq = score.shape[0]
    rows = tq // SEARCH_GROUPS
    parts = [(score[g * rows:(g + 1) * rows], k_eff[g * rows:(g + 1) * rows]) for g in range(SEARCH_GROUPS)]

    def search(it, prefixes):
        bit = lax.shift_left(jnp.int32(1), 31 - it)
        out = []
        for (s, k), prefix in zip(parts, prefixes):
            cand = prefix + bit
            cnt = jnp.sum(jnp.where(s >= _key_to_float(cand), 1.0, 0.0), axis=1, keepdims=True)
            out.append(jnp.where(cnt >= k, cand, prefix))
        return tuple(out)

    init = tuple(jnp.full((rows, 1), INT_MIN, jnp.int32) for _ in range(SEARCH_GROUPS))
    thr = _key_to_float(jnp.concatenate(lax.fori_loop(0, 32, search, init, unroll=4), axis=0))
    ge = score >= thr
    n_ge = jnp.sum(jnp.where(ge, 1.0, 0.0), axis=1, keepdims=True)
    sel_ref[:, :kv] = jnp.where(ge, 0.0, NEG_BIG)
    has_tie = jnp.max(jnp.where(n_ge != k_eff, 1.0, 0.0)) > 0.5

    @pl.when(has_tie)
    def _():
        gt = score > thr
        need = k_eff - jnp.sum(jnp.where(gt, 1.0, 0.0), axis=1, keepdims=True)
        sel_ref[:, :kv] = jnp.where(_tie_select(score == thr, gt, need, kv), 0.0, NEG_BIG)


def _dsa_block(qa_ref, qi_ref, wi_ref, ka_ref, va_ref, ki_ref, o_ref, sel_ref, score_ref, *, kv, k_sel):
    tq = qa_ref.shape[0]
    f32 = jnp.float32
    bf = jnp.bfloat16
    q_pos = (kv - tq) + lax.broadcasted_iota(jnp.int32, (tq, 1), 0)
    key_pos = lax.broadcasted_iota(jnp.int32, (tq, kv), 1)
    causal = key_pos <= q_pos

    if kv <= k_sel:
        sel_ref[:, :kv] = jnp.where(causal, 0.0, NEG_BIG)
    else:
        ki = ki_ref[:kv, :]
        lane = lax.broadcasted_iota(jnp.int32, (tq, LANES), 1)
        score_ref[:, :kv] = jnp.zeros((tq, kv), f32)

        def idx_slab(c, carry):
            qc = qi_ref[:, pl.ds(pl.multiple_of(c * LANES, LANES), LANES)]
            acc = score_ref[:, :kv]
            for hh in range(2):
                qm = jnp.where(_head_mask(qc.shape, IDX_DIM, hh), qc, jnp.zeros_like(qc))
                s = lax.dot_general(qm, ki, _NT, preferred_element_type=f32)
                w = jnp.sum(jnp.where(lane == 2 * c + hh, wi_ref[...], 0.0), axis=1, keepdims=True)
                acc = acc + jnp.maximum(s, 0.0) * w
            score_ref[:, :kv] = acc
            return carry

        lax.fori_loop(0, IDX_HEADS // 2, idx_slab, 0)
        score = jnp.where(causal, score_ref[:, :kv], -jnp.inf)
        k_eff = jnp.minimum(q_pos + 1, k_sel).astype(f32)
        _select_topk(score, k_eff, sel_ref, kv)

    def attn_slab(c, carry):
        sl = pl.ds(pl.multiple_of(c * LANES, LANES), LANES)
        qc = qa_ref[:, sl]
        kc = ka_ref[:kv, sl]
        vc = va_ref[:kv, sl]
        mask = sel_ref[:, :kv]
        logits = []
        for hh in range(2):
            qm = jnp.where(_head_mask(qc.shape, A_HEAD_DIM, hh), qc, jnp.zeros_like(qc))
            logits.append(lax.dot_general(qm, kc, _NT, preferred_element_type=f32) + mask)
        o_ref[:, sl] = _two_head_softmax_pv(logits, vc, A_HEAD_DIM).astype(bf)
        return carry

    lax.fori_loop(0, A_WIDTH // LANES, attn_slab, 0)


def _per_query_block(block_fn, refs, n_blocks, tq):
    j = pl.program_id(1)
    for b in range(n_blocks):
        pl.when(j == b)(functools.partial(block_fn, *refs, kv=(b + 1) * tq))


def _dsa_kernel(*refs, seq, k_sel):
    tq = refs[0].shape[0]
    _per_query_block(functools.partial(_dsa_block, k_sel=k_sel), refs, seq // tq, tq)


def _dsa(qa, qi, wi, ka, va, ki, bsz, seq):
    tq = min(Q_TILE, seq)
    nq = seq // tq
    qspec = lambda w: pl.BlockSpec((tq, w), lambda b, i: (b * nq + i, 0))
    kspec = lambda w: pl.BlockSpec((seq, w), lambda b, i: (b, 0))
    k_sel = min(TOPK_MAX, seq // 4)
    return pl.pallas_call(
        functools.partial(_dsa_kernel, seq=seq, k_sel=k_sel),
        grid=(bsz, nq),
        in_specs=[qspec(A_WIDTH), qspec(A_WIDTH), qspec(LANES), kspec(A_WIDTH), kspec(A_WIDTH), kspec(LANES)],
        out_specs=qspec(A_WIDTH),
        out_shape=jax.ShapeDtypeStruct((bsz * seq, A_WIDTH), jnp.bfloat16),
        scratch_shapes=[pltpu.VMEM((tq, seq), jnp.float32), pltpu.VMEM((tq, seq), jnp.float32)],
        compiler_params=_cparams(("arbitrary", "arbitrary")),
        name="dsa",
    )(qa, qi, wi, ka, va, ki)


def _mla_block(qn_ref, qr_ref, kn_ref, kr_ref, vb_ref, o_ref, *, kv):
    tq = qn_ref.shape[0]
    f32 = jnp.float32
    bf = jnp.bfloat16
    diag = lax.broadcasted_iota(jnp.int32, (tq, tq), 1) <= lax.broadcasted_iota(jnp.int32, (tq, tq), 0)
    kr = kr_ref[:kv, :]

    def causal_masked(lg):
        own = jnp.where(diag, lg[:, kv - tq:], NEG_BIG)
        return own if kv == tq else jnp.concatenate([lg[:, :kv - tq], own], axis=1)

    def slab(c, carry):
        sl = pl.ds(pl.multiple_of(c * LANES, LANES), LANES)
        qc = qn_ref[:, sl]
        kcat = jnp.concatenate([kn_ref[:kv, sl], kr], axis=1)
        vc = vb_ref[:kv, sl]
        qrc = qr_ref[:, pl.ds(pl.multiple_of((c // 2) * LANES, LANES), LANES)]
        logits = []
        for hh in range(2):
            h = 2 * c + hh
            qm = jnp.where(_head_mask(qc.shape, QK_NOPE, hh), qc, jnp.zeros_like(qc))
            qrm = jnp.where(_head_mask(qrc.shape, QK_ROPE, h % 4), qrc, jnp.zeros_like(qrc))
            qcat = jnp.concatenate([qm, qrm], axis=1)
            lg = lax.dot_general(qcat, kcat, _NT, preferred_element_type=f32)
            logits.append(causal_masked(lg))
        o_ref[:, sl] = _two_head_softmax_pv(logits, vc, V_DIM).astype(bf)
        return carry

    lax.fori_loop(0, B_WIDTH // LANES, slab, 0, unroll=2)


def _mla_kernel(*refs, seq):
    tq = refs[0].shape[0]
    _per_query_block(_mla_block, refs, seq // tq, tq)


def _mla(qn, qr, kn, kr, vb, bsz, seq):
    tq = min(Q_TILE, seq)
    nq = seq // tq
    qspec = lambda w: pl.BlockSpec((tq, w), lambda b, i: (b * nq + i, 0))
    kspec = lambda w: pl.BlockSpec((seq, w), lambda b, i: (b, 0))
    return pl.pallas_call(
        functools.partial(_mla_kernel, seq=seq),
        grid=(bsz, nq),
        in_specs=[qspec(B_HEADS * QK_NOPE), qspec(B_HEADS * QK_ROPE), kspec(B_HEADS * QK_NOPE),
                  kspec(LANES), kspec(B_WIDTH)],
        out_specs=qspec(B_WIDTH),
        out_shape=jax.ShapeDtypeStruct((bsz * seq, B_WIDTH), jnp.bfloat16),
        compiler_params=_cparams(("arbitrary", "arbitrary")),
        name="mla",
    )(qn, qr, kn, kr, vb)


def _post_kernel(x_ref, oa_ref, ob_ref, wg_ref, bg_ref, woa_ref, wob_ref, wout_ref, g1_ref, b1_ref,
                 wrh_ref, wrl_ref, br_ref, h_ref, ri_ref, rf_ref, cnt_ref, carry_ref):
    f32 = jnp.float32
    bf = jnp.bfloat16
    t = x_ref.shape[0]

    @pl.when(pl.program_id(0) == 0)
    def _():
        carry_ref[...] = jnp.zeros_like(carry_ref)

    x = x_ref[...]
    gate = jax.nn.sigmoid(jnp.dot(x.astype(bf), wg_ref[...], preferred_element_type=f32) + bg_ref[...])
    pa = jnp.dot(oa_ref[...], woa_ref[...], preferred_element_type=f32)
    pb = jnp.dot(ob_ref[...], wob_ref[...], preferred_element_type=f32)
    mix = gate[:, :D_MODEL] * pa + gate[:, D_MODEL:] * pb
    u = DN_ALPHA * x + jnp.dot(mix.astype(bf), wout_ref[...], preferred_element_type=f32)
    h = _layer_norm(u, g1_ref[...], b1_ref[...])
    _store_token_rows(h_ref, h)

    h_hi = h.astype(bf)
    h_lo = (h - h_hi.astype(f32)).astype(bf)
    logits = (jnp.dot(h_hi, wrh_ref[...], preferred_element_type=f32)
              + jnp.dot(h_lo, wrh_ref[...], preferred_element_type=f32)
              + jnp.dot(h_hi, wrl_ref[...], preferred_element_type=f32) + br_ref[...])
    lane = lax.broadcasted_iota(jnp.int32, (t, LANES), 1).astype(f32)
    lg = jnp.where(lane < N_EXPERTS, logits, -jnp.inf)
    vals, ids = [], []
    assign = jnp.zeros((t, LANES), f32)
    for _k in range(TOP_K):
        m = jnp.max(lg, axis=1, keepdims=True)
        idx = jnp.min(jnp.where(lg == m, lane, float(LANES)), axis=1, keepdims=True)
        hit = lane == idx
        vals.append(m)
        ids.append(idx)
        assign = jnp.where(hit, 1.0, assign)
        lg = jnp.where(hit, -jnp.inf, lg)
    exps = [jnp.exp(v - vals[0]) for v in vals]
    den = exps[0] + exps[1] + exps[2] + exps[3]

    r = lax.broadcasted_iota(jnp.int32, (t, t), 0)
    c = lax.broadcasted_iota(jnp.int32, (t, t), 1)
    lower = jnp.where(c < r, 1.0, 0.0).astype(bf)
    before = jnp.dot(lower, assign.astype(bf), preferred_element_type=f32) + carry_ref[0:1, :]
    ri = jnp.zeros((t, LANES), f32)
    rf = jnp.zeros((t, LANES), f32)
    for k in range(TOP_K):
        rank = jnp.sum(jnp.where(lane == ids[k], before, 0.0), axis=1, keepdims=True)
        ri = jnp.where(lane == float(k), ids[k], ri)
        ri = jnp.where(lane == float(TOP_K + k), rank, ri)
        rf = jnp.where(lane == float(k), exps[k] / den, rf)
    ri_ref[...] = ri.astype(jnp.int32)
    rf_ref[...] = rf
    total = carry_ref[0:1, :] + jnp.sum(assign, axis=0, keepdims=True)
    carry_ref[...] = jnp.broadcast_to(total, carry_ref.shape)
    cnt_ref[...] = jnp.broadcast_to(total, cnt_ref.shape)


def _post(x2, oa, ob, wg, bg, woa, wob, wout, g1, b1, wrh, wrl, br):
    n = x2.shape[0]
    t = PROJ_TILE
    tok = lambda w: pl.BlockSpec((t, w), lambda i: (i, 0))
    consts = (wg, bg, woa, wob, wout, g1, b1, wrh, wrl, br)
    return pl.pallas_call(
        _post_kernel,
        grid=(n // t,),
        in_specs=[tok(D_MODEL), tok(A_WIDTH), tok(B_WIDTH)] + [_full(a.shape) for a in consts],
        out_specs=[pl.BlockSpec((t * ROW_TILES, LANES), lambda i: (i, 0)), tok(LANES), tok(LANES),
                   _full((8, LANES))],
        out_shape=[jax.ShapeDtypeStruct((n * ROW_TILES, LANES), jnp.float32),
                   jax.ShapeDtypeStruct((n, LANES), jnp.int32),
                   jax.ShapeDtypeStruct((n, LANES), jnp.float32),
                   jax.ShapeDtypeStruct((8, LANES), jnp.float32)],
        scratch_shapes=[pltpu.VMEM((8, LANES), jnp.float32)],
        compiler_params=_cparams(("arbitrary",)),
        name="post",
    )(x2, oa, ob, *consts)


def _token_row(ref, r):
    start = r * ROW_TILES if isinstance(r, int) else pl.multiple_of(r * ROW_TILES, ROW_TILES)
    return ref.at[pl.ds(start, ROW_TILES)]


def _row_gather(idx_ref, src_ref, dst_ref, sem, rows):
    for j in range(rows):
        pltpu.make_async_copy(_token_row(src_ref, idx_ref[0, 0, j]), _token_row(dst_ref, j), sem).start(
            priority=j % 2)


def _rows_wait(src_ref, dst_ref, sem):
    pltpu.make_async_copy(src_ref, dst_ref, sem).wait()


def _dispatch_kernel(dest_ref, h_ref, xs_ref, sem):
    t = DISPATCH_TILE
    for k in range(TOP_K):
        for j in range(t):
            pltpu.make_async_copy(_token_row(h_ref, j), _token_row(xs_ref, dest_ref[0, 0, k * t + j]), sem).start(
                priority=j % 2)
    for k in range(TOP_K):
        _rows_wait(h_ref, xs_ref.at[pl.ds(0, t * ROW_TILES)], sem)


def _dispatch(dest3, h, rows):
    t = DISPATCH_TILE
    n = h.shape[0] // ROW_TILES
    return pl.pallas_call(
        _dispatch_kernel,
        grid=(n // t,),
        in_specs=[pl.BlockSpec((1, 1, t * TOP_K), lambda i: (i, 0, 0), memory_space=pltpu.SMEM),
                  pl.BlockSpec((t * ROW_TILES, LANES), lambda i: (i, 0))],
        out_specs=pl.BlockSpec(memory_space=pl.ANY),
        out_shape=jax.ShapeDtypeStruct((rows * ROW_TILES, LANES), jnp.float32),
        scratch_shapes=[pltpu.SemaphoreType.DMA(())],
        compiler_params=_cparams(("arbitrary",)),
        name="dispatch",
    )(dest3, h)


def _expert_kernel(te_ref, tt_ref, lo_ref, hi_ref, ns_ref, xs_ref, wu_ref, bu_ref, wd_ref, bd_ref, ys_ref,
                   wub_ref, wdb_ref):
    i = pl.program_id(0)
    f32 = jnp.float32
    bf = jnp.bfloat16
    t = EXPERT_TILE
    valid = i < ns_ref[0]
    before = jnp.maximum(i - 1, 0)
    fresh = valid & ((i == 0) | (te_ref[i] != te_ref[before]))
    first_visit = (i == 0) | (tt_ref[i] != tt_ref[before])

    @pl.when(fresh)
    def _():
        rows = 128

        def cast_up(j, c):
            r = pl.multiple_of(j * rows, rows)
            wub_ref[pl.ds(r, rows), :] = wu_ref[pl.ds(r, rows), :].astype(bf)
            return c

        lax.fori_loop(0, D_MODEL // rows, cast_up, 0)

        def cast_dn(j, c):
            r = pl.multiple_of(j * rows, rows)
            wdb_ref[pl.ds(r, rows), :] = wd_ref[pl.ds(r, rows), :].astype(bf)
            return c

        lax.fori_loop(0, D_FF // rows, cast_dn, 0)

    def mlp(first, rows):
        xb = _load_token_rows(xs_ref, first, rows).astype(bf)
        y = jnp.broadcast_to(bd_ref[...], (rows, D_MODEL))
        for c in range(D_FF // FF_CHUNK):
            gs = slice(c * FF_CHUNK, (c + 1) * FF_CHUNK)
            ls = slice(D_FF + c * FF_CHUNK, D_FF + (c + 1) * FF_CHUNK)
            ag = jnp.dot(xb, wub_ref[:, gs], preferred_element_type=f32) + bu_ref[:, gs]
            al = jnp.dot(xb, wub_ref[:, ls], preferred_element_type=f32) + bu_ref[:, ls]
            g = jnp.minimum(ag, SWIGLU_LIMIT)
            lin = jnp.clip(al, -SWIGLU_LIMIT, SWIGLU_LIMIT)
            act = (lin + 1.0) * (g * jax.nn.sigmoid(SWIGLU_ALPHA * g))
            y = y + jnp.dot(act.astype(bf), wdb_ref[gs, :], preferred_element_type=f32)
        return y

    lo, hi = lo_ref[i], hi_ref[i]
    half = t // 2
    whole = (lo == 0) & (hi == t)
    partial = valid & jnp.logical_not(whole)

    @pl.when(valid & whole)
    def _():
        _store_token_rows(ys_ref, mlp(0, t))

    @pl.when(partial & first_visit)
    def _():
        ys_ref[...] = jnp.zeros_like(ys_ref)

    def fill(first, rows):
        row = first + lax.broadcasted_iota(jnp.int32, (rows, 1), 0)
        mine = (row >= lo) & (row < hi)
        _store_token_rows(ys_ref, jnp.where(mine, mlp(first, rows), _load_token_rows(ys_ref, first, rows)), first)

    pl.when(partial & (lo < half) & (hi > half))(functools.partial(fill, 0, t))
    pl.when(partial & (hi <= half))(functools.partial(fill, 0, half))
    pl.when(partial & (lo >= half))(functools.partial(fill, half, half))


def _experts(steps, xs, w_up, b_up, w_down, b_down):
    t = EXPERT_TILE
    max_steps = steps[0].shape[0]
    row_block = pl.BlockSpec((t * ROW_TILES, LANES), lambda i, te, tt, lo, hi, ns: (tt[i], 0))
    per_expert = lambda shape: pl.BlockSpec((None,) + shape, lambda i, te, tt, lo, hi, ns: (te[i], 0, 0))
    grid_spec = pltpu.PrefetchScalarGridSpec(
        num_scalar_prefetch=5,
        grid=(max_steps,),
        in_specs=[row_block, per_expert((D_MODEL, 2 * D_FF)), per_expert((1, 2 * D_FF)),
                  per_expert((D_FF, D_MODEL)), per_expert((1, D_MODEL))],
        out_specs=row_block,
        scratch_shapes=[pltpu.VMEM((D_MODEL, 2 * D_FF), jnp.bfloat16),
                        pltpu.VMEM((D_FF, D_MODEL), jnp.bfloat16)],
    )
    return pl.pallas_call(
        _expert_kernel,
        grid_spec=grid_spec,
        out_shape=jax.ShapeDtypeStruct(xs.shape, jnp.float32),
        compiler_params=_cparams(("arbitrary",)),
        name="experts",
    )(*steps, xs, w_up, b_up.reshape(N_EXPERTS, 1, 2 * D_FF), w_down, b_down.reshape(N_EXPERTS, 1, D_MODEL))


def _combine_kernel(cur_ref, nxt_ref, ys_ref, h_ref, rf_ref, g2_ref, b2_ref, o_ref, buf_ref, sems):
    i = pl.program_id(0)
    n_steps = pl.num_programs(0)
    t = COMBINE_TILE
    rows = t * TOP_K
    slot = i % 2

    @pl.when(i == 0)
    def _():
        _row_gather(cur_ref, ys_ref, buf_ref.at[0], sems.at[0], rows)

    _rows_wait(ys_ref.at[pl.ds(0, rows * ROW_TILES)], buf_ref.at[slot], sems.at[slot])
    _row_gather(nxt_ref, ys_ref, buf_ref.at[1 - slot], sems.at[1 - slot], rows)

    rf = rf_ref[...]
    u = DN_ALPHA * _load_token_rows(h_ref, 0, t)
    for k in range(TOP_K):
        u = u + rf[:, k:k + 1] * _load_token_rows(buf_ref.at[slot], k * t, t)
    o_ref[...] = _layer_norm(u, g2_ref[...], b2_ref[...])

    @pl.when(i == n_steps - 1)
    def _():
        _rows_wait(ys_ref.at[pl.ds(0, rows * ROW_TILES)], buf_ref.at[1 - slot], sems.at[1 - slot])


def _combine(dest3, ys, h, rf, g2, b2):
    t = COMBINE_TILE
    n = h.shape[0] // ROW_TILES
    last = n // t - 1
    tok = lambda w: pl.BlockSpec((t, w), lambda i: (i, 0))
    idx_spec = lambda f: pl.BlockSpec((1, 1, t * TOP_K), f, memory_space=pltpu.SMEM)
    return pl.pallas_call(
        _combine_kernel,
        grid=(n // t,),
        in_specs=[idx_spec(lambda i: (i, 0, 0)), idx_spec(lambda i: (jnp.minimum(i + 1, last), 0, 0)),
                  pl.BlockSpec(memory_space=pl.ANY), pl.BlockSpec((t * ROW_TILES, LANES), lambda i: (i, 0)),
                  tok(LANES), _full(g2.shape), _full(b2.shape)],
        out_specs=tok(D_MODEL),
        out_shape=jax.ShapeDtypeStruct((n, D_MODEL), jnp.float32),
        scratch_shapes=[pltpu.VMEM((2, t * TOP_K * ROW_TILES, LANES), jnp.float32),
                        pltpu.SemaphoreType.DMA((2,))],
        compiler_params=_cparams(("arbitrary",)),
        name="combine",
    )(dest3, dest3, ys, h, rf, g2, b2)


def _rope_tables(positions, half):
    inv_freq = ROPE_THETA ** (-jnp.arange(half, dtype=jnp.float32) / half)
    ang = positions.astype(jnp.float32).reshape(-1, 1) * inv_freq
    reps = LANES // half
    return jnp.tile(jnp.cos(ang), (1, reps)), jnp.tile(jnp.sin(ang), (1, reps))


def _lookup(table, idx):
    experts = jnp.arange(N_EXPERTS, dtype=jnp.int32)
    return jnp.sum(jnp.where(idx[..., None] == experts, table, 0), axis=-1)


def _expert_schedule(starts, ends, rows):
    t = EXPERT_TILE
    first_tile = starts // t
    n_tile = jnp.where(ends > starts, (ends - 1) // t - first_tile + 1, 0)
    cum = jnp.cumsum(n_tile)
    n_steps = cum[-1]
    max_steps = rows // t + N_EXPERTS
    s = jnp.minimum(jnp.arange(max_steps, dtype=jnp.int32), n_steps - 1)
    expert = jnp.sum((cum[None, :] <= s[:, None]).astype(jnp.int32), axis=1)
    tile = _lookup(first_tile, expert) + s - _lookup(cum - n_tile, expert)
    lo = jnp.maximum(_lookup(starts, expert) - tile * t, 0)
    hi = jnp.minimum(_lookup(ends, expert) - tile * t, t)
    return expert, tile, lo, hi, n_steps.astype(jnp.int32).reshape(1)


def _layer(x2, tabs, bsz, seq, w_in, b_gate, rms_cq, rms_ckv, w_uq, w_ukv, w_o_a, w_o_b, w_out,
           ln1_g, ln1_b, w_router, b_router, w_up, b_up, w_down, b_down, ln2_g, ln2_b):
    bf = jnp.bfloat16
    n = x2.shape[0]
    off = np.concatenate([[0], np.cumsum(SPLITS)]).tolist()
    col = lambda j: w_in[:, off[j]:off[j + 1]]
    w1 = jnp.concatenate([col(0), col(1), col(2), col(3)], axis=1).astype(bf)
    wi_pad = jnp.zeros((D_MODEL, LANES - IDX_HEADS), w_in.dtype)
    ws = jnp.concatenate([col(4), col(4), jnp.tile(col(8), (1, LANES // QK_ROPE)), col(5), wi_pad], axis=1).astype(bf)
    wl = jnp.concatenate([col(6), col(7)], axis=1).astype(bf)
    wg = col(9).astype(bf)
    uq = w_uq.reshape(Q_LORA, B_HEADS, QK_NOPE + QK_ROPE)
    wuq = jnp.concatenate([uq[:, :, :QK_NOPE].reshape(Q_LORA, -1), uq[:, :, QK_NOPE:].reshape(Q_LORA, -1)],
                          axis=1).astype(bf)
    ukv = w_ukv.reshape(KV_LORA, B_HEADS, QK_NOPE + V_DIM)
    wukv = jnp.concatenate([ukv[:, :, :QK_NOPE].reshape(KV_LORA, -1), ukv[:, :, QK_NOPE:].reshape(KV_LORA, -1)],
                           axis=1).astype(bf)

    qa, ka, va, qi, ki, kr, wi, qn, qr, kn, vb = _proj(
        x2, tabs, w1, ws, wl, rms_cq.reshape(1, -1), rms_ckv.reshape(1, -1), wuq, wukv)
    o_a = _dsa(qa, qi, wi, ka, va, ki, bsz, seq)
    o_b = _mla(qn, qr, kn, kr, vb, bsz, seq)

    wr = jnp.concatenate([w_router, jnp.zeros((D_MODEL, LANES - N_EXPERTS), w_router.dtype)], axis=1)
    br = jnp.concatenate([b_router, jnp.zeros((LANES - N_EXPERTS,), b_router.dtype)]).reshape(1, -1)
    wr_hi = wr.astype(bf)
    wr_lo = (wr - wr_hi.astype(jnp.float32)).astype(bf)
    h, ri, rf, cnt = _post(x2, o_a, o_b, wg, b_gate.reshape(1, -1), w_o_a.astype(bf), w_o_b.astype(bf),
                           w_out.astype(bf), ln1_g.reshape(1, -1), ln1_b.reshape(1, -1), wr_hi, wr_lo, br)

    counts = cnt[0, :N_EXPERTS].astype(jnp.int32)
    ends = jnp.cumsum(counts)
    starts = ends - counts
    dest = _lookup(starts, ri[:, :TOP_K]) + ri[:, TOP_K:2 * TOP_K]
    by_step = lambda t: dest.reshape(n // t, t, TOP_K).transpose(0, 2, 1).reshape(n // t, 1, t * TOP_K)

    xs = _dispatch(by_step(DISPATCH_TILE), h, n * TOP_K)
    ys = _experts(_expert_schedule(starts, ends, n * TOP_K), xs, w_up, b_up, w_down, b_down)
    return _combine(by_step(COMBINE_TILE), ys, h, rf, ln2_g.reshape(1, -1), ln2_b.reshape(1, -1))


def kernel(x, positions, w_in, b_gate, rms_cq, rms_ckv, w_uq, w_ukv, w_o_a, w_o_b, w_out, ln1_g, ln1_b,
           w_router, b_router, w_up, b_up, w_down, b_down, ln2_g, ln2_b):
    bsz, seq, _ = x.shape
    x2 = x.reshape(bsz * seq, D_MODEL)
    c64, s64 = _rope_tables(positions, A_HEAD_DIM // 2)
    c32, s32 = _rope_tables(positions, QK_ROPE // 2)
    tabs = (c64, s64, c32, s32)
    for l in range(DEPTH):
        x2 = _layer(x2, tabs, bsz, seq, w_in[l], b_gate[l], rms_cq[l], rms_ckv[l], w_uq[l], w_ukv[l],
                    w_o_a[l], w_o_b[l], w_out[l], ln1_g[l], ln1_b[l], w_router[l], b_router[l],
                    w_up[l], b_up[l], w_down[l], b_down[l], ln2_g[l], ln2_b[l])
    return x2.reshape(bsz, seq, D_MODEL)
```

```python
import functools

import jax
import jax.numpy as jnp
import numpy as np
from jax import lax
from jax.experimental import pallas as pl
from jax.experimental.pallas import tpu as pltpu

D_MODEL = 1024
A_HEADS = 8
A_HEAD_DIM = 64
IDX_HEADS = 8
IDX_DIM = 64
TOPK_MAX = 256
B_HEADS = 8
Q_LORA = 384
KV_LORA = 256
QK_NOPE = 64
QK_ROPE = 32
V_DIM = 64
ROPE_THETA = 10000.0
N_EXPERTS = 32
TOP_K = 4
D_FF = 1024
SWIGLU_LIMIT = 7.0
SWIGLU_ALPHA = 1.702
DEPTH = 1
DN_ALPHA = (2 * DEPTH) ** 0.25
LN_EPS = 1e-5
RMS_EPS = 1e-6
A_WIDTH = A_HEADS * A_HEAD_DIM
B_WIDTH = B_HEADS * V_DIM
SPLITS = (A_WIDTH, A_WIDTH, A_WIDTH, IDX_HEADS * IDX_DIM, IDX_DIM, IDX_HEADS,
          Q_LORA, KV_LORA, QK_ROPE, 2 * D_MODEL)
LANES = 128
VMEM_LIMIT_BYTES = 52 * 1024 * 1024
PROJ_COLS = 4 * A_WIDTH + 3 * LANES + Q_LORA + KV_LORA

PROJ_TILE = 512
Q_TILE = 256
SEARCH_GROUPS = 2
EXPERT_TILE = 512
FF_CHUNK = 1024
DISPATCH_TILE = 512
COMBINE_TILE = 256

ROW_TILES = D_MODEL // LANES
assert ROW_TILES == 8

LOG2_E = 1.4426950408889634
INT_MIN = -(2 ** 31)
NEG_BIG = -1e30

_NT = (((1,), (1,)), ((), ()))


def _cparams(sem):
    return pltpu.CompilerParams(dimension_semantics=sem, vmem_limit_bytes=VMEM_LIMIT_BYTES)


def _full(shape):
    nd = len(shape)
    return pl.BlockSpec(shape, lambda *_: (0,) * nd)


def _rope_chunk(x, cos, sin, half):
    lane = lax.broadcasted_iota(jnp.int32, x.shape, 1)
    first = (lane % (2 * half)) < half
    up = pltpu.roll(x, LANES - half, 1)
    dn = pltpu.roll(x, half, 1)
    return x * cos + jnp.where(first, -up, dn) * sin


def _layer_norm(u, g, b):
    mu = jnp.mean(u, axis=-1, keepdims=True)
    d = u - mu
    var = jnp.mean(d * d, axis=-1, keepdims=True)
    return d * lax.rsqrt(var + LN_EPS) * g + b


def _load_token_rows(ref, first, t):
    return jnp.concatenate([ref[pl.ds(first * ROW_TILES + s, t, stride=ROW_TILES), :] for s in range(ROW_TILES)],
                           axis=1)


def _store_token_rows(ref, v, first=0):
    t = v.shape[0]
    for s in range(ROW_TILES):
        ref[pl.ds(first * ROW_TILES + s, t, stride=ROW_TILES), :] = v[:, s * LANES:(s + 1) * LANES]


def _two_head_softmax_pv(logits, vc, width):
    lane_v = lax.broadcasted_iota(jnp.int32, vc.shape, 1)
    pvs = []
    for hh in range(2):
        lg = logits[hh]
        p = jnp.exp2(lg - jnp.max(lg, axis=1, keepdims=True)).astype(jnp.bfloat16)
        v_h = jnp.where((lane_v // width) == hh, vc, jnp.ones_like(vc))
        pvs.append(jnp.dot(p, v_h, preferred_element_type=jnp.float32))
    lane = lax.broadcasted_iota(jnp.int32, pvs[0].shape, 1)
    first = lane < width
    num = jnp.where(first, pvs[0], pvs[1])
    den = jnp.where(first, pltpu.roll(pvs[0], width, 1), pltpu.roll(pvs[1], width, 1))
    return num / den


def _head_mask(shape, width, which):
    lane = lax.broadcasted_iota(jnp.int32, shape, 1)
    return (lane // width) == which


def _proj_kernel(x_ref, c64_ref, s64_ref, c32_ref, s32_ref, w_ref,
                 gcq_ref, gckv_ref, wuq_ref, wukv_ref,
                 qa_ref, ka_ref, va_ref, qi_ref, ki_ref, kr_ref, wi_ref,
                 qn_ref, qr_ref, kn_ref, vb_ref):
    xb = x_ref[...].astype(jnp.bfloat16)
    c64, s64 = c64_ref[...], s64_ref[...]
    c32, s32 = c32_ref[...], s32_ref[...]
    f32 = jnp.float32
    bf = jnp.bfloat16

    z1 = jnp.dot(xb, w_ref[:, :4 * A_WIDTH], preferred_element_type=f32)
    n_chunk = A_WIDTH // LANES
    a_scale = A_HEAD_DIM ** -0.5 * LOG2_E
    for c in range(n_chunk):
        sl = slice(c * LANES, (c + 1) * LANES)
        qa_ref[:, sl] = (_rope_chunk(z1[:, c * LANES:(c + 1) * LANES], c64, s64, 32) * a_scale).astype(bf)
        o = A_WIDTH + c * LANES
        ka_ref[:, sl] = _rope_chunk(z1[:, o:o + LANES], c64, s64, 32).astype(bf)
        o = 2 * A_WIDTH + c * LANES
        va_ref[:, sl] = z1[:, o:o + LANES].astype(bf)
        o = 3 * A_WIDTH + c * LANES
        qi_ref[:, sl] = _rope_chunk(z1[:, o:o + LANES], c64, s64, 32).astype(bf)

    small = 4 * A_WIDTH + 3 * LANES
    zs = jnp.dot(xb, w_ref[:, 4 * A_WIDTH:small], preferred_element_type=f32)
    ki_ref[...] = _rope_chunk(zs[:, 0:LANES], c64, s64, 32).astype(bf)
    kr_ref[...] = _rope_chunk(zs[:, LANES:2 * LANES], c32, s32, 16).astype(bf)
    wi_ref[...] = zs[:, 2 * LANES:3 * LANES] * (IDX_HEADS ** -0.5 * IDX_DIM ** -0.5)

    zl = jnp.dot(xb, w_ref[:, small:], preferred_element_type=f32)
    cq = zl[:, :Q_LORA]
    cq = cq * lax.rsqrt(jnp.mean(cq * cq, axis=-1, keepdims=True) + RMS_EPS) * gcq_ref[...]
    ckv = zl[:, Q_LORA:]
    ckv = ckv * lax.rsqrt(jnp.mean(ckv * ckv, axis=-1, keepdims=True) + RMS_EPS) * gckv_ref[...]
    qb = jnp.dot(cq.astype(bf), wuq_ref[...], preferred_element_type=f32)
    kv = jnp.dot(ckv.astype(bf), wukv_ref[...], preferred_element_type=f32)
    b_scale = (QK_NOPE + QK_ROPE) ** -0.5 * LOG2_E
    qn_ref[...] = (qb[:, :B_HEADS * QK_NOPE] * b_scale).astype(bf)
    for c in range(B_HEADS * QK_ROPE // LANES):
        o = B_HEADS * QK_NOPE + c * LANES
        qr_ref[:, c * LANES:(c + 1) * LANES] = (
            _rope_chunk(qb[:, o:o + LANES], c32, s32, 16) * b_scale).astype(bf)
    kn_ref[...] = kv[:, :B_HEADS * QK_NOPE].astype(bf)
    vb_ref[...] = kv[:, B_HEADS * QK_NOPE:].astype(bf)


def _proj(x2, tabs, w_slab, gcq, gckv, wuq, wukv):
    n = x2.shape[0]
    t = PROJ_TILE
    tok = lambda w: pl.BlockSpec((t, w), lambda i: (i, 0))
    bf = jnp.bfloat16
    outs = [(A_WIDTH, bf)] * 4 + [(LANES, bf), (LANES, bf), (LANES, jnp.float32),
                                  (B_HEADS * QK_NOPE, bf), (B_HEADS * QK_ROPE, bf),
                                  (B_HEADS * QK_NOPE, bf), (B_WIDTH, bf)]
    return pl.pallas_call(
        _proj_kernel,
        grid=(n // t,),
        in_specs=[tok(D_MODEL)] + [tok(LANES)] * 4 + [pl.BlockSpec((D_MODEL, PROJ_COLS), lambda i: (0, 1))]
        + [_full(a.shape) for a in (gcq, gckv, wuq, wukv)],
        out_specs=[tok(w) for w, _ in outs],
        out_shape=[jax.ShapeDtypeStruct((n, w), d) for w, d in outs],
        compiler_params=_cparams(("arbitrary",)),
        name="proj",
    )(x2, *tabs, w_slab, gcq, gckv, wuq, wukv)


def _tie_select(eq, gt, need, kv):
    chunk = 256
    r = lax.broadcasted_iota(jnp.int32, (chunk, chunk), 0)
    c = lax.broadcasted_iota(jnp.int32, (chunk, chunk), 1)
    upper = jnp.where(r <= c, 1.0, 0.0).astype(jnp.bfloat16)
    carry = jnp.zeros((eq.shape[0], 1), jnp.float32)
    parts = []
    for j in range(kv // chunk):
        e = eq[:, j * chunk:(j + 1) * chunk]
        ef = jnp.where(e, 1.0, 0.0).astype(jnp.bfloat16)
        pref = jnp.dot(ef, upper, preferred_element_type=jnp.float32) + carry
        parts.append(gt[:, j * chunk:(j + 1) * chunk] | (e & (pref <= need)))
        carry = carry + jnp.sum(ef.astype(jnp.float32), axis=1, keepdims=True)
    return jnp.concatenate(parts, axis=1)


def _key_to_float(key):
    return lax.bitcast_convert_type(key ^ ((key >> 31) & jnp.int32(0x7FFFFFFF)), jnp.float32)


def _select_topk(score, k_eff, sel_ref, kv):
    tq = score.shape[0]
    rows = tq // SEARCH_GROUPS
    parts = [(score[g * rows:(g + 1) * rows], k_eff[g * rows:(g + 1) * rows]) for g in range(SEARCH_GROUPS)]

    def search(it, prefixes):
        bit = lax.shift_left(jnp.int32(1), 31 - it)
        out = []
        for (s, k), prefix in zip(parts, prefixes):
            cand = prefix + bit
            cnt = jnp.sum(jnp.where(s >= _key_to_float(cand), 1.0, 0.0), axis=1, keepdims=True)
            out.append(jnp.where(cnt >= k, cand, prefix))
        return tuple(out)

    init = tuple(jnp.full((rows, 1), INT_MIN, jnp.int32) for _ in range(SEARCH_GROUPS))
    thr = _key_to_float(jnp.concatenate(lax.fori_loop(0, 32, search, init, unroll=4), axis=0))
    ge = score >= thr
    n_ge = jnp.sum(jnp.where(ge, 1.0, 0.0), axis=1, keepdims=True)
    sel_ref[:, :kv] = jnp.where(ge, 0.0, NEG_BIG)
    has_tie = jnp.max(jnp.where(n_ge != k_eff, 1.0, 0.0)) > 0.5

    @pl.when(has_tie)
    def _():
        gt = score > thr
        need = k_eff - jnp.sum(jnp.where(gt, 1.0, 0.0), axis=1, keepdims=True)
        sel_ref[:, :kv] = jnp.where(_tie_select(score == thr, gt, need, kv), 0.0, NEG_BIG)


def _dsa_block(qa_ref, qi_ref, wi_ref, ka_ref, va_ref, ki_ref, o_ref, sel_ref, score_ref, *, kv, k_sel):
    tq = qa_ref.shape[0]
    f32 = jnp.float32
    bf = jnp.bfloat16
    q_pos = (kv - tq) + lax.broadcasted_iota(jnp.int32, (tq, 1), 0)
    key_pos = lax.broadcasted_iota(jnp.int32, (tq, kv), 1)
    causal = key_pos <= q_pos

    if kv <= k_sel:
        sel_ref[:, :kv] = jnp.where(causal, 0.0, NEG_BIG)
    else:
        ki = ki_ref[:kv, :]
        lane = lax.broadcasted_iota(jnp.int32, (tq, LANES), 1)
        score_ref[:, :kv] = jnp.zeros((tq, kv), f32)

        def idx_slab(c, carry):
            qc = qi_ref[:, pl.ds(pl.multiple_of(c * LANES, LANES), LANES)]
            acc = score_ref[:, :kv]
            for hh in range(2):
                qm = jnp.where(_head_mask(qc.shape, IDX_DIM, hh), qc, jnp.zeros_like(qc))
                s = lax.dot_general(qm, ki, _NT, preferred_element_type=f32)
                w = jnp.sum(jnp.where(lane == 2 * c + hh, wi_ref[...], 0.0), axis=1, keepdims=True)
                acc = acc + jnp.maximum(s, 0.0) * w
            score_ref[:, :kv] = acc
            return carry

        lax.fori_loop(0, IDX_HEADS // 2, idx_slab, 0, unroll=2)
        score = jnp.where(causal, score_ref[:, :kv], -jnp.inf)
        k_eff = jnp.minimum(q_pos + 1, k_sel).astype(f32)
        _select_topk(score, k_eff, sel_ref, kv)

    def attn_slab(c, carry):
        sl = pl.ds(pl.multiple_of(c * LANES, LANES), LANES)
        qc = qa_ref[:, sl]
        kc = ka_ref[:kv, sl]
        vc = va_ref[:kv, sl]
        mask = sel_ref[:, :kv]
        logits = []
        for hh in range(2):
            qm = jnp.where(_head_mask(qc.shape, A_HEAD_DIM, hh), qc, jnp.zeros_like(qc))
            logits.append(lax.dot_general(qm, kc, _NT, preferred_element_type=f32) + mask)
        o_ref[:, sl] = _two_head_softmax_pv(logits, vc, A_HEAD_DIM).astype(bf)
        return carry

    lax.fori_loop(0, A_WIDTH // LANES, attn_slab, 0)


def _per_query_block(block_fn, refs, n_blocks, tq):
    j = pl.program_id(1)
    for b in range(n_blocks):
        pl.when(j == b)(functools.partial(block_fn, *refs, kv=(b + 1) * tq))


def _dsa_kernel(*refs, seq, k_sel):
    tq = refs[0].shape[0]
    _per_query_block(functools.partial(_dsa_block, k_sel=k_sel), refs, seq // tq, tq)


def _dsa(qa, qi, wi, ka, va, ki, bsz, seq):
    tq = min(Q_TILE, seq)
    nq = seq // tq
    qspec = lambda w: pl.BlockSpec((tq, w), lambda b, i: (b * nq + i, 0))
    kspec = lambda w: pl.BlockSpec((seq, w), lambda b, i: (b, 0))
    k_sel = min(TOPK_MAX, seq // 4)
    return pl.pallas_call(
        functools.partial(_dsa_kernel, seq=seq, k_sel=k_sel),
        grid=(bsz, nq),
        in_specs=[qspec(A_WIDTH), qspec(A_WIDTH), qspec(LANES), kspec(A_WIDTH), kspec(A_WIDTH), kspec(LANES)],
        out_specs=qspec(A_WIDTH),
        out_shape=jax.ShapeDtypeStruct((bsz * seq, A_WIDTH), jnp.bfloat16),
        scratch_shapes=[pltpu.VMEM((tq, seq), jnp.float32), pltpu.VMEM((tq, seq), jnp.float32)],
        compiler_params=_cparams(("arbitrary", "arbitrary")),
        name="dsa",
    )(qa, qi, wi, ka, va, ki)


def _mla_block(qn_ref, qr_ref, kn_ref, kr_ref, vb_ref, o_ref, *, kv):
    tq = qn_ref.shape[0]
    f32 = jnp.float32
    bf = jnp.bfloat16
    diag = lax.broadcasted_iota(jnp.int32, (tq, tq), 1) <= lax.broadcasted_iota(jnp.int32, (tq, tq), 0)
    kr = kr_ref[:kv, :]

    def causal_masked(lg):
        own = jnp.where(diag, lg[:, kv - tq:], NEG_BIG)
        return own if kv == tq else jnp.concatenate([lg[:, :kv - tq], own], axis=1)

    def slab(c, carry):
        sl = pl.ds(pl.multiple_of(c * LANES, LANES), LANES)
        qc = qn_ref[:, sl]
        kcat = jnp.concatenate([kn_ref[:kv, sl], kr], axis=1)
        vc = vb_ref[:kv, sl]
        qrc = qr_ref[:, pl.ds(pl.multiple_of((c // 2) * LANES, LANES), LANES)]
        logits = []
        for hh in range(2):
            h = 2 * c + hh
            qm = jnp.where(_head_mask(qc.shape, QK_NOPE, hh), qc, jnp.zeros_like(qc))
            qrm = jnp.where(_head_mask(qrc.shape, QK_ROPE, h % 4), qrc, jnp.zeros_like(qrc))
            qcat = jnp.concatenate([qm, qrm], axis=1)
            lg = lax.dot_general(qcat, kcat, _NT, preferred_element_type=f32)
            logits.append(causal_masked(lg))
        o_ref[:, sl] = _two_head_softmax_pv(logits, vc, V_DIM).astype(bf)
        return carry

    lax.fori_loop(0, B_WIDTH // LANES, slab, 0, unroll=2)


def _mla_kernel(*refs, seq):
    tq = refs[0].shape[0]
    _per_query_block(_mla_block, refs, seq // tq, tq)


def _mla(qn, qr, kn, kr, vb, bsz, seq):
    tq = min(Q_TILE, seq)
    nq = seq // tq
    qspec = lambda w: pl.BlockSpec((tq, w), lambda b, i: (b * nq + i, 0))
    kspec = lambda w: pl.BlockSpec((seq, w), lambda b, i: (b, 0))
    return pl.pallas_call(
        functools.partial(_mla_kernel, seq=seq),
        grid=(bsz, nq),
        in_specs=[qspec(B_HEADS * QK_NOPE), qspec(B_HEADS * QK_ROPE), kspec(B_HEADS * QK_NOPE),
                  kspec(LANES), kspec(B_WIDTH)],
        out_specs=qspec(B_WIDTH),
        out_shape=jax.ShapeDtypeStruct((bsz * seq, B_WIDTH), jnp.bfloat16),
        compiler_params=_cparams(("arbitrary", "arbitrary")),
        name="mla",
    )(qn, qr, kn, kr, vb)


def _post_kernel(x_ref, oa_ref, ob_ref, wg_ref, bg_ref, wo_ref, g1_ref, b1_ref,
                 wrh_ref, wrl_ref, br_ref, h_ref, ri_ref, rf_ref, cnt_ref, carry_ref):
    f32 = jnp.float32
    bf = jnp.bfloat16
    t = x_ref.shape[0]

    @pl.when(pl.program_id(0) == 0)
    def _():
        carry_ref[...] = jnp.zeros_like(carry_ref)

    x = x_ref[...]
    gate = jax.nn.sigmoid(jnp.dot(x.astype(bf), wg_ref[...], preferred_element_type=f32) + bg_ref[...])
    pa = jnp.dot(oa_ref[...], wo_ref[:A_WIDTH, :], preferred_element_type=f32)
    pb = jnp.dot(ob_ref[...], wo_ref[A_WIDTH:A_WIDTH + B_WIDTH, :], preferred_element_type=f32)
    mix = gate[:, :D_MODEL] * pa + gate[:, D_MODEL:] * pb
    u = DN_ALPHA * x + jnp.dot(mix.astype(bf), wo_ref[A_WIDTH + B_WIDTH:, :], preferred_element_type=f32)
    h = _layer_norm(u, g1_ref[...], b1_ref[...])
    _store_token_rows(h_ref, h)

    h_hi = h.astype(bf)
    h_lo = (h - h_hi.astype(f32)).astype(bf)
    logits = (jnp.dot(h_hi, wrh_ref[...], preferred_element_type=f32)
              + jnp.dot(h_lo, wrh_ref[...], preferred_element_type=f32)
              + jnp.dot(h_hi, wrl_ref[...], preferred_element_type=f32) + br_ref[...])
    lane = lax.broadcasted_iota(jnp.int32, (t, LANES), 1).astype(f32)
    lg = jnp.where(lane < N_EXPERTS, logits, -jnp.inf)
    vals, ids = [], []
    assign = jnp.zeros((t, LANES), f32)
    for _k in range(TOP_K):
        m = jnp.max(lg, axis=1, keepdims=True)
        idx = jnp.min(jnp.where(lg == m, lane, float(LANES)), axis=1, keepdims=True)
        hit = lane == idx
        vals.append(m)
        ids.append(idx)
        assign = jnp.where(hit, 1.0, assign)
        lg = jnp.where(hit, -jnp.inf, lg)
    exps = [jnp.exp(v - vals[0]) for v in vals]
    den = exps[0] + exps[1] + exps[2] + exps[3]

    r = lax.broadcasted_iota(jnp.int32, (t, t), 0)
    c = lax.broadcasted_iota(jnp.int32, (t, t), 1)
    lower = jnp.where(c < r, 1.0, 0.0).astype(bf)
    before = jnp.dot(lower, assign.astype(bf), preferred_element_type=f32) + carry_ref[0:1, :]
    ri = jnp.zeros((t, LANES), f32)
    rf = jnp.zeros((t, LANES), f32)
    for k in range(TOP_K):
        rank = jnp.sum(jnp.where(lane == ids[k], before, 0.0), axis=1, keepdims=True)
        ri = jnp.where(lane == float(k), ids[k], ri)
        ri = jnp.where(lane == float(TOP_K + k), rank, ri)
        rf = jnp.where(lane == float(k), exps[k] / den, rf)
    ri_ref[...] = ri.astype(jnp.int32)
    rf_ref[...] = rf
    total = carry_ref[0:1, :] + jnp.sum(assign, axis=0, keepdims=True)
    carry_ref[...] = jnp.broadcast_to(total, carry_ref.shape)
    cnt_ref[...] = jnp.broadcast_to(total, cnt_ref.shape)


def _post(x2, oa, ob, w_slab, bg, wo, g1, b1, wrh, wrl, br):
    n = x2.shape[0]
    t = PROJ_TILE
    tok = lambda w: pl.BlockSpec((t, w), lambda i: (i, 0))
    consts = (bg, wo, g1, b1, wrh, wrl, br)
    gates_w = pl.BlockSpec((D_MODEL, 2 * D_MODEL), lambda i: (0, 0))
    return pl.pallas_call(
        _post_kernel,
        grid=(n // t,),
        in_specs=[tok(D_MODEL), tok(A_WIDTH), tok(B_WIDTH), gates_w] + [_full(a.shape) for a in consts],
        out_specs=[pl.BlockSpec((t * ROW_TILES, LANES), lambda i: (i, 0)), tok(LANES), tok(LANES),
                   _full((8, LANES))],
        out_shape=[jax.ShapeDtypeStruct((n * ROW_TILES, LANES), jnp.float32),
                   jax.ShapeDtypeStruct((n, LANES), jnp.int32),
                   jax.ShapeDtypeStruct((n, LANES), jnp.float32),
                   jax.ShapeDtypeStruct((8, LANES), jnp.float32)],
        scratch_shapes=[pltpu.VMEM((8, LANES), jnp.float32)],
        compiler_params=_cparams(("arbitrary",)),
        name="post",
    )(x2, oa, ob, w_slab, *consts)


def _token_row(ref, r):
    start = r * ROW_TILES if isinstance(r, int) else pl.multiple_of(r * ROW_TILES, ROW_TILES)
    return ref.at[pl.ds(start, ROW_TILES)]


def _row_gather(idx_ref, src_ref, dst_ref, sem, rows):
    for j in range(rows):
        pltpu.make_async_copy(_token_row(src_ref, idx_ref[0, 0, j]), _token_row(dst_ref, j), sem).start(
            priority=j % 2)


def _rows_wait(src_ref, dst_ref, sem):
    pltpu.make_async_copy(src_ref, dst_ref, sem).wait()


def _dispatch_kernel(dest_ref, h_ref, xs_ref, sem):
    t = DISPATCH_TILE
    for k in range(TOP_K):
        for j in range(t):
            pltpu.make_async_copy(_token_row(h_ref, j), _token_row(xs_ref, dest_ref[0, 0, k * t + j]), sem).start(
                priority=j % 2)
    for k in range(TOP_K):
        _rows_wait(h_ref, xs_ref.at[pl.ds(0, t * ROW_TILES)], sem)


def _dispatch(dest3, h, rows):
    t = DISPATCH_TILE
    n = h.shape[0] // ROW_TILES
    return pl.pallas_call(
        _dispatch_kernel,
        grid=(n // t,),
        in_specs=[pl.BlockSpec((1, 1, t * TOP_K), lambda i: (i, 0, 0), memory_space=pltpu.SMEM),
                  pl.BlockSpec((t * ROW_TILES, LANES), lambda i: (i, 0))],
        out_specs=pl.BlockSpec(memory_space=pl.ANY),
        out_shape=jax.ShapeDtypeStruct((rows * ROW_TILES, LANES), jnp.float32),
        scratch_shapes=[pltpu.SemaphoreType.DMA(())],
        compiler_params=_cparams(("arbitrary",)),
        name="dispatch",
    )(dest3, h)


def _expert_kernel(te_ref, tt_ref, lo_ref, hi_ref, ns_ref, xs_ref, wu_ref, bu_ref, wd_ref, bd_ref, ys_ref,
                   wub_ref, wdb_ref):
    i = pl.program_id(0)
    f32 = jnp.float32
    bf = jnp.bfloat16
    t = EXPERT_TILE
    valid = i < ns_ref[0]
    before = jnp.maximum(i - 1, 0)
    fresh = valid & ((i == 0) | (te_ref[i] != te_ref[before]))
    first_visit = (i == 0) | (tt_ref[i] != tt_ref[before])

    @pl.when(fresh)
    def _():
        rows = 128

        def cast_up(j, c):
            r = pl.multiple_of(j * rows, rows)
            wub_ref[pl.ds(r, rows), :] = wu_ref[pl.ds(r, rows), :].astype(bf)
            return c

        lax.fori_loop(0, D_MODEL // rows, cast_up, 0)

        def cast_dn(j, c):
            r = pl.multiple_of(j * rows, rows)
            wdb_ref[pl.ds(r, rows), :] = wd_ref[pl.ds(r, rows), :].astype(bf)
            return c

        lax.fori_loop(0, D_FF // rows, cast_dn, 0)

    def mlp(first, rows):
        xb = _load_token_rows(xs_ref, first, rows).astype(bf)
        y = jnp.broadcast_to(bd_ref[...], (rows, D_MODEL))
        for c in range(D_FF // FF_CHUNK):
            gs = slice(c * FF_CHUNK, (c + 1) * FF_CHUNK)
            ls = slice(D_FF + c * FF_CHUNK, D_FF + (c + 1) * FF_CHUNK)
            ag = jnp.dot(xb, wub_ref[:, gs], preferred_element_type=f32) + bu_ref[:, gs]
            al = jnp.dot(xb, wub_ref[:, ls], preferred_element_type=f32) + bu_ref[:, ls]
            g = jnp.minimum(ag, SWIGLU_LIMIT)
            lin = jnp.clip(al, -SWIGLU_LIMIT, SWIGLU_LIMIT)
            act = (lin + 1.0) * (g * jax.nn.sigmoid(SWIGLU_ALPHA * g))
            y = y + jnp.dot(act.astype(bf), wdb_ref[gs, :], preferred_element_type=f32)
        return y

    lo, hi = lo_ref[i], hi_ref[i]
    half = t // 2
    whole = (lo == 0) & (hi == t)
    partial = valid & jnp.logical_not(whole)

    @pl.when(valid & whole)
    def _():
        _store_token_rows(ys_ref, mlp(0, t))

    @pl.when(partial & first_visit)
    def _():
        ys_ref[...] = jnp.zeros_like(ys_ref)

    def fill(first, rows):
        row = first + lax.broadcasted_iota(jnp.int32, (rows, 1), 0)
        mine = (row >= lo) & (row < hi)
        _store_token_rows(ys_ref, jnp.where(mine, mlp(first, rows), _load_token_rows(ys_ref, first, rows)), first)

    pl.when(partial & (lo < half) & (hi > half))(functools.partial(fill, 0, t))
    pl.when(partial & (hi <= half))(functools.partial(fill, 0, half))
    pl.when(partial & (lo >= half))(functools.partial(fill, half, half))


def _experts(steps, xs, w_up, b_up, w_down, b_down):
    t = EXPERT_TILE
    max_steps = steps[0].shape[0]
    row_block = pl.BlockSpec((t * ROW_TILES, LANES), lambda i, te, tt, lo, hi, ns: (tt[i], 0))
    per_expert = lambda shape: pl.BlockSpec((None,) + shape, lambda i, te, tt, lo, hi, ns: (te[i], 0, 0))
    grid_spec = pltpu.PrefetchScalarGridSpec(
        num_scalar_prefetch=5,
        grid=(max_steps,),
        in_specs=[row_block, per_expert((D_MODEL, 2 * D_FF)), per_expert((1, 2 * D_FF)),
                  per_expert((D_FF, D_MODEL)), per_expert((1, D_MODEL))],
        out_specs=row_block,
        scratch_shapes=[pltpu.VMEM((D_MODEL, 2 * D_FF), jnp.bfloat16),
                        pltpu.VMEM((D_FF, D_MODEL), jnp.bfloat16)],
    )
    return pl.pallas_call(
        _expert_kernel,
        grid_spec=grid_spec,
        out_shape=jax.ShapeDtypeStruct(xs.shape, jnp.float32),
        compiler_params=_cparams(("arbitrary",)),
        name="experts",
    )(*steps, xs, w_up, b_up.reshape(N_EXPERTS, 1, 2 * D_FF), w_down, b_down.reshape(N_EXPERTS, 1, D_MODEL))


def _combine_kernel(cur_ref, nxt_ref, ys_ref, h_ref, rf_ref, g2_ref, b2_ref, o_ref, buf_ref, sems):
    i = pl.program_id(0)
    n_steps = pl.num_programs(0)
    t = COMBINE_TILE
    rows = t * TOP_K
    slot = i % 2

    @pl.when(i == 0)
    def _():
        _row_gather(cur_ref, ys_ref, buf_ref.at[0], sems.at[0], rows)

    _rows_wait(ys_ref.at[pl.ds(0, rows * ROW_TILES)], buf_ref.at[slot], sems.at[slot])
    _row_gather(nxt_ref, ys_ref, buf_ref.at[1 - slot], sems.at[1 - slot], rows)

    rf = rf_ref[...]
    u = DN_ALPHA * _load_token_rows(h_ref, 0, t)
    for k in range(TOP_K):
        u = u + rf[:, k:k + 1] * _load_token_rows(buf_ref.at[slot], k * t, t)
    o_ref[...] = _layer_norm(u, g2_ref[...], b2_ref[...])

    @pl.when(i == n_steps - 1)
    def _():
        _rows_wait(ys_ref.at[pl.ds(0, rows * ROW_TILES)], buf_ref.at[1 - slot], sems.at[1 - slot])


def _combine(dest3, ys, h, rf, g2, b2):
    t = COMBINE_TILE
    n = h.shape[0] // ROW_TILES
    last = n // t - 1
    tok = lambda w: pl.BlockSpec((t, w), lambda i: (i, 0))
    idx_spec = lambda f: pl.BlockSpec((1, 1, t * TOP_K), f, memory_space=pltpu.SMEM)
    return pl.pallas_call(
        _combine_kernel,
        grid=(n // t,),
        in_specs=[idx_spec(lambda i: (i, 0, 0)), idx_spec(lambda i: (jnp.minimum(i + 1, last), 0, 0)),
                  pl.BlockSpec(memory_space=pl.ANY), pl.BlockSpec((t * ROW_TILES, LANES), lambda i: (i, 0)),
                  tok(LANES), _full(g2.shape), _full(b2.shape)],
        out_specs=tok(D_MODEL),
        out_shape=jax.ShapeDtypeStruct((n, D_MODEL), jnp.float32),
        scratch_shapes=[pltpu.VMEM((2, t * TOP_K * ROW_TILES, LANES), jnp.float32),
                        pltpu.SemaphoreType.DMA((2,))],
        compiler_params=_cparams(("arbitrary",)),
        name="combine",
    )(dest3, dest3, ys, h, rf, g2, b2)


def _rope_tables(positions, half):
    inv_freq = ROPE_THETA ** (-jnp.arange(half, dtype=jnp.float32) / half)
    ang = positions.astype(jnp.float32).reshape(-1, 1) * inv_freq
    reps = LANES // half
    return jnp.tile(jnp.cos(ang), (1, reps)), jnp.tile(jnp.sin(ang), (1, reps))


def _lookup(table, idx):
    experts = jnp.arange(N_EXPERTS, dtype=jnp.int32)
    return jnp.sum(jnp.where(idx[..., None] == experts, table, 0), axis=-1)


def _expert_schedule(starts, ends, rows):
    t = EXPERT_TILE
    first_tile = starts // t
    n_tile = jnp.where(ends > starts, (ends - 1) // t - first_tile + 1, 0)
    cum = jnp.cumsum(n_tile)
    n_steps = cum[-1]
    max_steps = rows // t + N_EXPERTS
    s = jnp.minimum(jnp.arange(max_steps, dtype=jnp.int32), n_steps - 1)
    expert = jnp.sum((cum[None, :] <= s[:, None]).astype(jnp.int32), axis=1)
    tile = _lookup(first_tile, expert) + s - _lookup(cum - n_tile, expert)
    lo = jnp.maximum(_lookup(starts, expert) - tile * t, 0)
    hi = jnp.minimum(_lookup(ends, expert) - tile * t, t)
    return expert, tile, lo, hi, n_steps.astype(jnp.int32).reshape(1)


def _layer(x2, tabs, bsz, seq, w_in, b_gate, rms_cq, rms_ckv, w_uq, w_ukv, w_o_a, w_o_b, w_out,
           ln1_g, ln1_b, w_router, b_router, w_up, b_up, w_down, b_down, ln2_g, ln2_b):
    bf = jnp.bfloat16
    n = x2.shape[0]
    off = np.concatenate([[0], np.cumsum(SPLITS)]).tolist()
    col = lambda j: w_in[:, off[j]:off[j + 1]]
    zeros = lambda w: jnp.zeros((D_MODEL, w), w_in.dtype)
    w_slab = jnp.concatenate(
        [col(9), zeros(PROJ_COLS - 2 * D_MODEL), col(0), col(1), col(2), col(3),
         col(4), col(4), jnp.tile(col(8), (1, LANES // QK_ROPE)), col(5), zeros(LANES - IDX_HEADS),
         col(6), col(7)], axis=1).astype(bf)
    w_o = jnp.concatenate([w_o_a, w_o_b, w_out], axis=0).astype(bf)
    uq = w_uq.reshape(Q_LORA, B_HEADS, QK_NOPE + QK_ROPE)
    wuq = jnp.concatenate([uq[:, :, :QK_NOPE].reshape(Q_LORA, -1), uq[:, :, QK_NOPE:].reshape(Q_LORA, -1)],
                          axis=1).astype(bf)
    ukv = w_ukv.reshape(KV_LORA, B_HEADS, QK_NOPE + V_DIM)
    wukv = jnp.concatenate([ukv[:, :, :QK_NOPE].reshape(KV_LORA, -1), ukv[:, :, QK_NOPE:].reshape(KV_LORA, -1)],
                           axis=1).astype(bf)

    qa, ka, va, qi, ki, kr, wi, qn, qr, kn, vb = _proj(
        x2, tabs, w_slab, rms_cq.reshape(1, -1), rms_ckv.reshape(1, -1), wuq, wukv)
    o_a = _dsa(qa, qi, wi, ka, va, ki, bsz, seq)
    o_b = _mla(qn, qr, kn, kr, vb, bsz, seq)

    wr = jnp.concatenate([w_router, jnp.zeros((D_MODEL, LANES - N_EXPERTS), w_router.dtype)], axis=1)
    br = jnp.concatenate([b_router, jnp.zeros((LANES - N_EXPERTS,), b_router.dtype)]).reshape(1, -1)
    wr_hi = wr.astype(bf)
    wr_lo = (wr - wr_hi.astype(jnp.float32)).astype(bf)
    h, ri, rf, cnt = _post(x2, o_a, o_b, w_slab, b_gate.reshape(1, -1), w_o, ln1_g.reshape(1, -1),
                           ln1_b.reshape(1, -1), wr_hi, wr_lo, br)

    counts = cnt[0, :N_EXPERTS].astype(jnp.int32)
    ends = jnp.cumsum(counts)
    starts = ends - counts
    dest = _lookup(starts, ri[:, :TOP_K]) + ri[:, TOP_K:2 * TOP_K]
    by_step = lambda t: dest.reshape(n // t, t, TOP_K).transpose(0, 2, 1).reshape(n // t, 1, t * TOP_K)

    xs = _dispatch(by_step(DISPATCH_TILE), h, n * TOP_K)
    ys = _experts(_expert_schedule(starts, ends, n * TOP_K), xs, w_up, b_up, w_down, b_down)
    return _combine(by_step(COMBINE_TILE), ys, h, rf, ln2_g.reshape(1, -1), ln2_b.reshape(1, -1))


def kernel(x, positions, w_in, b_gate, rms_cq, rms_ckv, w_uq, w_ukv, w_o_a, w_o_b, w_out, ln1_g, ln1_b,
           w_router, b_router, w_up, b_up, w_down, b_down, ln2_g, ln2_b):
    bsz, seq, _ = x.shape
    x2 = x.reshape(bsz * seq, D_MODEL)
    c64, s64 = _rope_tables(positions, A_HEAD_DIM // 2)
    c32, s32 = _rope_tables(positions, QK_ROPE // 2)
    tabs = (c64, s64, c32, s32)
    for l in range(DEPTH):
        x2 = _layer(x2, tabs, bsz, seq, w_in[l], b_gate[l], rms_cq[l], rms_ckv[l], w_uq[l], w_ukv[l],
                    w_o_a[l], w_o_b[l], w_out[l], ln1_g[l], ln1_b[l], w_router[l], b_router[l],
                    w_up[l], b_up[l], w_down[l], b_down[l], ln2_g[l], ln2_b[l])
    return x2.reshape(bsz, seq, D_MODEL)
```

```python
import functools

import jax
import jax.numpy as jnp
import numpy as np
from jax import lax
from jax.experimental import pallas as pl
from jax.experimental.pallas import tpu as pltpu

D_MODEL = 1024
A_HEADS = 8
A_HEAD_DIM = 64
IDX_HEADS = 8
IDX_DIM = 64
TOPK_MAX = 256
B_HEADS = 8
Q_LORA = 384
KV_LORA = 256
QK_NOPE = 64
QK_ROPE = 32
V_DIM = 64
ROPE_THETA = 10000.0
N_EXPERTS = 32
TOP_K = 4
D_FF = 1024
SWIGLU_LIMIT = 7.0
SWIGLU_ALPHA = 1.702
DEPTH = 1
DN_ALPHA = (2 * DEPTH) ** 0.25
LN_EPS = 1e-5
RMS_EPS = 1e-6
A_WIDTH = A_HEADS * A_HEAD_DIM
B_WIDTH = B_HEADS * V_DIM
SPLITS = (A_WIDTH, A_WIDTH, A_WIDTH, IDX_HEADS * IDX_DIM, IDX_DIM, IDX_HEADS,
          Q_LORA, KV_LORA, QK_ROPE, 2 * D_MODEL)
LANES = 128
VMEM_LIMIT_BYTES = 52 * 1024 * 1024
PROJ_COLS = 4 * A_WIDTH + 3 * LANES + Q_LORA + KV_LORA

PROJ_TILE = 512
Q_TILE = 256
SEARCH_GROUPS = 2
EXPERT_TILE = 512
FF_CHUNK = 1024
DISPATCH_TILE = 512
COMBINE_TILE = 256

ROW_TILES = D_MODEL // LANES
assert ROW_TILES == 8

LOG2_E = 1.4426950408889634
INT_MIN = -(2 ** 31)
NEG_BIG = -1e30

_NT = (((1,), (1,)), ((), ()))


def _cparams(sem):
    return pltpu.CompilerParams(dimension_semantics=sem, vmem_limit_bytes=VMEM_LIMIT_BYTES)


def _full(shape):
    nd = len(shape)
    return pl.BlockSpec(shape, lambda *_: (0,) * nd)


def _rope_chunk(x, cos, sin, half):
    lane = lax.broadcasted_iota(jnp.int32, x.shape, 1)
    first = (lane % (2 * half)) < half
    up = pltpu.roll(x, LANES - half, 1)
    dn = pltpu.roll(x, half, 1)
    return x * cos + jnp.where(first, -up, dn) * sin


def _layer_norm(u, g, b):
    mu = jnp.mean(u, axis=-1, keepdims=True)
    d = u - mu
    var = jnp.mean(d * d, axis=-1, keepdims=True)
    return d * lax.rsqrt(var + LN_EPS) * g + b


def _load_token_rows(ref, first, t):
    return jnp.concatenate([ref[pl.ds(first * ROW_TILES + s, t, stride=ROW_TILES), :] for s in range(ROW_TILES)],
                           axis=1)


def _store_token_rows(ref, v, first=0):
    t = v.shape[0]
    for s in range(ROW_TILES):
        ref[pl.ds(first * ROW_TILES + s, t, stride=ROW_TILES), :] = v[:, s * LANES:(s + 1) * LANES]


def _two_head_softmax_pv(logits, vc, width):
    lane_v = lax.broadcasted_iota(jnp.int32, vc.shape, 1)
    pvs = []
    for hh in range(2):
        lg = logits[hh]
        p = jnp.exp2(lg - jnp.max(lg, axis=1, keepdims=True)).astype(jnp.bfloat16)
        v_h = jnp.where((lane_v // width) == hh, vc, jnp.ones_like(vc))
        pvs.append(jnp.dot(p, v_h, preferred_element_type=jnp.float32))
    lane = lax.broadcasted_iota(jnp.int32, pvs[0].shape, 1)
    first = lane < width
    num = jnp.where(first, pvs[0], pvs[1])
    den = jnp.where(first, pltpu.roll(pvs[0], width, 1), pltpu.roll(pvs[1], width, 1))
    return num / den


def _head_mask(shape, width, which):
    lane = lax.broadcasted_iota(jnp.int32, shape, 1)
    return (lane // width) == which


def _proj_kernel(x_ref, c64_ref, s64_ref, c32_ref, s32_ref, w_ref,
                 gcq_ref, gckv_ref, wuq_ref, wukv_ref,
                 qa_ref, ka_ref, va_ref, qi_ref, ki_ref, kr_ref, wi_ref,
                 qn_ref, qr_ref, kn_ref, vb_ref):
    xb = x_ref[...].astype(jnp.bfloat16)
    c64, s64 = c64_ref[...], s64_ref[...]
    c32, s32 = c32_ref[...], s32_ref[...]
    f32 = jnp.float32
    bf = jnp.bfloat16

    z1 = jnp.dot(xb, w_ref[:, :4 * A_WIDTH], preferred_element_type=f32)
    n_chunk = A_WIDTH // LANES
    a_scale = A_HEAD_DIM ** -0.5 * LOG2_E
    for c in range(n_chunk):
        sl = slice(c * LANES, (c + 1) * LANES)
        qa_ref[:, sl] = (_rope_chunk(z1[:, c * LANES:(c + 1) * LANES], c64, s64, 32) * a_scale).astype(bf)
        o = A_WIDTH + c * LANES
        ka_ref[:, sl] = _rope_chunk(z1[:, o:o + LANES], c64, s64, 32).astype(bf)
        o = 2 * A_WIDTH + c * LANES
        va_ref[:, sl] = z1[:, o:o + LANES].astype(bf)
        o = 3 * A_WIDTH + c * LANES
        qi_ref[:, sl] = _rope_chunk(z1[:, o:o + LANES], c64, s64, 32).astype(bf)

    small = 4 * A_WIDTH + 3 * LANES
    zs = jnp.dot(xb, w_ref[:, 4 * A_WIDTH:small], preferred_element_type=f32)
    ki_ref[...] = _rope_chunk(zs[:, 0:LANES], c64, s64, 32).astype(bf)
    kr_ref[...] = _rope_chunk(zs[:, LANES:2 * LANES], c32, s32, 16).astype(bf)
    wi_ref[...] = zs[:, 2 * LANES:3 * LANES] * (IDX_HEADS ** -0.5 * IDX_DIM ** -0.5)

    zl = jnp.dot(xb, w_ref[:, small:], preferred_element_type=f32)
    cq = zl[:, :Q_LORA]
    cq = cq * lax.rsqrt(jnp.mean(cq * cq, axis=-1, keepdims=True) + RMS_EPS) * gcq_ref[...]
    ckv = zl[:, Q_LORA:]
    ckv = ckv * lax.rsqrt(jnp.mean(ckv * ckv, axis=-1, keepdims=True) + RMS_EPS) * gckv_ref[...]
    qb = jnp.dot(cq.astype(bf), wuq_ref[...], preferred_element_type=f32)
    kv = jnp.dot(ckv.astype(bf), wukv_ref[...], preferred_element_type=f32)
    b_scale = (QK_NOPE + QK_ROPE) ** -0.5 * LOG2_E
    qn_ref[...] = (qb[:, :B_HEADS * QK_NOPE] * b_scale).astype(bf)
    for c in range(B_HEADS * QK_ROPE // LANES):
        o = B_HEADS * QK_NOPE + c * LANES
        qr_ref[:, c * LANES:(c + 1) * LANES] = (
            _rope_chunk(qb[:, o:o + LANES], c32, s32, 16) * b_scale).astype(bf)
    kn_ref[...] = kv[:, :B_HEADS * QK_NOPE].astype(bf)
    vb_ref[...] = kv[:, B_HEADS * QK_NOPE:].astype(bf)


def _proj(x2, tabs, w_slab, gcq, gckv, wuq, wukv):
    n = x2.shape[0]
    t = PROJ_TILE
    tok = lambda w: pl.BlockSpec((t, w), lambda i: (i, 0))
    bf = jnp.bfloat16
    outs = [(A_WIDTH, bf)] * 4 + [(LANES, bf), (LANES, bf), (LANES, jnp.float32),
                                  (B_HEADS * QK_NOPE, bf), (B_HEADS * QK_ROPE, bf),
                                  (B_HEADS * QK_NOPE, bf), (B_WIDTH, bf)]
    return pl.pallas_call(
        _proj_kernel,
        grid=(n // t,),
        in_specs=[tok(D_MODEL)] + [tok(LANES)] * 4 + [pl.BlockSpec((D_MODEL, PROJ_COLS), lambda i: (0, 1))]
        + [_full(a.shape) for a in (gcq, gckv, wuq, wukv)],
        out_specs=[tok(w) for w, _ in outs],
        out_shape=[jax.ShapeDtypeStruct((n, w), d) for w, d in outs],
        compiler_params=_cparams(("arbitrary",)),
        name="proj",
    )(x2, *tabs, w_slab, gcq, gckv, wuq, wukv)


def _tie_select(eq, gt, need, kv):
    chunk = 256
    r = lax.broadcasted_iota(jnp.int32, (chunk, chunk), 0)
    c = lax.broadcasted_iota(jnp.int32, (chunk, chunk), 1)
    upper = jnp.where(r <= c, 1.0, 0.0).astype(jnp.bfloat16)
    carry = jnp.zeros((eq.shape[0], 1), jnp.float32)
    parts = []
    for j in range(kv // chunk):
        e = eq[:, j * chunk:(j + 1) * chunk]
        ef = jnp.where(e, 1.0, 0.0).astype(jnp.bfloat16)
        pref = jnp.dot(ef, upper, preferred_element_type=jnp.float32) + carry
        parts.append(gt[:, j * chunk:(j + 1) * chunk] | (e & (pref <= need)))
        carry = carry + jnp.sum(ef.astype(jnp.float32), axis=1, keepdims=True)
    return jnp.concatenate(parts, axis=1)


def _key_to_float(key):
    return lax.bitcast_convert_type(key ^ ((key >> 31) & jnp.int32(0x7FFFFFFF)), jnp.float32)


def _select_topk(score, k_eff, sel_ref, kv):
    tq = score.shape[0]
    rows = tq // SEARCH_GROUPS
    parts = [(score[g * rows:(g + 1) * rows], k_eff[g * rows:(g + 1) * rows]) for g in range(SEARCH_GROUPS)]

    def search(it, prefixes):
        bit = lax.shift_left(jnp.int32(1), 31 - it)
        out = []
        for (s, k), prefix in zip(parts, prefixes):
            cand = prefix + bit
            cnt = jnp.sum(jnp.where(s >= _key_to_float(cand), 1.0, 0.0), axis=1, keepdims=True)
            out.append(jnp.where(cnt >= k, cand, prefix))
        return tuple(out)

    init = tuple(jnp.full((rows, 1), INT_MIN, jnp.int32) for _ in range(SEARCH_GROUPS))
    thr = _key_to_float(jnp.concatenate(lax.fori_loop(0, 32, search, init, unroll=4), axis=0))
    ge = score >= thr
    n_ge = jnp.sum(jnp.where(ge, 1.0, 0.0), axis=1, keepdims=True)
    sel_ref[:, :kv] = jnp.where(ge, 0.0, NEG_BIG)
    has_tie = jnp.max(jnp.where(n_ge != k_eff, 1.0, 0.0)) > 0.5

    @pl.when(has_tie)
    def _():
        gt = score > thr
        need = k_eff - jnp.sum(jnp.where(gt, 1.0, 0.0), axis=1, keepdims=True)
        sel_ref[:, :kv] = jnp.where(_tie_select(score == thr, gt, need, kv), 0.0, NEG_BIG)


def _dsa_block(qa_ref, qi_ref, wi_ref, ka_ref, va_ref, ki_ref, o_ref, sel_ref, score_ref, *, kv, k_sel):
    tq = qa_ref.shape[0]
    f32 = jnp.float32
    bf = jnp.bfloat16
    q_pos = (kv - tq) + lax.broadcasted_iota(jnp.int32, (tq, 1), 0)
    key_pos = lax.broadcasted_iota(jnp.int32, (tq, kv), 1)
    causal = key_pos <= q_pos

    if kv <= k_sel:
        sel_ref[:, :kv] = jnp.where(causal, 0.0, NEG_BIG)
    else:
        ki = ki_ref[:kv, :]
        lane = lax.broadcasted_iota(jnp.int32, (tq, LANES), 1)
        score_ref[:, :kv] = jnp.zeros((tq, kv), f32)

        def idx_slab(c, carry):
            qc = qi_ref[:, pl.ds(pl.multiple_of(c * LANES, LANES), LANES)]
            acc = score_ref[:, :kv]
            for hh in range(2):
                qm = jnp.where(_head_mask(qc.shape, IDX_DIM, hh), qc, jnp.zeros_like(qc))
                s = lax.dot_general(qm, ki, _NT, preferred_element_type=f32)
                w = jnp.sum(jnp.where(lane == 2 * c + hh, wi_ref[...], 0.0), axis=1, keepdims=True)
                acc = acc + jnp.maximum(s, 0.0) * w
            score_ref[:, :kv] = acc
            return carry

        lax.fori_loop(0, IDX_HEADS // 2, idx_slab, 0, unroll=2)
        score = jnp.where(causal, score_ref[:, :kv], -jnp.inf)
        k_eff = jnp.minimum(q_pos + 1, k_sel).astype(f32)
        _select_topk(score, k_eff, sel_ref, kv)

    def attn_slab(c, carry):
        sl = pl.ds(pl.multiple_of(c * LANES, LANES), LANES)
        qc = qa_ref[:, sl]
        kc = ka_ref[:kv, sl]
        vc = va_ref[:kv, sl]
        mask = sel_ref[:, :kv]
        logits = []
        for hh in range(2):
            qm = jnp.where(_head_mask(qc.shape, A_HEAD_DIM, hh), qc, jnp.zeros_like(qc))
            logits.append(lax.dot_general(qm, kc, _NT, preferred_element_type=f32) + mask)
        o_ref[:, sl] = _two_head_softmax_pv(logits, vc, A_HEAD_DIM).astype(bf)
        return carry

    lax.fori_loop(0, A_WIDTH // LANES, attn_slab, 0)


def _per_query_block(block_fn, refs, n_blocks, tq):
    j = pl.program_id(1)
    for b in range(n_blocks):
        pl.when(j == b)(functools.partial(block_fn, *refs, kv=(b + 1) * tq))


def _dsa_kernel(*refs, seq, k_sel):
    tq = refs[0].shape[0]
    _per_query_block(functools.partial(_dsa_block, k_sel=k_sel), refs, seq // tq, tq)


def _dsa(qa, qi, wi, ka, va, ki, bsz, seq):
    tq = min(Q_TILE, seq)
    nq = seq // tq
    qspec = lambda w: pl.BlockSpec((tq, w), lambda b, i: (b * nq + i, 0))
    kspec = lambda w: pl.BlockSpec((seq, w), lambda b, i: (b, 0))
    k_sel = min(TOPK_MAX, seq // 4)
    return pl.pallas_call(
        functools.partial(_dsa_kernel, seq=seq, k_sel=k_sel),
        grid=(bsz, nq),
        in_specs=[qspec(A_WIDTH), qspec(A_WIDTH), qspec(LANES), kspec(A_WIDTH), kspec(A_WIDTH), kspec(LANES)],
        out_specs=qspec(A_WIDTH),
        out_shape=jax.ShapeDtypeStruct((bsz * seq, A_WIDTH), jnp.bfloat16),
        scratch_shapes=[pltpu.VMEM((tq, seq), jnp.float32), pltpu.VMEM((tq, seq), jnp.float32)],
        compiler_params=_cparams(("arbitrary", "arbitrary")),
        name="dsa",
    )(qa, qi, wi, ka, va, ki)


def _mla_block(qn_ref, qr_ref, kn_ref, kr_ref, vb_ref, o_ref, *, kv):
    tq = qn_ref.shape[0]
    f32 = jnp.float32
    bf = jnp.bfloat16
    diag = lax.broadcasted_iota(jnp.int32, (tq, tq), 1) <= lax.broadcasted_iota(jnp.int32, (tq, tq), 0)
    kr = kr_ref[:kv, :]

    def causal_masked(lg):
        own = jnp.where(diag, lg[:, kv - tq:], NEG_BIG)
        return own if kv == tq else jnp.concatenate([lg[:, :kv - tq], own], axis=1)

    def slab(c, carry):
        sl = pl.ds(pl.multiple_of(c * LANES, LANES), LANES)
        qc = qn_ref[:, sl]
        kcat = jnp.concatenate([kn_ref[:kv, sl], kr], axis=1)
        vc = vb_ref[:kv, sl]
        qrc = qr_ref[:, pl.ds(pl.multiple_of((c // 2) * LANES, LANES), LANES)]
        logits = []
        for hh in range(2):
            h = 2 * c + hh
            qm = jnp.where(_head_mask(qc.shape, QK_NOPE, hh), qc, jnp.zeros_like(qc))
            qrm = jnp.where(_head_mask(qrc.shape, QK_ROPE, h % 4), qrc, jnp.zeros_like(qrc))
            qcat = jnp.concatenate([qm, qrm], axis=1)
            lg = lax.dot_general(qcat, kcat, _NT, preferred_element_type=f32)
            logits.append(causal_masked(lg))
        o_ref[:, sl] = _two_head_softmax_pv(logits, vc, V_DIM).astype(bf)
        return carry

    lax.fori_loop(0, B_WIDTH // LANES, slab, 0, unroll=2)


def _mla_kernel(*refs, seq):
    tq = refs[0].shape[0]
    _per_query_block(_mla_block, refs, seq // tq, tq)


def _mla(qn, qr, kn, kr, vb, bsz, seq):
    tq = min(Q_TILE, seq)
    nq = seq // tq
    qspec = lambda w: pl.BlockSpec((tq, w), lambda b, i: (b * nq + i, 0))
    kspec = lambda w: pl.BlockSpec((seq, w), lambda b, i: (b, 0))
    return pl.pallas_call(
        functools.partial(_mla_kernel, seq=seq),
        grid=(bsz, nq),
        in_specs=[qspec(B_HEADS * QK_NOPE), qspec(B_HEADS * QK_ROPE), kspec(B_HEADS * QK_NOPE),
                  kspec(LANES), kspec(B_WIDTH)],
        out_specs=qspec(B_WIDTH),
        out_shape=jax.ShapeDtypeStruct((bsz * seq, B_WIDTH), jnp.bfloat16),
        compiler_params=_cparams(("arbitrary", "arbitrary")),
        name="mla",
    )(qn, qr, kn, kr, vb)


def _post_kernel(x_ref, oa_ref, ob_ref, wg_ref, bg_ref, wo_ref, g1_ref, b1_ref,
                 wrh_ref, wrl_ref, br_ref, h_ref, ri_ref, rf_ref, cnt_ref, carry_ref):
    f32 = jnp.float32
    bf = jnp.bfloat16
    t = x_ref.shape[0]

    @pl.when(pl.program_id(0) == 0)
    def _():
        carry_ref[...] = jnp.zeros_like(carry_ref)

    x = x_ref[...]
    gate = jax.nn.sigmoid(jnp.dot(x.astype(bf), wg_ref[...], preferred_element_type=f32) + bg_ref[...])
    pa = jnp.dot(oa_ref[...], wo_ref[:A_WIDTH, :], preferred_element_type=f32)
    pb = jnp.dot(ob_ref[...], wo_ref[A_WIDTH:A_WIDTH + B_WIDTH, :], preferred_element_type=f32)
    mix = gate[:, :D_MODEL] * pa + gate[:, D_MODEL:] * pb
    u = DN_ALPHA * x + jnp.dot(mix.astype(bf), wo_ref[A_WIDTH + B_WIDTH:, :], preferred_element_type=f32)
    h = _layer_norm(u, g1_ref[...], b1_ref[...])
    _store_token_rows(h_ref, h)

    h_hi = h.astype(bf)
    h_lo = (h - h_hi.astype(f32)).astype(bf)
    logits = (jnp.dot(h_hi, wrh_ref[...], preferred_element_type=f32)
              + jnp.dot(h_lo, wrh_ref[...], preferred_element_type=f32)
              + jnp.dot(h_hi, wrl_ref[...], preferred_element_type=f32) + br_ref[...])
    lane = lax.broadcasted_iota(jnp.int32, (t, LANES), 1).astype(f32)
    lg = jnp.where(lane < N_EXPERTS, logits, -jnp.inf)
    vals, ids = [], []
    assign = jnp.zeros((t, LANES), f32)
    for _k in range(TOP_K):
        m = jnp.max(lg, axis=1, keepdims=True)
        idx = jnp.min(jnp.where(lg == m, lane, float(LANES)), axis=1, keepdims=True)
        hit = lane == idx
        vals.append(m)
        ids.append(idx)
        assign = jnp.where(hit, 1.0, assign)
        lg = jnp.where(hit, -jnp.inf, lg)
    exps = [jnp.exp(v - vals[0]) for v in vals]
    den = exps[0] + exps[1] + exps[2] + exps[3]

    r = lax.broadcasted_iota(jnp.int32, (t, t), 0)
    c = lax.broadcasted_iota(jnp.int32, (t, t), 1)
    lower = jnp.where(c < r, 1.0, 0.0).astype(bf)
    before = jnp.dot(lower, assign.astype(bf), preferred_element_type=f32) + carry_ref[0:1, :]
    ri = jnp.zeros((t, LANES), f32)
    rf = jnp.zeros((t, LANES), f32)
    for k in range(TOP_K):
        rank = jnp.sum(jnp.where(lane == ids[k], before, 0.0), axis=1, keepdims=True)
        ri = jnp.where(lane == float(k), ids[k], ri)
        ri = jnp.where(lane == float(TOP_K + k), rank, ri)
        rf = jnp.where(lane == float(k), exps[k] / den, rf)
    ri_ref[...] = ri.astype(jnp.int32)
    rf_ref[...] = rf
    total = carry_ref[0:1, :] + jnp.sum(assign, axis=0, keepdims=True)
    carry_ref[...] = jnp.broadcast_to(total, carry_ref.shape)
    cnt_ref[...] = jnp.broadcast_to(total, cnt_ref.shape)


def _post(x2, oa, ob, w_slab, bg, wo, g1, b1, wrh, wrl, br):
    n = x2.shape[0]
    t = PROJ_TILE
    tok = lambda w: pl.BlockSpec((t, w), lambda i: (i, 0))
    consts = (bg, wo, g1, b1, wrh, wrl, br)
    gates_w = pl.BlockSpec((D_MODEL, 2 * D_MODEL), lambda i: (0, 0))
    return pl.pallas_call(
        _post_kernel,
        grid=(n // t,),
        in_specs=[tok(D_MODEL), tok(A_WIDTH), tok(B_WIDTH), gates_w] + [_full(a.shape) for a in consts],
        out_specs=[pl.BlockSpec((t * ROW_TILES, LANES), lambda i: (i, 0)), tok(LANES), tok(LANES),
                   _full((8, LANES))],
        out_shape=[jax.ShapeDtypeStruct((n * ROW_TILES, LANES), jnp.float32),
                   jax.ShapeDtypeStruct((n, LANES), jnp.int32),
                   jax.ShapeDtypeStruct((n, LANES), jnp.float32),
                   jax.ShapeDtypeStruct((8, LANES), jnp.float32)],
        scratch_shapes=[pltpu.VMEM((8, LANES), jnp.float32)],
        compiler_params=_cparams(("arbitrary",)),
        name="post",
    )(x2, oa, ob, w_slab, *consts)


def _token_row(ref, r):
    start = r * ROW_TILES if isinstance(r, int) else pl.multiple_of(r * ROW_TILES, ROW_TILES)
    return ref.at[pl.ds(start, ROW_TILES)]


def _row_gather(idx_ref, src_ref, dst_ref, sem, rows):
    for j in range(rows):
        pltpu.make_async_copy(_token_row(src_ref, idx_ref[0, 0, j]), _token_row(dst_ref, j), sem).start(
            priority=j % 2)


def _rows_wait(src_ref, dst_ref, sem):
    pltpu.make_async_copy(src_ref, dst_ref, sem).wait()


def _dispatch_kernel(dest_ref, h_ref, xs_ref, sem):
    t = DISPATCH_TILE
    for k in range(TOP_K):
        for j in range(t):
            pltpu.make_async_copy(_token_row(h_ref, j), _token_row(xs_ref, dest_ref[0, 0, k * t + j]), sem).start(
                priority=j % 2)
    for k in range(TOP_K):
        _rows_wait(h_ref, xs_ref.at[pl.ds(0, t * ROW_TILES)], sem)


def _dispatch(dest3, h, rows):
    t = DISPATCH_TILE
    n = h.shape[0] // ROW_TILES
    return pl.pallas_call(
        _dispatch_kernel,
        grid=(n // t,),
        in_specs=[pl.BlockSpec((1, 1, t * TOP_K), lambda i: (i, 0, 0), memory_space=pltpu.SMEM),
                  pl.BlockSpec((t * ROW_TILES, LANES), lambda i: (i, 0))],
        out_specs=pl.BlockSpec(memory_space=pl.ANY),
        out_shape=jax.ShapeDtypeStruct((rows * ROW_TILES, LANES), jnp.float32),
        scratch_shapes=[pltpu.SemaphoreType.DMA(())],
        compiler_params=_cparams(("arbitrary",)),
        name="dispatch",
    )(dest3, h)


def _expert_kernel(te_ref, tt_ref, lo_ref, hi_ref, ns_ref, xs_ref, wu_ref, bu_ref, wd_ref, bd_ref, ys_ref,
                   wub_ref, wdb_ref):
    i = pl.program_id(0)
    f32 = jnp.float32
    bf = jnp.bfloat16
    t = EXPERT_TILE
    valid = i < ns_ref[0]
    before = jnp.maximum(i - 1, 0)
    fresh = valid & ((i == 0) | (te_ref[i] != te_ref[before]))
    first_visit = (i == 0) | (tt_ref[i] != tt_ref[before])

    @pl.when(fresh)
    def _():
        rows = 128

        def cast_up(j, c):
            r = pl.multiple_of(j * rows, rows)
            wub_ref[pl.ds(r, rows), :] = wu_ref[pl.ds(r, rows), :].astype(bf)
            return c

        lax.fori_loop(0, D_MODEL // rows, cast_up, 0)

        def cast_dn(j, c):
            r = pl.multiple_of(j * rows, rows)
            wdb_ref[pl.ds(r, rows), :] = wd_ref[pl.ds(r, rows), :].astype(bf)
            return c

        lax.fori_loop(0, D_FF // rows, cast_dn, 0)

    def mlp(first, rows):
        xb = _load_token_rows(xs_ref, first, rows).astype(bf)
        y = jnp.broadcast_to(bd_ref[...], (rows, D_MODEL))
        for c in range(D_FF // FF_CHUNK):
            gs = slice(c * FF_CHUNK, (c + 1) * FF_CHUNK)
            ls = slice(D_FF + c * FF_CHUNK, D_FF + (c + 1) * FF_CHUNK)
            ag = jnp.dot(xb, wub_ref[:, gs], preferred_element_type=f32) + bu_ref[:, gs]
            al = jnp.dot(xb, wub_ref[:, ls], preferred_element_type=f32) + bu_ref[:, ls]
            g = jnp.minimum(ag, SWIGLU_LIMIT)
            lin = jnp.clip(al, -SWIGLU_LIMIT, SWIGLU_LIMIT)
            act = (lin + 1.0) * (g * jax.nn.sigmoid(SWIGLU_ALPHA * g))
            y = y + jnp.dot(act.astype(bf), wdb_ref[gs, :], preferred_element_type=f32)
        return y

    lo, hi = lo_ref[i], hi_ref[i]
    half = t // 2
    whole = (lo == 0) & (hi == t)
    partial = valid & jnp.logical_not(whole)

    @pl.when(valid & whole)
    def _():
        _store_token_rows(ys_ref, mlp(0, t))

    @pl.when(partial & first_visit)
    def _():
        ys_ref[...] = jnp.zeros_like(ys_ref)

    def fill(first, rows):
        row = first + lax.broadcasted_iota(jnp.int32, (rows, 1), 0)
        mine = (row >= lo) & (row < hi)
        _store_token_rows(ys_ref, jnp.where(mine, mlp(first, rows), _load_token_rows(ys_ref, first, rows)), first)

    pl.when(partial & (lo < half) & (hi > half))(functools.partial(fill, 0, t))
    pl.when(partial & (hi <= half))(functools.partial(fill, 0, half))
    pl.when(partial & (lo >= half))(functools.partial(fill, half, half))


def _experts(steps, xs, w_up, b_up, w_down, b_down):
    t = EXPERT_TILE
    max_steps = steps[0].shape[0]
    row_block = pl.BlockSpec((t * ROW_TILES, LANES), lambda i, te, tt, lo, hi, ns: (tt[i], 0))
    per_expert = lambda shape: pl.BlockSpec((None,) + shape, lambda i, te, tt, lo, hi, ns: (te[i], 0, 0))
    grid_spec = pltpu.PrefetchScalarGridSpec(
        num_scalar_prefetch=5,
        grid=(max_steps,),
        in_specs=[row_block, per_expert((D_MODEL, 2 * D_FF)), per_expert((1, 2 * D_FF)),
                  per_expert((D_FF, D_MODEL)), per_expert((1, D_MODEL))],
        out_specs=row_block,
        scratch_shapes=[pltpu.VMEM((D_MODEL, 2 * D_FF), jnp.bfloat16),
                        pltpu.VMEM((D_FF, D_MODEL), jnp.bfloat16)],
    )
    return pl.pallas_call(
        _expert_kernel,
        grid_spec=grid_spec,
        out_shape=jax.ShapeDtypeStruct(xs.shape, jnp.float32),
        compiler_params=_cparams(("arbitrary",)),
        name="experts",
    )(*steps, xs, w_up, b_up.reshape(N_EXPERTS, 1, 2 * D_FF), w_down, b_down.reshape(N_EXPERTS, 1, D_MODEL))


def _combine_kernel(cur_ref, nxt_ref, ys_ref, h_ref, rf_ref, g2_ref, b2_ref, o_ref, buf_ref, sems):
    i = pl.program_id(0)
    n_steps = pl.num_programs(0)
    t = COMBINE_TILE
    rows = t * TOP_K
    slot = i % 2

    @pl.when(i == 0)
    def _():
        _row_gather(cur_ref, ys_ref, buf_ref.at[0], sems.at[0], rows)

    _rows_wait(ys_ref.at[pl.ds(0, rows * ROW_TILES)], buf_ref.at[slot], sems.at[slot])
    _row_gather(nxt_ref, ys_ref, buf_ref.at[1 - slot], sems.at[1 - slot], rows)

    rf = rf_ref[...]
    u = DN_ALPHA * _load_token_rows(h_ref, 0, t)
    for k in range(TOP_K):
        u = u + rf[:, k:k + 1] * _load_token_rows(buf_ref.at[slot], k * t, t)
    o_ref[...] = _layer_norm(u, g2_ref[...], b2_ref[...])

    @pl.when(i == n_steps - 1)
    def _():
        _rows_wait(ys_ref.at[pl.ds(0, rows * ROW_TILES)], buf_ref.at[1 - slot], sems.at[1 - slot])


def _combine(dest3, ys, h, rf, g2, b2):
    t = COMBINE_TILE
    n = h.shape[0] // ROW_TILES
    last = n // t - 1
    tok = lambda w: pl.BlockSpec((t, w), lambda i: (i, 0))
    idx_spec = lambda f: pl.BlockSpec((1, 1, t * TOP_K), f, memory_space=pltpu.SMEM)
    return pl.pallas_call(
        _combine_kernel,
        grid=(n // t,),
        in_specs=[idx_spec(lambda i: (i, 0, 0)), idx_spec(lambda i: (jnp.minimum(i + 1, last), 0, 0)),
                  pl.BlockSpec(memory_space=pl.ANY), pl.BlockSpec((t * ROW_TILES, LANES), lambda i: (i, 0)),
                  tok(LANES), _full(g2.shape), _full(b2.shape)],
        out_specs=tok(D_MODEL),
        out_shape=jax.ShapeDtypeStruct((n, D_MODEL), jnp.float32),
        scratch_shapes=[pltpu.VMEM((2, t * TOP_K * ROW_TILES, LANES), jnp.float32),
                        pltpu.SemaphoreType.DMA((2,))],
        compiler_params=_cparams(("arbitrary",)),
        name="combine",
    )(dest3, dest3, ys, h, rf, g2, b2)


def _rope_tables(positions, half):
    inv_freq = ROPE_THETA ** (-jnp.arange(half, dtype=jnp.float32) / half)
    ang = positions.astype(jnp.float32).reshape(-1, 1) * inv_freq
    reps = LANES // half
    return jnp.tile(jnp.cos(ang), (1, reps)), jnp.tile(jnp.sin(ang), (1, reps))


def _lookup(table, idx):
    experts = jnp.arange(N_EXPERTS, dtype=jnp.int32)
    return jnp.sum(jnp.where(idx[..., None] == experts, table, 0), axis=-1)


def _expert_schedule(starts, ends, rows):
    t = EXPERT_TILE
    first_tile = starts // t
    n_tile = jnp.where(ends > starts, (ends - 1) // t - first_tile + 1, 0)
    cum = jnp.cumsum(n_tile)
    n_steps = cum[-1]
    max_steps = rows // t + N_EXPERTS
    s = jnp.minimum(jnp.arange(max_steps, dtype=jnp.int32), n_steps - 1)
    expert = jnp.sum((cum[None, :] <= s[:, None]).astype(jnp.int32), axis=1)
    tile = _lookup(first_tile, expert) + s - _lookup(cum - n_tile, expert)
    lo = jnp.maximum(_lookup(starts, expert) - tile * t, 0)
    hi = jnp.minimum(_lookup(ends, expert) - tile * t, t)
    return expert, tile, lo, hi, n_steps.astype(jnp.int32).reshape(1)


def _layer(x2, tabs, bsz, seq, w_in, b_gate, rms_cq, rms_ckv, w_uq, w_ukv, w_o_a, w_o_b, w_out,
           ln1_g, ln1_b, w_router, b_router, w_up, b_up, w_down, b_down, ln2_g, ln2_b):
    bf = jnp.bfloat16
    n = x2.shape[0]
    off = np.concatenate([[0], np.cumsum(SPLITS)]).tolist()
    w_in_b = w_in.astype(bf)
    col = lambda j: w_in_b[:, off[j]:off[j + 1]]
    zeros = lambda w: jnp.zeros((D_MODEL, w), bf)
    w_slab = jnp.concatenate(
        [col(9), zeros(PROJ_COLS - 2 * D_MODEL), w_in_b[:, :off[4]],
         col(4), col(4), jnp.tile(col(8), (1, LANES // QK_ROPE)), col(5), zeros(LANES - IDX_HEADS),
         col(6), col(7)], axis=1)
    w_o = jnp.concatenate([w_o_a, w_o_b, w_out], axis=0).astype(bf)
    uq = w_uq.reshape(Q_LORA, B_HEADS, QK_NOPE + QK_ROPE)
    wuq = jnp.concatenate([uq[:, :, :QK_NOPE].reshape(Q_LORA, -1), uq[:, :, QK_NOPE:].reshape(Q_LORA, -1)],
                          axis=1).astype(bf)
    ukv = w_ukv.reshape(KV_LORA, B_HEADS, QK_NOPE + V_DIM)
    wukv = jnp.concatenate([ukv[:, :, :QK_NOPE].reshape(KV_LORA, -1), ukv[:, :, QK_NOPE:].reshape(KV_LORA, -1)],
                           axis=1).astype(bf)

    qa, ka, va, qi, ki, kr, wi, qn, qr, kn, vb = _proj(
        x2, tabs, w_slab, rms_cq.reshape(1, -1), rms_ckv.reshape(1, -1), wuq, wukv)
    o_a = _dsa(qa, qi, wi, ka, va, ki, bsz, seq)
    o_b = _mla(qn, qr, kn, kr, vb, bsz, seq)

    wr = jnp.concatenate([w_router, jnp.zeros((D_MODEL, LANES - N_EXPERTS), w_router.dtype)], axis=1)
    br = jnp.concatenate([b_router, jnp.zeros((LANES - N_EXPERTS,), b_router.dtype)]).reshape(1, -1)
    wr_hi = wr.astype(bf)
    wr_lo = (wr - wr_hi.astype(jnp.float32)).astype(bf)
    h, ri, rf, cnt = _post(x2, o_a, o_b, w_slab, b_gate.reshape(1, -1), w_o, ln1_g.reshape(1, -1),
                           ln1_b.reshape(1, -1), wr_hi, wr_lo, br)

    counts = cnt[0, :N_EXPERTS].astype(jnp.int32)
    ends = jnp.cumsum(counts)
    starts = ends - counts
    dest = _lookup(starts, ri[:, :TOP_K]) + ri[:, TOP_K:2 * TOP_K]
    by_step = lambda t: dest.reshape(n // t, t, TOP_K).transpose(0, 2, 1).reshape(n // t, 1, t * TOP_K)

    xs = _dispatch(by_step(DISPATCH_TILE), h, n * TOP_K)
    ys = _experts(_expert_schedule(starts, ends, n * TOP_K), xs, w_up, b_up, w_down, b_down)
    return _combine(by_step(COMBINE_TILE), ys, h, rf, ln2_g.reshape(1, -1), ln2_b.reshape(1, -1))


def kernel(x, positions, w_in, b_gate, rms_cq, rms_ckv, w_uq, w_ukv, w_o_a, w_o_b, w_out, ln1_g, ln1_b,
           w_router, b_router, w_up, b_up, w_down, b_down, ln2_g, ln2_b):
    bsz, seq, _ = x.shape
    x2 = x.reshape(bsz * seq, D_MODEL)
    c64, s64 = _rope_tables(positions, A_HEAD_DIM // 2)
    c32, s32 = _rope_tables(positions, QK_ROPE // 2)
    tabs = (c64, s64, c32, s32)
    for l in range(DEPTH):
        x2 = _layer(x2, tabs, bsz, seq, w_in[l], b_gate[l], rms_cq[l], rms_ckv[l], w_uq[l], w_ukv[l],
                    w_o_a[l], w_o_b[l], w_out[l], ln1_g[l], ln1_b[l], w_router[l], b_router[l],
                    w_up[l], b_up[l], w_down[l], b_down[l], ln2_g[l], ln2_b[l])
    return x2.reshape(bsz, seq, D_MODEL)
```

```python
import functools

import jax
import jax.numpy as jnp
import numpy as np
from jax import lax
from jax.experimental import pallas as pl
from jax.experimental.pallas import tpu as pltpu

D_MODEL = 1024
A_HEADS = 8
A_HEAD_DIM = 64
IDX_HEADS = 8
IDX_DIM = 64
TOPK_MAX = 256
B_HEADS = 8
Q_LORA = 384
KV_LORA = 256
QK_NOPE = 64
QK_ROPE = 32
V_DIM = 64
ROPE_THETA = 10000.0
N_EXPERTS = 32
TOP_K = 4
D_FF = 1024
SWIGLU_LIMIT = 7.0
SWIGLU_ALPHA = 1.702
DEPTH = 1
DN_ALPHA = (2 * DEPTH) ** 0.25
LN_EPS = 1e-5
RMS_EPS = 1e-6
A_WIDTH = A_HEADS * A_HEAD_DIM
B_WIDTH = B_HEADS * V_DIM
SPLITS = (A_WIDTH, A_WIDTH, A_WIDTH, IDX_HEADS * IDX_DIM, IDX_DIM, IDX_HEADS,
          Q_LORA, KV_LORA, QK_ROPE, 2 * D_MODEL)
LANES = 128
VMEM_LIMIT_BYTES = 52 * 1024 * 1024
PROJ_COLS = 4 * A_WIDTH + 3 * LANES + Q_LORA + KV_LORA

PROJ_TILE = 512
Q_TILE = 256
SEARCH_GROUPS = 2
EXPERT_TILE = 512
FF_CHUNK = 1024
DISPATCH_TILE = 512
COMBINE_TILE = 256

ROW_TILES = D_MODEL // LANES
assert ROW_TILES == 8

LOG2_E = 1.4426950408889634
INT_MIN = -(2 ** 31)
NEG_BIG = -1e30

_NT = (((1,), (1,)), ((), ()))


def _cparams(sem):
    return pltpu.CompilerParams(dimension_semantics=sem, vmem_limit_bytes=VMEM_LIMIT_BYTES)


def _full(shape):
    nd = len(shape)
    return pl.BlockSpec(shape, lambda *_: (0,) * nd)


def _rope_chunk(x, cos, sin, half):
    lane = lax.broadcasted_iota(jnp.int32, x.shape, 1)
    first = (lane % (2 * half)) < half
    up = pltpu.roll(x, LANES - half, 1)
    dn = pltpu.roll(x, half, 1)
    return x * cos + jnp.where(first, -up, dn) * sin


def _layer_norm(u, g, b):
    mu = jnp.mean(u, axis=-1, keepdims=True)
    d = u - mu
    var = jnp.mean(d * d, axis=-1, keepdims=True)
    return d * lax.rsqrt(var + LN_EPS) * g + b


def _load_token_rows(ref, first, t):
    return jnp.concatenate([ref[pl.ds(first * ROW_TILES + s, t, stride=ROW_TILES), :] for s in range(ROW_TILES)],
                           axis=1)


def _store_token_rows(ref, v, first=0):
    t = v.shape[0]
    for s in range(ROW_TILES):
        ref[pl.ds(first * ROW_TILES + s, t, stride=ROW_TILES), :] = v[:, s * LANES:(s + 1) * LANES]


def _two_head_softmax_pv(logits, vc, width):
    lane_v = lax.broadcasted_iota(jnp.int32, vc.shape, 1)
    pvs = []
    for hh in range(2):
        lg = logits[hh]
        p = jnp.exp2(lg - jnp.max(lg, axis=1, keepdims=True)).astype(jnp.bfloat16)
        v_h = jnp.where((lane_v // width) == hh, vc, jnp.ones_like(vc))
        pvs.append(jnp.dot(p, v_h, preferred_element_type=jnp.float32))
    lane = lax.broadcasted_iota(jnp.int32, pvs[0].shape, 1)
    first = lane < width
    num = jnp.where(first, pvs[0], pvs[1])
    den = jnp.where(first, pltpu.roll(pvs[0], width, 1), pltpu.roll(pvs[1], width, 1))
    return num / den


def _head_mask(shape, width, which):
    lane = lax.broadcasted_iota(jnp.int32, shape, 1)
    return (lane // width) == which


def _proj_kernel(x_ref, c64_ref, s64_ref, c32_ref, s32_ref, w_ref,
                 gcq_ref, gckv_ref, wuq_ref, wukv_ref,
                 qa_ref, ka_ref, va_ref, qi_ref, ki_ref, kr_ref, wi_ref,
                 qn_ref, qr_ref, kn_ref, vb_ref):
    xb = x_ref[...].astype(jnp.bfloat16)
    c64, s64 = c64_ref[...], s64_ref[...]
    c32, s32 = c32_ref[...], s32_ref[...]
    f32 = jnp.float32
    bf = jnp.bfloat16

    z1 = jnp.dot(xb, w_ref[:, :4 * A_WIDTH], preferred_element_type=f32)
    n_chunk = A_WIDTH // LANES
    a_scale = A_HEAD_DIM ** -0.5 * LOG2_E
    for c in range(n_chunk):
        sl = slice(c * LANES, (c + 1) * LANES)
        qa_ref[:, sl] = (_rope_chunk(z1[:, c * LANES:(c + 1) * LANES], c64, s64, 32) * a_scale).astype(bf)
        o = A_WIDTH + c * LANES
        ka_ref[:, sl] = _rope_chunk(z1[:, o:o + LANES], c64, s64, 32).astype(bf)
        o = 2 * A_WIDTH + c * LANES
        va_ref[:, sl] = z1[:, o:o + LANES].astype(bf)
        o = 3 * A_WIDTH + c * LANES
        qi_ref[:, sl] = _rope_chunk(z1[:, o:o + LANES], c64, s64, 32).astype(bf)

    small = 4 * A_WIDTH + 3 * LANES
    zs = jnp.dot(xb, w_ref[:, 4 * A_WIDTH:small], preferred_element_type=f32)
    ki_ref[...] = _rope_chunk(zs[:, 0:LANES], c64, s64, 32).astype(bf)
    kr_ref[...] = _rope_chunk(zs[:, LANES:2 * LANES], c32, s32, 16).astype(bf)
    wi_ref[...] = zs[:, 2 * LANES:3 * LANES] * (IDX_HEADS ** -0.5 * IDX_DIM ** -0.5)

    zl = jnp.dot(xb, w_ref[:, small:], preferred_element_type=f32)
    cq = zl[:, :Q_LORA]
    cq = cq * lax.rsqrt(jnp.mean(cq * cq, axis=-1, keepdims=True) + RMS_EPS) * gcq_ref[...]
    ckv = zl[:, Q_LORA:]
    ckv = ckv * lax.rsqrt(jnp.mean(ckv * ckv, axis=-1, keepdims=True) + RMS_EPS) * gckv_ref[...]
    qb = jnp.dot(cq.astype(bf), wuq_ref[...], preferred_element_type=f32)
    kv = jnp.dot(ckv.astype(bf), wukv_ref[...], preferred_element_type=f32)
    b_scale = (QK_NOPE + QK_ROPE) ** -0.5 * LOG2_E
    qn_ref[...] = (qb[:, :B_HEADS * QK_NOPE] * b_scale).astype(bf)
    for c in range(B_HEADS * QK_ROPE // LANES):
        o = B_HEADS * QK_NOPE + c * LANES
        qr_ref[:, c * LANES:(c + 1) * LANES] = (
            _rope_chunk(qb[:, o:o + LANES], c32, s32, 16) * b_scale).astype(bf)
    kn_ref[...] = kv[:, :B_HEADS * QK_NOPE].astype(bf)
    vb_ref[...] = kv[:, B_HEADS * QK_NOPE:].astype(bf)


def _proj(x2, tabs, w_slab, gcq, gckv, wuq, wukv):
    n = x2.shape[0]
    t = PROJ_TILE
    tok = lambda w: pl.BlockSpec((t, w), lambda i: (i, 0))
    bf = jnp.bfloat16
    outs = [(A_WIDTH, bf)] * 4 + [(LANES, bf), (LANES, bf), (LANES, jnp.float32),
                                  (B_HEADS * QK_NOPE, bf), (B_HEADS * QK_ROPE, bf),
                                  (B_HEADS * QK_NOPE, bf), (B_WIDTH, bf)]
    return pl.pallas_call(
        _proj_kernel,
        grid=(n // t,),
        in_specs=[tok(D_MODEL)] + [tok(LANES)] * 4 + [pl.BlockSpec((D_MODEL, PROJ_COLS), lambda i: (0, 1))]
        + [_full(a.shape) for a in (gcq, gckv, wuq, wukv)],
        out_specs=[tok(w) for w, _ in outs],
        out_shape=[jax.ShapeDtypeStruct((n, w), d) for w, d in outs],
        compiler_params=_cparams(("arbitrary",)),
        name="proj",
    )(x2, *tabs, w_slab, gcq, gckv, wuq, wukv)


def _tie_select(eq, gt, need, kv):
    chunk = 256
    r = lax.broadcasted_iota(jnp.int32, (chunk, chunk), 0)
    c = lax.broadcasted_iota(jnp.int32, (chunk, chunk), 1)
    upper = jnp.where(r <= c, 1.0, 0.0).astype(jnp.bfloat16)
    carry = jnp.zeros((eq.shape[0], 1), jnp.float32)
    parts = []
    for j in range(kv // chunk):
        e = eq[:, j * chunk:(j + 1) * chunk]
        ef = jnp.where(e, 1.0, 0.0).astype(jnp.bfloat16)
        pref = jnp.dot(ef, upper, preferred_element_type=jnp.float32) + carry
        parts.append(gt[:, j * chunk:(j + 1) * chunk] | (e & (pref <= need)))
        carry = carry + jnp.sum(ef.astype(jnp.float32), axis=1, keepdims=True)
    return jnp.concatenate(parts, axis=1)


def _key_to_float(key):
    return lax.bitcast_convert_type(key ^ ((key >> 31) & jnp.int32(0x7FFFFFFF)), jnp.float32)


def _select_topk(score, k_eff, sel_ref, kv):
    tq = score.shape[0]
    rows = tq // SEARCH_GROUPS
    parts = [(score[g * rows:(g + 1) * rows], k_eff[g * rows:(g + 1) * rows]) for g in range(SEARCH_GROUPS)]

    def search(it, prefixes):
        bit = lax.shift_left(jnp.int32(1), 31 - it)
        out = []
        for (s, k), prefix in zip(parts, prefixes):
            cand = prefix + bit
            cnt = jnp.sum(jnp.where(s >= _key_to_float(cand), 1.0, 0.0), axis=1, keepdims=True)
            out.append(jnp.where(cnt >= k, cand, prefix))
        return tuple(out)

    init = tuple(jnp.full((rows, 1), INT_MIN, jnp.int32) for _ in range(SEARCH_GROUPS))
    thr = _key_to_float(jnp.concatenate(lax.fori_loop(0, 32, search, init, unroll=4), axis=0))
    ge = score >= thr
    n_ge = jnp.sum(jnp.where(ge, 1.0, 0.0), axis=1, keepdims=True)
    sel_ref[:, :kv] = jnp.where(ge, 0.0, NEG_BIG)
    has_tie = jnp.max(jnp.where(n_ge != k_eff, 1.0, 0.0)) > 0.5

    @pl.when(has_tie)
    def _():
        gt = score > thr
        need = k_eff - jnp.sum(jnp.where(gt, 1.0, 0.0), axis=1, keepdims=True)
        sel_ref[:, :kv] = jnp.where(_tie_select(score == thr, gt, need, kv), 0.0, NEG_BIG)


def _dsa_block(qa_ref, qi_ref, wi_ref, ka_ref, va_ref, ki_ref, o_ref, sel_ref, score_ref, *, kv, k_sel):
    tq = qa_ref.shape[0]
    f32 = jnp.float32
    bf = jnp.bfloat16
    q_pos = (kv - tq) + lax.broadcasted_iota(jnp.int32, (tq, 1), 0)
    key_pos = lax.broadcasted_iota(jnp.int32, (tq, kv), 1)
    causal = key_pos <= q_pos

    if kv <= k_sel:
        sel_ref[:, :kv] = jnp.where(causal, 0.0, NEG_BIG)
    else:
        ki = ki_ref[:kv, :]
        lane = lax.broadcasted_iota(jnp.int32, (tq, LANES), 1)
        score_ref[:, :kv] = jnp.zeros((tq, kv), f32)

        def idx_slab(c, carry):
            qc = qi_ref[:, pl.ds(pl.multiple_of(c * LANES, LANES), LANES)]
            acc = score_ref[:, :kv]
            for hh in range(2):
                qm = jnp.where(_head_mask(qc.shape, IDX_DIM, hh), qc, jnp.zeros_like(qc))
                s = lax.dot_general(qm, ki, _NT, preferred_element_type=f32)
                w = jnp.sum(jnp.where(lane == 2 * c + hh, wi_ref[...], 0.0), axis=1, keepdims=True)
                acc = acc + jnp.maximum(s, 0.0) * w
            score_ref[:, :kv] = acc
            return carry

        lax.fori_loop(0, IDX_HEADS // 2, idx_slab, 0, unroll=2)
        score = jnp.where(causal, score_ref[:, :kv], -jnp.inf)
        k_eff = jnp.minimum(q_pos + 1, k_sel).astype(f32)
        _select_topk(score, k_eff, sel_ref, kv)

    def attn_slab(c, carry):
        sl = pl.ds(pl.multiple_of(c * LANES, LANES), LANES)
        qc = qa_ref[:, sl]
        kc = ka_ref[:kv, sl]
        vc = va_ref[:kv, sl]
        mask = sel_ref[:, :kv]
        logits = []
        for hh in range(2):
            qm = jnp.where(_head_mask(qc.shape, A_HEAD_DIM, hh), qc, jnp.zeros_like(qc))
            logits.append(lax.dot_general(qm, kc, _NT, preferred_element_type=f32) + mask)
        o_ref[:, sl] = _two_head_softmax_pv(logits, vc, A_HEAD_DIM).astype(bf)
        return carry

    lax.fori_loop(0, A_WIDTH // LANES, attn_slab, 0)


def _per_query_block(block_fn, refs, n_blocks, tq):
    j = pl.program_id(1)
    for b in range(n_blocks):
        pl.when(j == b)(functools.partial(block_fn, *refs, kv=(b + 1) * tq))


def _dsa_kernel(*refs, seq, k_sel):
    tq = refs[0].shape[0]
    _per_query_block(functools.partial(_dsa_block, k_sel=k_sel), refs, seq // tq, tq)


def _dsa(qa, qi, wi, ka, va, ki, bsz, seq):
    tq = min(Q_TILE, seq)
    nq = seq // tq
    qspec = lambda w: pl.BlockSpec((tq, w), lambda b, i: (b * nq + i, 0))
    kspec = lambda w: pl.BlockSpec((seq, w), lambda b, i: (b, 0))
    k_sel = min(TOPK_MAX, seq // 4)
    return pl.pallas_call(
        functools.partial(_dsa_kernel, seq=seq, k_sel=k_sel),
        grid=(bsz, nq),
        in_specs=[qspec(A_WIDTH), qspec(A_WIDTH), qspec(LANES), kspec(A_WIDTH), kspec(A_WIDTH), kspec(LANES)],
        out_specs=qspec(A_WIDTH),
        out_shape=jax.ShapeDtypeStruct((bsz * seq, A_WIDTH), jnp.bfloat16),
        scratch_shapes=[pltpu.VMEM((tq, seq), jnp.float32), pltpu.VMEM((tq, seq), jnp.float32)],
        compiler_params=_cparams(("arbitrary", "arbitrary")),
        name="dsa",
    )(qa, qi, wi, ka, va, ki)


def _mla_block(qn_ref, qr_ref, kn_ref, kr_ref, vb_ref, o_ref, *, kv):
    tq = qn_ref.shape[0]
    f32 = jnp.float32
    bf = jnp.bfloat16
    diag = lax.broadcasted_iota(jnp.int32, (tq, tq), 1) <= lax.broadcasted_iota(jnp.int32, (tq, tq), 0)
    kr = kr_ref[:kv, :]

    def causal_masked(lg):
        own = jnp.where(diag, lg[:, kv - tq:], NEG_BIG)
        return own if kv == tq else jnp.concatenate([lg[:, :kv - tq], own], axis=1)

    def slab(c, carry):
        sl = pl.ds(pl.multiple_of(c * LANES, LANES), LANES)
        qc = qn_ref[:, sl]
        kcat = jnp.concatenate([kn_ref[:kv, sl], kr], axis=1)
        vc = vb_ref[:kv, sl]
        qrc = qr_ref[:, pl.ds(pl.multiple_of((c // 2) * LANES, LANES), LANES)]
        logits = []
        for hh in range(2):
            h = 2 * c + hh
            qm = jnp.where(_head_mask(qc.shape, QK_NOPE, hh), qc, jnp.zeros_like(qc))
            qrm = jnp.where(_head_mask(qrc.shape, QK_ROPE, h % 4), qrc, jnp.zeros_like(qrc))
            qcat = jnp.concatenate([qm, qrm], axis=1)
            lg = lax.dot_general(qcat, kcat, _NT, preferred_element_type=f32)
            logits.append(causal_masked(lg))
        o_ref[:, sl] = _two_head_softmax_pv(logits, vc, V_DIM).astype(bf)
        return carry

    lax.fori_loop(0, B_WIDTH // LANES, slab, 0, unroll=2)


def _mla_kernel(*refs, seq):
    tq = refs[0].shape[0]
    _per_query_block(_mla_block, refs, seq // tq, tq)


def _mla(qn, qr, kn, kr, vb, bsz, seq):
    tq = min(Q_TILE, seq)
    nq = seq // tq
    qspec = lambda w: pl.BlockSpec((tq, w), lambda b, i: (b * nq + i, 0))
    kspec = lambda w: pl.BlockSpec((seq, w), lambda b, i: (b, 0))
    return pl.pallas_call(
        functools.partial(_mla_kernel, seq=seq),
        grid=(bsz, nq),
        in_specs=[qspec(B_HEADS * QK_NOPE), qspec(B_HEADS * QK_ROPE), kspec(B_HEADS * QK_NOPE),
                  kspec(LANES), kspec(B_WIDTH)],
        out_specs=qspec(B_WIDTH),
        out_shape=jax.ShapeDtypeStruct((bsz * seq, B_WIDTH), jnp.bfloat16),
        compiler_params=_cparams(("arbitrary", "arbitrary")),
        name="mla",
    )(qn, qr, kn, kr, vb)


def _post_kernel(x_ref, oa_ref, ob_ref, wg_ref, bg_ref, wo_ref, g1_ref, b1_ref,
                 wrh_ref, wrl_ref, br_ref, h_ref, ri_ref, rf_ref, cnt_ref, carry_ref):
    f32 = jnp.float32
    bf = jnp.bfloat16
    t = x_ref.shape[0]

    @pl.when(pl.program_id(0) == 0)
    def _():
        carry_ref[...] = jnp.zeros_like(carry_ref)

    x = x_ref[...]
    gate = jax.nn.sigmoid(jnp.dot(x.astype(bf), wg_ref[...], preferred_element_type=f32) + bg_ref[...])
    pa = jnp.dot(oa_ref[...], wo_ref[:A_WIDTH, :], preferred_element_type=f32)
    pb = jnp.dot(ob_ref[...], wo_ref[A_WIDTH:A_WIDTH + B_WIDTH, :], preferred_element_type=f32)
    mix = gate[:, :D_MODEL] * pa + gate[:, D_MODEL:] * pb
    u = DN_ALPHA * x + jnp.dot(mix.astype(bf), wo_ref[A_WIDTH + B_WIDTH:, :], preferred_element_type=f32)
    h = _layer_norm(u, g1_ref[...], b1_ref[...])
    _store_token_rows(h_ref, h)

    h_hi = h.astype(bf)
    h_lo = (h - h_hi.astype(f32)).astype(bf)
    logits = (jnp.dot(h_hi, wrh_ref[...], preferred_element_type=f32)
              + jnp.dot(h_lo, wrh_ref[...], preferred_element_type=f32)
              + jnp.dot(h_hi, wrl_ref[...], preferred_element_type=f32) + br_ref[...])
    lane = lax.broadcasted_iota(jnp.int32, (t, LANES), 1).astype(f32)
    lg = jnp.where(lane < N_EXPERTS, logits, -jnp.inf)
    vals, ids = [], []
    assign = jnp.zeros((t, LANES), f32)
    for _k in range(TOP_K):
        m = jnp.max(lg, axis=1, keepdims=True)
        idx = jnp.min(jnp.where(lg == m, lane, float(LANES)), axis=1, keepdims=True)
        hit = lane == idx
        vals.append(m)
        ids.append(idx)
        assign = jnp.where(hit, 1.0, assign)
        lg = jnp.where(hit, -jnp.inf, lg)
    exps = [jnp.exp(v - vals[0]) for v in vals]
    den = exps[0] + exps[1] + exps[2] + exps[3]

    r = lax.broadcasted_iota(jnp.int32, (t, t), 0)
    c = lax.broadcasted_iota(jnp.int32, (t, t), 1)
    lower = jnp.where(c < r, 1.0, 0.0).astype(bf)
    before = jnp.dot(lower, assign.astype(bf), preferred_element_type=f32) + carry_ref[0:1, :]
    ri = jnp.zeros((t, LANES), f32)
    rf = jnp.zeros((t, LANES), f32)
    for k in range(TOP_K):
        rank = jnp.sum(jnp.where(lane == ids[k], before, 0.0), axis=1, keepdims=True)
        ri = jnp.where(lane == float(k), ids[k], ri)
        ri = jnp.where(lane == float(TOP_K + k), rank, ri)
        rf = jnp.where(lane == float(k), exps[k] / den, rf)
    ri_ref[...] = ri.T[:2 * TOP_K, :].astype(jnp.int32)
    rf_ref[...] = rf
    total = carry_ref[0:1, :] + jnp.sum(assign, axis=0, keepdims=True)
    carry_ref[...] = jnp.broadcast_to(total, carry_ref.shape)
    cnt_ref[...] = jnp.broadcast_to(total, cnt_ref.shape)


def _post(x2, oa, ob, w_slab, bg, wo, g1, b1, wrh, wrl, br):
    n = x2.shape[0]
    t = PROJ_TILE
    tok = lambda w: pl.BlockSpec((t, w), lambda i: (i, 0))
    consts = (bg, wo, g1, b1, wrh, wrl, br)
    gates_w = pl.BlockSpec((D_MODEL, 2 * D_MODEL), lambda i: (0, 0))
    return pl.pallas_call(
        _post_kernel,
        grid=(n // t,),
        in_specs=[tok(D_MODEL), tok(A_WIDTH), tok(B_WIDTH), gates_w] + [_full(a.shape) for a in consts],
        out_specs=[pl.BlockSpec((t * ROW_TILES, LANES), lambda i: (i, 0)),
                   pl.BlockSpec((2 * TOP_K, t), lambda i: (0, i)), tok(LANES), _full((8, LANES))],
        out_shape=[jax.ShapeDtypeStruct((n * ROW_TILES, LANES), jnp.float32),
                   jax.ShapeDtypeStruct((2 * TOP_K, n), jnp.int32),
                   jax.ShapeDtypeStruct((n, LANES), jnp.float32),
                   jax.ShapeDtypeStruct((8, LANES), jnp.float32)],
        scratch_shapes=[pltpu.VMEM((8, LANES), jnp.float32)],
        compiler_params=_cparams(("arbitrary",)),
        name="post",
    )(x2, oa, ob, w_slab, *consts)


def _token_row(ref, r):
    start = r * ROW_TILES if isinstance(r, int) else pl.multiple_of(r * ROW_TILES, ROW_TILES)
    return ref.at[pl.ds(start, ROW_TILES)]


def _row_gather(idx_ref, src_ref, dst_ref, sem, rows):
    for j in range(rows):
        pltpu.make_async_copy(_token_row(src_ref, idx_ref[0, 0, j]), _token_row(dst_ref, j), sem).start(
            priority=j % 2)


def _rows_wait(src_ref, dst_ref, sem):
    pltpu.make_async_copy(src_ref, dst_ref, sem).wait()


def _dispatch_kernel(dest_ref, h_ref, xs_ref, sem):
    t = DISPATCH_TILE
    for k in range(TOP_K):
        for j in range(t):
            pltpu.make_async_copy(_token_row(h_ref, j), _token_row(xs_ref, dest_ref[0, 0, k * t + j]), sem).start(
                priority=j % 2)
    for k in range(TOP_K):
        _rows_wait(h_ref, xs_ref.at[pl.ds(0, t * ROW_TILES)], sem)


def _dispatch(dest3, h, rows):
    t = DISPATCH_TILE
    n = h.shape[0] // ROW_TILES
    return pl.pallas_call(
        _dispatch_kernel,
        grid=(n // t,),
        in_specs=[pl.BlockSpec((1, 1, t * TOP_K), lambda i: (i, 0, 0), memory_space=pltpu.SMEM),
                  pl.BlockSpec((t * ROW_TILES, LANES), lambda i: (i, 0))],
        out_specs=pl.BlockSpec(memory_space=pl.ANY),
        out_shape=jax.ShapeDtypeStruct((rows * ROW_TILES, LANES), jnp.float32),
        scratch_shapes=[pltpu.SemaphoreType.DMA(())],
        compiler_params=_cparams(("arbitrary",)),
        name="dispatch",
    )(dest3, h)


def _expert_kernel(te_ref, tt_ref, lo_ref, hi_ref, ns_ref, xs_ref, wu_ref, bu_ref, wd_ref, bd_ref, ys_ref,
                   wub_ref, wdb_ref):
    i = pl.program_id(0)
    f32 = jnp.float32
    bf = jnp.bfloat16
    t = EXPERT_TILE
    valid = i < ns_ref[0]
    before = jnp.maximum(i - 1, 0)
    fresh = valid & ((i == 0) | (te_ref[i] != te_ref[before]))
    first_visit = (i == 0) | (tt_ref[i] != tt_ref[before])

    @pl.when(fresh)
    def _():
        rows = 128

        def cast_up(j, c):
            r = pl.multiple_of(j * rows, rows)
            wub_ref[pl.ds(r, rows), :] = wu_ref[pl.ds(r, rows), :].astype(bf)
            return c

        lax.fori_loop(0, D_MODEL // rows, cast_up, 0)

        def cast_dn(j, c):
            r = pl.multiple_of(j * rows, rows)
            wdb_ref[pl.ds(r, rows), :] = wd_ref[pl.ds(r, rows), :].astype(bf)
            return c

        lax.fori_loop(0, D_FF // rows, cast_dn, 0)

    def mlp(first, rows):
        xb = _load_token_rows(xs_ref, first, rows).astype(bf)
        y = jnp.broadcast_to(bd_ref[...], (rows, D_MODEL))
        for c in range(D_FF // FF_CHUNK):
            gs = slice(c * FF_CHUNK, (c + 1) * FF_CHUNK)
            ls = slice(D_FF + c * FF_CHUNK, D_FF + (c + 1) * FF_CHUNK)
            ag = jnp.dot(xb, wub_ref[:, gs], preferred_element_type=f32) + bu_ref[:, gs]
            al = jnp.dot(xb, wub_ref[:, ls], preferred_element_type=f32) + bu_ref[:, ls]
            g = jnp.minimum(ag, SWIGLU_LIMIT)
            lin = jnp.clip(al, -SWIGLU_LIMIT, SWIGLU_LIMIT)
            act = (lin + 1.0) * (g * jax.nn.sigmoid(SWIGLU_ALPHA * g))
            y = y + jnp.dot(act.astype(bf), wdb_ref[gs, :], preferred_element_type=f32)
        return y

    lo, hi = lo_ref[i], hi_ref[i]
    half = t // 2
    whole = (lo == 0) & (hi == t)
    partial = valid & jnp.logical_not(whole)

    @pl.when(valid & whole)
    def _():
        _store_token_rows(ys_ref, mlp(0, t))

    @pl.when(partial & first_visit)
    def _():
        ys_ref[...] = jnp.zeros_like(ys_ref)

    def fill(first, rows):
        row = first + lax.broadcasted_iota(jnp.int32, (rows, 1), 0)
        mine = (row >= lo) & (row < hi)
        _store_token_rows(ys_ref, jnp.where(mine, mlp(first, rows), _load_token_rows(ys_ref, first, rows)), first)

    pl.when(partial & (lo < half) & (hi > half))(functools.partial(fill, 0, t))
    pl.when(partial & (hi <= half))(functools.partial(fill, 0, half))
    pl.when(partial & (lo >= half))(functools.partial(fill, half, half))


def _experts(steps, xs, w_up, b_up, w_down, b_down):
    t = EXPERT_TILE
    max_steps = steps[0].shape[0]
    row_block = pl.BlockSpec((t * ROW_TILES, LANES), lambda i, te, tt, lo, hi, ns: (tt[i], 0))
    per_expert = lambda shape: pl.BlockSpec((None,) + shape, lambda i, te, tt, lo, hi, ns: (te[i], 0, 0))
    grid_spec = pltpu.PrefetchScalarGridSpec(
        num_scalar_prefetch=5,
        grid=(max_steps,),
        in_specs=[row_block, per_expert((D_MODEL, 2 * D_FF)), per_expert((1, 2 * D_FF)),
                  per_expert((D_FF, D_MODEL)), per_expert((1, D_MODEL))],
        out_specs=row_block,
        scratch_shapes=[pltpu.VMEM((D_MODEL, 2 * D_FF), jnp.bfloat16),
                        pltpu.VMEM((D_FF, D_MODEL), jnp.bfloat16)],
    )
    return pl.pallas_call(
        _expert_kernel,
        grid_spec=grid_spec,
        out_shape=jax.ShapeDtypeStruct(xs.shape, jnp.float32),
        compiler_params=_cparams(("arbitrary",)),
        name="experts",
    )(*steps, xs, w_up, b_up.reshape(N_EXPERTS, 1, 2 * D_FF), w_down, b_down.reshape(N_EXPERTS, 1, D_MODEL))


def _combine_kernel(cur_ref, nxt_ref, ys_ref, h_ref, rf_ref, g2_ref, b2_ref, o_ref, buf_ref, sems):
    i = pl.program_id(0)
    n_steps = pl.num_programs(0)
    t = COMBINE_TILE
    rows = t * TOP_K
    slot = i % 2

    @pl.when(i == 0)
    def _():
        _row_gather(cur_ref, ys_ref, buf_ref.at[0], sems.at[0], rows)

    _rows_wait(ys_ref.at[pl.ds(0, rows * ROW_TILES)], buf_ref.at[slot], sems.at[slot])
    _row_gather(nxt_ref, ys_ref, buf_ref.at[1 - slot], sems.at[1 - slot], rows)

    rf = rf_ref[...]
    u = DN_ALPHA * _load_token_rows(h_ref, 0, t)
    for k in range(TOP_K):
        u = u + rf[:, k:k + 1] * _load_token_rows(buf_ref.at[slot], k * t, t)
    o_ref[...] = _layer_norm(u, g2_ref[...], b2_ref[...])

    @pl.when(i == n_steps - 1)
    def _():
        _rows_wait(ys_ref.at[pl.ds(0, rows * ROW_TILES)], buf_ref.at[1 - slot], sems.at[1 - slot])


def _combine(dest3, ys, h, rf, g2, b2):
    t = COMBINE_TILE
    n = h.shape[0] // ROW_TILES
    last = n // t - 1
    tok = lambda w: pl.BlockSpec((t, w), lambda i: (i, 0))
    idx_spec = lambda f: pl.BlockSpec((1, 1, t * TOP_K), f, memory_space=pltpu.SMEM)
    return pl.pallas_call(
        _combine_kernel,
        grid=(n // t,),
        in_specs=[idx_spec(lambda i: (i, 0, 0)), idx_spec(lambda i: (jnp.minimum(i + 1, last), 0, 0)),
                  pl.BlockSpec(memory_space=pl.ANY), pl.BlockSpec((t * ROW_TILES, LANES), lambda i: (i, 0)),
                  tok(LANES), _full(g2.shape), _full(b2.shape)],
        out_specs=tok(D_MODEL),
        out_shape=jax.ShapeDtypeStruct((n, D_MODEL), jnp.float32),
        scratch_shapes=[pltpu.VMEM((2, t * TOP_K * ROW_TILES, LANES), jnp.float32),
                        pltpu.SemaphoreType.DMA((2,))],
        compiler_params=_cparams(("arbitrary",)),
        name="combine",
    )(dest3, dest3, ys, h, rf, g2, b2)


def _rope_tables(positions, half):
    inv_freq = ROPE_THETA ** (-jnp.arange(half, dtype=jnp.float32) / half)
    ang = positions.astype(jnp.float32).reshape(-1, 1) * jnp.tile(inv_freq, LANES // half)
    return jnp.cos(ang), jnp.sin(ang)


def _lookup(table, idx):
    experts = jnp.arange(N_EXPERTS, dtype=jnp.int32)
    return jnp.sum(jnp.where(idx[..., None] == experts, table, 0), axis=-1)


def _expert_schedule(starts, ends, rows):
    t = EXPERT_TILE
    first_tile = starts // t
    n_tile = jnp.where(ends > starts, (ends - 1) // t - first_tile + 1, 0)
    cum = jnp.cumsum(n_tile)
    n_steps = cum[-1]
    max_steps = rows // t + N_EXPERTS
    s = jnp.minimum(jnp.arange(max_steps, dtype=jnp.int32), n_steps - 1)
    expert = jnp.sum((cum[None, :] <= s[:, None]).astype(jnp.int32), axis=1)
    tile = _lookup(first_tile, expert) + s - _lookup(cum - n_tile, expert)
    lo = jnp.maximum(_lookup(starts, expert) - tile * t, 0)
    hi = jnp.minimum(_lookup(ends, expert) - tile * t, t)
    return expert, tile, lo, hi, n_steps.astype(jnp.int32).reshape(1)


def _layer(x2, tabs, bsz, seq, w_in, b_gate, rms_cq, rms_ckv, w_uq, w_ukv, w_o_a, w_o_b, w_out,
           ln1_g, ln1_b, w_router, b_router, w_up, b_up, w_down, b_down, ln2_g, ln2_b):
    bf = jnp.bfloat16
    n = x2.shape[0]
    off = np.concatenate([[0], np.cumsum(SPLITS)]).tolist()
    w_in_b = w_in.astype(bf)
    col = lambda j: w_in_b[:, off[j]:off[j + 1]]
    zeros = lambda w: jnp.zeros((D_MODEL, w), bf)
    w_slab = jnp.concatenate(
        [col(9), zeros(PROJ_COLS - 2 * D_MODEL), w_in_b[:, :off[4]],
         col(4), col(4), jnp.tile(col(8), (1, LANES // QK_ROPE)), col(5), zeros(LANES - IDX_HEADS),
         col(6), col(7)], axis=1)
    w_o = jnp.concatenate([w_o_a, w_o_b, w_out], axis=0).astype(bf)
    uq = w_uq.reshape(Q_LORA, B_HEADS, QK_NOPE + QK_ROPE)
    wuq = jnp.concatenate([uq[:, :, :QK_NOPE].reshape(Q_LORA, -1), uq[:, :, QK_NOPE:].reshape(Q_LORA, -1)],
                          axis=1).astype(bf)
    ukv = w_ukv.reshape(KV_LORA, B_HEADS, QK_NOPE + V_DIM)
    wukv = jnp.concatenate([ukv[:, :, :QK_NOPE].reshape(KV_LORA, -1), ukv[:, :, QK_NOPE:].reshape(KV_LORA, -1)],
                           axis=1).astype(bf)

    qa, ka, va, qi, ki, kr, wi, qn, qr, kn, vb = _proj(
        x2, tabs, w_slab, rms_cq.reshape(1, -1), rms_ckv.reshape(1, -1), wuq, wukv)
    o_a = _dsa(qa, qi, wi, ka, va, ki, bsz, seq)
    o_b = _mla(qn, qr, kn, kr, vb, bsz, seq)

    wr = jnp.concatenate([w_router, jnp.zeros((D_MODEL, LANES - N_EXPERTS), w_router.dtype)], axis=1)
    br = jnp.concatenate([b_router, jnp.zeros((LANES - N_EXPERTS,), b_router.dtype)]).reshape(1, -1)
    wr_hi = wr.astype(bf)
    wr_lo = (wr - wr_hi.astype(jnp.float32)).astype(bf)
    h, ri, rf, cnt = _post(x2, o_a, o_b, w_slab, b_gate.reshape(1, -1), w_o, ln1_g.reshape(1, -1),
                           ln1_b.reshape(1, -1), wr_hi, wr_lo, br)

    counts = cnt[0, :N_EXPERTS].astype(jnp.int32)
    ends = jnp.cumsum(counts)
    starts = ends - counts
    dest = _lookup(starts, ri[:TOP_K]) + ri[TOP_K:]
    by_step = lambda t: dest.reshape(TOP_K, n // t, t).transpose(1, 0, 2).reshape(n // t, 1, t * TOP_K)

    xs = _dispatch(by_step(DISPATCH_TILE), h, n * TOP_K)
    ys = _experts(_expert_schedule(starts, ends, n * TOP_K), xs, w_up, b_up, w_down, b_down)
    return _combine(by_step(COMBINE_TILE), ys, h, rf, ln2_g.reshape(1, -1), ln2_b.reshape(1, -1))


def kernel(x, positions, w_in, b_gate, rms_cq, rms_ckv, w_uq, w_ukv, w_o_a, w_o_b, w_out, ln1_g, ln1_b,
           w_router, b_router, w_up, b_up, w_down, b_down, ln2_g, ln2_b):
    bsz, seq, _ = x.shape
    x2 = x.reshape(bsz * seq, D_MODEL)
    c64, s64 = _rope_tables(positions, A_HEAD_DIM // 2)
    c32, s32 = _rope_tables(positions, QK_ROPE // 2)
    tabs = (c64, s64, c32, s32)
    for l in range(DEPTH):
        x2 = _layer(x2, tabs, bsz, seq, w_in[l], b_gate[l], rms_cq[l], rms_ckv[l], w_uq[l], w_ukv[l],
                    w_o_a[l], w_o_b[l], w_out[l], ln1_g[l], ln1_b[l], w_router[l], b_router[l],
                    w_up[l], b_up[l], w_down[l], b_down[l], ln2_g[l], ln2_b[l])
    return x2.reshape(bsz, seq, D_MODEL)
```

```python
import functools

import jax
import jax.numpy as jnp
import numpy as np
from jax import lax
from jax.experimental import pallas as pl
from jax.experimental.pallas import tpu as pltpu

D_MODEL = 1024
A_HEADS = 8
A_HEAD_DIM = 64
IDX_HEADS = 8
IDX_DIM = 64
TOPK_MAX = 256
B_HEADS = 8
Q_LORA = 384
KV_LORA = 256
QK_NOPE = 64
QK_ROPE = 32
V_DIM = 64
ROPE_THETA = 10000.0
N_EXPERTS = 32
TOP_K = 4
D_FF = 1024
SWIGLU_LIMIT = 7.0
SWIGLU_ALPHA = 1.702
DEPTH = 1
DN_ALPHA = (2 * DEPTH) ** 0.25
LN_EPS = 1e-5
RMS_EPS = 1e-6
A_WIDTH = A_HEADS * A_HEAD_DIM
B_WIDTH = B_HEADS * V_DIM
SPLITS = (A_WIDTH, A_WIDTH, A_WIDTH, IDX_HEADS * IDX_DIM, IDX_DIM, IDX_HEADS,
          Q_LORA, KV_LORA, QK_ROPE, 2 * D_MODEL)
LANES = 128
VMEM_LIMIT_BYTES = 52 * 1024 * 1024
PROJ_COLS = 4 * A_WIDTH + 3 * LANES + Q_LORA + KV_LORA

PROJ_TILE = 512
Q_TILE = 256
SEARCH_GROUPS = 2
EXPERT_TILE = 512
FF_CHUNK = 1024
DISPATCH_TILE = 512
COMBINE_TILE = 256

ROW_TILES = D_MODEL // LANES
assert ROW_TILES == 8

LOG2_E = 1.4426950408889634
INT_MIN = -(2 ** 31)
NEG_BIG = -1e30

_NT = (((1,), (1,)), ((), ()))


def _cparams(sem):
    return pltpu.CompilerParams(dimension_semantics=sem, vmem_limit_bytes=VMEM_LIMIT_BYTES)


def _full(shape):
    nd = len(shape)
    return pl.BlockSpec(shape, lambda *_: (0,) * nd)


def _rope_chunk(x, cos, sin, half):
    lane = lax.broadcasted_iota(jnp.int32, x.shape, 1)
    first = (lane % (2 * half)) < half
    up = pltpu.roll(x, LANES - half, 1)
    dn = pltpu.roll(x, half, 1)
    return x * cos + jnp.where(first, -up, dn) * sin


def _layer_norm(u, g, b):
    mu = jnp.mean(u, axis=-1, keepdims=True)
    d = u - mu
    var = jnp.mean(d * d, axis=-1, keepdims=True)
    return d * lax.rsqrt(var + LN_EPS) * g + b


def _load_token_rows(ref, first, t):
    return jnp.concatenate([ref[pl.ds(first * ROW_TILES + s, t, stride=ROW_TILES), :] for s in range(ROW_TILES)],
                           axis=1)


def _store_token_rows(ref, v, first=0):
    t = v.shape[0]
    for s in range(ROW_TILES):
        ref[pl.ds(first * ROW_TILES + s, t, stride=ROW_TILES), :] = v[:, s * LANES:(s + 1) * LANES]


def _two_head_softmax_pv(logits, vc, width):
    lane_v = lax.broadcasted_iota(jnp.int32, vc.shape, 1)
    pvs = []
    for hh in range(2):
        lg = logits[hh]
        p = jnp.exp2(lg - jnp.max(lg, axis=1, keepdims=True)).astype(jnp.bfloat16)
        v_h = jnp.where((lane_v // width) == hh, vc, jnp.ones_like(vc))
        pvs.append(jnp.dot(p, v_h, preferred_element_type=jnp.float32))
    lane = lax.broadcasted_iota(jnp.int32, pvs[0].shape, 1)
    first = lane < width
    num = jnp.where(first, pvs[0], pvs[1])
    den = jnp.where(first, pltpu.roll(pvs[0], width, 1), pltpu.roll(pvs[1], width, 1))
    return num / den


def _head_mask(shape, width, which):
    lane = lax.broadcasted_iota(jnp.int32, shape, 1)
    return (lane // width) == which


def _proj_kernel(x_ref, trig_ref, expand_ref, w_ref,
                 gcq_ref, gckv_ref, wuq_ref, wukv_ref,
                 qa_ref, ka_ref, va_ref, qi_ref, ki_ref, kr_ref, wi_ref,
                 qn_ref, qr_ref, kn_ref, vb_ref):
    xb = x_ref[...].astype(jnp.bfloat16)
    f32 = jnp.float32
    bf = jnp.bfloat16

    trig = trig_ref[...]
    trig_hi = trig.astype(bf)
    trig_lo = (trig - trig_hi.astype(f32)).astype(bf)
    tabs = (jnp.dot(trig_hi, expand_ref[...], preferred_element_type=f32)
            + jnp.dot(trig_lo, expand_ref[...], preferred_element_type=f32))
    c64, s64, c32, s32 = (tabs[:, j * LANES:(j + 1) * LANES] for j in range(4))

    z1 = jnp.dot(xb, w_ref[:, :4 * A_WIDTH], preferred_element_type=f32)
    n_chunk = A_WIDTH // LANES
    a_scale = A_HEAD_DIM ** -0.5 * LOG2_E
    for c in range(n_chunk):
        sl = slice(c * LANES, (c + 1) * LANES)
        qa_ref[:, sl] = (_rope_chunk(z1[:, c * LANES:(c + 1) * LANES], c64, s64, 32) * a_scale).astype(bf)
        o = A_WIDTH + c * LANES
        ka_ref[:, sl] = _rope_chunk(z1[:, o:o + LANES], c64, s64, 32).astype(bf)
        o = 2 * A_WIDTH + c * LANES
        va_ref[:, sl] = z1[:, o:o + LANES].astype(bf)
        o = 3 * A_WIDTH + c * LANES
        qi_ref[:, sl] = _rope_chunk(z1[:, o:o + LANES], c64, s64, 32).astype(bf)

    small = 4 * A_WIDTH + 3 * LANES
    zs = jnp.dot(xb, w_ref[:, 4 * A_WIDTH:small], preferred_element_type=f32)
    ki_ref[...] = _rope_chunk(zs[:, 0:LANES], c64, s64, 32).astype(bf)
    kr_ref[...] = _rope_chunk(zs[:, LANES:2 * LANES], c32, s32, 16).astype(bf)
    wi_ref[...] = zs[:, 2 * LANES:3 * LANES] * (IDX_HEADS ** -0.5 * IDX_DIM ** -0.5)

    zl = jnp.dot(xb, w_ref[:, small:], preferred_element_type=f32)
    cq = zl[:, :Q_LORA]
    cq = cq * lax.rsqrt(jnp.mean(cq * cq, axis=-1, keepdims=True) + RMS_EPS) * gcq_ref[...]
    ckv = zl[:, Q_LORA:]
    ckv = ckv * lax.rsqrt(jnp.mean(ckv * ckv, axis=-1, keepdims=True) + RMS_EPS) * gckv_ref[...]
    qb = jnp.dot(cq.astype(bf), wuq_ref[...], preferred_element_type=f32)
    kv = jnp.dot(ckv.astype(bf), wukv_ref[...], preferred_element_type=f32)
    b_scale = (QK_NOPE + QK_ROPE) ** -0.5 * LOG2_E
    qn_ref[...] = (qb[:, :B_HEADS * QK_NOPE] * b_scale).astype(bf)
    for c in range(B_HEADS * QK_ROPE // LANES):
        o = B_HEADS * QK_NOPE + c * LANES
        qr_ref[:, c * LANES:(c + 1) * LANES] = (
            _rope_chunk(qb[:, o:o + LANES], c32, s32, 16) * b_scale).astype(bf)
    kn_ref[...] = kv[:, :B_HEADS * QK_NOPE].astype(bf)
    vb_ref[...] = kv[:, B_HEADS * QK_NOPE:].astype(bf)


def _proj(x2, tabs, w_slab, gcq, gckv, wuq, wukv):
    n = x2.shape[0]
    t = PROJ_TILE
    tok = lambda w: pl.BlockSpec((t, w), lambda i: (i, 0))
    bf = jnp.bfloat16
    outs = [(A_WIDTH, bf)] * 4 + [(LANES, bf), (LANES, bf), (LANES, jnp.float32),
                                  (B_HEADS * QK_NOPE, bf), (B_HEADS * QK_ROPE, bf),
                                  (B_HEADS * QK_NOPE, bf), (B_WIDTH, bf)]
    return pl.pallas_call(
        _proj_kernel,
        grid=(n // t,),
        in_specs=[tok(D_MODEL), tok(LANES), _full(tabs[1].shape),
                  pl.BlockSpec((D_MODEL, PROJ_COLS), lambda i: (0, 1))]
        + [_full(a.shape) for a in (gcq, gckv, wuq, wukv)],
        out_specs=[tok(w) for w, _ in outs],
        out_shape=[jax.ShapeDtypeStruct((n, w), d) for w, d in outs],
        compiler_params=_cparams(("arbitrary",)),
        name="proj",
    )(x2, *tabs, w_slab, gcq, gckv, wuq, wukv)


def _tie_select(eq, gt, need, kv):
    chunk = 256
    r = lax.broadcasted_iota(jnp.int32, (chunk, chunk), 0)
    c = lax.broadcasted_iota(jnp.int32, (chunk, chunk), 1)
    upper = jnp.where(r <= c, 1.0, 0.0).astype(jnp.bfloat16)
    carry = jnp.zeros((eq.shape[0], 1), jnp.float32)
    parts = []
    for j in range(kv // chunk):
        e = eq[:, j * chunk:(j + 1) * chunk]
        ef = jnp.where(e, 1.0, 0.0).astype(jnp.bfloat16)
        pref = jnp.dot(ef, upper, preferred_element_type=jnp.float32) + carry
        parts.append(gt[:, j * chunk:(j + 1) * chunk] | (e & (pref <= need)))
        carry = carry + jnp.sum(ef.astype(jnp.float32), axis=1, keepdims=True)
    return jnp.concatenate(parts, axis=1)


def _key_to_float(key):
    return lax.bitcast_convert_type(key ^ ((key >> 31) & jnp.int32(0x7FFFFFFF)), jnp.float32)


def _select_topk(score, k_eff, sel_ref, kv):
    tq = score.shape[0]
    rows = tq // SEARCH_GROUPS
    parts = [(score[g * rows:(g + 1) * rows], k_eff[g * rows:(g + 1) * rows]) for g in range(SEARCH_GROUPS)]

    def search(it, prefixes):
        bit = lax.shift_left(jnp.int32(1), 31 - it)
        out = []
        for (s, k), prefix in zip(parts, prefixes):
            cand = prefix + bit
            cnt = jnp.sum(jnp.where(s >= _key_to_float(cand), 1.0, 0.0), axis=1, keepdims=True)
            out.append(jnp.where(cnt >= k, cand, prefix))
        return tuple(out)

    init = tuple(jnp.full((rows, 1), INT_MIN, jnp.int32) for _ in range(SEARCH_GROUPS))
    thr = _key_to_float(jnp.concatenate(lax.fori_loop(0, 32, search, init, unroll=4), axis=0))
    ge = score >= thr
    n_ge = jnp.sum(jnp.where(ge, 1.0, 0.0), axis=1, keepdims=True)
    sel_ref[:, :kv] = jnp.where(ge, 0.0, NEG_BIG)
    has_tie = jnp.max(jnp.where(n_ge != k_eff, 1.0, 0.0)) > 0.5

    @pl.when(has_tie)
    def _():
        gt = score > thr
        need = k_eff - jnp.sum(jnp.where(gt, 1.0, 0.0), axis=1, keepdims=True)
        sel_ref[:, :kv] = jnp.where(_tie_select(score == thr, gt, need, kv), 0.0, NEG_BIG)


def _dsa_block(qa_ref, qi_ref, wi_ref, ka_ref, va_ref, ki_ref, o_ref, sel_ref, score_ref, *, kv, k_sel):
    tq = qa_ref.shape[0]
    f32 = jnp.float32
    bf = jnp.bfloat16
    q_pos = (kv - tq) + lax.broadcasted_iota(jnp.int32, (tq, 1), 0)
    key_pos = lax.broadcasted_iota(jnp.int32, (tq, kv), 1)
    causal = key_pos <= q_pos

    if kv <= k_sel:
        sel_ref[:, :kv] = jnp.where(causal, 0.0, NEG_BIG)
    else:
        ki = ki_ref[:kv, :]
        lane = lax.broadcasted_iota(jnp.int32, (tq, LANES), 1)
        score_ref[:, :kv] = jnp.zeros((tq, kv), f32)

        def idx_slab(c, carry):
            qc = qi_ref[:, pl.ds(pl.multiple_of(c * LANES, LANES), LANES)]
            acc = score_ref[:, :kv]
            for hh in range(2):
                qm = jnp.where(_head_mask(qc.shape, IDX_DIM, hh), qc, jnp.zeros_like(qc))
                s = lax.dot_general(qm, ki, _NT, preferred_element_type=f32)
                w = jnp.sum(jnp.where(lane == 2 * c + hh, wi_ref[...], 0.0), axis=1, keepdims=True)
                acc = acc + jnp.maximum(s, 0.0) * w
            score_ref[:, :kv] = acc
            return carry

        lax.fori_loop(0, IDX_HEADS // 2, idx_slab, 0, unroll=2)
        score = jnp.where(causal, score_ref[:, :kv], -jnp.inf)
        k_eff = jnp.minimum(q_pos + 1, k_sel).astype(f32)
        _select_topk(score, k_eff, sel_ref, kv)

    def attn_slab(c, carry):
        sl = pl.ds(pl.multiple_of(c * LANES, LANES), LANES)
        qc = qa_ref[:, sl]
        kc = ka_ref[:kv, sl]
        vc = va_ref[:kv, sl]
        mask = sel_ref[:, :kv]
        logits = []
        for hh in range(2):
            qm = jnp.where(_head_mask(qc.shape, A_HEAD_DIM, hh), qc, jnp.zeros_like(qc))
            logits.append(lax.dot_general(qm, kc, _NT, preferred_element_type=f32) + mask)
        o_ref[:, sl] = _two_head_softmax_pv(logits, vc, A_HEAD_DIM).astype(bf)
        return carry

    lax.fori_loop(0, A_WIDTH // LANES, attn_slab, 0)


def _per_query_block(block_fn, refs, n_blocks, tq):
    j = pl.program_id(1)
    for b in range(n_blocks):
        pl.when(j == b)(functools.partial(block_fn, *refs, kv=(b + 1) * tq))


def _dsa_kernel(*refs, seq, k_sel):
    tq = refs[0].shape[0]
    _per_query_block(functools.partial(_dsa_block, k_sel=k_sel), refs, seq // tq, tq)


def _dsa(qa, qi, wi, ka, va, ki, bsz, seq):
    tq = min(Q_TILE, seq)
    nq = seq // tq
    qspec = lambda w: pl.BlockSpec((tq, w), lambda b, i: (b * nq + i, 0))
    kspec = lambda w: pl.BlockSpec((seq, w), lambda b, i: (b, 0))
    k_sel = min(TOPK_MAX, seq // 4)
    return pl.pallas_call(
        functools.partial(_dsa_kernel, seq=seq, k_sel=k_sel),
        grid=(bsz, nq),
        in_specs=[qspec(A_WIDTH), qspec(A_WIDTH), qspec(LANES), kspec(A_WIDTH), kspec(A_WIDTH), kspec(LANES)],
        out_specs=qspec(A_WIDTH),
        out_shape=jax.ShapeDtypeStruct((bsz * seq, A_WIDTH), jnp.bfloat16),
        scratch_shapes=[pltpu.VMEM((tq, seq), jnp.float32), pltpu.VMEM((tq, seq), jnp.float32)],
        compiler_params=_cparams(("arbitrary", "arbitrary")),
        name="dsa",
    )(qa, qi, wi, ka, va, ki)


def _mla_block(qn_ref, qr_ref, kn_ref, kr_ref, vb_ref, o_ref, *, kv):
    tq = qn_ref.shape[0]
    f32 = jnp.float32
    bf = jnp.bfloat16
    diag = lax.broadcasted_iota(jnp.int32, (tq, tq), 1) <= lax.broadcasted_iota(jnp.int32, (tq, tq), 0)
    kr = kr_ref[:kv, :]

    def causal_masked(lg):
        own = jnp.where(diag, lg[:, kv - tq:], NEG_BIG)
        return own if kv == tq else jnp.concatenate([lg[:, :kv - tq], own], axis=1)

    def slab(c, carry):
        sl = pl.ds(pl.multiple_of(c * LANES, LANES), LANES)
        qc = qn_ref[:, sl]
        kcat = jnp.concatenate([kn_ref[:kv, sl], kr], axis=1)
        vc = vb_ref[:kv, sl]
        qrc = qr_ref[:, pl.ds(pl.multiple_of((c // 2) * LANES, LANES), LANES)]
        logits = []
        for hh in range(2):
            h = 2 * c + hh
            qm = jnp.where(_head_mask(qc.shape, QK_NOPE, hh), qc, jnp.zeros_like(qc))
            qrm = jnp.where(_head_mask(qrc.shape, QK_ROPE, h % 4), qrc, jnp.zeros_like(qrc))
            qcat = jnp.concatenate([qm, qrm], axis=1)
            lg = lax.dot_general(qcat, kcat, _NT, preferred_element_type=f32)
            logits.append(causal_masked(lg))
        o_ref[:, sl] = _two_head_softmax_pv(logits, vc, V_DIM).astype(bf)
        return carry

    lax.fori_loop(0, B_WIDTH // LANES, slab, 0, unroll=True)


def _mla_kernel(*refs, seq):
    tq = refs[0].shape[0]
    _per_query_block(_mla_block, refs, seq // tq, tq)


def _mla(qn, qr, kn, kr, vb, bsz, seq):
    tq = min(Q_TILE, seq)
    nq = seq // tq
    qspec = lambda w: pl.BlockSpec((tq, w), lambda b, i: (b * nq + i, 0))
    kspec = lambda w: pl.BlockSpec((seq, w), lambda b, i: (b, 0))
    return pl.pallas_call(
        functools.partial(_mla_kernel, seq=seq),
        grid=(bsz, nq),
        in_specs=[qspec(B_HEADS * QK_NOPE), qspec(B_HEADS * QK_ROPE), kspec(B_HEADS * QK_NOPE),
                  kspec(LANES), kspec(B_WIDTH)],
        out_specs=qspec(B_WIDTH),
        out_shape=jax.ShapeDtypeStruct((bsz * seq, B_WIDTH), jnp.bfloat16),
        compiler_params=_cparams(("arbitrary", "arbitrary")),
        name="mla",
    )(qn, qr, kn, kr, vb)


def _post_kernel(x_ref, oa_ref, ob_ref, wg_ref, bg_ref, wo_ref, g1_ref, b1_ref,
                 wrh_ref, wrl_ref, br_ref, h_ref, ri_ref, rf_ref, cnt_ref, carry_ref):
    f32 = jnp.float32
    bf = jnp.bfloat16
    t = x_ref.shape[0]

    @pl.when(pl.program_id(0) == 0)
    def _():
        carry_ref[...] = jnp.zeros_like(carry_ref)

    x = x_ref[...]
    gate = jax.nn.sigmoid(jnp.dot(x.astype(bf), wg_ref[...], preferred_element_type=f32) + bg_ref[...])
    pa = jnp.dot(oa_ref[...], wo_ref[:A_WIDTH, :], preferred_element_type=f32)
    pb = jnp.dot(ob_ref[...], wo_ref[A_WIDTH:A_WIDTH + B_WIDTH, :], preferred_element_type=f32)
    mix = gate[:, :D_MODEL] * pa + gate[:, D_MODEL:] * pb
    u = DN_ALPHA * x + jnp.dot(mix.astype(bf), wo_ref[A_WIDTH + B_WIDTH:, :], preferred_element_type=f32)
    h = _layer_norm(u, g1_ref[...], b1_ref[...])
    _store_token_rows(h_ref, h)

    h_hi = h.astype(bf)
    h_lo = (h - h_hi.astype(f32)).astype(bf)
    logits = (jnp.dot(h_hi, wrh_ref[...], preferred_element_type=f32)
              + jnp.dot(h_lo, wrh_ref[...], preferred_element_type=f32)
              + jnp.dot(h_hi, wrl_ref[...], preferred_element_type=f32) + br_ref[...])
    lane = lax.broadcasted_iota(jnp.int32, (t, LANES), 1).astype(f32)
    lg = jnp.where(lane < N_EXPERTS, logits, -jnp.inf)
    vals, ids = [], []
    assign = jnp.zeros((t, LANES), f32)
    for _k in range(TOP_K):
        m = jnp.max(lg, axis=1, keepdims=True)
        idx = jnp.min(jnp.where(lg == m, lane, float(LANES)), axis=1, keepdims=True)
        hit = lane == idx
        vals.append(m)
        ids.append(idx)
        assign = jnp.where(hit, 1.0, assign)
        lg = jnp.where(hit, -jnp.inf, lg)
    exps = [jnp.exp(v - vals[0]) for v in vals]
    den = exps[0] + exps[1] + exps[2] + exps[3]

    r = lax.broadcasted_iota(jnp.int32, (t, t), 0)
    c = lax.broadcasted_iota(jnp.int32, (t, t), 1)
    lower = jnp.where(c < r, 1.0, 0.0).astype(bf)
    before = jnp.dot(lower, assign.astype(bf), preferred_element_type=f32) + carry_ref[0:1, :]
    ri = jnp.zeros((t, LANES), f32)
    rf = jnp.zeros((t, LANES), f32)
    for k in range(TOP_K):
        rank = jnp.sum(jnp.where(lane == ids[k], before, 0.0), axis=1, keepdims=True)
        ri = jnp.where(lane == float(k), ids[k], ri)
        ri = jnp.where(lane == float(TOP_K + k), rank, ri)
        rf = jnp.where(lane == float(k), exps[k] / den, rf)
    ri_ref[...] = ri.T[:2 * TOP_K, :].astype(jnp.int32)
    rf_ref[...] = rf
    total = carry_ref[0:1, :] + jnp.sum(assign, axis=0, keepdims=True)
    carry_ref[...] = jnp.broadcast_to(total, carry_ref.shape)
    cnt_ref[...] = jnp.broadcast_to(total, cnt_ref.shape)


def _post(x2, oa, ob, w_slab, bg, wo, g1, b1, wrh, wrl, br):
    n = x2.shape[0]
    t = PROJ_TILE
    tok = lambda w: pl.BlockSpec((t, w), lambda i: (i, 0))
    consts = (bg, wo, g1, b1, wrh, wrl, br)
    gates_w = pl.BlockSpec((D_MODEL, 2 * D_MODEL), lambda i: (0, 0))
    return pl.pallas_call(
        _post_kernel,
        grid=(n // t,),
        in_specs=[tok(D_MODEL), tok(A_WIDTH), tok(B_WIDTH), gates_w] + [_full(a.shape) for a in consts],
        out_specs=[pl.BlockSpec((t * ROW_TILES, LANES), lambda i: (i, 0)),
                   pl.BlockSpec((2 * TOP_K, t), lambda i: (0, i)), tok(LANES), _full((8, LANES))],
        out_shape=[jax.ShapeDtypeStruct((n * ROW_TILES, LANES), jnp.float32),
                   jax.ShapeDtypeStruct((2 * TOP_K, n), jnp.int32),
                   jax.ShapeDtypeStruct((n, LANES), jnp.float32),
                   jax.ShapeDtypeStruct((8, LANES), jnp.float32)],
        scratch_shapes=[pltpu.VMEM((8, LANES), jnp.float32)],
        compiler_params=_cparams(("arbitrary",)),
        name="post",
    )(x2, oa, ob, w_slab, *consts)


def _token_row(ref, r):
    start = r * ROW_TILES if isinstance(r, int) else pl.multiple_of(r * ROW_TILES, ROW_TILES)
    return ref.at[pl.ds(start, ROW_TILES)]


def _row_gather(idx_ref, src_ref, dst_ref, sem, rows):
    for j in range(rows):
        pltpu.make_async_copy(_token_row(src_ref, idx_ref[0, 0, j]), _token_row(dst_ref, j), sem).start(
            priority=j % 2)


def _rows_wait(src_ref, dst_ref, sem):
    pltpu.make_async_copy(src_ref, dst_ref, sem).wait()


def _dispatch_kernel(dest_ref, h_ref, xs_ref, sem):
    t = DISPATCH_TILE
    for k in range(TOP_K):
        for j in range(t):
            pltpu.make_async_copy(_token_row(h_ref, j), _token_row(xs_ref, dest_ref[0, 0, k * t + j]), sem).start(
                priority=j % 2)
    for k in range(TOP_K):
        _rows_wait(h_ref, xs_ref.at[pl.ds(0, t * ROW_TILES)], sem)


def _dispatch(dest3, h, rows):
    t = DISPATCH_TILE
    n = h.shape[0] // ROW_TILES
    return pl.pallas_call(
        _dispatch_kernel,
        grid=(n // t,),
        in_specs=[pl.BlockSpec((1, 1, t * TOP_K), lambda i: (i, 0, 0), memory_space=pltpu.SMEM),
                  pl.BlockSpec((t * ROW_TILES, LANES), lambda i: (i, 0))],
        out_specs=pl.BlockSpec(memory_space=pl.ANY),
        out_shape=jax.ShapeDtypeStruct((rows * ROW_TILES, LANES), jnp.float32),
        scratch_shapes=[pltpu.SemaphoreType.DMA(())],
        compiler_params=_cparams(("arbitrary",)),
        name="dispatch",
    )(dest3, h)


def _expert_kernel(te_ref, tt_ref, lo_ref, hi_ref, ns_ref, xs_ref, wu_ref, bu_ref, wd_ref, bd_ref, ys_ref,
                   wub_ref, wdb_ref):
    i = pl.program_id(0)
    f32 = jnp.float32
    bf = jnp.bfloat16
    t = EXPERT_TILE
    valid = i < ns_ref[0]
    before = jnp.maximum(i - 1, 0)
    fresh = valid & ((i == 0) | (te_ref[i] != te_ref[before]))
    first_visit = (i == 0) | (tt_ref[i] != tt_ref[before])

    @pl.when(fresh)
    def _():
        rows = 128

        def cast_up(j, c):
            r = pl.multiple_of(j * rows, rows)
            wub_ref[pl.ds(r, rows), :] = wu_ref[pl.ds(r, rows), :].astype(bf)
            return c

        lax.fori_loop(0, D_MODEL // rows, cast_up, 0)

        def cast_dn(j, c):
            r = pl.multiple_of(j * rows, rows)
            wdb_ref[pl.ds(r, rows), :] = wd_ref[pl.ds(r, rows), :].astype(bf)
            return c

        lax.fori_loop(0, D_FF // rows, cast_dn, 0)

    def mlp(first, rows):
        xb = _load_token_rows(xs_ref, first, rows).astype(bf)
        y = jnp.broadcast_to(bd_ref[...], (rows, D_MODEL))
        for c in range(D_FF // FF_CHUNK):
            gs = slice(c * FF_CHUNK, (c + 1) * FF_CHUNK)
            ls = slice(D_FF + c * FF_CHUNK, D_FF + (c + 1) * FF_CHUNK)
            ag = jnp.dot(xb, wub_ref[:, gs], preferred_element_type=f32) + bu_ref[:, gs]
            al = jnp.dot(xb, wub_ref[:, ls], preferred_element_type=f32) + bu_ref[:, ls]
            g = jnp.minimum(ag, SWIGLU_LIMIT)
            lin = jnp.clip(al, -SWIGLU_LIMIT, SWIGLU_LIMIT)
            act = (lin + 1.0) * (g * jax.nn.sigmoid(SWIGLU_ALPHA * g))
            y = y + jnp.dot(act.astype(bf), wdb_ref[gs, :], preferred_element_type=f32)
        return y

    lo, hi = lo_ref[i], hi_ref[i]
    half = t // 2
    whole = (lo == 0) & (hi == t)
    partial = valid & jnp.logical_not(whole)

    @pl.when(valid & whole)
    def _():
        _store_token_rows(ys_ref, mlp(0, t))

    @pl.when(partial & first_visit)
    def _():
        ys_ref[...] = jnp.zeros_like(ys_ref)

    def fill(first, rows):
        row = first + lax.broadcasted_iota(jnp.int32, (rows, 1), 0)
        mine = (row >= lo) & (row < hi)
        _store_token_rows(ys_ref, jnp.where(mine, mlp(first, rows), _load_token_rows(ys_ref, first, rows)), first)

    pl.when(partial & (lo < half) & (hi > half))(functools.partial(fill, 0, t))
    pl.when(partial & (hi <= half))(functools.partial(fill, 0, half))
    pl.when(partial & (lo >= half))(functools.partial(fill, half, half))


def _experts(steps, xs, w_up, b_up, w_down, b_down):
    t = EXPERT_TILE
    max_steps = steps[0].shape[0]
    row_block = pl.BlockSpec((t * ROW_TILES, LANES), lambda i, te, tt, lo, hi, ns: (tt[i], 0))
    per_expert = lambda shape: pl.BlockSpec((None,) + shape, lambda i, te, tt, lo, hi, ns: (te[i], 0, 0))
    grid_spec = pltpu.PrefetchScalarGridSpec(
        num_scalar_prefetch=5,
        grid=(max_steps,),
        in_specs=[row_block, per_expert((D_MODEL, 2 * D_FF)), per_expert((1, 2 * D_FF)),
                  per_expert((D_FF, D_MODEL)), per_expert((1, D_MODEL))],
        out_specs=row_block,
        scratch_shapes=[pltpu.VMEM((D_MODEL, 2 * D_FF), jnp.bfloat16),
                        pltpu.VMEM((D_FF, D_MODEL), jnp.bfloat16)],
    )
    return pl.pallas_call(
        _expert_kernel,
        grid_spec=grid_spec,
        out_shape=jax.ShapeDtypeStruct(xs.shape, jnp.float32),
        compiler_params=_cparams(("arbitrary",)),
        name="experts",
    )(*steps, xs, w_up, b_up.reshape(N_EXPERTS, 1, 2 * D_FF), w_down, b_down.reshape(N_EXPERTS, 1, D_MODEL))


def _combine_kernel(cur_ref, nxt_ref, ys_ref, h_ref, rf_ref, g2_ref, b2_ref, o_ref, buf_ref, sems):
    i = pl.program_id(0)
    n_steps = pl.num_programs(0)
    t = COMBINE_TILE
    rows = t * TOP_K
    slot = i % 2

    @pl.when(i == 0)
    def _():
        _row_gather(cur_ref, ys_ref, buf_ref.at[0], sems.at[0], rows)

    _rows_wait(ys_ref.at[pl.ds(0, rows * ROW_TILES)], buf_ref.at[slot], sems.at[slot])
    _row_gather(nxt_ref, ys_ref, buf_ref.at[1 - slot], sems.at[1 - slot], rows)

    rf = rf_ref[...]
    u = DN_ALPHA * _load_token_rows(h_ref, 0, t)
    for k in range(TOP_K):
        u = u + rf[:, k:k + 1] * _load_token_rows(buf_ref.at[slot], k * t, t)
    o_ref[...] = _layer_norm(u, g2_ref[...], b2_ref[...])

    @pl.when(i == n_steps - 1)
    def _():
        _rows_wait(ys_ref.at[pl.ds(0, rows * ROW_TILES)], buf_ref.at[1 - slot], sems.at[1 - slot])


def _combine(dest3, ys, h, rf, g2, b2):
    t = COMBINE_TILE
    n = h.shape[0] // ROW_TILES
    last = n // t - 1
    tok = lambda w: pl.BlockSpec((t, w), lambda i: (i, 0))
    idx_spec = lambda f: pl.BlockSpec((1, 1, t * TOP_K), f, memory_space=pltpu.SMEM)
    return pl.pallas_call(
        _combine_kernel,
        grid=(n // t,),
        in_specs=[idx_spec(lambda i: (i, 0, 0)), idx_spec(lambda i: (jnp.minimum(i + 1, last), 0, 0)),
                  pl.BlockSpec(memory_space=pl.ANY), pl.BlockSpec((t * ROW_TILES, LANES), lambda i: (i, 0)),
                  tok(LANES), _full(g2.shape), _full(b2.shape)],
        out_specs=tok(D_MODEL),
        out_shape=jax.ShapeDtypeStruct((n, D_MODEL), jnp.float32),
        scratch_shapes=[pltpu.VMEM((2, t * TOP_K * ROW_TILES, LANES), jnp.float32),
                        pltpu.SemaphoreType.DMA((2,))],
        compiler_params=_cparams(("arbitrary",)),
        name="combine",
    )(dest3, dest3, ys, h, rf, g2, b2)


def _rope_tables(positions):
    pos = positions.astype(jnp.float32).reshape(-1, 1)
    halves = (A_HEAD_DIM // 2, QK_ROPE // 2)
    pieces, rows, src = [], [], 0
    for half in halves:
        ang = pos * ROPE_THETA ** (-jnp.arange(half, dtype=jnp.float32) / half)
        pieces += [jnp.cos(ang), jnp.sin(ang)]
        for _ in range(2):
            rows.append(src + np.arange(LANES) % half)
            src += half
    trig = jnp.concatenate(pieces + [jnp.zeros((pos.shape[0], LANES - src), jnp.float32)], axis=1)
    expand = (np.arange(LANES)[:, None] == np.concatenate(rows)[None, :]).astype(np.float32)
    return trig, jnp.asarray(expand, jnp.bfloat16)


def _lookup(table, idx):
    experts = jnp.arange(N_EXPERTS, dtype=jnp.int32)
    return jnp.sum(jnp.where(idx[..., None] == experts, table, 0), axis=-1)


def _expert_schedule(starts, ends, rows):
    t = EXPERT_TILE
    first_tile = starts // t
    n_tile = jnp.where(ends > starts, (ends - 1) // t - first_tile + 1, 0)
    cum = jnp.cumsum(n_tile)
    n_steps = cum[-1]
    max_steps = rows // t + N_EXPERTS
    s = jnp.minimum(jnp.arange(max_steps, dtype=jnp.int32), n_steps - 1)
    expert = jnp.sum((cum[None, :] <= s[:, None]).astype(jnp.int32), axis=1)
    tile = _lookup(first_tile, expert) + s - _lookup(cum - n_tile, expert)
    lo = jnp.maximum(_lookup(starts, expert) - tile * t, 0)
    hi = jnp.minimum(_lookup(ends, expert) - tile * t, t)
    return expert, tile, lo, hi, n_steps.astype(jnp.int32).reshape(1)


def _layer(x2, tabs, bsz, seq, w_in, b_gate, rms_cq, rms_ckv, w_uq, w_ukv, w_o_a, w_o_b, w_out,
           ln1_g, ln1_b, w_router, b_router, w_up, b_up, w_down, b_down, ln2_g, ln2_b):
    bf = jnp.bfloat16
    n = x2.shape[0]
    off = np.concatenate([[0], np.cumsum(SPLITS)]).tolist()
    w_in_b = w_in.astype(bf)
    col = lambda j: w_in_b[:, off[j]:off[j + 1]]
    zeros = lambda w: jnp.zeros((D_MODEL, w), bf)
    w_slab = jnp.concatenate(
        [col(9), zeros(PROJ_COLS - 2 * D_MODEL), w_in_b[:, :off[4]],
         col(4), col(4), jnp.tile(col(8), (1, LANES // QK_ROPE)), col(5), zeros(LANES - IDX_HEADS),
         col(6), col(7)], axis=1)
    w_o = jnp.concatenate([w_o_a, w_o_b, w_out], axis=0).astype(bf)
    uq = w_uq.reshape(Q_LORA, B_HEADS, QK_NOPE + QK_ROPE)
    wuq = jnp.concatenate([uq[:, :, :QK_NOPE].reshape(Q_LORA, -1), uq[:, :, QK_NOPE:].reshape(Q_LORA, -1)],
                          axis=1).astype(bf)
    ukv = w_ukv.reshape(KV_LORA, B_HEADS, QK_NOPE + V_DIM)
    wukv = jnp.concatenate([ukv[:, :, :QK_NOPE].reshape(KV_LORA, -1), ukv[:, :, QK_NOPE:].reshape(KV_LORA, -1)],
                           axis=1).astype(bf)

    qa, ka, va, qi, ki, kr, wi, qn, qr, kn, vb = _proj(
        x2, tabs, w_slab, rms_cq.reshape(1, -1), rms_ckv.reshape(1, -1), wuq, wukv)
    o_a = _dsa(qa, qi, wi, ka, va, ki, bsz, seq)
    o_b = _mla(qn, qr, kn, kr, vb, bsz, seq)

    wr = jnp.concatenate([w_router, jnp.zeros((D_MODEL, LANES - N_EXPERTS), w_router.dtype)], axis=1)
    br = jnp.concatenate([b_router, jnp.zeros((LANES - N_EXPERTS,), b_router.dtype)]).reshape(1, -1)
    wr_hi = wr.astype(bf)
    wr_lo = (wr - wr_hi.astype(jnp.float32)).astype(bf)
    h, ri, rf, cnt = _post(x2, o_a, o_b, w_slab, b_gate.reshape(1, -1), w_o, ln1_g.reshape(1, -1),
                           ln1_b.reshape(1, -1), wr_hi, wr_lo, br)

    counts = cnt[0, :N_EXPERTS].astype(jnp.int32)
    ends = jnp.cumsum(counts)
    starts = ends - counts
    dest = _lookup(starts, ri[:TOP_K]) + ri[TOP_K:]
    by_step = lambda t: dest.reshape(TOP_K, n // t, t).transpose(1, 0, 2).reshape(n // t, 1, t * TOP_K)

    xs = _dispatch(by_step(DISPATCH_TILE), h, n * TOP_K)
    ys = _experts(_expert_schedule(starts, ends, n * TOP_K), xs, w_up, b_up, w_down, b_down)
    return _combine(by_step(COMBINE_TILE), ys, h, rf, ln2_g.reshape(1, -1), ln2_b.reshape(1, -1))


def kernel(x, positions, w_in, b_gate, rms_cq, rms_ckv, w_uq, w_ukv, w_o_a, w_o_b, w_out, ln1_g, ln1_b,
           w_router, b_router, w_up, b_up, w_down, b_down, ln2_g, ln2_b):
    bsz, seq, _ = x.shape
    x2 = x.reshape(bsz * seq, D_MODEL)
    tabs = _rope_tables(positions)
    for l in range(DEPTH):
        x2 = _layer(x2, tabs, bsz, seq, w_in[l], b_gate[l], rms_cq[l], rms_ckv[l], w_uq[l], w_ukv[l],
                    w_o_a[l], w_o_b[l], w_out[l], ln1_g[l], ln1_b[l], w_router[l], b_router[l],
                    w_up[l], b_up[l], w_down[l], b_down[l], ln2_g[l], ln2_b[l])
    return x2.reshape(bsz, seq, D_MODEL)
```

```python
import functools

import jax
import jax.numpy as jnp
import numpy as np
from jax import lax
from jax.experimental import pallas as pl
from jax.experimental.pallas import tpu as pltpu

D_MODEL = 1024
A_HEADS = 8
A_HEAD_DIM = 64
IDX_HEADS = 8
IDX_DIM = 64
TOPK_MAX = 256
B_HEADS = 8
Q_LORA = 384
KV_LORA = 256
QK_NOPE = 64
QK_ROPE = 32
V_DIM = 64
ROPE_THETA = 10000.0
N_EXPERTS = 32
TOP_K = 4
D_FF = 1024
SWIGLU_LIMIT = 7.0
SWIGLU_ALPHA = 1.702
DEPTH = 1
DN_ALPHA = (2 * DEPTH) ** 0.25
LN_EPS = 1e-5
RMS_EPS = 1e-6
A_WIDTH = A_HEADS * A_HEAD_DIM
B_WIDTH = B_HEADS * V_DIM
SPLITS = (A_WIDTH, A_WIDTH, A_WIDTH, IDX_HEADS * IDX_DIM, IDX_DIM, IDX_HEADS,
          Q_LORA, KV_LORA, QK_ROPE, 2 * D_MODEL)
LANES = 128
VMEM_LIMIT_BYTES = 52 * 1024 * 1024
PROJ_COLS = 4 * A_WIDTH + 3 * LANES + Q_LORA + KV_LORA

PROJ_TILE = 512
Q_TILE = 256
SEARCH_GROUPS = 2
EXPERT_TILE = 512
FF_CHUNK = 1024
DISPATCH_TILE = 512
COMBINE_TILE = 256

ROW_TILES = D_MODEL // LANES
assert ROW_TILES == 8

LOG2_E = 1.4426950408889634
INT_MIN = -(2 ** 31)
NEG_BIG = -1e30

_NT = (((1,), (1,)), ((), ()))


def _cparams(sem):
    return pltpu.CompilerParams(dimension_semantics=sem, vmem_limit_bytes=VMEM_LIMIT_BYTES)


def _full(shape):
    nd = len(shape)
    return pl.BlockSpec(shape, lambda *_: (0,) * nd)


def _rope_chunk(x, cos, sin, half):
    lane = lax.broadcasted_iota(jnp.int32, x.shape, 1)
    first = (lane % (2 * half)) < half
    up = pltpu.roll(x, LANES - half, 1)
    dn = pltpu.roll(x, half, 1)
    return x * cos + jnp.where(first, -up, dn) * sin


def _layer_norm(u, g, b):
    mu = jnp.mean(u, axis=-1, keepdims=True)
    d = u - mu
    var = jnp.mean(d * d, axis=-1, keepdims=True)
    return d * lax.rsqrt(var + LN_EPS) * g + b


def _load_token_rows(ref, first, t):
    return jnp.concatenate([ref[pl.ds(first * ROW_TILES + s, t, stride=ROW_TILES), :] for s in range(ROW_TILES)],
                           axis=1)


def _store_token_rows(ref, v, first=0):
    t = v.shape[0]
    for s in range(ROW_TILES):
        ref[pl.ds(first * ROW_TILES + s, t, stride=ROW_TILES), :] = v[:, s * LANES:(s + 1) * LANES]


def _two_head_softmax_pv(logits, vc, width):
    lane_v = lax.broadcasted_iota(jnp.int32, vc.shape, 1)
    pvs = []
    for hh in range(2):
        lg = logits[hh]
        p = jnp.exp2(lg - jnp.max(lg, axis=1, keepdims=True)).astype(jnp.bfloat16)
        v_h = jnp.where((lane_v // width) == hh, vc, jnp.ones_like(vc))
        pvs.append(jnp.dot(p, v_h, preferred_element_type=jnp.float32))
    lane = lax.broadcasted_iota(jnp.int32, pvs[0].shape, 1)
    first = lane < width
    num = jnp.where(first, pvs[0], pvs[1])
    den = jnp.where(first, pltpu.roll(pvs[0], width, 1), pltpu.roll(pvs[1], width, 1))
    return num / den


def _head_mask(shape, width, which):
    lane = lax.broadcasted_iota(jnp.int32, shape, 1)
    return (lane // width) == which


def _proj_kernel(x_ref, trig_ref, expand_ref, w_ref,
                 gcq_ref, gckv_ref, wuq_ref, wukv_ref,
                 qa_ref, ka_ref, va_ref, qi_ref, ki_ref, kr_ref, wi_ref,
                 qn_ref, qr_ref, kn_ref, vb_ref):
    xb = x_ref[...].astype(jnp.bfloat16)
    f32 = jnp.float32
    bf = jnp.bfloat16

    trig = trig_ref[...].T
    trig_hi = trig.astype(bf)
    trig_lo = (trig - trig_hi.astype(f32)).astype(bf)
    tabs = (jnp.dot(trig_hi, expand_ref[...], preferred_element_type=f32)
            + jnp.dot(trig_lo, expand_ref[...], preferred_element_type=f32))
    c64, s64, c32, s32 = (tabs[:, j * LANES:(j + 1) * LANES] for j in range(4))

    z1 = jnp.dot(xb, w_ref[:, :4 * A_WIDTH], preferred_element_type=f32)
    n_chunk = A_WIDTH // LANES
    a_scale = A_HEAD_DIM ** -0.5 * LOG2_E
    for c in range(n_chunk):
        sl = slice(c * LANES, (c + 1) * LANES)
        qa_ref[:, sl] = (_rope_chunk(z1[:, c * LANES:(c + 1) * LANES], c64, s64, 32) * a_scale).astype(bf)
        o = A_WIDTH + c * LANES
        ka_ref[:, sl] = _rope_chunk(z1[:, o:o + LANES], c64, s64, 32).astype(bf)
        o = 2 * A_WIDTH + c * LANES
        va_ref[:, sl] = z1[:, o:o + LANES].astype(bf)
        o = 3 * A_WIDTH + c * LANES
        qi_ref[:, sl] = _rope_chunk(z1[:, o:o + LANES], c64, s64, 32).astype(bf)

    small = 4 * A_WIDTH + 3 * LANES
    zs = jnp.dot(xb, w_ref[:, 4 * A_WIDTH:small], preferred_element_type=f32)
    ki_ref[...] = _rope_chunk(zs[:, 0:LANES], c64, s64, 32).astype(bf)
    kr_ref[...] = _rope_chunk(zs[:, LANES:2 * LANES], c32, s32, 16).astype(bf)
    wi_ref[...] = zs[:, 2 * LANES:3 * LANES] * (IDX_HEADS ** -0.5 * IDX_DIM ** -0.5)

    zl = jnp.dot(xb, w_ref[:, small:], preferred_element_type=f32)
    cq = zl[:, :Q_LORA]
    cq = cq * lax.rsqrt(jnp.mean(cq * cq, axis=-1, keepdims=True) + RMS_EPS) * gcq_ref[...]
    ckv = zl[:, Q_LORA:]
    ckv = ckv * lax.rsqrt(jnp.mean(ckv * ckv, axis=-1, keepdims=True) + RMS_EPS) * gckv_ref[...]
    qb = jnp.dot(cq.astype(bf), wuq_ref[...], preferred_element_type=f32)
    kv = jnp.dot(ckv.astype(bf), wukv_ref[...], preferred_element_type=f32)
    b_scale = (QK_NOPE + QK_ROPE) ** -0.5 * LOG2_E
    qn_ref[...] = (qb[:, :B_HEADS * QK_NOPE] * b_scale).astype(bf)
    for c in range(B_HEADS * QK_ROPE // LANES):
        o = B_HEADS * QK_NOPE + c * LANES
        qr_ref[:, c * LANES:(c + 1) * LANES] = (
            _rope_chunk(qb[:, o:o + LANES], c32, s32, 16) * b_scale).astype(bf)
    kn_ref[...] = kv[:, :B_HEADS * QK_NOPE].astype(bf)
    vb_ref[...] = kv[:, B_HEADS * QK_NOPE:].astype(bf)


def _proj(x2, tabs, w_slab, gcq, gckv, wuq, wukv):
    n = x2.shape[0]
    t = PROJ_TILE
    tok = lambda w: pl.BlockSpec((t, w), lambda i: (i, 0))
    bf = jnp.bfloat16
    outs = [(A_WIDTH, bf)] * 4 + [(LANES, bf), (LANES, bf), (LANES, jnp.float32),
                                  (B_HEADS * QK_NOPE, bf), (B_HEADS * QK_ROPE, bf),
                                  (B_HEADS * QK_NOPE, bf), (B_WIDTH, bf)]
    return pl.pallas_call(
        _proj_kernel,
        grid=(n // t,),
        in_specs=[tok(D_MODEL), pl.BlockSpec((LANES, t), lambda i: (0, i)), _full(tabs[1].shape),
                  pl.BlockSpec((D_MODEL, PROJ_COLS), lambda i: (0, 1))]
        + [_full(a.shape) for a in (gcq, gckv, wuq, wukv)],
        out_specs=[tok(w) for w, _ in outs],
        out_shape=[jax.ShapeDtypeStruct((n, w), d) for w, d in outs],
        compiler_params=_cparams(("arbitrary",)),
        name="proj",
    )(x2, *tabs, w_slab, gcq, gckv, wuq, wukv)


def _tie_select(eq, gt, need, kv):
    chunk = 256
    r = lax.broadcasted_iota(jnp.int32, (chunk, chunk), 0)
    c = lax.broadcasted_iota(jnp.int32, (chunk, chunk), 1)
    upper = jnp.where(r <= c, 1.0, 0.0).astype(jnp.bfloat16)
    carry = jnp.zeros((eq.shape[0], 1), jnp.float32)
    parts = []
    for j in range(kv // chunk):
        e = eq[:, j * chunk:(j + 1) * chunk]
        ef = jnp.where(e, 1.0, 0.0).astype(jnp.bfloat16)
        pref = jnp.dot(ef, upper, preferred_element_type=jnp.float32) + carry
        parts.append(gt[:, j * chunk:(j + 1) * chunk] | (e & (pref <= need)))
        carry = carry + jnp.sum(ef.astype(jnp.float32), axis=1, keepdims=True)
    return jnp.concatenate(parts, axis=1)


def _key_to_float(key):
    return lax.bitcast_convert_type(key ^ ((key >> 31) & jnp.int32(0x7FFFFFFF)), jnp.float32)


def _select_topk(score, k_eff, sel_ref, kv):
    tq = score.shape[0]
    rows = tq // SEARCH_GROUPS
    parts = [(score[g * rows:(g + 1) * rows], k_eff[g * rows:(g + 1) * rows]) for g in range(SEARCH_GROUPS)]

    def search(it, prefixes):
        bit = lax.shift_left(jnp.int32(1), 31 - it)
        out = []
        for (s, k), prefix in zip(parts, prefixes):
            cand = prefix + bit
            cnt = jnp.sum(jnp.where(s >= _key_to_float(cand), 1.0, 0.0), axis=1, keepdims=True)
            out.append(jnp.where(cnt >= k, cand, prefix))
        return tuple(out)

    init = tuple(jnp.full((rows, 1), INT_MIN, jnp.int32) for _ in range(SEARCH_GROUPS))
    thr = _key_to_float(jnp.concatenate(lax.fori_loop(0, 32, search, init, unroll=4), axis=0))
    ge = score >= thr
    n_ge = jnp.sum(jnp.where(ge, 1.0, 0.0), axis=1, keepdims=True)
    sel_ref[:, :kv] = jnp.where(ge, 0.0, NEG_BIG)
    has_tie = jnp.max(jnp.where(n_ge != k_eff, 1.0, 0.0)) > 0.5

    @pl.when(has_tie)
    def _():
        gt = score > thr
        need = k_eff - jnp.sum(jnp.where(gt, 1.0, 0.0), axis=1, keepdims=True)
        sel_ref[:, :kv] = jnp.where(_tie_select(score == thr, gt, need, kv), 0.0, NEG_BIG)


def _dsa_block(qa_ref, qi_ref, wi_ref, ka_ref, va_ref, ki_ref, o_ref, sel_ref, score_ref, *, kv, k_sel):
    tq = qa_ref.shape[0]
    f32 = jnp.float32
    bf = jnp.bfloat16
    q_pos = (kv - tq) + lax.broadcasted_iota(jnp.int32, (tq, 1), 0)
    key_pos = lax.broadcasted_iota(jnp.int32, (tq, kv), 1)
    causal = key_pos <= q_pos

    if kv <= k_sel:
        sel_ref[:, :kv] = jnp.where(causal, 0.0, NEG_BIG)
    else:
        ki = ki_ref[:kv, :]
        lane = lax.broadcasted_iota(jnp.int32, (tq, LANES), 1)
        score_ref[:, :kv] = jnp.zeros((tq, kv), f32)

        def idx_slab(c, carry):
            qc = qi_ref[:, pl.ds(pl.multiple_of(c * LANES, LANES), LANES)]
            acc = score_ref[:, :kv]
            for hh in range(2):
                qm = jnp.where(_head_mask(qc.shape, IDX_DIM, hh), qc, jnp.zeros_like(qc))
                s = lax.dot_general(qm, ki, _NT, preferred_element_type=f32)
                w = jnp.sum(jnp.where(lane == 2 * c + hh, wi_ref[...], 0.0), axis=1, keepdims=True)
                acc = acc + jnp.maximum(s, 0.0) * w
            score_ref[:, :kv] = acc
            return carry

        lax.fori_loop(0, IDX_HEADS // 2, idx_slab, 0, unroll=2)
        score = jnp.where(causal, score_ref[:, :kv], -jnp.inf)
        k_eff = jnp.minimum(q_pos + 1, k_sel).astype(f32)
        _select_topk(score, k_eff, sel_ref, kv)

    def attn_slab(c, carry):
        sl = pl.ds(pl.multiple_of(c * LANES, LANES), LANES)
        qc = qa_ref[:, sl]
        kc = ka_ref[:kv, sl]
        vc = va_ref[:kv, sl]
        mask = sel_ref[:, :kv]
        logits = []
        for hh in range(2):
            qm = jnp.where(_head_mask(qc.shape, A_HEAD_DIM, hh), qc, jnp.zeros_like(qc))
            logits.append(lax.dot_general(qm, kc, _NT, preferred_element_type=f32) + mask)
        o_ref[:, sl] = _two_head_softmax_pv(logits, vc, A_HEAD_DIM).astype(bf)
        return carry

    lax.fori_loop(0, A_WIDTH // LANES, attn_slab, 0)


def _per_query_block(block_fn, refs, n_blocks, tq):
    j = pl.program_id(1)
    for b in range(n_blocks):
        pl.when(j == b)(functools.partial(block_fn, *refs, kv=(b + 1) * tq))


def _dsa_kernel(*refs, seq, k_sel):
    tq = refs[0].shape[0]
    _per_query_block(functools.partial(_dsa_block, k_sel=k_sel), refs, seq // tq, tq)


def _dsa(qa, qi, wi, ka, va, ki, bsz, seq):
    tq = min(Q_TILE, seq)
    nq = seq // tq
    qspec = lambda w: pl.BlockSpec((tq, w), lambda b, i: (b * nq + i, 0))
    kspec = lambda w: pl.BlockSpec((seq, w), lambda b, i: (b, 0))
    k_sel = min(TOPK_MAX, seq // 4)
    return pl.pallas_call(
        functools.partial(_dsa_kernel, seq=seq, k_sel=k_sel),
        grid=(bsz, nq),
        in_specs=[qspec(A_WIDTH), qspec(A_WIDTH), qspec(LANES), kspec(A_WIDTH), kspec(A_WIDTH), kspec(LANES)],
        out_specs=qspec(A_WIDTH),
        out_shape=jax.ShapeDtypeStruct((bsz * seq, A_WIDTH), jnp.bfloat16),
        scratch_shapes=[pltpu.VMEM((tq, seq), jnp.float32), pltpu.VMEM((tq, seq), jnp.float32)],
        compiler_params=_cparams(("arbitrary", "arbitrary")),
        name="dsa",
    )(qa, qi, wi, ka, va, ki)


def _mla_block(qn_ref, qr_ref, kn_ref, kr_ref, vb_ref, o_ref, *, kv):
    tq = qn_ref.shape[0]
    f32 = jnp.float32
    bf = jnp.bfloat16
    diag = lax.broadcasted_iota(jnp.int32, (tq, tq), 1) <= lax.broadcasted_iota(jnp.int32, (tq, tq), 0)
    kr = kr_ref[:kv, :]

    def causal_masked(lg):
        own = jnp.where(diag, lg[:, kv - tq:], NEG_BIG)
        return own if kv == tq else jnp.concatenate([lg[:, :kv - tq], own], axis=1)

    def slab(c, carry):
        sl = pl.ds(pl.multiple_of(c * LANES, LANES), LANES)
        qc = qn_ref[:, sl]
        kcat = jnp.concatenate([kn_ref[:kv, sl], kr], axis=1)
        vc = vb_ref[:kv, sl]
        qrc = qr_ref[:, pl.ds(pl.multiple_of((c // 2) * LANES, LANES), LANES)]
        logits = []
        for hh in range(2):
            h = 2 * c + hh
            qm = jnp.where(_head_mask(qc.shape, QK_NOPE, hh), qc, jnp.zeros_like(qc))
            qrm = jnp.where(_head_mask(qrc.shape, QK_ROPE, h % 4), qrc, jnp.zeros_like(qrc))
            qcat = jnp.concatenate([qm, qrm], axis=1)
            lg = lax.dot_general(qcat, kcat, _NT, preferred_element_type=f32)
            logits.append(causal_masked(lg))
        o_ref[:, sl] = _two_head_softmax_pv(logits, vc, V_DIM).astype(bf)
        return carry

    lax.fori_loop(0, B_WIDTH // LANES, slab, 0, unroll=True)


def _mla_kernel(*refs, seq):
    tq = refs[0].shape[0]
    _per_query_block(_mla_block, refs, seq // tq, tq)


def _mla(qn, qr, kn, kr, vb, bsz, seq):
    tq = min(Q_TILE, seq)
    nq = seq // tq
    qspec = lambda w: pl.BlockSpec((tq, w), lambda b, i: (b * nq + i, 0))
    kspec = lambda w: pl.BlockSpec((seq, w), lambda b, i: (b, 0))
    return pl.pallas_call(
        functools.partial(_mla_kernel, seq=seq),
        grid=(bsz, nq),
        in_specs=[qspec(B_HEADS * QK_NOPE), qspec(B_HEADS * QK_ROPE), kspec(B_HEADS * QK_NOPE),
                  kspec(LANES), kspec(B_WIDTH)],
        out_specs=qspec(B_WIDTH),
        out_shape=jax.ShapeDtypeStruct((bsz * seq, B_WIDTH), jnp.bfloat16),
        compiler_params=_cparams(("arbitrary", "arbitrary")),
        name="mla",
    )(qn, qr, kn, kr, vb)


def _post_kernel(x_ref, oa_ref, ob_ref, wg_ref, bg_ref, wo_ref, g1_ref, b1_ref,
                 wrh_ref, wrl_ref, br_ref, h_ref, ri_ref, rf_ref, cnt_ref, carry_ref):
    f32 = jnp.float32
    bf = jnp.bfloat16
    t = x_ref.shape[0]

    @pl.when(pl.program_id(0) == 0)
    def _():
        carry_ref[...] = jnp.zeros_like(carry_ref)

    x = x_ref[...]
    gate = jax.nn.sigmoid(jnp.dot(x.astype(bf), wg_ref[...], preferred_element_type=f32) + bg_ref[...])
    pa = jnp.dot(oa_ref[...], wo_ref[:A_WIDTH, :], preferred_element_type=f32)
    pb = jnp.dot(ob_ref[...], wo_ref[A_WIDTH:A_WIDTH + B_WIDTH, :], preferred_element_type=f32)
    mix = gate[:, :D_MODEL] * pa + gate[:, D_MODEL:] * pb
    u = DN_ALPHA * x + jnp.dot(mix.astype(bf), wo_ref[A_WIDTH + B_WIDTH:, :], preferred_element_type=f32)
    h = _layer_norm(u, g1_ref[...], b1_ref[...])
    _store_token_rows(h_ref, h)

    h_hi = h.astype(bf)
    h_lo = (h - h_hi.astype(f32)).astype(bf)
    logits = (jnp.dot(h_hi, wrh_ref[...], preferred_element_type=f32)
              + jnp.dot(h_lo, wrh_ref[...], preferred_element_type=f32)
              + jnp.dot(h_hi, wrl_ref[...], preferred_element_type=f32) + br_ref[...])
    lane = lax.broadcasted_iota(jnp.int32, (t, LANES), 1).astype(f32)
    lg = jnp.where(lane < N_EXPERTS, logits, -jnp.inf)
    vals, ids = [], []
    assign = jnp.zeros((t, LANES), f32)
    for _k in range(TOP_K):
        m = jnp.max(lg, axis=1, keepdims=True)
        idx = jnp.min(jnp.where(lg == m, lane, float(LANES)), axis=1, keepdims=True)
        hit = lane == idx
        vals.append(m)
        ids.append(idx)
        assign = jnp.where(hit, 1.0, assign)
        lg = jnp.where(hit, -jnp.inf, lg)
    exps = [jnp.exp(v - vals[0]) for v in vals]
    den = exps[0] + exps[1] + exps[2] + exps[3]

    r = lax.broadcasted_iota(jnp.int32, (t, t), 0)
    c = lax.broadcasted_iota(jnp.int32, (t, t), 1)
    lower = jnp.where(c < r, 1.0, 0.0).astype(bf)
    before = jnp.dot(lower, assign.astype(bf), preferred_element_type=f32) + carry_ref[0:1, :]
    ri = jnp.zeros((t, LANES), f32)
    rf = jnp.zeros((t, LANES), f32)
    for k in range(TOP_K):
        rank = jnp.sum(jnp.where(lane == ids[k], before, 0.0), axis=1, keepdims=True)
        ri = jnp.where(lane == float(k), ids[k], ri)
        ri = jnp.where(lane == float(TOP_K + k), rank, ri)
        rf = jnp.where(lane == float(k), exps[k] / den, rf)
    ri_ref[...] = ri.T[:2 * TOP_K, :].astype(jnp.int32)
    rf_ref[...] = rf
    total = carry_ref[0:1, :] + jnp.sum(assign, axis=0, keepdims=True)
    carry_ref[...] = jnp.broadcast_to(total, carry_ref.shape)
    cnt_ref[...] = jnp.broadcast_to(total, cnt_ref.shape)


def _post(x2, oa, ob, w_slab, bg, wo, g1, b1, wrh, wrl, br):
    n = x2.shape[0]
    t = PROJ_TILE
    tok = lambda w: pl.BlockSpec((t, w), lambda i: (i, 0))
    consts = (bg, wo, g1, b1, wrh, wrl, br)
    gates_w = pl.BlockSpec((D_MODEL, 2 * D_MODEL), lambda i: (0, 0))
    return pl.pallas_call(
        _post_kernel,
        grid=(n // t,),
        in_specs=[tok(D_MODEL), tok(A_WIDTH), tok(B_WIDTH), gates_w] + [_full(a.shape) for a in consts],
        out_specs=[pl.BlockSpec((t * ROW_TILES, LANES), lambda i: (i, 0)),
                   pl.BlockSpec((2 * TOP_K, t), lambda i: (0, i)), tok(LANES), _full((8, LANES))],
        out_shape=[jax.ShapeDtypeStruct((n * ROW_TILES, LANES), jnp.float32),
                   jax.ShapeDtypeStruct((2 * TOP_K, n), jnp.int32),
                   jax.ShapeDtypeStruct((n, LANES), jnp.float32),
                   jax.ShapeDtypeStruct((8, LANES), jnp.float32)],
        scratch_shapes=[pltpu.VMEM((8, LANES), jnp.float32)],
        compiler_params=_cparams(("arbitrary",)),
        name="post",
    )(x2, oa, ob, w_slab, *consts)


def _token_row(ref, r):
    start = r * ROW_TILES if isinstance(r, int) else pl.multiple_of(r * ROW_TILES, ROW_TILES)
    return ref.at[pl.ds(start, ROW_TILES)]


def _row_gather(idx_ref, src_ref, dst_ref, sem, rows):
    for j in range(rows):
        pltpu.make_async_copy(_token_row(src_ref, idx_ref[0, 0, j]), _token_row(dst_ref, j), sem).start(
            priority=j % 2)


def _rows_wait(src_ref, dst_ref, sem):
    pltpu.make_async_copy(src_ref, dst_ref, sem).wait()


def _dispatch_kernel(dest_ref, h_ref, xs_ref, sem):
    t = DISPATCH_TILE
    for k in range(TOP_K):
        for j in range(t):
            pltpu.make_async_copy(_token_row(h_ref, j), _token_row(xs_ref, dest_ref[0, 0, k * t + j]), sem).start(
                priority=j % 2)
    for k in range(TOP_K):
        _rows_wait(h_ref, xs_ref.at[pl.ds(0, t * ROW_TILES)], sem)


def _dispatch(dest3, h, rows):
    t = DISPATCH_TILE
    n = h.shape[0] // ROW_TILES
    return pl.pallas_call(
        _dispatch_kernel,
        grid=(n // t,),
        in_specs=[pl.BlockSpec((1, 1, t * TOP_K), lambda i: (i, 0, 0), memory_space=pltpu.SMEM),
                  pl.BlockSpec((t * ROW_TILES, LANES), lambda i: (i, 0))],
        out_specs=pl.BlockSpec(memory_space=pl.ANY),
        out_shape=jax.ShapeDtypeStruct((rows * ROW_TILES, LANES), jnp.float32),
        scratch_shapes=[pltpu.SemaphoreType.DMA(())],
        compiler_params=_cparams(("arbitrary",)),
        name="dispatch",
    )(dest3, h)


def _expert_kernel(te_ref, tt_ref, lo_ref, hi_ref, ns_ref, xs_ref, wu_ref, bu_ref, wd_ref, bd_ref, ys_ref,
                   wub_ref, wdb_ref):
    i = pl.program_id(0)
    f32 = jnp.float32
    bf = jnp.bfloat16
    t = EXPERT_TILE
    valid = i < ns_ref[0]
    before = jnp.maximum(i - 1, 0)
    fresh = valid & ((i == 0) | (te_ref[i] != te_ref[before]))
    first_visit = (i == 0) | (tt_ref[i] != tt_ref[before])

    @pl.when(fresh)
    def _():
        rows = 128

        def cast_up(j, c):
            r = pl.multiple_of(j * rows, rows)
            wub_ref[pl.ds(r, rows), :] = wu_ref[pl.ds(r, rows), :].astype(bf)
            return c

        lax.fori_loop(0, D_MODEL // rows, cast_up, 0)

        def cast_dn(j, c):
            r = pl.multiple_of(j * rows, rows)
            wdb_ref[pl.ds(r, rows), :] = wd_ref[pl.ds(r, rows), :].astype(bf)
            return c

        lax.fori_loop(0, D_FF // rows, cast_dn, 0)

    def mlp(first, rows):
        xb = _load_token_rows(xs_ref, first, rows).astype(bf)
        y = jnp.broadcast_to(bd_ref[...], (rows, D_MODEL))
        for c in range(D_FF // FF_CHUNK):
            gs = slice(c * FF_CHUNK, (c + 1) * FF_CHUNK)
            ls = slice(D_FF + c * FF_CHUNK, D_FF + (c + 1) * FF_CHUNK)
            ag = jnp.dot(xb, wub_ref[:, gs], preferred_element_type=f32) + bu_ref[:, gs]
            al = jnp.dot(xb, wub_ref[:, ls], preferred_element_type=f32) + bu_ref[:, ls]
            g = jnp.minimum(ag, SWIGLU_LIMIT)
            lin = jnp.clip(al, -SWIGLU_LIMIT, SWIGLU_LIMIT)
            act = (lin + 1.0) * (g * jax.nn.sigmoid(SWIGLU_ALPHA * g))
            y = y + jnp.dot(act.astype(bf), wdb_ref[gs, :], preferred_element_type=f32)
        return y

    lo, hi = lo_ref[i], hi_ref[i]
    half = t // 2
    whole = (lo == 0) & (hi == t)
    partial = valid & jnp.logical_not(whole)

    @pl.when(valid & whole)
    def _():
        _store_token_rows(ys_ref, mlp(0, t))

    @pl.when(partial & first_visit)
    def _():
        ys_ref[...] = jnp.zeros_like(ys_ref)

    def fill(first, rows):
        row = first + lax.broadcasted_iota(jnp.int32, (rows, 1), 0)
        mine = (row >= lo) & (row < hi)
        _store_token_rows(ys_ref, jnp.where(mine, mlp(first, rows), _load_token_rows(ys_ref, first, rows)), first)

    pl.when(partial & (lo < half) & (hi > half))(functools.partial(fill, 0, t))
    pl.when(partial & (hi <= half))(functools.partial(fill, 0, half))
    pl.when(partial & (lo >= half))(functools.partial(fill, half, half))


def _experts(steps, xs, w_up, b_up, w_down, b_down):
    t = EXPERT_TILE
    max_steps = steps[0].shape[0]
    row_block = pl.BlockSpec((t * ROW_TILES, LANES), lambda i, te, tt, lo, hi, ns: (tt[i], 0))
    per_expert = lambda shape: pl.BlockSpec((None,) + shape, lambda i, te, tt, lo, hi, ns: (te[i], 0, 0))
    grid_spec = pltpu.PrefetchScalarGridSpec(
        num_scalar_prefetch=5,
        grid=(max_steps,),
        in_specs=[row_block, per_expert((D_MODEL, 2 * D_FF)), per_expert((1, 2 * D_FF)),
                  per_expert((D_FF, D_MODEL)), per_expert((1, D_MODEL))],
        out_specs=row_block,
        scratch_shapes=[pltpu.VMEM((D_MODEL, 2 * D_FF), jnp.bfloat16),
                        pltpu.VMEM((D_FF, D_MODEL), jnp.bfloat16)],
    )
    return pl.pallas_call(
        _expert_kernel,
        grid_spec=grid_spec,
        out_shape=jax.ShapeDtypeStruct(xs.shape, jnp.float32),
        compiler_params=_cparams(("arbitrary",)),
        name="experts",
    )(*steps, xs, w_up, b_up.reshape(N_EXPERTS, 1, 2 * D_FF), w_down, b_down.reshape(N_EXPERTS, 1, D_MODEL))


def _combine_kernel(cur_ref, nxt_ref, ys_ref, h_ref, rf_ref, g2_ref, b2_ref, o_ref, buf_ref, sems):
    i = pl.program_id(0)
    n_steps = pl.num_programs(0)
    t = COMBINE_TILE
    rows = t * TOP_K
    slot = i % 2

    @pl.when(i == 0)
    def _():
        _row_gather(cur_ref, ys_ref, buf_ref.at[0], sems.at[0], rows)

    _rows_wait(ys_ref.at[pl.ds(0, rows * ROW_TILES)], buf_ref.at[slot], sems.at[slot])
    _row_gather(nxt_ref, ys_ref, buf_ref.at[1 - slot], sems.at[1 - slot], rows)

    rf = rf_ref[...]
    u = DN_ALPHA * _load_token_rows(h_ref, 0, t)
    for k in range(TOP_K):
        u = u + rf[:, k:k + 1] * _load_token_rows(buf_ref.at[slot], k * t, t)
    o_ref[...] = _layer_norm(u, g2_ref[...], b2_ref[...])

    @pl.when(i == n_steps - 1)
    def _():
        _rows_wait(ys_ref.at[pl.ds(0, rows * ROW_TILES)], buf_ref.at[1 - slot], sems.at[1 - slot])


def _combine(dest3, ys, h, rf, g2, b2):
    t = COMBINE_TILE
    n = h.shape[0] // ROW_TILES
    last = n // t - 1
    tok = lambda w: pl.BlockSpec((t, w), lambda i: (i, 0))
    idx_spec = lambda f: pl.BlockSpec((1, 1, t * TOP_K), f, memory_space=pltpu.SMEM)
    return pl.pallas_call(
        _combine_kernel,
        grid=(n // t,),
        in_specs=[idx_spec(lambda i: (i, 0, 0)), idx_spec(lambda i: (jnp.minimum(i + 1, last), 0, 0)),
                  pl.BlockSpec(memory_space=pl.ANY), pl.BlockSpec((t * ROW_TILES, LANES), lambda i: (i, 0)),
                  tok(LANES), _full(g2.shape), _full(b2.shape)],
        out_specs=tok(D_MODEL),
        out_shape=jax.ShapeDtypeStruct((n, D_MODEL), jnp.float32),
        scratch_shapes=[pltpu.VMEM((2, t * TOP_K * ROW_TILES, LANES), jnp.float32),
                        pltpu.SemaphoreType.DMA((2,))],
        compiler_params=_cparams(("arbitrary",)),
        name="combine",
    )(dest3, dest3, ys, h, rf, g2, b2)


def _rope_tables(positions):
    pos = positions.astype(jnp.float32).reshape(1, -1)
    halves = (A_HEAD_DIM // 2, QK_ROPE // 2)
    pieces, rows, src = [], [], 0
    for half in halves:
        ang = (ROPE_THETA ** (-jnp.arange(half, dtype=jnp.float32) / half)).reshape(-1, 1) * pos
        pieces += [jnp.cos(ang), jnp.sin(ang)]
        for _ in range(2):
            rows.append(src + np.arange(LANES) % half)
            src += half
    trig = jnp.concatenate(pieces + [jnp.zeros((LANES - src, pos.shape[1]), jnp.float32)], axis=0)
    expand = (np.arange(LANES)[:, None] == np.concatenate(rows)[None, :]).astype(np.float32)
    return trig, jnp.asarray(expand, jnp.bfloat16)


def _lookup(table, idx):
    experts = jnp.arange(N_EXPERTS, dtype=jnp.int32)
    return jnp.sum(jnp.where(idx[..., None] == experts, table, 0), axis=-1)


def _expert_schedule(starts, ends, rows):
    t = EXPERT_TILE
    first_tile = starts // t
    n_tile = jnp.where(ends > starts, (ends - 1) // t - first_tile + 1, 0)
    cum = jnp.cumsum(n_tile)
    n_steps = cum[-1]
    max_steps = rows // t + N_EXPERTS
    s = jnp.minimum(jnp.arange(max_steps, dtype=jnp.int32), n_steps - 1)
    expert = jnp.sum((cum[None, :] <= s[:, None]).astype(jnp.int32), axis=1)
    tile = _lookup(first_tile, expert) + s - _lookup(cum - n_tile, expert)
    lo = jnp.maximum(_lookup(starts, expert) - tile * t, 0)
    hi = jnp.minimum(_lookup(ends, expert) - tile * t, t)
    return expert, tile, lo, hi, n_steps.astype(jnp.int32).reshape(1)


def _layer(x2, tabs, bsz, seq, w_in, b_gate, rms_cq, rms_ckv, w_uq, w_ukv, w_o_a, w_o_b, w_out,
           ln1_g, ln1_b, w_router, b_router, w_up, b_up, w_down, b_down, ln2_g, ln2_b):
    bf = jnp.bfloat16
    n = x2.shape[0]
    off = np.concatenate([[0], np.cumsum(SPLITS)]).tolist()
    w_in_b = w_in.astype(bf)
    col = lambda j: w_in_b[:, off[j]:off[j + 1]]
    zeros = lambda w: jnp.zeros((D_MODEL, w), bf)
    w_slab = jnp.concatenate(
        [col(9), zeros(PROJ_COLS - 2 * D_MODEL), w_in_b[:, :off[4]],
         col(4), col(4), jnp.tile(col(8), (1, LANES // QK_ROPE)), col(5), zeros(LANES - IDX_HEADS),
         col(6), col(7)], axis=1)
    w_o = jnp.concatenate([w_o_a, w_o_b, w_out], axis=0).astype(bf)
    uq = w_uq.reshape(Q_LORA, B_HEADS, QK_NOPE + QK_ROPE)
    wuq = jnp.concatenate([uq[:, :, :QK_NOPE].reshape(Q_LORA, -1), uq[:, :, QK_NOPE:].reshape(Q_LORA, -1)],
                          axis=1).astype(bf)
    ukv = w_ukv.reshape(KV_LORA, B_HEADS, QK_NOPE + V_DIM)
    wukv = jnp.concatenate([ukv[:, :, :QK_NOPE].reshape(KV_LORA, -1), ukv[:, :, QK_NOPE:].reshape(KV_LORA, -1)],
                           axis=1).astype(bf)

    qa, ka, va, qi, ki, kr, wi, qn, qr, kn, vb = _proj(
        x2, tabs, w_slab, rms_cq.reshape(1, -1), rms_ckv.reshape(1, -1), wuq, wukv)
    o_a = _dsa(qa, qi, wi, ka, va, ki, bsz, seq)
    o_b = _mla(qn, qr, kn, kr, vb, bsz, seq)

    wr = jnp.concatenate([w_router, jnp.zeros((D_MODEL, LANES - N_EXPERTS), w_router.dtype)], axis=1)
    br = jnp.concatenate([b_router, jnp.zeros((LANES - N_EXPERTS,), b_router.dtype)]).reshape(1, -1)
    wr_hi = wr.astype(bf)
    wr_lo = (wr - wr_hi.astype(jnp.float32)).astype(bf)
    h, ri, rf, cnt = _post(x2, o_a, o_b, w_slab, b_gate.reshape(1, -1), w_o, ln1_g.reshape(1, -1),
                           ln1_b.reshape(1, -1), wr_hi, wr_lo, br)

    counts = cnt[0, :N_EXPERTS].astype(jnp.int32)
    ends = jnp.cumsum(counts)
    starts = ends - counts
    dest = _lookup(starts, ri[:TOP_K]) + ri[TOP_K:]
    by_step = lambda t: dest.reshape(TOP_K, n // t, t).transpose(1, 0, 2).reshape(n // t, 1, t * TOP_K)

    xs = _dispatch(by_step(DISPATCH_TILE), h, n * TOP_K)
    ys = _experts(_expert_schedule(starts, ends, n * TOP_K), xs, w_up, b_up, w_down, b_down)
    return _combine(by_step(COMBINE_TILE), ys, h, rf, ln2_g.reshape(1, -1), ln2_b.reshape(1, -1))


def kernel(x, positions, w_in, b_gate, rms_cq, rms_ckv, w_uq, w_ukv, w_o_a, w_o_b, w_out, ln1_g, ln1_b,
           w_router, b_router, w_up, b_up, w_down, b_down, ln2_g, ln2_b):
    bsz, seq, _ = x.shape
    x2 = x.reshape(bsz * seq, D_MODEL)
    tabs = _rope_tables(positions)
    for l in range(DEPTH):
        x2 = _layer(x2, tabs, bsz, seq, w_in[l], b_gate[l], rms_cq[l], rms_ckv[l], w_uq[l], w_ukv[l],
                    w_o_a[l], w_o_b[l], w_out[l], ln1_g[l], ln1_b[l], w_router[l], b_router[l],
                    w_up[l], b_up[l], w_down[l], b_down[l], ln2_g[l], ln2_b[l])
    return x2.reshape(bsz, seq, D_MODEL)
```

```python
import functools

import jax
import jax.numpy as jnp
import numpy as np
from jax import lax
from jax.experimental import pallas as pl
from jax.experimental.pallas import tpu as pltpu

D_MODEL = 1024
A_HEADS = 8
A_HEAD_DIM = 64
IDX_HEADS = 8
IDX_DIM = 64
TOPK_MAX = 256
B_HEADS = 8
Q_LORA = 384
KV_LORA = 256
QK_NOPE = 64
QK_ROPE = 32
V_DIM = 64
ROPE_THETA = 10000.0
N_EXPERTS = 32
TOP_K = 4
D_FF = 1024
SWIGLU_LIMIT = 7.0
SWIGLU_ALPHA = 1.702
DEPTH = 1
DN_ALPHA = (2 * DEPTH) ** 0.25
LN_EPS = 1e-5
RMS_EPS = 1e-6
A_WIDTH = A_HEADS * A_HEAD_DIM
B_WIDTH = B_HEADS * V_DIM
SPLITS = (A_WIDTH, A_WIDTH, A_WIDTH, IDX_HEADS * IDX_DIM, IDX_DIM, IDX_HEADS,
          Q_LORA, KV_LORA, QK_ROPE, 2 * D_MODEL)
LANES = 128
VMEM_LIMIT_BYTES = 52 * 1024 * 1024
PROJ_COLS = 4 * A_WIDTH + 3 * LANES + Q_LORA + KV_LORA

PROJ_TILE = 512
Q_TILE = 256
SEARCH_GROUPS = 2
ATTN_UNROLL_MIN_KEYS = 1536
EXPERT_TILE = 512
FF_CHUNK = 1024
DISPATCH_TILE = 512
COMBINE_TILE = 256

ROW_TILES = D_MODEL // LANES
assert ROW_TILES == 8

LOG2_E = 1.4426950408889634
INT_MIN = -(2 ** 31)
NEG_BIG = -1e30

_NT = (((1,), (1,)), ((), ()))


def _cparams(sem):
    return pltpu.CompilerParams(dimension_semantics=sem, vmem_limit_bytes=VMEM_LIMIT_BYTES)


def _full(shape):
    nd = len(shape)
    return pl.BlockSpec(shape, lambda *_: (0,) * nd)


def _rope_chunk(x, cos, sin, half):
    lane = lax.broadcasted_iota(jnp.int32, x.shape, 1)
    first = (lane % (2 * half)) < half
    up = pltpu.roll(x, LANES - half, 1)
    dn = pltpu.roll(x, half, 1)
    return x * cos + jnp.where(first, -up, dn) * sin


def _layer_norm(u, g, b):
    mu = jnp.mean(u, axis=-1, keepdims=True)
    d = u - mu
    var = jnp.mean(d * d, axis=-1, keepdims=True)
    return d * lax.rsqrt(var + LN_EPS) * g + b


def _load_token_rows(ref, first, t):
    return jnp.concatenate([ref[pl.ds(first * ROW_TILES + s, t, stride=ROW_TILES), :] for s in range(ROW_TILES)],
                           axis=1)


def _store_token_rows(ref, v, first=0):
    t = v.shape[0]
    for s in range(ROW_TILES):
        ref[pl.ds(first * ROW_TILES + s, t, stride=ROW_TILES), :] = v[:, s * LANES:(s + 1) * LANES]


def _two_head_softmax_pv(logits, vc, width):
    lane_v = lax.broadcasted_iota(jnp.int32, vc.shape, 1)
    pvs = []
    for hh in range(2):
        lg = logits[hh]
        p = jnp.exp2(lg - jnp.max(lg, axis=1, keepdims=True)).astype(jnp.bfloat16)
        v_h = jnp.where((lane_v // width) == hh, vc, jnp.ones_like(vc))
        pvs.append(jnp.dot(p, v_h, preferred_element_type=jnp.float32))
    lane = lax.broadcasted_iota(jnp.int32, pvs[0].shape, 1)
    first = lane < width
    num = jnp.where(first, pvs[0], pvs[1])
    den = jnp.where(first, pltpu.roll(pvs[0], width, 1), pltpu.roll(pvs[1], width, 1))
    return num / den


def _head_mask(shape, width, which):
    lane = lax.broadcasted_iota(jnp.int32, shape, 1)
    return (lane // width) == which


def _proj_kernel(x_ref, trig_ref, expand_ref, w_ref,
                 gcq_ref, gckv_ref, wuq_ref, wukv_ref,
                 qa_ref, ka_ref, va_ref, qi_ref, ki_ref, kr_ref, wi_ref,
                 qn_ref, qr_ref, kn_ref, vb_ref):
    xb = x_ref[...].astype(jnp.bfloat16)
    f32 = jnp.float32
    bf = jnp.bfloat16

    trig = trig_ref[...].T
    trig_hi = trig.astype(bf)
    trig_lo = (trig - trig_hi.astype(f32)).astype(bf)
    tabs = (jnp.dot(trig_hi, expand_ref[...], preferred_element_type=f32)
            + jnp.dot(trig_lo, expand_ref[...], preferred_element_type=f32))
    c64, s64, c32, s32 = (tabs[:, j * LANES:(j + 1) * LANES] for j in range(4))

    z1 = jnp.dot(xb, w_ref[:, :4 * A_WIDTH], preferred_element_type=f32)
    n_chunk = A_WIDTH // LANES
    a_scale = A_HEAD_DIM ** -0.5 * LOG2_E
    for c in range(n_chunk):
        sl = slice(c * LANES, (c + 1) * LANES)
        qa_ref[:, sl] = (_rope_chunk(z1[:, c * LANES:(c + 1) * LANES], c64, s64, 32) * a_scale).astype(bf)
        o = A_WIDTH + c * LANES
        ka_ref[:, sl] = _rope_chunk(z1[:, o:o + LANES], c64, s64, 32).astype(bf)
        o = 2 * A_WIDTH + c * LANES
        va_ref[:, sl] = z1[:, o:o + LANES].astype(bf)
        o = 3 * A_WIDTH + c * LANES
        qi_ref[:, sl] = _rope_chunk(z1[:, o:o + LANES], c64, s64, 32).astype(bf)

    small = 4 * A_WIDTH + 3 * LANES
    zs = jnp.dot(xb, w_ref[:, 4 * A_WIDTH:small], preferred_element_type=f32)
    ki_ref[...] = _rope_chunk(zs[:, 0:LANES], c64, s64, 32).astype(bf)
    kr_ref[...] = _rope_chunk(zs[:, LANES:2 * LANES], c32, s32, 16).astype(bf)
    wi_ref[...] = zs[:, 2 * LANES:3 * LANES] * (IDX_HEADS ** -0.5 * IDX_DIM ** -0.5)

    zl = jnp.dot(xb, w_ref[:, small:], preferred_element_type=f32)
    cq = zl[:, :Q_LORA]
    cq = cq * lax.rsqrt(jnp.mean(cq * cq, axis=-1, keepdims=True) + RMS_EPS) * gcq_ref[...]
    ckv = zl[:, Q_LORA:]
    ckv = ckv * lax.rsqrt(jnp.mean(ckv * ckv, axis=-1, keepdims=True) + RMS_EPS) * gckv_ref[...]
    qb = jnp.dot(cq.astype(bf), wuq_ref[...], preferred_element_type=f32)
    kv = jnp.dot(ckv.astype(bf), wukv_ref[...], preferred_element_type=f32)
    b_scale = (QK_NOPE + QK_ROPE) ** -0.5 * LOG2_E
    qn_ref[...] = (qb[:, :B_HEADS * QK_NOPE] * b_scale).astype(bf)
    for c in range(B_HEADS * QK_ROPE // LANES):
        o = B_HEADS * QK_NOPE + c * LANES
        qr_ref[:, c * LANES:(c + 1) * LANES] = (
            _rope_chunk(qb[:, o:o + LANES], c32, s32, 16) * b_scale).astype(bf)
    kn_ref[...] = kv[:, :B_HEADS * QK_NOPE].astype(bf)
    vb_ref[...] = kv[:, B_HEADS * QK_NOPE:].astype(bf)


def _proj(x2, tabs, w_slab, gcq, gckv, wuq, wukv):
    n = x2.shape[0]
    t = PROJ_TILE
    tok = lambda w: pl.BlockSpec((t, w), lambda i: (i, 0))
    bf = jnp.bfloat16
    outs = [(A_WIDTH, bf)] * 4 + [(LANES, bf), (LANES, bf), (LANES, jnp.float32),
                                  (B_HEADS * QK_NOPE, bf), (B_HEADS * QK_ROPE, bf),
                                  (B_HEADS * QK_NOPE, bf), (B_WIDTH, bf)]
    return pl.pallas_call(
        _proj_kernel,
        grid=(n // t,),
        in_specs=[tok(D_MODEL), pl.BlockSpec((LANES, t), lambda i: (0, i)), _full(tabs[1].shape),
                  pl.BlockSpec((D_MODEL, PROJ_COLS), lambda i: (0, 1))]
        + [_full(a.shape) for a in (gcq, gckv, wuq, wukv)],
        out_specs=[tok(w) for w, _ in outs],
        out_shape=[jax.ShapeDtypeStruct((n, w), d) for w, d in outs],
        compiler_params=_cparams(("arbitrary",)),
        name="proj",
    )(x2, *tabs, w_slab, gcq, gckv, wuq, wukv)


def _tie_select(eq, gt, need, kv):
    chunk = 256
    r = lax.broadcasted_iota(jnp.int32, (chunk, chunk), 0)
    c = lax.broadcasted_iota(jnp.int32, (chunk, chunk), 1)
    upper = jnp.where(r <= c, 1.0, 0.0).astype(jnp.bfloat16)
    carry = jnp.zeros((eq.shape[0], 1), jnp.float32)
    parts = []
    for j in range(kv // chunk):
        e = eq[:, j * chunk:(j + 1) * chunk]
        ef = jnp.where(e, 1.0, 0.0).astype(jnp.bfloat16)
        pref = jnp.dot(ef, upper, preferred_element_type=jnp.float32) + carry
        parts.append(gt[:, j * chunk:(j + 1) * chunk] | (e & (pref <= need)))
        carry = carry + jnp.sum(ef.astype(jnp.float32), axis=1, keepdims=True)
    return jnp.concatenate(parts, axis=1)


def _key_to_float(key):
    return lax.bitcast_convert_type(key ^ ((key >> 31) & jnp.int32(0x7FFFFFFF)), jnp.float32)


def _select_topk(score, k_eff, sel_ref, kv):
    tq = score.shape[0]
    rows = tq // SEARCH_GROUPS
    parts = [(score[g * rows:(g + 1) * rows], k_eff[g * rows:(g + 1) * rows]) for g in range(SEARCH_GROUPS)]

    def search(it, prefixes):
        bit = lax.shift_left(jnp.int32(1), 31 - it)
        out = []
        for (s, k), prefix in zip(parts, prefixes):
            cand = prefix + bit
            cnt = jnp.sum(jnp.where(s >= _key_to_float(cand), 1.0, 0.0), axis=1, keepdims=True)
            out.append(jnp.where(cnt >= k, cand, prefix))
        return tuple(out)

    init = tuple(jnp.full((rows, 1), INT_MIN, jnp.int32) for _ in range(SEARCH_GROUPS))
    thr = _key_to_float(jnp.concatenate(lax.fori_loop(0, 32, search, init, unroll=4), axis=0))
    ge = score >= thr
    n_ge = jnp.sum(jnp.where(ge, 1.0, 0.0), axis=1, keepdims=True)
    sel_ref[:, :kv] = jnp.where(ge, 0.0, NEG_BIG)
    has_tie = jnp.max(jnp.where(n_ge != k_eff, 1.0, 0.0)) > 0.5

    @pl.when(has_tie)
    def _():
        gt = score > thr
        need = k_eff - jnp.sum(jnp.where(gt, 1.0, 0.0), axis=1, keepdims=True)
        sel_ref[:, :kv] = jnp.where(_tie_select(score == thr, gt, need, kv), 0.0, NEG_BIG)


def _dsa_block(qa_ref, qi_ref, wi_ref, ka_ref, va_ref, ki_ref, o_ref, sel_ref, score_ref, *, kv, k_sel):
    tq = qa_ref.shape[0]
    f32 = jnp.float32
    bf = jnp.bfloat16
    q_pos = (kv - tq) + lax.broadcasted_iota(jnp.int32, (tq, 1), 0)
    key_pos = lax.broadcasted_iota(jnp.int32, (tq, kv), 1)
    causal = key_pos <= q_pos

    if kv <= k_sel:
        sel_ref[:, :kv] = jnp.where(causal, 0.0, NEG_BIG)
    else:
        ki = ki_ref[:kv, :]
        lane = lax.broadcasted_iota(jnp.int32, (tq, LANES), 1)
        score_ref[:, :kv] = jnp.zeros((tq, kv), f32)

        def idx_slab(c, carry):
            qc = qi_ref[:, pl.ds(pl.multiple_of(c * LANES, LANES), LANES)]
            acc = score_ref[:, :kv]
            for hh in range(2):
                qm = jnp.where(_head_mask(qc.shape, IDX_DIM, hh), qc, jnp.zeros_like(qc))
                s = lax.dot_general(qm, ki, _NT, preferred_element_type=f32)
                w = jnp.sum(jnp.where(lane == 2 * c + hh, wi_ref[...], 0.0), axis=1, keepdims=True)
                acc = acc + jnp.maximum(s, 0.0) * w
            score_ref[:, :kv] = acc
            return carry

        lax.fori_loop(0, IDX_HEADS // 2, idx_slab, 0, unroll=2)
        score = jnp.where(causal, score_ref[:, :kv], -jnp.inf)
        k_eff = jnp.minimum(q_pos + 1, k_sel).astype(f32)
        _select_topk(score, k_eff, sel_ref, kv)

    def attn_slab(c, carry):
        sl = pl.ds(pl.multiple_of(c * LANES, LANES), LANES)
        qc = qa_ref[:, sl]
        kc = ka_ref[:kv, sl]
        vc = va_ref[:kv, sl]
        mask = sel_ref[:, :kv]
        logits = []
        for hh in range(2):
            qm = jnp.where(_head_mask(qc.shape, A_HEAD_DIM, hh), qc, jnp.zeros_like(qc))
            logits.append(lax.dot_general(qm, kc, _NT, preferred_element_type=f32) + mask)
        o_ref[:, sl] = _two_head_softmax_pv(logits, vc, A_HEAD_DIM).astype(bf)
        return carry

    lax.fori_loop(0, A_WIDTH // LANES, attn_slab, 0, unroll=2 if kv >= ATTN_UNROLL_MIN_KEYS else 1)


def _per_query_block(block_fn, refs, n_blocks, tq):
    j = pl.program_id(1)
    for b in range(n_blocks):
        pl.when(j == b)(functools.partial(block_fn, *refs, kv=(b + 1) * tq))


def _dsa_kernel(*refs, seq, k_sel):
    tq = refs[0].shape[0]
    _per_query_block(functools.partial(_dsa_block, k_sel=k_sel), refs, seq // tq, tq)


def _dsa(qa, qi, wi, ka, va, ki, bsz, seq):
    tq = min(Q_TILE, seq)
    nq = seq // tq
    qspec = lambda w: pl.BlockSpec((tq, w), lambda b, i: (b * nq + i, 0))
    kspec = lambda w: pl.BlockSpec((seq, w), lambda b, i: (b, 0))
    k_sel = min(TOPK_MAX, seq // 4)
    return pl.pallas_call(
        functools.partial(_dsa_kernel, seq=seq, k_sel=k_sel),
        grid=(bsz, nq),
        in_specs=[qspec(A_WIDTH), qspec(A_WIDTH), qspec(LANES), kspec(A_WIDTH), kspec(A_WIDTH), kspec(LANES)],
        out_specs=qspec(A_WIDTH),
        out_shape=jax.ShapeDtypeStruct((bsz * seq, A_WIDTH), jnp.bfloat16),
        scratch_shapes=[pltpu.VMEM((tq, seq), jnp.float32), pltpu.VMEM((tq, seq), jnp.float32)],
        compiler_params=_cparams(("arbitrary", "arbitrary")),
        name="dsa",
    )(qa, qi, wi, ka, va, ki)


def _mla_block(qn_ref, qr_ref, kn_ref, kr_ref, vb_ref, o_ref, *, kv):
    tq = qn_ref.shape[0]
    f32 = jnp.float32
    bf = jnp.bfloat16
    diag = lax.broadcasted_iota(jnp.int32, (tq, tq), 1) <= lax.broadcasted_iota(jnp.int32, (tq, tq), 0)
    kr = kr_ref[:kv, :]

    def causal_masked(lg):
        own = jnp.where(diag, lg[:, kv - tq:], NEG_BIG)
        return own if kv == tq else jnp.concatenate([lg[:, :kv - tq], own], axis=1)

    def slab(c, carry):
        sl = pl.ds(pl.multiple_of(c * LANES, LANES), LANES)
        qc = qn_ref[:, sl]
        kcat = jnp.concatenate([kn_ref[:kv, sl], kr], axis=1)
        vc = vb_ref[:kv, sl]
        qrc = qr_ref[:, pl.ds(pl.multiple_of((c // 2) * LANES, LANES), LANES)]
        logits = []
        for hh in range(2):
            h = 2 * c + hh
            qm = jnp.where(_head_mask(qc.shape, QK_NOPE, hh), qc, jnp.zeros_like(qc))
            qrm = jnp.where(_head_mask(qrc.shape, QK_ROPE, h % 4), qrc, jnp.zeros_like(qrc))
            qcat = jnp.concatenate([qm, qrm], axis=1)
            lg = lax.dot_general(qcat, kcat, _NT, preferred_element_type=f32)
            logits.append(causal_masked(lg))
        o_ref[:, sl] = _two_head_softmax_pv(logits, vc, V_DIM).astype(bf)
        return carry

    lax.fori_loop(0, B_WIDTH // LANES, slab, 0, unroll=True)


def _mla_kernel(*refs, seq):
    tq = refs[0].shape[0]
    _per_query_block(_mla_block, refs, seq // tq, tq)


def _mla(qn, qr, kn, kr, vb, bsz, seq):
    tq = min(Q_TILE, seq)
    nq = seq // tq
    qspec = lambda w: pl.BlockSpec((tq, w), lambda b, i: (b * nq + i, 0))
    kspec = lambda w: pl.BlockSpec((seq, w), lambda b, i: (b, 0))
    return pl.pallas_call(
        functools.partial(_mla_kernel, seq=seq),
        grid=(bsz, nq),
        in_specs=[qspec(B_HEADS * QK_NOPE), qspec(B_HEADS * QK_ROPE), kspec(B_HEADS * QK_NOPE),
                  kspec(LANES), kspec(B_WIDTH)],
        out_specs=qspec(B_WIDTH),
        out_shape=jax.ShapeDtypeStruct((bsz * seq, B_WIDTH), jnp.bfloat16),
        compiler_params=_cparams(("arbitrary", "arbitrary")),
        name="mla",
    )(qn, qr, kn, kr, vb)


def _post_kernel(x_ref, oa_ref, ob_ref, wg_ref, bg_ref, wo_ref, g1_ref, b1_ref,
                 wrh_ref, wrl_ref, br_ref, h_ref, ri_ref, rf_ref, cnt_ref, carry_ref):
    f32 = jnp.float32
    bf = jnp.bfloat16
    t = x_ref.shape[0]

    @pl.when(pl.program_id(0) == 0)
    def _():
        carry_ref[...] = jnp.zeros_like(carry_ref)

    x = x_ref[...]
    gate = jax.nn.sigmoid(jnp.dot(x.astype(bf), wg_ref[...], preferred_element_type=f32) + bg_ref[...])
    pa = jnp.dot(oa_ref[...], wo_ref[:A_WIDTH, :], preferred_element_type=f32)
    pb = jnp.dot(ob_ref[...], wo_ref[A_WIDTH:A_WIDTH + B_WIDTH, :], preferred_element_type=f32)
    mix = gate[:, :D_MODEL] * pa + gate[:, D_MODEL:] * pb
    u = DN_ALPHA * x + jnp.dot(mix.astype(bf), wo_ref[A_WIDTH + B_WIDTH:, :], preferred_element_type=f32)
    h = _layer_norm(u, g1_ref[...], b1_ref[...])
    _store_token_rows(h_ref, h)

    h_hi = h.astype(bf)
    h_lo = (h - h_hi.astype(f32)).astype(bf)
    logits = (jnp.dot(h_hi, wrh_ref[...], preferred_element_type=f32)
              + jnp.dot(h_lo, wrh_ref[...], preferred_element_type=f32)
              + jnp.dot(h_hi, wrl_ref[...], preferred_element_type=f32) + br_ref[...])
    lane = lax.broadcasted_iota(jnp.int32, (t, LANES), 1).astype(f32)
    lg = jnp.where(lane < N_EXPERTS, logits, -jnp.inf)
    vals, ids = [], []
    assign = jnp.zeros((t, LANES), f32)
    for _k in range(TOP_K):
        m = jnp.max(lg, axis=1, keepdims=True)
        idx = jnp.min(jnp.where(lg == m, lane, float(LANES)), axis=1, keepdims=True)
        hit = lane == idx
        vals.append(m)
        ids.append(idx)
        assign = jnp.where(hit, 1.0, assign)
        lg = jnp.where(hit, -jnp.inf, lg)
    exps = [jnp.exp(v - vals[0]) for v in vals]
    den = exps[0] + exps[1] + exps[2] + exps[3]

    r = lax.broadcasted_iota(jnp.int32, (t, t), 0)
    c = lax.broadcasted_iota(jnp.int32, (t, t), 1)
    lower = jnp.where(c < r, 1.0, 0.0).astype(bf)
    before = jnp.dot(lower, assign.astype(bf), preferred_element_type=f32) + carry_ref[0:1, :]
    ri = jnp.zeros((t, LANES), f32)
    rf = jnp.zeros((t, LANES), f32)
    for k in range(TOP_K):
        rank = jnp.sum(jnp.where(lane == ids[k], before, 0.0), axis=1, keepdims=True)
        ri = jnp.where(lane == float(k), ids[k], ri)
        ri = jnp.where(lane == float(TOP_K + k), rank, ri)
        rf = jnp.where(lane == float(k), exps[k] / den, rf)
    ri_ref[...] = ri.T[:2 * TOP_K, :].astype(jnp.int32)
    rf_ref[...] = rf
    total = carry_ref[0:1, :] + jnp.sum(assign, axis=0, keepdims=True)
    carry_ref[...] = jnp.broadcast_to(total, carry_ref.shape)
    cnt_ref[...] = jnp.broadcast_to(total, cnt_ref.shape)


def _post(x2, oa, ob, w_slab, bg, wo, g1, b1, wrh, wrl, br):
    n = x2.shape[0]
    t = PROJ_TILE
    tok = lambda w: pl.BlockSpec((t, w), lambda i: (i, 0))
    consts = (bg, wo, g1, b1, wrh, wrl, br)
    gates_w = pl.BlockSpec((D_MODEL, 2 * D_MODEL), lambda i: (0, 0))
    return pl.pallas_call(
        _post_kernel,
        grid=(n // t,),
        in_specs=[tok(D_MODEL), tok(A_WIDTH), tok(B_WIDTH), gates_w] + [_full(a.shape) for a in consts],
        out_specs=[pl.BlockSpec((t * ROW_TILES, LANES), lambda i: (i, 0)),
                   pl.BlockSpec((2 * TOP_K, t), lambda i: (0, i)), tok(LANES), _full((8, LANES))],
        out_shape=[jax.ShapeDtypeStruct((n * ROW_TILES, LANES), jnp.float32),
                   jax.ShapeDtypeStruct((2 * TOP_K, n), jnp.int32),
                   jax.ShapeDtypeStruct((n, LANES), jnp.float32),
                   jax.ShapeDtypeStruct((8, LANES), jnp.float32)],
        scratch_shapes=[pltpu.VMEM((8, LANES), jnp.float32)],
        compiler_params=_cparams(("arbitrary",)),
        name="post",
    )(x2, oa, ob, w_slab, *consts)


def _token_row(ref, r):
    start = r * ROW_TILES if isinstance(r, int) else pl.multiple_of(r * ROW_TILES, ROW_TILES)
    return ref.at[pl.ds(start, ROW_TILES)]


def _row_gather(idx_ref, src_ref, dst_ref, sem, rows):
    for j in range(rows):
        pltpu.make_async_copy(_token_row(src_ref, idx_ref[0, 0, j]), _token_row(dst_ref, j), sem).start(
            priority=j % 2)


def _rows_wait(src_ref, dst_ref, sem):
    pltpu.make_async_copy(src_ref, dst_ref, sem).wait()


def _dispatch_kernel(dest_ref, h_ref, xs_ref, sem):
    t = DISPATCH_TILE
    for k in range(TOP_K):
        for j in range(t):
            pltpu.make_async_copy(_token_row(h_ref, j), _token_row(xs_ref, dest_ref[0, 0, k * t + j]), sem).start(
                priority=j % 2)
    for k in range(TOP_K):
        _rows_wait(h_ref, xs_ref.at[pl.ds(0, t * ROW_TILES)], sem)


def _dispatch(dest3, h, rows):
    t = DISPATCH_TILE
    n = h.shape[0] // ROW_TILES
    return pl.pallas_call(
        _dispatch_kernel,
        grid=(n // t,),
        in_specs=[pl.BlockSpec((1, 1, t * TOP_K), lambda i: (i, 0, 0), memory_space=pltpu.SMEM),
                  pl.BlockSpec((t * ROW_TILES, LANES), lambda i: (i, 0))],
        out_specs=pl.BlockSpec(memory_space=pl.ANY),
        out_shape=jax.ShapeDtypeStruct((rows * ROW_TILES, LANES), jnp.float32),
        scratch_shapes=[pltpu.SemaphoreType.DMA(())],
        compiler_params=_cparams(("arbitrary",)),
        name="dispatch",
    )(dest3, h)


def _expert_kernel(te_ref, tt_ref, lo_ref, hi_ref, ns_ref, xs_ref, wu_ref, bu_ref, wd_ref, bd_ref, ys_ref,
                   wub_ref, wdb_ref):
    i = pl.program_id(0)
    f32 = jnp.float32
    bf = jnp.bfloat16
    t = EXPERT_TILE
    valid = i < ns_ref[0]
    before = jnp.maximum(i - 1, 0)
    fresh = valid & ((i == 0) | (te_ref[i] != te_ref[before]))
    first_visit = (i == 0) | (tt_ref[i] != tt_ref[before])

    @pl.when(fresh)
    def _():
        rows = 128

        def cast_up(j, c):
            r = pl.multiple_of(j * rows, rows)
            wub_ref[pl.ds(r, rows), :] = wu_ref[pl.ds(r, rows), :].astype(bf)
            return c

        lax.fori_loop(0, D_MODEL // rows, cast_up, 0)

        def cast_dn(j, c):
            r = pl.multiple_of(j * rows, rows)
            wdb_ref[pl.ds(r, rows), :] = wd_ref[pl.ds(r, rows), :].astype(bf)
            return c

        lax.fori_loop(0, D_FF // rows, cast_dn, 0)

    def mlp(first, rows):
        xb = _load_token_rows(xs_ref, first, rows).astype(bf)
        y = jnp.broadcast_to(bd_ref[...], (rows, D_MODEL))
        for c in range(D_FF // FF_CHUNK):
            gs = slice(c * FF_CHUNK, (c + 1) * FF_CHUNK)
            ls = slice(D_FF + c * FF_CHUNK, D_FF + (c + 1) * FF_CHUNK)
            ag = jnp.dot(xb, wub_ref[:, gs], preferred_element_type=f32) + bu_ref[:, gs]
            al = jnp.dot(xb, wub_ref[:, ls], preferred_element_type=f32) + bu_ref[:, ls]
            g = jnp.minimum(ag, SWIGLU_LIMIT)
            lin = jnp.clip(al, -SWIGLU_LIMIT, SWIGLU_LIMIT)
            act = (lin + 1.0) * (g * jax.nn.sigmoid(SWIGLU_ALPHA * g))
            y = y + jnp.dot(act.astype(bf), wdb_ref[gs, :], preferred_element_type=f32)
        return y

    lo, hi = lo_ref[i], hi_ref[i]
    half = t // 2
    whole = (lo == 0) & (hi == t)
    partial = valid & jnp.logical_not(whole)

    @pl.when(valid & whole)
    def _():
        _store_token_rows(ys_ref, mlp(0, t))

    @pl.when(partial & first_visit)
    def _():
        ys_ref[...] = jnp.zeros_like(ys_ref)

    def fill(first, rows):
        row = first + lax.broadcasted_iota(jnp.int32, (rows, 1), 0)
        mine = (row >= lo) & (row < hi)
        _store_token_rows(ys_ref, jnp.where(mine, mlp(first, rows), _load_token_rows(ys_ref, first, rows)), first)

    pl.when(partial & (lo < half) & (hi > half))(functools.partial(fill, 0, t))
    pl.when(partial & (hi <= half))(functools.partial(fill, 0, half))
    pl.when(partial & (lo >= half))(functools.partial(fill, half, half))


def _experts(steps, xs, w_up, b_up, w_down, b_down):
    t = EXPERT_TILE
    max_steps = steps[0].shape[0]
    row_block = pl.BlockSpec((t * ROW_TILES, LANES), lambda i, te, tt, lo, hi, ns: (tt[i], 0))
    per_expert = lambda shape: pl.BlockSpec((None,) + shape, lambda i, te, tt, lo, hi, ns: (te[i], 0, 0))
    grid_spec = pltpu.PrefetchScalarGridSpec(
        num_scalar_prefetch=5,
        grid=(max_steps,),
        in_specs=[row_block, per_expert((D_MODEL, 2 * D_FF)), per_expert((1, 2 * D_FF)),
                  per_expert((D_FF, D_MODEL)), per_expert((1, D_MODEL))],
        out_specs=row_block,
        scratch_shapes=[pltpu.VMEM((D_MODEL, 2 * D_FF), jnp.bfloat16),
                        pltpu.VMEM((D_FF, D_MODEL), jnp.bfloat16)],
    )
    return pl.pallas_call(
        _expert_kernel,
        grid_spec=grid_spec,
        out_shape=jax.ShapeDtypeStruct(xs.shape, jnp.float32),
        compiler_params=_cparams(("arbitrary",)),
        name="experts",
    )(*steps, xs, w_up, b_up.reshape(N_EXPERTS, 1, 2 * D_FF), w_down, b_down.reshape(N_EXPERTS, 1, D_MODEL))


def _combine_kernel(cur_ref, nxt_ref, ys_ref, h_ref, rf_ref, g2_ref, b2_ref, o_ref, buf_ref, sems):
    i = pl.program_id(0)
    n_steps = pl.num_programs(0)
    t = COMBINE_TILE
    rows = t * TOP_K
    slot = i % 2

    @pl.when(i == 0)
    def _():
        _row_gather(cur_ref, ys_ref, buf_ref.at[0], sems.at[0], rows)

    _rows_wait(ys_ref.at[pl.ds(0, rows * ROW_TILES)], buf_ref.at[slot], sems.at[slot])
    _row_gather(nxt_ref, ys_ref, buf_ref.at[1 - slot], sems.at[1 - slot], rows)

    rf = rf_ref[...]
    u = DN_ALPHA * _load_token_rows(h_ref, 0, t)
    for k in range(TOP_K):
        u = u + rf[:, k:k + 1] * _load_token_rows(buf_ref.at[slot], k * t, t)
    o_ref[...] = _layer_norm(u, g2_ref[...], b2_ref[...])

    @pl.when(i == n_steps - 1)
    def _():
        _rows_wait(ys_ref.at[pl.ds(0, rows * ROW_TILES)], buf_ref.at[1 - slot], sems.at[1 - slot])


def _combine(dest3, ys, h, rf, g2, b2):
    t = COMBINE_TILE
    n = h.shape[0] // ROW_TILES
    last = n // t - 1
    tok = lambda w: pl.BlockSpec((t, w), lambda i: (i, 0))
    idx_spec = lambda f: pl.BlockSpec((1, 1, t * TOP_K), f, memory_space=pltpu.SMEM)
    return pl.pallas_call(
        _combine_kernel,
        grid=(n // t,),
        in_specs=[idx_spec(lambda i: (i, 0, 0)), idx_spec(lambda i: (jnp.minimum(i + 1, last), 0, 0)),
                  pl.BlockSpec(memory_space=pl.ANY), pl.BlockSpec((t * ROW_TILES, LANES), lambda i: (i, 0)),
                  tok(LANES), _full(g2.shape), _full(b2.shape)],
        out_specs=tok(D_MODEL),
        out_shape=jax.ShapeDtypeStruct((n, D_MODEL), jnp.float32),
        scratch_shapes=[pltpu.VMEM((2, t * TOP_K * ROW_TILES, LANES), jnp.float32),
                        pltpu.SemaphoreType.DMA((2,))],
        compiler_params=_cparams(("arbitrary",)),
        name="combine",
    )(dest3, dest3, ys, h, rf, g2, b2)


def _rope_tables(positions):
    pos = positions.astype(jnp.float32).reshape(1, -1)
    halves = (A_HEAD_DIM // 2, QK_ROPE // 2)
    pieces, rows, src = [], [], 0
    for half in halves:
        ang = (ROPE_THETA ** (-jnp.arange(half, dtype=jnp.float32) / half)).reshape(-1, 1) * pos
        pieces += [jnp.cos(ang), jnp.sin(ang)]
        for _ in range(2):
            rows.append(src + np.arange(LANES) % half)
            src += half
    trig = jnp.concatenate(pieces + [jnp.zeros((LANES - src, pos.shape[1]), jnp.float32)], axis=0)
    expand = (np.arange(LANES)[:, None] == np.concatenate(rows)[None, :]).astype(np.float32)
    return trig, jnp.asarray(expand, jnp.bfloat16)


def _lookup(table, idx):
    experts = jnp.arange(N_EXPERTS, dtype=jnp.int32)
    return jnp.sum(jnp.where(idx[..., None] == experts, table, 0), axis=-1)


def _expert_schedule(starts, ends, rows):
    t = EXPERT_TILE
    first_tile = starts // t
    n_tile = jnp.where(ends > starts, (ends - 1) // t - first_tile + 1, 0)
    cum = jnp.cumsum(n_tile)
    n_steps = cum[-1]
    max_steps = rows // t + N_EXPERTS
    s = jnp.minimum(jnp.arange(max_steps, dtype=jnp.int32), n_steps - 1)
    expert = jnp.sum((cum[None, :] <= s[:, None]).astype(jnp.int32), axis=1)
    tile = _lookup(first_tile, expert) + s - _lookup(cum - n_tile, expert)
    lo = jnp.maximum(_lookup(starts, expert) - tile * t, 0)
    hi = jnp.minimum(_lookup(ends, expert) - tile * t, t)
    return expert, tile, lo, hi, n_steps.astype(jnp.int32).reshape(1)


def _layer(x2, tabs, bsz, seq, w_in, b_gate, rms_cq, rms_ckv, w_uq, w_ukv, w_o_a, w_o_b, w_out,
           ln1_g, ln1_b, w_router, b_router, w_up, b_up, w_down, b_down, ln2_g, ln2_b):
    bf = jnp.bfloat16
    n = x2.shape[0]
    off = np.concatenate([[0], np.cumsum(SPLITS)]).tolist()
    w_in_b = w_in.astype(bf)
    col = lambda j: w_in_b[:, off[j]:off[j + 1]]
    zeros = lambda w: jnp.zeros((D_MODEL, w), bf)
    w_slab = jnp.concatenate(
        [col(9), zeros(PROJ_COLS - 2 * D_MODEL), w_in_b[:, :off[4]],
         col(4), col(4), jnp.tile(col(8), (1, LANES // QK_ROPE)), col(5), zeros(LANES - IDX_HEADS),
         col(6), col(7)], axis=1)
    w_o = jnp.concatenate([w_o_a, w_o_b, w_out], axis=0).astype(bf)
    uq = w_uq.reshape(Q_LORA, B_HEADS, QK_NOPE + QK_ROPE)
    wuq = jnp.concatenate([uq[:, :, :QK_NOPE].reshape(Q_LORA, -1), uq[:, :, QK_NOPE:].reshape(Q_LORA, -1)],
                          axis=1).astype(bf)
    ukv = w_ukv.reshape(KV_LORA, B_HEADS, QK_NOPE + V_DIM)
    wukv = jnp.concatenate([ukv[:, :, :QK_NOPE].reshape(KV_LORA, -1), ukv[:, :, QK_NOPE:].reshape(KV_LORA, -1)],
                           axis=1).astype(bf)

    qa, ka, va, qi, ki, kr, wi, qn, qr, kn, vb = _proj(
        x2, tabs, w_slab, rms_cq.reshape(1, -1), rms_ckv.reshape(1, -1), wuq, wukv)
    o_a = _dsa(qa, qi, wi, ka, va, ki, bsz, seq)
    o_b = _mla(qn, qr, kn, kr, vb, bsz, seq)

    wr = jnp.concatenate([w_router, jnp.zeros((D_MODEL, LANES - N_EXPERTS), w_router.dtype)], axis=1)
    br = jnp.concatenate([b_router, jnp.zeros((LANES - N_EXPERTS,), b_router.dtype)]).reshape(1, -1)
    wr_hi = wr.astype(bf)
    wr_lo = (wr - wr_hi.astype(jnp.float32)).astype(bf)
    h, ri, rf, cnt = _post(x2, o_a, o_b, w_slab, b_gate.reshape(1, -1), w_o, ln1_g.reshape(1, -1),
                           ln1_b.reshape(1, -1), wr_hi, wr_lo, br)

    counts = cnt[0, :N_EXPERTS].astype(jnp.int32)
    ends = jnp.cumsum(counts)
    starts = ends - counts
    dest = _lookup(starts, ri[:TOP_K]) + ri[TOP_K:]
    by_step = lambda t: dest.reshape(TOP_K, n // t, t).transpose(1, 0, 2).reshape(n // t, 1, t * TOP_K)

    xs = _dispatch(by_step(DISPATCH_TILE), h, n * TOP_K)
    ys = _experts(_expert_schedule(starts, ends, n * TOP_K), xs, w_up, b_up, w_down, b_down)
    return _combine(by_step(COMBINE_TILE), ys, h, rf, ln2_g.reshape(1, -1), ln2_b.reshape(1, -1))


def kernel(x, positions, w_in, b_gate, rms_cq, rms_ckv, w_uq, w_ukv, w_o_a, w_o_b, w_out, ln1_g, ln1_b,
           w_router, b_router, w_up, b_up, w_down, b_down, ln2_g, ln2_b):
    bsz, seq, _ = x.shape
    x2 = x.reshape(bsz * seq, D_MODEL)
    tabs = _rope_tables(positions)
    for l in range(DEPTH):
        x2 = _layer(x2, tabs, bsz, seq, w_in[l], b_gate[l], rms_cq[l], rms_ckv[l], w_uq[l], w_ukv[l],
                    w_o_a[l], w_o_b[l], w_out[l], ln1_g[l], ln1_b[l], w_router[l], b_router[l],
                    w_up[l], b_up[l], w_down[l], b_down[l], ln2_g[l], ln2_b[l])
    return x2.reshape(bsz, seq, D_MODEL)
```

```python
import functools

import jax
import jax.numpy as jnp
import numpy as np
from jax import lax
from jax.experimental import pallas as pl
from jax.experimental.pallas import tpu as pltpu

D_MODEL = 1024
A_HEADS = 8
A_HEAD_DIM = 64
IDX_HEADS = 8
IDX_DIM = 64
TOPK_MAX = 256
B_HEADS = 8
Q_LORA = 384
KV_LORA = 256
QK_NOPE = 64
QK_ROPE = 32
V_DIM = 64
ROPE_THETA = 10000.0
N_EXPERTS = 32
TOP_K = 4
D_FF = 1024
SWIGLU_LIMIT = 7.0
SWIGLU_ALPHA = 1.702
DEPTH = 1
DN_ALPHA = (2 * DEPTH) ** 0.25
LN_EPS = 1e-5
RMS_EPS = 1e-6
A_WIDTH = A_HEADS * A_HEAD_DIM
B_WIDTH = B_HEADS * V_DIM
SPLITS = (A_WIDTH, A_WIDTH, A_WIDTH, IDX_HEADS * IDX_DIM, IDX_DIM, IDX_HEADS,
          Q_LORA, KV_LORA, QK_ROPE, 2 * D_MODEL)
LANES = 128
VMEM_LIMIT_BYTES = 52 * 1024 * 1024
SMALL_COLS = 3 * LANES + Q_LORA + KV_LORA
assert (2 * D_MODEL) % SMALL_COLS == 0

PROJ_TILE = 512
Q_TILE = 256
SEARCH_GROUPS = 2
ATTN_UNROLL_MIN_KEYS = 1536
EXPERT_TILE = 512
FF_CHUNK = 1024
DISPATCH_TILE = 512
COMBINE_TILE = 256

ROW_TILES = D_MODEL // LANES
assert ROW_TILES == 8

LOG2_E = 1.4426950408889634
INT_MIN = -(2 ** 31)
NEG_BIG = -1e30

_NT = (((1,), (1,)), ((), ()))


def _cparams(sem):
    return pltpu.CompilerParams(dimension_semantics=sem, vmem_limit_bytes=VMEM_LIMIT_BYTES)


def _full(shape):
    nd = len(shape)
    return pl.BlockSpec(shape, lambda *_: (0,) * nd)


def _rope_chunk(x, cos, sin, half):
    lane = lax.broadcasted_iota(jnp.int32, x.shape, 1)
    first = (lane % (2 * half)) < half
    up = pltpu.roll(x, LANES - half, 1)
    dn = pltpu.roll(x, half, 1)
    return x * cos + jnp.where(first, -up, dn) * sin


def _layer_norm(u, g, b):
    mu = jnp.mean(u, axis=-1, keepdims=True)
    d = u - mu
    var = jnp.mean(d * d, axis=-1, keepdims=True)
    return d * lax.rsqrt(var + LN_EPS) * g + b


def _load_token_rows(ref, first, t):
    return jnp.concatenate([ref[pl.ds(first * ROW_TILES + s, t, stride=ROW_TILES), :] for s in range(ROW_TILES)],
                           axis=1)


def _store_token_rows(ref, v, first=0):
    t = v.shape[0]
    for s in range(ROW_TILES):
        ref[pl.ds(first * ROW_TILES + s, t, stride=ROW_TILES), :] = v[:, s * LANES:(s + 1) * LANES]


def _two_head_softmax_pv(logits, vc, width):
    lane_v = lax.broadcasted_iota(jnp.int32, vc.shape, 1)
    pvs = []
    for hh in range(2):
        lg = logits[hh]
        p = jnp.exp2(lg - jnp.max(lg, axis=1, keepdims=True)).astype(jnp.bfloat16)
        v_h = jnp.where((lane_v // width) == hh, vc, jnp.ones_like(vc))
        pvs.append(jnp.dot(p, v_h, preferred_element_type=jnp.float32))
    lane = lax.broadcasted_iota(jnp.int32, pvs[0].shape, 1)
    first = lane < width
    num = jnp.where(first, pvs[0], pvs[1])
    den = jnp.where(first, pltpu.roll(pvs[0], width, 1), pltpu.roll(pvs[1], width, 1))
    return num / den


def _head_mask(shape, width, which):
    lane = lax.broadcasted_iota(jnp.int32, shape, 1)
    return (lane // width) == which


def _proj_kernel(x_ref, trig_ref, expand_ref, w_main_ref, w_rest_ref,
                 gcq_ref, gckv_ref, wuq_ref, wukv_ref,
                 qa_ref, ka_ref, va_ref, qi_ref, ki_ref, kr_ref, wi_ref,
                 qn_ref, qr_ref, kn_ref, vb_ref):
    xb = x_ref[...].astype(jnp.bfloat16)
    f32 = jnp.float32
    bf = jnp.bfloat16

    trig = trig_ref[...].T
    trig_hi = trig.astype(bf)
    trig_lo = (trig - trig_hi.astype(f32)).astype(bf)
    tabs = (jnp.dot(trig_hi, expand_ref[...], preferred_element_type=f32)
            + jnp.dot(trig_lo, expand_ref[...], preferred_element_type=f32))
    c64, s64, c32, s32 = (tabs[:, j * LANES:(j + 1) * LANES] for j in range(4))

    z1 = jnp.dot(xb, w_main_ref[...], preferred_element_type=f32)
    n_chunk = A_WIDTH // LANES
    a_scale = A_HEAD_DIM ** -0.5 * LOG2_E
    for c in range(n_chunk):
        sl = slice(c * LANES, (c + 1) * LANES)
        qa_ref[:, sl] = (_rope_chunk(z1[:, c * LANES:(c + 1) * LANES], c64, s64, 32) * a_scale).astype(bf)
        o = A_WIDTH + c * LANES
        ka_ref[:, sl] = _rope_chunk(z1[:, o:o + LANES], c64, s64, 32).astype(bf)
        o = 2 * A_WIDTH + c * LANES
        va_ref[:, sl] = z1[:, o:o + LANES].astype(bf)
        o = 3 * A_WIDTH + c * LANES
        qi_ref[:, sl] = _rope_chunk(z1[:, o:o + LANES], c64, s64, 32).astype(bf)

    small = 3 * LANES
    zs = jnp.dot(xb, w_rest_ref[:, :small], preferred_element_type=f32)
    ki_ref[...] = _rope_chunk(zs[:, 0:LANES], c64, s64, 32).astype(bf)
    kr_ref[...] = _rope_chunk(zs[:, LANES:2 * LANES], c32, s32, 16).astype(bf)
    wi_ref[...] = zs[:, 2 * LANES:3 * LANES] * (IDX_HEADS ** -0.5 * IDX_DIM ** -0.5)

    zl = jnp.dot(xb, w_rest_ref[:, small:], preferred_element_type=f32)
    cq = zl[:, :Q_LORA]
    cq = cq * lax.rsqrt(jnp.mean(cq * cq, axis=-1, keepdims=True) + RMS_EPS) * gcq_ref[...]
    ckv = zl[:, Q_LORA:]
    ckv = ckv * lax.rsqrt(jnp.mean(ckv * ckv, axis=-1, keepdims=True) + RMS_EPS) * gckv_ref[...]
    qb = jnp.dot(cq.astype(bf), wuq_ref[...], preferred_element_type=f32)
    kv = jnp.dot(ckv.astype(bf), wukv_ref[...], preferred_element_type=f32)
    b_scale = (QK_NOPE + QK_ROPE) ** -0.5 * LOG2_E
    qn_ref[...] = (qb[:, :B_HEADS * QK_NOPE] * b_scale).astype(bf)
    for c in range(B_HEADS * QK_ROPE // LANES):
        o = B_HEADS * QK_NOPE + c * LANES
        qr_ref[:, c * LANES:(c + 1) * LANES] = (
            _rope_chunk(qb[:, o:o + LANES], c32, s32, 16) * b_scale).astype(bf)
    kn_ref[...] = kv[:, :B_HEADS * QK_NOPE].astype(bf)
    vb_ref[...] = kv[:, B_HEADS * QK_NOPE:].astype(bf)


def _proj(x2, tabs, w_in_b, w_slab, gcq, gckv, wuq, wukv):
    n = x2.shape[0]
    t = PROJ_TILE
    tok = lambda w: pl.BlockSpec((t, w), lambda i: (i, 0))
    bf = jnp.bfloat16
    outs = [(A_WIDTH, bf)] * 4 + [(LANES, bf), (LANES, bf), (LANES, jnp.float32),
                                  (B_HEADS * QK_NOPE, bf), (B_HEADS * QK_ROPE, bf),
                                  (B_HEADS * QK_NOPE, bf), (B_WIDTH, bf)]
    return pl.pallas_call(
        _proj_kernel,
        grid=(n // t,),
        in_specs=[tok(D_MODEL), pl.BlockSpec((LANES, t), lambda i: (0, i)), _full(tabs[1].shape),
                  pl.BlockSpec((D_MODEL, 4 * A_WIDTH), lambda i: (0, 0)),
                  pl.BlockSpec((D_MODEL, SMALL_COLS), lambda i: (0, 2 * D_MODEL // SMALL_COLS))]
        + [_full(a.shape) for a in (gcq, gckv, wuq, wukv)],
        out_specs=[tok(w) for w, _ in outs],
        out_shape=[jax.ShapeDtypeStruct((n, w), d) for w, d in outs],
        compiler_params=_cparams(("arbitrary",)),
        name="proj",
    )(x2, *tabs, w_in_b, w_slab, gcq, gckv, wuq, wukv)


def _tie_select(eq, gt, need, kv):
    chunk = 256
    r = lax.broadcasted_iota(jnp.int32, (chunk, chunk), 0)
    c = lax.broadcasted_iota(jnp.int32, (chunk, chunk), 1)
    upper = jnp.where(r <= c, 1.0, 0.0).astype(jnp.bfloat16)
    carry = jnp.zeros((eq.shape[0], 1), jnp.float32)
    parts = []
    for j in range(kv // chunk):
        e = eq[:, j * chunk:(j + 1) * chunk]
        ef = jnp.where(e, 1.0, 0.0).astype(jnp.bfloat16)
        pref = jnp.dot(ef, upper, preferred_element_type=jnp.float32) + carry
        parts.append(gt[:, j * chunk:(j + 1) * chunk] | (e & (pref <= need)))
        carry = carry + jnp.sum(ef.astype(jnp.float32), axis=1, keepdims=True)
    return jnp.concatenate(parts, axis=1)


def _key_to_float(key):
    return lax.bitcast_convert_type(key ^ ((key >> 31) & jnp.int32(0x7FFFFFFF)), jnp.float32)


def _select_topk(score, k_eff, sel_ref, kv):
    tq = score.shape[0]
    rows = tq // SEARCH_GROUPS
    parts = [(score[g * rows:(g + 1) * rows], k_eff[g * rows:(g + 1) * rows]) for g in range(SEARCH_GROUPS)]

    def search(it, prefixes):
        bit = lax.shift_left(jnp.int32(1), 31 - it)
        out = []
        for (s, k), prefix in zip(parts, prefixes):
            cand = prefix + bit
            cnt = jnp.sum(jnp.where(s >= _key_to_float(cand), 1.0, 0.0), axis=1, keepdims=True)
            out.append(jnp.where(cnt >= k, cand, prefix))
        return tuple(out)

    init = tuple(jnp.full((rows, 1), INT_MIN, jnp.int32) for _ in range(SEARCH_GROUPS))
    thr = _key_to_float(jnp.concatenate(lax.fori_loop(0, 32, search, init, unroll=4), axis=0))
    ge = score >= thr
    n_ge = jnp.sum(jnp.where(ge, 1.0, 0.0), axis=1, keepdims=True)
    sel_ref[:, :kv] = jnp.where(ge, 0.0, NEG_BIG)
    has_tie = jnp.max(jnp.where(n_ge != k_eff, 1.0, 0.0)) > 0.5

    @pl.when(has_tie)
    def _():
        gt = score > thr
        need = k_eff - jnp.sum(jnp.where(gt, 1.0, 0.0), axis=1, keepdims=True)
        sel_ref[:, :kv] = jnp.where(_tie_select(score == thr, gt, need, kv), 0.0, NEG_BIG)


def _dsa_block(qa_ref, qi_ref, wi_ref, ka_ref, va_ref, ki_ref, o_ref, sel_ref, score_ref, *, kv, k_sel):
    tq = qa_ref.shape[0]
    f32 = jnp.float32
    bf = jnp.bfloat16
    q_pos = (kv - tq) + lax.broadcasted_iota(jnp.int32, (tq, 1), 0)
    key_pos = lax.broadcasted_iota(jnp.int32, (tq, kv), 1)
    causal = key_pos <= q_pos

    if kv <= k_sel:
        sel_ref[:, :kv] = jnp.where(causal, 0.0, NEG_BIG)
    else:
        ki = ki_ref[:kv, :]
        lane = lax.broadcasted_iota(jnp.int32, (tq, LANES), 1)
        score_ref[:, :kv] = jnp.zeros((tq, kv), f32)

        def idx_slab(c, carry):
            qc = qi_ref[:, pl.ds(pl.multiple_of(c * LANES, LANES), LANES)]
            acc = score_ref[:, :kv]
            for hh in range(2):
                qm = jnp.where(_head_mask(qc.shape, IDX_DIM, hh), qc, jnp.zeros_like(qc))
                s = lax.dot_general(qm, ki, _NT, preferred_element_type=f32)
                w = jnp.sum(jnp.where(lane == 2 * c + hh, wi_ref[...], 0.0), axis=1, keepdims=True)
                acc = acc + jnp.maximum(s, 0.0) * w
            score_ref[:, :kv] = acc
            return carry

        lax.fori_loop(0, IDX_HEADS // 2, idx_slab, 0, unroll=2)
        score = jnp.where(causal, score_ref[:, :kv], -jnp.inf)
        k_eff = jnp.minimum(q_pos + 1, k_sel).astype(f32)
        _select_topk(score, k_eff, sel_ref, kv)

    def attn_slab(c, carry):
        sl = pl.ds(pl.multiple_of(c * LANES, LANES), LANES)
        qc = qa_ref[:, sl]
        kc = ka_ref[:kv, sl]
        vc = va_ref[:kv, sl]
        mask = sel_ref[:, :kv]
        logits = []
        for hh in range(2):
            qm = jnp.where(_head_mask(qc.shape, A_HEAD_DIM, hh), qc, jnp.zeros_like(qc))
            logits.append(lax.dot_general(qm, kc, _NT, preferred_element_type=f32) + mask)
        o_ref[:, sl] = _two_head_softmax_pv(logits, vc, A_HEAD_DIM).astype(bf)
        return carry

    lax.fori_loop(0, A_WIDTH // LANES, attn_slab, 0, unroll=2 if kv >= ATTN_UNROLL_MIN_KEYS else 1)


def _per_query_block(block_fn, refs, n_blocks, tq):
    j = pl.program_id(1)
    for b in range(n_blocks):
        pl.when(j == b)(functools.partial(block_fn, *refs, kv=(b + 1) * tq))


def _dsa_kernel(*refs, seq, k_sel):
    tq = refs[0].shape[0]
    _per_query_block(functools.partial(_dsa_block, k_sel=k_sel), refs, seq // tq, tq)


def _dsa(qa, qi, wi, ka, va, ki, bsz, seq):
    tq = min(Q_TILE, seq)
    nq = seq // tq
    qspec = lambda w: pl.BlockSpec((tq, w), lambda b, i: (b * nq + i, 0))
    kspec = lambda w: pl.BlockSpec((seq, w), lambda b, i: (b, 0))
    k_sel = min(TOPK_MAX, seq // 4)
    return pl.pallas_call(
        functools.partial(_dsa_kernel, seq=seq, k_sel=k_sel),
        grid=(bsz, nq),
        in_specs=[qspec(A_WIDTH), qspec(A_WIDTH), qspec(LANES), kspec(A_WIDTH), kspec(A_WIDTH), kspec(LANES)],
        out_specs=qspec(A_WIDTH),
        out_shape=jax.ShapeDtypeStruct((bsz * seq, A_WIDTH), jnp.bfloat16),
        scratch_shapes=[pltpu.VMEM((tq, seq), jnp.float32), pltpu.VMEM((tq, seq), jnp.float32)],
        compiler_params=_cparams(("arbitrary", "arbitrary")),
        name="dsa",
    )(qa, qi, wi, ka, va, ki)


def _mla_block(qn_ref, qr_ref, kn_ref, kr_ref, vb_ref, o_ref, *, kv):
    tq = qn_ref.shape[0]
    f32 = jnp.float32
    bf = jnp.bfloat16
    diag = lax.broadcasted_iota(jnp.int32, (tq, tq), 1) <= lax.broadcasted_iota(jnp.int32, (tq, tq), 0)
    kr = kr_ref[:kv, :]

    def causal_masked(lg):
        own = jnp.where(diag, lg[:, kv - tq:], NEG_BIG)
        return own if kv == tq else jnp.concatenate([lg[:, :kv - tq], own], axis=1)

    def slab(c, carry):
        sl = pl.ds(pl.multiple_of(c * LANES, LANES), LANES)
        qc = qn_ref[:, sl]
        kcat = jnp.concatenate([kn_ref[:kv, sl], kr], axis=1)
        vc = vb_ref[:kv, sl]
        qrc = qr_ref[:, pl.ds(pl.multiple_of((c // 2) * LANES, LANES), LANES)]
        logits = []
        for hh in range(2):
            h = 2 * c + hh
            qm = jnp.where(_head_mask(qc.shape, QK_NOPE, hh), qc, jnp.zeros_like(qc))
            qrm = jnp.where(_head_mask(qrc.shape, QK_ROPE, h % 4), qrc, jnp.zeros_like(qrc))
            qcat = jnp.concatenate([qm, qrm], axis=1)
            lg = lax.dot_general(qcat, kcat, _NT, preferred_element_type=f32)
            logits.append(causal_masked(lg))
        o_ref[:, sl] = _two_head_softmax_pv(logits, vc, V_DIM).astype(bf)
        return carry

    lax.fori_loop(0, B_WIDTH // LANES, slab, 0, unroll=True)


def _mla_kernel(*refs, seq):
    tq = refs[0].shape[0]
    _per_query_block(_mla_block, refs, seq // tq, tq)


def _mla(qn, qr, kn, kr, vb, bsz, seq):
    tq = min(Q_TILE, seq)
    nq = seq // tq
    qspec = lambda w: pl.BlockSpec((tq, w), lambda b, i: (b * nq + i, 0))
    kspec = lambda w: pl.BlockSpec((seq, w), lambda b, i: (b, 0))
    return pl.pallas_call(
        functools.partial(_mla_kernel, seq=seq),
        grid=(bsz, nq),
        in_specs=[qspec(B_HEADS * QK_NOPE), qspec(B_HEADS * QK_ROPE), kspec(B_HEADS * QK_NOPE),
                  kspec(LANES), kspec(B_WIDTH)],
        out_specs=qspec(B_WIDTH),
        out_shape=jax.ShapeDtypeStruct((bsz * seq, B_WIDTH), jnp.bfloat16),
        compiler_params=_cparams(("arbitrary", "arbitrary")),
        name="mla",
    )(qn, qr, kn, kr, vb)


def _post_kernel(x_ref, oa_ref, ob_ref, wg_ref, bg_ref, wo_ref, g1_ref, b1_ref,
                 wrh_ref, wrl_ref, br_ref, h_ref, ri_ref, rf_ref, cnt_ref, carry_ref):
    f32 = jnp.float32
    bf = jnp.bfloat16
    t = x_ref.shape[0]

    @pl.when(pl.program_id(0) == 0)
    def _():
        carry_ref[...] = jnp.zeros_like(carry_ref)

    x = x_ref[...]
    gate = jax.nn.sigmoid(jnp.dot(x.astype(bf), wg_ref[...], preferred_element_type=f32) + bg_ref[...])
    pa = jnp.dot(oa_ref[...], wo_ref[:A_WIDTH, :], preferred_element_type=f32)
    pb = jnp.dot(ob_ref[...], wo_ref[A_WIDTH:A_WIDTH + B_WIDTH, :], preferred_element_type=f32)
    mix = gate[:, :D_MODEL] * pa + gate[:, D_MODEL:] * pb
    u = DN_ALPHA * x + jnp.dot(mix.astype(bf), wo_ref[A_WIDTH + B_WIDTH:, :], preferred_element_type=f32)
    h = _layer_norm(u, g1_ref[...], b1_ref[...])
    _store_token_rows(h_ref, h)

    h_hi = h.astype(bf)
    h_lo = (h - h_hi.astype(f32)).astype(bf)
    logits = (jnp.dot(h_hi, wrh_ref[...], preferred_element_type=f32)
              + jnp.dot(h_lo, wrh_ref[...], preferred_element_type=f32)
              + jnp.dot(h_hi, wrl_ref[...], preferred_element_type=f32) + br_ref[...])
    lane = lax.broadcasted_iota(jnp.int32, (t, LANES), 1).astype(f32)
    lg = jnp.where(lane < N_EXPERTS, logits, -jnp.inf)
    vals, ids = [], []
    assign = jnp.zeros((t, LANES), f32)
    for _k in range(TOP_K):
        m = jnp.max(lg, axis=1, keepdims=True)
        idx = jnp.min(jnp.where(lg == m, lane, float(LANES)), axis=1, keepdims=True)
        hit = lane == idx
        vals.append(m)
        ids.append(idx)
        assign = jnp.where(hit, 1.0, assign)
        lg = jnp.where(hit, -jnp.inf, lg)
    exps = [jnp.exp(v - vals[0]) for v in vals]
    den = exps[0] + exps[1] + exps[2] + exps[3]

    r = lax.broadcasted_iota(jnp.int32, (t, t), 0)
    c = lax.broadcasted_iota(jnp.int32, (t, t), 1)
    lower = jnp.where(c < r, 1.0, 0.0).astype(bf)
    before = jnp.dot(lower, assign.astype(bf), preferred_element_type=f32) + carry_ref[0:1, :]
    ri = jnp.zeros((t, LANES), f32)
    rf = jnp.zeros((t, LANES), f32)
    for k in range(TOP_K):
        rank = jnp.sum(jnp.where(lane == ids[k], before, 0.0), axis=1, keepdims=True)
        ri = jnp.where(lane == float(k), ids[k], ri)
        ri = jnp.where(lane == float(TOP_K + k), rank, ri)
        rf = jnp.where(lane == float(k), exps[k] / den, rf)
    ri_ref[...] = ri.T[:2 * TOP_K, :].astype(jnp.int32)
    rf_ref[...] = rf
    total = carry_ref[0:1, :] + jnp.sum(assign, axis=0, keepdims=True)
    carry_ref[...] = jnp.broadcast_to(total, carry_ref.shape)
    cnt_ref[...] = jnp.broadcast_to(total, cnt_ref.shape)


def _post(x2, oa, ob, w_slab, bg, wo, g1, b1, wrh, wrl, br):
    n = x2.shape[0]
    t = PROJ_TILE
    tok = lambda w: pl.BlockSpec((t, w), lambda i: (i, 0))
    consts = (bg, wo, g1, b1, wrh, wrl, br)
    gates_w = pl.BlockSpec((D_MODEL, 2 * D_MODEL), lambda i: (0, 0))
    return pl.pallas_call(
        _post_kernel,
        grid=(n // t,),
        in_specs=[tok(D_MODEL), tok(A_WIDTH), tok(B_WIDTH), gates_w] + [_full(a.shape) for a in consts],
        out_specs=[pl.BlockSpec((t * ROW_TILES, LANES), lambda i: (i, 0)),
                   pl.BlockSpec((2 * TOP_K, t), lambda i: (0, i)), tok(LANES), _full((8, LANES))],
        out_shape=[jax.ShapeDtypeStruct((n * ROW_TILES, LANES), jnp.float32),
                   jax.ShapeDtypeStruct((2 * TOP_K, n), jnp.int32),
                   jax.ShapeDtypeStruct((n, LANES), jnp.float32),
                   jax.ShapeDtypeStruct((8, LANES), jnp.float32)],
        scratch_shapes=[pltpu.VMEM((8, LANES), jnp.float32)],
        compiler_params=_cparams(("arbitrary",)),
        name="post",
    )(x2, oa, ob, w_slab, *consts)


def _token_row(ref, r):
    start = r * ROW_TILES if isinstance(r, int) else pl.multiple_of(r * ROW_TILES, ROW_TILES)
    return ref.at[pl.ds(start, ROW_TILES)]


def _row_gather(idx_ref, src_ref, dst_ref, sem, rows):
    for j in range(rows):
        pltpu.make_async_copy(_token_row(src_ref, idx_ref[0, 0, j]), _token_row(dst_ref, j), sem).start(
            priority=j % 2)


def _rows_wait(src_ref, dst_ref, sem):
    pltpu.make_async_copy(src_ref, dst_ref, sem).wait()


def _dispatch_kernel(dest_ref, h_ref, xs_ref, sem):
    t = DISPATCH_TILE
    for k in range(TOP_K):
        for j in range(t):
            pltpu.make_async_copy(_token_row(h_ref, j), _token_row(xs_ref, dest_ref[0, 0, k * t + j]), sem).start(
                priority=j % 2)
    for k in range(TOP_K):
        _rows_wait(h_ref, xs_ref.at[pl.ds(0, t * ROW_TILES)], sem)


def _dispatch(dest3, h, rows):
    t = DISPATCH_TILE
    n = h.shape[0] // ROW_TILES
    return pl.pallas_call(
        _dispatch_kernel,
        grid=(n // t,),
        in_specs=[pl.BlockSpec((1, 1, t * TOP_K), lambda i: (i, 0, 0), memory_space=pltpu.SMEM),
                  pl.BlockSpec((t * ROW_TILES, LANES), lambda i: (i, 0))],
        out_specs=pl.BlockSpec(memory_space=pl.ANY),
        out_shape=jax.ShapeDtypeStruct((rows * ROW_TILES, LANES), jnp.float32),
        scratch_shapes=[pltpu.SemaphoreType.DMA(())],
        compiler_params=_cparams(("arbitrary",)),
        name="dispatch",
    )(dest3, h)


def _expert_kernel(te_ref, tt_ref, lo_ref, hi_ref, ns_ref, xs_ref, wu_ref, bu_ref, wd_ref, bd_ref, ys_ref,
                   wub_ref, wdb_ref):
    i = pl.program_id(0)
    f32 = jnp.float32
    bf = jnp.bfloat16
    t = EXPERT_TILE
    valid = i < ns_ref[0]
    before = jnp.maximum(i - 1, 0)
    fresh = valid & ((i == 0) | (te_ref[i] != te_ref[before]))
    first_visit = (i == 0) | (tt_ref[i] != tt_ref[before])

    @pl.when(fresh)
    def _():
        rows = 128

        def cast_up(j, c):
            r = pl.multiple_of(j * rows, rows)
            wub_ref[pl.ds(r, rows), :] = wu_ref[pl.ds(r, rows), :].astype(bf)
            return c

        lax.fori_loop(0, D_MODEL // rows, cast_up, 0)

        def cast_dn(j, c):
            r = pl.multiple_of(j * rows, rows)
            wdb_ref[pl.ds(r, rows), :] = wd_ref[pl.ds(r, rows), :].astype(bf)
            return c

        lax.fori_loop(0, D_FF // rows, cast_dn, 0)

    def mlp(first, rows):
        xb = _load_token_rows(xs_ref, first, rows).astype(bf)
        y = jnp.broadcast_to(bd_ref[...], (rows, D_MODEL))
        for c in range(D_FF // FF_CHUNK):
            gs = slice(c * FF_CHUNK, (c + 1) * FF_CHUNK)
            ls = slice(D_FF + c * FF_CHUNK, D_FF + (c + 1) * FF_CHUNK)
            ag = jnp.dot(xb, wub_ref[:, gs], preferred_element_type=f32) + bu_ref[:, gs]
            al = jnp.dot(xb, wub_ref[:, ls], preferred_element_type=f32) + bu_ref[:, ls]
            g = jnp.minimum(ag, SWIGLU_LIMIT)
            lin = jnp.clip(al, -SWIGLU_LIMIT, SWIGLU_LIMIT)
            act = (lin + 1.0) * (g * jax.nn.sigmoid(SWIGLU_ALPHA * g))
            y = y + jnp.dot(act.astype(bf), wdb_ref[gs, :], preferred_element_type=f32)
        return y

    lo, hi = lo_ref[i], hi_ref[i]
    half = t // 2
    whole = (lo == 0) & (hi == t)
    partial = valid & jnp.logical_not(whole)

    @pl.when(valid & whole)
    def _():
        _store_token_rows(ys_ref, mlp(0, t))

    @pl.when(partial & first_visit)
    def _():
        ys_ref[...] = jnp.zeros_like(ys_ref)

    def fill(first, rows):
        row = first + lax.broadcasted_iota(jnp.int32, (rows, 1), 0)
        mine = (row >= lo) & (row < hi)
        _store_token_rows(ys_ref, jnp.where(mine, mlp(first, rows), _load_token_rows(ys_ref, first, rows)), first)

    pl.when(partial & (lo < half) & (hi > half))(functools.partial(fill, 0, t))
    pl.when(partial & (hi <= half))(functools.partial(fill, 0, half))
    pl.when(partial & (lo >= half))(functools.partial(fill, half, half))


def _experts(steps, xs, w_up, b_up, w_down, b_down):
    t = EXPERT_TILE
    max_steps = steps[0].shape[0]
    row_block = pl.BlockSpec((t * ROW_TILES, LANES), lambda i, te, tt, lo, hi, ns: (tt[i], 0))
    per_expert = lambda shape: pl.BlockSpec((None,) + shape, lambda i, te, tt, lo, hi, ns: (te[i], 0, 0))
    grid_spec = pltpu.PrefetchScalarGridSpec(
        num_scalar_prefetch=5,
        grid=(max_steps,),
        in_specs=[row_block, per_expert((D_MODEL, 2 * D_FF)), per_expert((1, 2 * D_FF)),
                  per_expert((D_FF, D_MODEL)), per_expert((1, D_MODEL))],
        out_specs=row_block,
        scratch_shapes=[pltpu.VMEM((D_MODEL, 2 * D_FF), jnp.bfloat16),
                        pltpu.VMEM((D_FF, D_MODEL), jnp.bfloat16)],
    )
    return pl.pallas_call(
        _expert_kernel,
        grid_spec=grid_spec,
        out_shape=jax.ShapeDtypeStruct(xs.shape, jnp.float32),
        compiler_params=_cparams(("arbitrary",)),
        name="experts",
    )(*steps, xs, w_up, b_up.reshape(N_EXPERTS, 1, 2 * D_FF), w_down, b_down.reshape(N_EXPERTS, 1, D_MODEL))


def _combine_kernel(cur_ref, nxt_ref, ys_ref, h_ref, rf_ref, g2_ref, b2_ref, o_ref, buf_ref, sems):
    i = pl.program_id(0)
    n_steps = pl.num_programs(0)
    t = COMBINE_TILE
    rows = t * TOP_K
    slot = i % 2

    @pl.when(i == 0)
    def _():
        _row_gather(cur_ref, ys_ref, buf_ref.at[0], sems.at[0], rows)

    _rows_wait(ys_ref.at[pl.ds(0, rows * ROW_TILES)], buf_ref.at[slot], sems.at[slot])
    _row_gather(nxt_ref, ys_ref, buf_ref.at[1 - slot], sems.at[1 - slot], rows)

    rf = rf_ref[...]
    u = DN_ALPHA * _load_token_rows(h_ref, 0, t)
    for k in range(TOP_K):
        u = u + rf[:, k:k + 1] * _load_token_rows(buf_ref.at[slot], k * t, t)
    o_ref[...] = _layer_norm(u, g2_ref[...], b2_ref[...])

    @pl.when(i == n_steps - 1)
    def _():
        _rows_wait(ys_ref.at[pl.ds(0, rows * ROW_TILES)], buf_ref.at[1 - slot], sems.at[1 - slot])


def _combine(dest3, ys, h, rf, g2, b2):
    t = COMBINE_TILE
    n = h.shape[0] // ROW_TILES
    last = n // t - 1
    tok = lambda w: pl.BlockSpec((t, w), lambda i: (i, 0))
    idx_spec = lambda f: pl.BlockSpec((1, 1, t * TOP_K), f, memory_space=pltpu.SMEM)
    return pl.pallas_call(
        _combine_kernel,
        grid=(n // t,),
        in_specs=[idx_spec(lambda i: (i, 0, 0)), idx_spec(lambda i: (jnp.minimum(i + 1, last), 0, 0)),
                  pl.BlockSpec(memory_space=pl.ANY), pl.BlockSpec((t * ROW_TILES, LANES), lambda i: (i, 0)),
                  tok(LANES), _full(g2.shape), _full(b2.shape)],
        out_specs=tok(D_MODEL),
        out_shape=jax.ShapeDtypeStruct((n, D_MODEL), jnp.float32),
        scratch_shapes=[pltpu.VMEM((2, t * TOP_K * ROW_TILES, LANES), jnp.float32),
                        pltpu.SemaphoreType.DMA((2,))],
        compiler_params=_cparams(("arbitrary",)),
        name="combine",
    )(dest3, dest3, ys, h, rf, g2, b2)


def _rope_tables(positions):
    pos = positions.astype(jnp.float32).reshape(1, -1)
    halves = (A_HEAD_DIM // 2, QK_ROPE // 2)
    pieces, rows, src = [], [], 0
    for half in halves:
        ang = (ROPE_THETA ** (-jnp.arange(half, dtype=jnp.float32) / half)).reshape(-1, 1) * pos
        pieces += [jnp.cos(ang), jnp.sin(ang)]
        for _ in range(2):
            rows.append(src + np.arange(LANES) % half)
            src += half
    trig = jnp.concatenate(pieces + [jnp.zeros((LANES - src, pos.shape[1]), jnp.float32)], axis=0)
    expand = (np.arange(LANES)[:, None] == np.concatenate(rows)[None, :]).astype(np.float32)
    return trig, jnp.asarray(expand, jnp.bfloat16)


def _lookup(table, idx):
    experts = jnp.arange(N_EXPERTS, dtype=jnp.int32)
    return jnp.sum(jnp.where(idx[..., None] == experts, table, 0), axis=-1)


def _expert_schedule(starts, ends, rows):
    t = EXPERT_TILE
    first_tile = starts // t
    n_tile = jnp.where(ends > starts, (ends - 1) // t - first_tile + 1, 0)
    cum = jnp.cumsum(n_tile)
    n_steps = cum[-1]
    max_steps = rows // t + N_EXPERTS
    s = jnp.minimum(jnp.arange(max_steps, dtype=jnp.int32), n_steps - 1)
    expert = jnp.sum((cum[None, :] <= s[:, None]).astype(jnp.int32), axis=1)
    tile = _lookup(first_tile, expert) + s - _lookup(cum - n_tile, expert)
    lo = jnp.maximum(_lookup(starts, expert) - tile * t, 0)
    hi = jnp.minimum(_lookup(ends, expert) - tile * t, t)
    return expert, tile, lo, hi, n_steps.astype(jnp.int32).reshape(1)


def _layer(x2, tabs, bsz, seq, w_in, b_gate, rms_cq, rms_ckv, w_uq, w_ukv, w_o_a, w_o_b, w_out,
           ln1_g, ln1_b, w_router, b_router, w_up, b_up, w_down, b_down, ln2_g, ln2_b):
    bf = jnp.bfloat16
    n = x2.shape[0]
    off = np.concatenate([[0], np.cumsum(SPLITS)]).tolist()
    w_in_b = w_in.astype(bf)
    col = lambda j: w_in_b[:, off[j]:off[j + 1]]
    w_slab = jnp.concatenate(
        [col(9), col(4), col(4), jnp.tile(col(8), (1, LANES // QK_ROPE)), col(5),
         jnp.zeros((D_MODEL, LANES - IDX_HEADS), bf), col(6), col(7)], axis=1)
    w_o = jnp.concatenate([w_o_a, w_o_b, w_out], axis=0).astype(bf)
    uq = w_uq.reshape(Q_LORA, B_HEADS, QK_NOPE + QK_ROPE)
    wuq = jnp.concatenate([uq[:, :, :QK_NOPE].reshape(Q_LORA, -1), uq[:, :, QK_NOPE:].reshape(Q_LORA, -1)],
                          axis=1).astype(bf)
    ukv = w_ukv.reshape(KV_LORA, B_HEADS, QK_NOPE + V_DIM)
    wukv = jnp.concatenate([ukv[:, :, :QK_NOPE].reshape(KV_LORA, -1), ukv[:, :, QK_NOPE:].reshape(KV_LORA, -1)],
                           axis=1).astype(bf)

    qa, ka, va, qi, ki, kr, wi, qn, qr, kn, vb = _proj(
        x2, tabs, w_in_b, w_slab, rms_cq.reshape(1, -1), rms_ckv.reshape(1, -1), wuq, wukv)
    o_a = _dsa(qa, qi, wi, ka, va, ki, bsz, seq)
    o_b = _mla(qn, qr, kn, kr, vb, bsz, seq)

    wr = jnp.concatenate([w_router, jnp.zeros((D_MODEL, LANES - N_EXPERTS), w_router.dtype)], axis=1)
    br = jnp.concatenate([b_router, jnp.zeros((LANES - N_EXPERTS,), b_router.dtype)]).reshape(1, -1)
    wr_hi = wr.astype(bf)
    wr_lo = (wr - wr_hi.astype(jnp.float32)).astype(bf)
    h, ri, rf, cnt = _post(x2, o_a, o_b, w_slab, b_gate.reshape(1, -1), w_o, ln1_g.reshape(1, -1),
                           ln1_b.reshape(1, -1), wr_hi, wr_lo, br)

    counts = cnt[0, :N_EXPERTS].astype(jnp.int32)
    ends = jnp.cumsum(counts)
    starts = ends - counts
    dest = _lookup(starts, ri[:TOP_K]) + ri[TOP_K:]
    by_step = lambda t: dest.reshape(TOP_K, n // t, t).transpose(1, 0, 2).reshape(n // t, 1, t * TOP_K)

    xs = _dispatch(by_step(DISPATCH_TILE), h, n * TOP_K)
    ys = _experts(_expert_schedule(starts, ends, n * TOP_K), xs, w_up, b_up, w_down, b_down)
    return _combine(by_step(COMBINE_TILE), ys, h, rf, ln2_g.reshape(1, -1), ln2_b.reshape(1, -1))


def kernel(x, positions, w_in, b_gate, rms_cq, rms_ckv, w_uq, w_ukv, w_o_a, w_o_b, w_out, ln1_g, ln1_b,
           w_router, b_router, w_up, b_up, w_down, b_down, ln2_g, ln2_b):
    bsz, seq, _ = x.shape
    x2 = x.reshape(bsz * seq, D_MODEL)
    tabs = _rope_tables(positions)
    for l in range(DEPTH):
        x2 = _layer(x2, tabs, bsz, seq, w_in[l], b_gate[l], rms_cq[l], rms_ckv[l], w_uq[l], w_ukv[l],
                    w_o_a[l], w_o_b[l], w_out[l], ln1_g[l], ln1_b[l], w_router[l], b_router[l],
                    w_up[l], b_up[l], w_down[l], b_down[l], ln2_g[l], ln2_b[l])
    return x2.reshape(bsz, seq, D_MODEL)
```

```python
import functools

import jax
import jax.numpy as jnp
import numpy as np
from jax import lax
from jax.experimental import pallas as pl
from jax.experimental.pallas import tpu as pltpu

D_MODEL = 1024
A_HEADS = 8
A_HEAD_DIM = 64
IDX_HEADS = 8
IDX_DIM = 64
TOPK_MAX = 256
B_HEADS = 8
Q_LORA = 384
KV_LORA = 256
QK_NOPE = 64
QK_ROPE = 32
V_DIM = 64
ROPE_THETA = 10000.0
N_EXPERTS = 32
TOP_K = 4
D_FF = 1024
SWIGLU_LIMIT = 7.0
SWIGLU_ALPHA = 1.702
DEPTH = 1
DN_ALPHA = (2 * DEPTH) ** 0.25
LN_EPS = 1e-5
RMS_EPS = 1e-6
A_WIDTH = A_HEADS * A_HEAD_DIM
B_WIDTH = B_HEADS * V_DIM
SPLITS = (A_WIDTH, A_WIDTH, A_WIDTH, IDX_HEADS * IDX_DIM, IDX_DIM, IDX_HEADS,
          Q_LORA, KV_LORA, QK_ROPE, 2 * D_MODEL)
LANES = 128
VMEM_LIMIT_BYTES = 52 * 1024 * 1024
SMALL_COLS = 3 * LANES + Q_LORA + KV_LORA
assert (2 * D_MODEL) % SMALL_COLS == 0

PROJ_TILE = 512
Q_TILE = 256
SEARCH_GROUPS = 2
ATTN_UNROLL_MIN_KEYS = 1024
EXPERT_TILE = 512
FF_CHUNK = 1024
DISPATCH_TILE = 512
COMBINE_TILE = 256

ROW_TILES = D_MODEL // LANES
assert ROW_TILES == 8

LOG2_E = 1.4426950408889634
INT_MIN = -(2 ** 31)
NEG_BIG = -1e30

_NT = (((1,), (1,)), ((), ()))


def _cparams(sem):
    return pltpu.CompilerParams(dimension_semantics=sem, vmem_limit_bytes=VMEM_LIMIT_BYTES)


def _full(shape):
    nd = len(shape)
    return pl.BlockSpec(shape, lambda *_: (0,) * nd)


def _rope_chunk(x, cos, sin, half):
    lane = lax.broadcasted_iota(jnp.int32, x.shape, 1)
    first = (lane % (2 * half)) < half
    up = pltpu.roll(x, LANES - half, 1)
    dn = pltpu.roll(x, half, 1)
    return x * cos + jnp.where(first, -up, dn) * sin


def _layer_norm(u, g, b):
    mu = jnp.mean(u, axis=-1, keepdims=True)
    d = u - mu
    var = jnp.mean(d * d, axis=-1, keepdims=True)
    return d * lax.rsqrt(var + LN_EPS) * g + b


def _load_token_rows(ref, first, t):
    return jnp.concatenate([ref[pl.ds(first * ROW_TILES + s, t, stride=ROW_TILES), :] for s in range(ROW_TILES)],
                           axis=1)


def _store_token_rows(ref, v, first=0):
    t = v.shape[0]
    for s in range(ROW_TILES):
        ref[pl.ds(first * ROW_TILES + s, t, stride=ROW_TILES), :] = v[:, s * LANES:(s + 1) * LANES]


def _two_head_softmax_pv(logits, vc, width):
    lane_v = lax.broadcasted_iota(jnp.int32, vc.shape, 1)
    pvs = []
    for hh in range(2):
        lg = logits[hh]
        p = jnp.exp2(lg - jnp.max(lg, axis=1, keepdims=True)).astype(jnp.bfloat16)
        v_h = jnp.where((lane_v // width) == hh, vc, jnp.ones_like(vc))
        pvs.append(jnp.dot(p, v_h, preferred_element_type=jnp.float32))
    lane = lax.broadcasted_iota(jnp.int32, pvs[0].shape, 1)
    first = lane < width
    num = jnp.where(first, pvs[0], pvs[1])
    den = jnp.where(first, pltpu.roll(pvs[0], width, 1), pltpu.roll(pvs[1], width, 1))
    return num / den


def _head_mask(shape, width, which):
    lane = lax.broadcasted_iota(jnp.int32, shape, 1)
    return (lane // width) == which


def _proj_kernel(x_ref, trig_ref, expand_ref, w_main_ref, w_rest_ref,
                 gcq_ref, gckv_ref, wuq_ref, wukv_ref,
                 qa_ref, ka_ref, va_ref, qi_ref, ki_ref, kr_ref, wi_ref,
                 qn_ref, qr_ref, kn_ref, vb_ref):
    xb = x_ref[...].astype(jnp.bfloat16)
    f32 = jnp.float32
    bf = jnp.bfloat16

    trig = trig_ref[...].T
    trig_hi = trig.astype(bf)
    trig_lo = (trig - trig_hi.astype(f32)).astype(bf)
    tabs = (jnp.dot(trig_hi, expand_ref[...], preferred_element_type=f32)
            + jnp.dot(trig_lo, expand_ref[...], preferred_element_type=f32))
    c64, s64, c32, s32 = (tabs[:, j * LANES:(j + 1) * LANES] for j in range(4))

    z1 = jnp.dot(xb, w_main_ref[...], preferred_element_type=f32)
    n_chunk = A_WIDTH // LANES
    a_scale = A_HEAD_DIM ** -0.5 * LOG2_E
    for c in range(n_chunk):
        sl = slice(c * LANES, (c + 1) * LANES)
        qa_ref[:, sl] = (_rope_chunk(z1[:, c * LANES:(c + 1) * LANES], c64, s64, 32) * a_scale).astype(bf)
        o = A_WIDTH + c * LANES
        ka_ref[:, sl] = _rope_chunk(z1[:, o:o + LANES], c64, s64, 32).astype(bf)
        o = 2 * A_WIDTH + c * LANES
        va_ref[:, sl] = z1[:, o:o + LANES].astype(bf)
        o = 3 * A_WIDTH + c * LANES
        qi_ref[:, sl] = _rope_chunk(z1[:, o:o + LANES], c64, s64, 32).astype(bf)

    small = 3 * LANES
    zs = jnp.dot(xb, w_rest_ref[:, :small], preferred_element_type=f32)
    ki_ref[...] = _rope_chunk(zs[:, 0:LANES], c64, s64, 32).astype(bf)
    kr_ref[...] = _rope_chunk(zs[:, LANES:2 * LANES], c32, s32, 16).astype(bf)
    wi_ref[...] = zs[:, 2 * LANES:3 * LANES] * (IDX_HEADS ** -0.5 * IDX_DIM ** -0.5)

    zl = jnp.dot(xb, w_rest_ref[:, small:], preferred_element_type=f32)
    cq = zl[:, :Q_LORA]
    cq = cq * lax.rsqrt(jnp.mean(cq * cq, axis=-1, keepdims=True) + RMS_EPS) * gcq_ref[...]
    ckv = zl[:, Q_LORA:]
    ckv = ckv * lax.rsqrt(jnp.mean(ckv * ckv, axis=-1, keepdims=True) + RMS_EPS) * gckv_ref[...]
    qb = jnp.dot(cq.astype(bf), wuq_ref[...], preferred_element_type=f32)
    kv = jnp.dot(ckv.astype(bf), wukv_ref[...], preferred_element_type=f32)
    b_scale = (QK_NOPE + QK_ROPE) ** -0.5 * LOG2_E
    qn_ref[...] = (qb[:, :B_HEADS * QK_NOPE] * b_scale).astype(bf)
    for c in range(B_HEADS * QK_ROPE // LANES):
        o = B_HEADS * QK_NOPE + c * LANES
        qr_ref[:, c * LANES:(c + 1) * LANES] = (
            _rope_chunk(qb[:, o:o + LANES], c32, s32, 16) * b_scale).astype(bf)
    kn_ref[...] = kv[:, :B_HEADS * QK_NOPE].astype(bf)
    vb_ref[...] = kv[:, B_HEADS * QK_NOPE:].astype(bf)


def _proj(x2, tabs, w_in_b, w_slab, gcq, gckv, wuq, wukv):
    n = x2.shape[0]
    t = PROJ_TILE
    tok = lambda w: pl.BlockSpec((t, w), lambda i: (i, 0))
    bf = jnp.bfloat16
    outs = [(A_WIDTH, bf)] * 4 + [(LANES, bf), (LANES, bf), (LANES, jnp.float32),
                                  (B_HEADS * QK_NOPE, bf), (B_HEADS * QK_ROPE, bf),
                                  (B_HEADS * QK_NOPE, bf), (B_WIDTH, bf)]
    return pl.pallas_call(
        _proj_kernel,
        grid=(n // t,),
        in_specs=[tok(D_MODEL), pl.BlockSpec((LANES, t), lambda i: (0, i)), _full(tabs[1].shape),
                  pl.BlockSpec((D_MODEL, 4 * A_WIDTH), lambda i: (0, 0)),
                  pl.BlockSpec((D_MODEL, SMALL_COLS), lambda i: (0, 2 * D_MODEL // SMALL_COLS))]
        + [_full(a.shape) for a in (gcq, gckv, wuq, wukv)],
        out_specs=[tok(w) for w, _ in outs],
        out_shape=[jax.ShapeDtypeStruct((n, w), d) for w, d in outs],
        compiler_params=_cparams(("arbitrary",)),
        name="proj",
    )(x2, *tabs, w_in_b, w_slab, gcq, gckv, wuq, wukv)


def _tie_select(eq, gt, need, kv):
    chunk = 256
    r = lax.broadcasted_iota(jnp.int32, (chunk, chunk), 0)
    c = lax.broadcasted_iota(jnp.int32, (chunk, chunk), 1)
    upper = jnp.where(r <= c, 1.0, 0.0).astype(jnp.bfloat16)
    carry = jnp.zeros((eq.shape[0], 1), jnp.float32)
    parts = []
    for j in range(kv // chunk):
        e = eq[:, j * chunk:(j + 1) * chunk]
        ef = jnp.where(e, 1.0, 0.0).astype(jnp.bfloat16)
        pref = jnp.dot(ef, upper, preferred_element_type=jnp.float32) + carry
        parts.append(gt[:, j * chunk:(j + 1) * chunk] | (e & (pref <= need)))
        carry = carry + jnp.sum(ef.astype(jnp.float32), axis=1, keepdims=True)
    return jnp.concatenate(parts, axis=1)


def _key_to_float(key):
    return lax.bitcast_convert_type(key ^ ((key >> 31) & jnp.int32(0x7FFFFFFF)), jnp.float32)


def _select_topk(score, k_eff, sel_ref, kv):
    tq = score.shape[0]
    rows = tq // SEARCH_GROUPS
    parts = [(score[g * rows:(g + 1) * rows], k_eff[g * rows:(g + 1) * rows]) for g in range(SEARCH_GROUPS)]

    def search(it, prefixes):
        bit = lax.shift_left(jnp.int32(1), 31 - it)
        out = []
        for (s, k), prefix in zip(parts, prefixes):
            cand = prefix + bit
            cnt = jnp.sum(jnp.where(s >= _key_to_float(cand), 1.0, 0.0), axis=1, keepdims=True)
            out.append(jnp.where(cnt >= k, cand, prefix))
        return tuple(out)

    init = tuple(jnp.full((rows, 1), INT_MIN, jnp.int32) for _ in range(SEARCH_GROUPS))
    thr = _key_to_float(jnp.concatenate(lax.fori_loop(0, 32, search, init, unroll=4), axis=0))
    ge = score >= thr
    n_ge = jnp.sum(jnp.where(ge, 1.0, 0.0), axis=1, keepdims=True)
    sel_ref[:, :kv] = jnp.where(ge, 0.0, NEG_BIG)
    has_tie = jnp.max(jnp.where(n_ge != k_eff, 1.0, 0.0)) > 0.5

    @pl.when(has_tie)
    def _():
        gt = score > thr
        need = k_eff - jnp.sum(jnp.where(gt, 1.0, 0.0), axis=1, keepdims=True)
        sel_ref[:, :kv] = jnp.where(_tie_select(score == thr, gt, need, kv), 0.0, NEG_BIG)


def _dsa_block(qa_ref, qi_ref, wi_ref, ka_ref, va_ref, ki_ref, o_ref, sel_ref, score_ref, *, kv, k_sel):
    tq = qa_ref.shape[0]
    f32 = jnp.float32
    bf = jnp.bfloat16
    q_pos = (kv - tq) + lax.broadcasted_iota(jnp.int32, (tq, 1), 0)
    key_pos = lax.broadcasted_iota(jnp.int32, (tq, kv), 1)
    causal = key_pos <= q_pos

    if kv <= k_sel:
        sel_ref[:, :kv] = jnp.where(causal, 0.0, NEG_BIG)
    else:
        ki = ki_ref[:kv, :]
        lane = lax.broadcasted_iota(jnp.int32, (tq, LANES), 1)
        score_ref[:, :kv] = jnp.zeros((tq, kv), f32)

        def idx_slab(c, carry):
            qc = qi_ref[:, pl.ds(pl.multiple_of(c * LANES, LANES), LANES)]
            acc = score_ref[:, :kv]
            for hh in range(2):
                qm = jnp.where(_head_mask(qc.shape, IDX_DIM, hh), qc, jnp.zeros_like(qc))
                s = lax.dot_general(qm, ki, _NT, preferred_element_type=f32)
                w = jnp.sum(jnp.where(lane == 2 * c + hh, wi_ref[...], 0.0), axis=1, keepdims=True)
                acc = acc + jnp.maximum(s, 0.0) * w
            score_ref[:, :kv] = acc
            return carry

        lax.fori_loop(0, IDX_HEADS // 2, idx_slab, 0, unroll=2)
        score = jnp.where(causal, score_ref[:, :kv], -jnp.inf)
        k_eff = jnp.minimum(q_pos + 1, k_sel).astype(f32)
        _select_topk(score, k_eff, sel_ref, kv)

    def attn_slab(c, carry):
        sl = pl.ds(pl.multiple_of(c * LANES, LANES), LANES)
        qc = qa_ref[:, sl]
        kc = ka_ref[:kv, sl]
        vc = va_ref[:kv, sl]
        mask = sel_ref[:, :kv]
        logits = []
        for hh in range(2):
            qm = jnp.where(_head_mask(qc.shape, A_HEAD_DIM, hh), qc, jnp.zeros_like(qc))
            logits.append(lax.dot_general(qm, kc, _NT, preferred_element_type=f32) + mask)
        o_ref[:, sl] = _two_head_softmax_pv(logits, vc, A_HEAD_DIM).astype(bf)
        return carry

    lax.fori_loop(0, A_WIDTH // LANES, attn_slab, 0, unroll=2 if kv >= ATTN_UNROLL_MIN_KEYS else 1)


def _per_query_block(block_fn, refs, n_blocks, tq):
    j = pl.program_id(1)
    for b in range(n_blocks):
        pl.when(j == b)(functools.partial(block_fn, *refs, kv=(b + 1) * tq))


def _dsa_kernel(*refs, seq, k_sel):
    tq = refs[0].shape[0]
    _per_query_block(functools.partial(_dsa_block, k_sel=k_sel), refs, seq // tq, tq)


def _dsa(qa, qi, wi, ka, va, ki, bsz, seq):
    tq = min(Q_TILE, seq)
    nq = seq // tq
    qspec = lambda w: pl.BlockSpec((tq, w), lambda b, i: (b * nq + i, 0))
    kspec = lambda w: pl.BlockSpec((seq, w), lambda b, i: (b, 0))
    k_sel = min(TOPK_MAX, seq // 4)
    return pl.pallas_call(
        functools.partial(_dsa_kernel, seq=seq, k_sel=k_sel),
        grid=(bsz, nq),
        in_specs=[qspec(A_WIDTH), qspec(A_WIDTH), qspec(LANES), kspec(A_WIDTH), kspec(A_WIDTH), kspec(LANES)],
        out_specs=qspec(A_WIDTH),
        out_shape=jax.ShapeDtypeStruct((bsz * seq, A_WIDTH), jnp.bfloat16),
        scratch_shapes=[pltpu.VMEM((tq, seq), jnp.float32), pltpu.VMEM((tq, seq), jnp.float32)],
        compiler_params=_cparams(("arbitrary", "arbitrary")),
        name="dsa",
    )(qa, qi, wi, ka, va, ki)


def _mla_block(qn_ref, qr_ref, kn_ref, kr_ref, vb_ref, o_ref, *, kv):
    tq = qn_ref.shape[0]
    f32 = jnp.float32
    bf = jnp.bfloat16
    diag = lax.broadcasted_iota(jnp.int32, (tq, tq), 1) <= lax.broadcasted_iota(jnp.int32, (tq, tq), 0)
    kr = kr_ref[:kv, :]

    def causal_masked(lg):
        own = jnp.where(diag, lg[:, kv - tq:], NEG_BIG)
        return own if kv == tq else jnp.concatenate([lg[:, :kv - tq], own], axis=1)

    def slab(c, carry):
        sl = pl.ds(pl.multiple_of(c * LANES, LANES), LANES)
        qc = qn_ref[:, sl]
        kcat = jnp.concatenate([kn_ref[:kv, sl], kr], axis=1)
        vc = vb_ref[:kv, sl]
        qrc = qr_ref[:, pl.ds(pl.multiple_of((c // 2) * LANES, LANES), LANES)]
        logits = []
        for hh in range(2):
            h = 2 * c + hh
            qm = jnp.where(_head_mask(qc.shape, QK_NOPE, hh), qc, jnp.zeros_like(qc))
            qrm = jnp.where(_head_mask(qrc.shape, QK_ROPE, h % 4), qrc, jnp.zeros_like(qrc))
            qcat = jnp.concatenate([qm, qrm], axis=1)
            lg = lax.dot_general(qcat, kcat, _NT, preferred_element_type=f32)
            logits.append(causal_masked(lg))
        o_ref[:, sl] = _two_head_softmax_pv(logits, vc, V_DIM).astype(bf)
        return carry

    lax.fori_loop(0, B_WIDTH // LANES, slab, 0, unroll=True)


def _mla_kernel(*refs, seq):
    tq = refs[0].shape[0]
    _per_query_block(_mla_block, refs, seq // tq, tq)


def _mla(qn, qr, kn, kr, vb, bsz, seq):
    tq = min(Q_TILE, seq)
    nq = seq // tq
    qspec = lambda w: pl.BlockSpec((tq, w), lambda b, i: (b * nq + i, 0))
    kspec = lambda w: pl.BlockSpec((seq, w), lambda b, i: (b, 0))
    return pl.pallas_call(
        functools.partial(_mla_kernel, seq=seq),
        grid=(bsz, nq),
        in_specs=[qspec(B_HEADS * QK_NOPE), qspec(B_HEADS * QK_ROPE), kspec(B_HEADS * QK_NOPE),
                  kspec(LANES), kspec(B_WIDTH)],
        out_specs=qspec(B_WIDTH),
        out_shape=jax.ShapeDtypeStruct((bsz * seq, B_WIDTH), jnp.bfloat16),
        compiler_params=_cparams(("arbitrary", "arbitrary")),
        name="mla",
    )(qn, qr, kn, kr, vb)


def _post_kernel(x_ref, oa_ref, ob_ref, wg_ref, bg_ref, wo_ref, g1_ref, b1_ref,
                 wrh_ref, wrl_ref, br_ref, h_ref, ri_ref, rf_ref, cnt_ref, carry_ref):
    f32 = jnp.float32
    bf = jnp.bfloat16
    t = x_ref.shape[0]

    @pl.when(pl.program_id(0) == 0)
    def _():
        carry_ref[...] = jnp.zeros_like(carry_ref)

    x = x_ref[...]
    gate = jax.nn.sigmoid(jnp.dot(x.astype(bf), wg_ref[...], preferred_element_type=f32) + bg_ref[...])
    pa = jnp.dot(oa_ref[...], wo_ref[:A_WIDTH, :], preferred_element_type=f32)
    pb = jnp.dot(ob_ref[...], wo_ref[A_WIDTH:A_WIDTH + B_WIDTH, :], preferred_element_type=f32)
    mix = gate[:, :D_MODEL] * pa + gate[:, D_MODEL:] * pb
    u = DN_ALPHA * x + jnp.dot(mix.astype(bf), wo_ref[A_WIDTH + B_WIDTH:, :], preferred_element_type=f32)
    h = _layer_norm(u, g1_ref[...], b1_ref[...])
    _store_token_rows(h_ref, h)

    h_hi = h.astype(bf)
    h_lo = (h - h_hi.astype(f32)).astype(bf)
    logits = (jnp.dot(h_hi, wrh_ref[...], preferred_element_type=f32)
              + jnp.dot(h_lo, wrh_ref[...], preferred_element_type=f32)
              + jnp.dot(h_hi, wrl_ref[...], preferred_element_type=f32) + br_ref[...])
    lane = lax.broadcasted_iota(jnp.int32, (t, LANES), 1).astype(f32)
    lg = jnp.where(lane < N_EXPERTS, logits, -jnp.inf)
    vals, ids = [], []
    assign = jnp.zeros((t, LANES), f32)
    for _k in range(TOP_K):
        m = jnp.max(lg, axis=1, keepdims=True)
        idx = jnp.min(jnp.where(lg == m, lane, float(LANES)), axis=1, keepdims=True)
        hit = lane == idx
        vals.append(m)
        ids.append(idx)
        assign = jnp.where(hit, 1.0, assign)
        lg = jnp.where(hit, -jnp.inf, lg)
    exps = [jnp.exp(v - vals[0]) for v in vals]
    den = exps[0] + exps[1] + exps[2] + exps[3]

    r = lax.broadcasted_iota(jnp.int32, (t, t), 0)
    c = lax.broadcasted_iota(jnp.int32, (t, t), 1)
    lower = jnp.where(c < r, 1.0, 0.0).astype(bf)
    before = jnp.dot(lower, assign.astype(bf), preferred_element_type=f32) + carry_ref[0:1, :]
    ri = jnp.zeros((t, LANES), f32)
    rf = jnp.zeros((t, LANES), f32)
    for k in range(TOP_K):
        rank = jnp.sum(jnp.where(lane == ids[k], before, 0.0), axis=1, keepdims=True)
        ri = jnp.where(lane == float(k), ids[k], ri)
        ri = jnp.where(lane == float(TOP_K + k), rank, ri)
        rf = jnp.where(lane == float(k), exps[k] / den, rf)
    ri_ref[...] = ri.T[:2 * TOP_K, :].astype(jnp.int32)
    rf_ref[...] = rf
    total = carry_ref[0:1, :] + jnp.sum(assign, axis=0, keepdims=True)
    carry_ref[...] = jnp.broadcast_to(total, carry_ref.shape)
    cnt_ref[...] = jnp.broadcast_to(total, cnt_ref.shape)


def _post(x2, oa, ob, w_slab, bg, wo, g1, b1, wrh, wrl, br):
    n = x2.shape[0]
    t = PROJ_TILE
    tok = lambda w: pl.BlockSpec((t, w), lambda i: (i, 0))
    consts = (bg, wo, g1, b1, wrh, wrl, br)
    gates_w = pl.BlockSpec((D_MODEL, 2 * D_MODEL), lambda i: (0, 0))
    return pl.pallas_call(
        _post_kernel,
        grid=(n // t,),
        in_specs=[tok(D_MODEL), tok(A_WIDTH), tok(B_WIDTH), gates_w] + [_full(a.shape) for a in consts],
        out_specs=[pl.BlockSpec((t * ROW_TILES, LANES), lambda i: (i, 0)),
                   pl.BlockSpec((2 * TOP_K, t), lambda i: (0, i)), tok(LANES), _full((8, LANES))],
        out_shape=[jax.ShapeDtypeStruct((n * ROW_TILES, LANES), jnp.float32),
                   jax.ShapeDtypeStruct((2 * TOP_K, n), jnp.int32),
                   jax.ShapeDtypeStruct((n, LANES), jnp.float32),
                   jax.ShapeDtypeStruct((8, LANES), jnp.float32)],
        scratch_shapes=[pltpu.VMEM((8, LANES), jnp.float32)],
        compiler_params=_cparams(("arbitrary",)),
        name="post",
    )(x2, oa, ob, w_slab, *consts)


def _token_row(ref, r):
    start = r * ROW_TILES if isinstance(r, int) else pl.multiple_of(r * ROW_TILES, ROW_TILES)
    return ref.at[pl.ds(start, ROW_TILES)]


def _row_gather(idx_ref, src_ref, dst_ref, sem, rows):
    for j in range(rows):
        pltpu.make_async_copy(_token_row(src_ref, idx_ref[0, 0, j]), _token_row(dst_ref, j), sem).start(
            priority=j % 2)


def _rows_wait(src_ref, dst_ref, sem):
    pltpu.make_async_copy(src_ref, dst_ref, sem).wait()


def _dispatch_kernel(dest_ref, h_ref, xs_ref, sem):
    t = DISPATCH_TILE
    for k in range(TOP_K):
        for j in range(t):
            pltpu.make_async_copy(_token_row(h_ref, j), _token_row(xs_ref, dest_ref[0, 0, k * t + j]), sem).start(
                priority=j % 2)
    for k in range(TOP_K):
        _rows_wait(h_ref, xs_ref.at[pl.ds(0, t * ROW_TILES)], sem)


def _dispatch(dest3, h, rows):
    t = DISPATCH_TILE
    n = h.shape[0] // ROW_TILES
    return pl.pallas_call(
        _dispatch_kernel,
        grid=(n // t,),
        in_specs=[pl.BlockSpec((1, 1, t * TOP_K), lambda i: (i, 0, 0), memory_space=pltpu.SMEM),
                  pl.BlockSpec((t * ROW_TILES, LANES), lambda i: (i, 0))],
        out_specs=pl.BlockSpec(memory_space=pl.ANY),
        out_shape=jax.ShapeDtypeStruct((rows * ROW_TILES, LANES), jnp.float32),
        scratch_shapes=[pltpu.SemaphoreType.DMA(())],
        compiler_params=_cparams(("arbitrary",)),
        name="dispatch",
    )(dest3, h)


def _expert_kernel(te_ref, tt_ref, lo_ref, hi_ref, ns_ref, xs_ref, wu_ref, bu_ref, wd_ref, bd_ref, ys_ref,
                   wub_ref, wdb_ref):
    i = pl.program_id(0)
    f32 = jnp.float32
    bf = jnp.bfloat16
    t = EXPERT_TILE
    valid = i < ns_ref[0]
    before = jnp.maximum(i - 1, 0)
    fresh = valid & ((i == 0) | (te_ref[i] != te_ref[before]))
    first_visit = (i == 0) | (tt_ref[i] != tt_ref[before])

    @pl.when(fresh)
    def _():
        rows = 128

        def cast_up(j, c):
            r = pl.multiple_of(j * rows, rows)
            wub_ref[pl.ds(r, rows), :] = wu_ref[pl.ds(r, rows), :].astype(bf)
            return c

        lax.fori_loop(0, D_MODEL // rows, cast_up, 0)

        def cast_dn(j, c):
            r = pl.multiple_of(j * rows, rows)
            wdb_ref[pl.ds(r, rows), :] = wd_ref[pl.ds(r, rows), :].astype(bf)
            return c

        lax.fori_loop(0, D_FF // rows, cast_dn, 0)

    def mlp(first, rows):
        xb = _load_token_rows(xs_ref, first, rows).astype(bf)
        y = jnp.broadcast_to(bd_ref[...], (rows, D_MODEL))
        for c in range(D_FF // FF_CHUNK):
            gs = slice(c * FF_CHUNK, (c + 1) * FF_CHUNK)
            ls = slice(D_FF + c * FF_CHUNK, D_FF + (c + 1) * FF_CHUNK)
            ag = jnp.dot(xb, wub_ref[:, gs], preferred_element_type=f32) + bu_ref[:, gs]
            al = jnp.dot(xb, wub_ref[:, ls], preferred_element_type=f32) + bu_ref[:, ls]
            g = jnp.minimum(ag, SWIGLU_LIMIT)
            lin = jnp.clip(al, -SWIGLU_LIMIT, SWIGLU_LIMIT)
            act = (lin + 1.0) * (g * jax.nn.sigmoid(SWIGLU_ALPHA * g))
            y = y + jnp.dot(act.astype(bf), wdb_ref[gs, :], preferred_element_type=f32)
        return y

    lo, hi = lo_ref[i], hi_ref[i]
    half = t // 2
    whole = (lo == 0) & (hi == t)
    partial = valid & jnp.logical_not(whole)

    @pl.when(valid & whole)
    def _():
        _store_token_rows(ys_ref, mlp(0, t))

    @pl.when(partial & first_visit)
    def _():
        ys_ref[...] = jnp.zeros_like(ys_ref)

    def fill(first, rows):
        row = first + lax.broadcasted_iota(jnp.int32, (rows, 1), 0)
        mine = (row >= lo) & (row < hi)
        _store_token_rows(ys_ref, jnp.where(mine, mlp(first, rows), _load_token_rows(ys_ref, first, rows)), first)

    pl.when(partial & (lo < half) & (hi > half))(functools.partial(fill, 0, t))
    pl.when(partial & (hi <= half))(functools.partial(fill, 0, half))
    pl.when(partial & (lo >= half))(functools.partial(fill, half, half))


def _experts(steps, xs, w_up, b_up, w_down, b_down):
    t = EXPERT_TILE
    max_steps = steps[0].shape[0]
    row_block = pl.BlockSpec((t * ROW_TILES, LANES), lambda i, te, tt, lo, hi, ns: (tt[i], 0))
    per_expert = lambda shape: pl.BlockSpec((None,) + shape, lambda i, te, tt, lo, hi, ns: (te[i], 0, 0))
    grid_spec = pltpu.PrefetchScalarGridSpec(
        num_scalar_prefetch=5,
        grid=(max_steps,),
        in_specs=[row_block, per_expert((D_MODEL, 2 * D_FF)), per_expert((1, 2 * D_FF)),
                  per_expert((D_FF, D_MODEL)), per_expert((1, D_MODEL))],
        out_specs=row_block,
        scratch_shapes=[pltpu.VMEM((D_MODEL, 2 * D_FF), jnp.bfloat16),
                        pltpu.VMEM((D_FF, D_MODEL), jnp.bfloat16)],
    )
    return pl.pallas_call(
        _expert_kernel,
        grid_spec=grid_spec,
        out_shape=jax.ShapeDtypeStruct(xs.shape, jnp.float32),
        compiler_params=_cparams(("arbitrary",)),
        name="experts",
    )(*steps, xs, w_up, b_up.reshape(N_EXPERTS, 1, 2 * D_FF), w_down, b_down.reshape(N_EXPERTS, 1, D_MODEL))


def _combine_kernel(cur_ref, nxt_ref, ys_ref, h_ref, rf_ref, g2_ref, b2_ref, o_ref, buf_ref, sems):
    i = pl.program_id(0)
    n_steps = pl.num_programs(0)
    t = COMBINE_TILE
    rows = t * TOP_K
    slot = i % 2

    @pl.when(i == 0)
    def _():
        _row_gather(cur_ref, ys_ref, buf_ref.at[0], sems.at[0], rows)

    _rows_wait(ys_ref.at[pl.ds(0, rows * ROW_TILES)], buf_ref.at[slot], sems.at[slot])
    _row_gather(nxt_ref, ys_ref, buf_ref.at[1 - slot], sems.at[1 - slot], rows)

    rf = rf_ref[...]
    u = DN_ALPHA * _load_token_rows(h_ref, 0, t)
    for k in range(TOP_K):
        u = u + rf[:, k:k + 1] * _load_token_rows(buf_ref.at[slot], k * t, t)
    o_ref[...] = _layer_norm(u, g2_ref[...], b2_ref[...])

    @pl.when(i == n_steps - 1)
    def _():
        _rows_wait(ys_ref.at[pl.ds(0, rows * ROW_TILES)], buf_ref.at[1 - slot], sems.at[1 - slot])


def _combine(dest3, ys, h, rf, g2, b2):
    t = COMBINE_TILE
    n = h.shape[0] // ROW_TILES
    last = n // t - 1
    tok = lambda w: pl.BlockSpec((t, w), lambda i: (i, 0))
    idx_spec = lambda f: pl.BlockSpec((1, 1, t * TOP_K), f, memory_space=pltpu.SMEM)
    return pl.pallas_call(
        _combine_kernel,
        grid=(n // t,),
        in_specs=[idx_spec(lambda i: (i, 0, 0)), idx_spec(lambda i: (jnp.minimum(i + 1, last), 0, 0)),
                  pl.BlockSpec(memory_space=pl.ANY), pl.BlockSpec((t * ROW_TILES, LANES), lambda i: (i, 0)),
                  tok(LANES), _full(g2.shape), _full(b2.shape)],
        out_specs=tok(D_MODEL),
        out_shape=jax.ShapeDtypeStruct((n, D_MODEL), jnp.float32),
        scratch_shapes=[pltpu.VMEM((2, t * TOP_K * ROW_TILES, LANES), jnp.float32),
                        pltpu.SemaphoreType.DMA((2,))],
        compiler_params=_cparams(("arbitrary",)),
        name="combine",
    )(dest3, dest3, ys, h, rf, g2, b2)


def _rope_tables(positions):
    pos = positions.astype(jnp.float32).reshape(1, -1)
    halves = (A_HEAD_DIM // 2, QK_ROPE // 2)
    pieces, rows, src = [], [], 0
    for half in halves:
        ang = (ROPE_THETA ** (-jnp.arange(half, dtype=jnp.float32) / half)).reshape(-1, 1) * pos
        pieces += [jnp.cos(ang), jnp.sin(ang)]
        for _ in range(2):
            rows.append(src + np.arange(LANES) % half)
            src += half
    trig = jnp.concatenate(pieces + [jnp.zeros((LANES - src, pos.shape[1]), jnp.float32)], axis=0)
    expand = (np.arange(LANES)[:, None] == np.concatenate(rows)[None, :]).astype(np.float32)
    return trig, jnp.asarray(expand, jnp.bfloat16)


def _lookup(table, idx):
    experts = jnp.arange(N_EXPERTS, dtype=jnp.int32)
    return jnp.sum(jnp.where(idx[..., None] == experts, table, 0), axis=-1)


def _expert_schedule(starts, ends, rows):
    t = EXPERT_TILE
    first_tile = starts // t
    n_tile = jnp.where(ends > starts, (ends - 1) // t - first_tile + 1, 0)
    cum = jnp.cumsum(n_tile)
    n_steps = cum[-1]
    max_steps = rows // t + N_EXPERTS
    s = jnp.minimum(jnp.arange(max_steps, dtype=jnp.int32), n_steps - 1)
    expert = jnp.sum((cum[None, :] <= s[:, None]).astype(jnp.int32), axis=1)
    tile = _lookup(first_tile, expert) + s - _lookup(cum - n_tile, expert)
    lo = jnp.maximum(_lookup(starts, expert) - tile * t, 0)
    hi = jnp.minimum(_lookup(ends, expert) - tile * t, t)
    return expert, tile, lo, hi, n_steps.astype(jnp.int32).reshape(1)


def _layer(x2, tabs, bsz, seq, w_in, b_gate, rms_cq, rms_ckv, w_uq, w_ukv, w_o_a, w_o_b, w_out,
           ln1_g, ln1_b, w_router, b_router, w_up, b_up, w_down, b_down, ln2_g, ln2_b):
    bf = jnp.bfloat16
    n = x2.shape[0]
    off = np.concatenate([[0], np.cumsum(SPLITS)]).tolist()
    w_in_b = w_in.astype(bf)
    col = lambda j: w_in_b[:, off[j]:off[j + 1]]
    w_slab = jnp.concatenate(
        [col(9), col(4), col(4), jnp.tile(col(8), (1, LANES // QK_ROPE)), col(5),
         jnp.zeros((D_MODEL, LANES - IDX_HEADS), bf), col(6), col(7)], axis=1)
    w_o = jnp.concatenate([w_o_a, w_o_b, w_out], axis=0).astype(bf)
    uq = w_uq.reshape(Q_LORA, B_HEADS, QK_NOPE + QK_ROPE)
    wuq = jnp.concatenate([uq[:, :, :QK_NOPE].reshape(Q_LORA, -1), uq[:, :, QK_NOPE:].reshape(Q_LORA, -1)],
                          axis=1).astype(bf)
    ukv = w_ukv.reshape(KV_LORA, B_HEADS, QK_NOPE + V_DIM)
    wukv = jnp.concatenate([ukv[:, :, :QK_NOPE].reshape(KV_LORA, -1), ukv[:, :, QK_NOPE:].reshape(KV_LORA, -1)],
                           axis=1).astype(bf)

    qa, ka, va, qi, ki, kr, wi, qn, qr, kn, vb = _proj(
        x2, tabs, w_in_b, w_slab, rms_cq.reshape(1, -1), rms_ckv.reshape(1, -1), wuq, wukv)
    o_a = _dsa(qa, qi, wi, ka, va, ki, bsz, seq)
    o_b = _mla(qn, qr, kn, kr, vb, bsz, seq)

    wr = jnp.concatenate([w_router, jnp.zeros((D_MODEL, LANES - N_EXPERTS), w_router.dtype)], axis=1)
    br = jnp.concatenate([b_router, jnp.zeros((LANES - N_EXPERTS,), b_router.dtype)]).reshape(1, -1)
    wr_hi = wr.astype(bf)
    wr_lo = (wr - wr_hi.astype(jnp.float32)).astype(bf)
    h, ri, rf, cnt = _post(x2, o_a, o_b, w_slab, b_gate.reshape(1, -1), w_o, ln1_g.reshape(1, -1),
                           ln1_b.reshape(1, -1), wr_hi, wr_lo, br)

    counts = cnt[0, :N_EXPERTS].astype(jnp.int32)
    ends = jnp.cumsum(counts)
    starts = ends - counts
    dest = _lookup(starts, ri[:TOP_K]) + ri[TOP_K:]
    by_step = lambda t: dest.reshape(TOP_K, n // t, t).transpose(1, 0, 2).reshape(n // t, 1, t * TOP_K)

    xs = _dispatch(by_step(DISPATCH_TILE), h, n * TOP_K)
    ys = _experts(_expert_schedule(starts, ends, n * TOP_K), xs, w_up, b_up, w_down, b_down)
    return _combine(by_step(COMBINE_TILE), ys, h, rf, ln2_g.reshape(1, -1), ln2_b.reshape(1, -1))


def kernel(x, positions, w_in, b_gate, rms_cq, rms_ckv, w_uq, w_ukv, w_o_a, w_o_b, w_out, ln1_g, ln1_b,
           w_router, b_router, w_up, b_up, w_down, b_down, ln2_g, ln2_b):
    bsz, seq, _ = x.shape
    x2 = x.reshape(bsz * seq, D_MODEL)
    tabs = _rope_tables(positions)
    for l in range(DEPTH):
        x2 = _layer(x2, tabs, bsz, seq, w_in[l], b_gate[l], rms_cq[l], rms_ckv[l], w_uq[l], w_ukv[l],
                    w_o_a[l], w_o_b[l], w_out[l], ln1_g[l], ln1_b[l], w_router[l], b_router[l],
                    w_up[l], b_up[l], w_down[l], b_down[l], ln2_g[l], ln2_b[l])
    return x2.reshape(bsz, seq, D_MODEL)
```
